```python
import math
import jax, jax.numpy as jnp
from jax import lax
import numpy as np

D_MODEL = 1024
BATCH = 8
SEQ = 2048
DEPTH = 1
DEC_BATCH = 128
DEC_SEQ = 8
PAST_LEN = 16384
PAGE_SIZE = 128

FFN_DIM = 2816
CONV_W = 4
CHUNK = 64
EPS = 1e-6

GDN_DK = 128
GDN_DV = 128
GDN_HEADS = D_MODEL // 128
GDN_QK = GDN_HEADS * GDN_DK
GDN_V = GDN_HEADS * GDN_DV
GDN_CONV_CH = 2 * GDN_QK + GDN_V

SSM_DINNER = 2 * D_MODEL
SSM_HEADDIM = 64
SSM_HEADS = SSM_DINNER // SSM_HEADDIM
SSM_GROUPS = 4
SSM_DSTATE = 128
SSM_CONV_CH = SSM_DINNER + 2 * SSM_GROUPS * SSM_DSTATE

N_MEM = 256
MEM_HEADS = 4
MEM_HEAD_DIM = 64
MEM_WIDTH = MEM_HEADS * MEM_HEAD_DIM

N_BRANCH = 3
MIX_WIDTH = GDN_V + SSM_DINNER + MEM_WIDTH
IN_SPLITS = (GDN_CONV_CH, GDN_HEADS, GDN_HEADS, GDN_V,
             SSM_DINNER, SSM_CONV_CH, SSM_HEADS,
             MEM_WIDTH,
             N_BRANCH * D_MODEL)
IN_COLS = sum(IN_SPLITS)

kernel_name = 'hybrid_gdn_ssd_macaron_step'


def rmsnorm(x, g):
    xf = x.astype(jnp.float32)
    y = xf * lax.rsqrt(jnp.mean(xf * xf, axis=-1, keepdims=True) + EPS)
    return (y * g.astype(jnp.float32)).astype(x.dtype)


def l2norm(x):
    xf = x.astype(jnp.float32)
    return xf * lax.rsqrt(jnp.sum(xf * xf, axis=-1, keepdims=True) + EPS)


def swiglu(x, w_in, w_out):
    gate, up = jnp.split(x @ w_in, 2, axis=-1)
    return (jax.nn.silu(gate) * up) @ w_out


def causal_conv(x, buf, w):
    L = x.shape[1]
    xp = jnp.concatenate([buf.astype(x.dtype), x], axis=1)
    y = xp[:, 0:L] * w[0]
    for j in range(1, CONV_W):
        y = y + xp[:, j:j + L] * w[j]
    return y, xp[:, L:]


def gated_delta_chunked(q, k, v, beta, g, s0):
    bsz, L, H, _ = q.shape
    DV = v.shape[-1]
    c = math.gcd(L, CHUNK)
    n = L // c
    f32 = jnp.float32

    def to_chunks(t):
        t = t.astype(f32).reshape((bsz, n, c, H) + t.shape[3:])
        return jnp.moveaxis(t, 3, 1)

    q, k, v, beta, g = map(to_chunks, (q, k, v, beta, g))
    gc = jnp.cumsum(g, axis=-1)
    incl = jnp.tril(jnp.ones((c, c), bool))
    strict = jnp.tril(jnp.ones((c, c), bool), -1)
    decay = jnp.exp(jnp.where(incl, gc[..., :, None] - gc[..., None, :], -jnp.inf))
    kb = k * beta[..., None]
    a_mat = jnp.where(strict, jnp.einsum('bhnid,bhnjd->bhnij', kb, k) * decay, 0.0)
    eye = jnp.eye(c, dtype=f32)
    t_inv = lax.linalg.triangular_solve(eye + a_mat, jnp.broadcast_to(eye, a_mat.shape),
                                        left_side=True, lower=True)
    u = jnp.einsum('bhnij,bhnjd->bhnid', t_inv, v * beta[..., None])
    w = jnp.einsum('bhnij,bhnjd->bhnid', t_inv, kb * jnp.exp(gc)[..., None])
    qk = jnp.einsum('bhnid,bhnjd->bhnij', q, k) * decay
    q_dec = q * jnp.exp(gc)[..., None]
    g_last = gc[..., -1]
    k_dec = k * jnp.exp(g_last[..., None] - gc)[..., None]

    def step(s, inp):
        u_i, w_i, qk_i, qd_i, kd_i, gl_i = inp
        v_new = u_i - jnp.einsum('bhcd,bhde->bhce', w_i, s)
        o_i = jnp.einsum('bhcd,bhde->bhce', qd_i, s) + jnp.einsum('bhij,bhje->bhie', qk_i, v_new)
        s = s * jnp.exp(gl_i)[..., None, None] + jnp.einsum('bhcd,bhce->bhde', kd_i, v_new)
        return s, o_i

    xs = tuple(jnp.moveaxis(t, 2, 0) for t in (u, w, qk, q_dec, k_dec, g_last))
    s, o = lax.scan(step, s0.astype(f32), xs)
    o = jnp.transpose(o, (1, 0, 3, 2, 4)).reshape(bsz, L, H, DV)
    return o, s


def ssd_chunked(x, dt, a, bm, cm, h0):
    bsz, L, H, P = x.shape
    G, N = bm.shape[2], bm.shape[3]
    R = H // G
    c = math.gcd(L, CHUNK)
    n = L // c
    f32 = jnp.float32
    x = x.astype(f32).reshape(bsz, n, c, G, R, P)
    dt = dt.astype(f32).reshape(bsz, n, c, G, R)
    bm = bm.astype(f32).reshape(bsz, n, c, G, N)
    cm = cm.astype(f32).reshape(bsz, n, c, G, N)
    gc = jnp.cumsum(dt * a.astype(f32).reshape(G, R), axis=2)
    mask = jnp.tril(jnp.ones((c, c), bool))[:, :, None, None]
    decay = jnp.exp(jnp.where(mask, gc[:, :, :, None] - gc[:, :, None, :], -jnp.inf))
    xdt = x * dt[..., None]
    cb = jnp.einsum('bnigs,bnjgs->bnijg', cm, bm)
    y_intra = jnp.einsum('bnijg,bnijgr,bnjgrp->bnigrp', cb, decay, xdt)
    g_last = gc[:, :, -1]
    xw = xdt * jnp.exp(g_last[:, :, None] - gc)[..., None]

    def step(h, inp):
        c_i, b_i, xw_i, eg_i, egl_i = inp
        y_i = jnp.einsum('bigs,bgrps->bigrp', c_i, h) * eg_i[..., None]
        h = h * egl_i[..., None, None] + jnp.einsum('bjgs,bjgrp->bgrps', b_i, xw_i)
        return h, y_i

    xs = tuple(jnp.moveaxis(t, 1, 0) for t in (cm, bm, xw, jnp.exp(gc), jnp.exp(g_last)))
    h, y_inter = lax.scan(step, h0.astype(f32).reshape(bsz, G, R, P, N), xs)
    y = y_intra + jnp.moveaxis(y_inter, 0, 1)
    return y.reshape(bsz, L, H, P), h.reshape(bsz, H, P, N)


def memory_kv(mem, norm_mem, w_mem_kv):
    kv = rmsnorm(mem, norm_mem) @ w_mem_kv
    k, v = jnp.split(kv, 2, axis=-1)
    shp = mem.shape[:2] + (MEM_HEADS, MEM_HEAD_DIM)
    return k.reshape(shp), v.reshape(shp)


def hybrid_layer(x, mem_k, mem_v, s_gdn, c_gdn, s_ssm, c_ssm,
                 norm_ff1, w_ff1_in, w_ff1_out, norm_mix, w_in,
                 gdn_conv_w, gdn_a_log, gdn_dt_bias, gdn_norm,
                 ssm_conv_w, ssm_conv_b, ssm_a_log, ssm_dt_bias, ssm_d, ssm_norm,
                 w_branch, w_out, norm_ff2, w_ff2_in, w_ff2_out):
    f32 = jnp.float32
    bsz, L, _ = x.shape
    h = x + 0.5 * swiglu(rmsnorm(x, norm_ff1), w_ff1_in, w_ff1_out)
    u = rmsnorm(h, norm_mix)
    offs = np.cumsum(IN_SPLITS)[:-1].tolist()
    (qkv, a_raw, b_raw, z_gdn, z_ssm, xbc, dt_raw, q_mem, gate_raw) = jnp.split(u @ w_in, offs, axis=-1)

    qkv, c_gdn_new = causal_conv(qkv, c_gdn, gdn_conv_w)
    qkv = jax.nn.silu(qkv)
    q, k, v = jnp.split(qkv, [GDN_QK, 2 * GDN_QK], axis=-1)
    q = l2norm(q.reshape(bsz, L, GDN_HEADS, GDN_DK)) * (GDN_DK ** -0.5)
    k = l2norm(k.reshape(bsz, L, GDN_HEADS, GDN_DK))
    v = v.reshape(bsz, L, GDN_HEADS, GDN_DV)
    beta = jax.nn.sigmoid(b_raw.astype(f32))
    g = -jnp.exp(gdn_a_log.astype(f32)) * jax.nn.softplus(a_raw.astype(f32) + gdn_dt_bias.astype(f32))
    o, s_gdn_new = gated_delta_chunked(q, k, v, beta, g, s_gdn)
    o = rmsnorm(o.astype(x.dtype), gdn_norm) * jax.nn.silu(z_gdn.reshape(bsz, L, GDN_HEADS, GDN_DV))
    y_gdn = o.reshape(bsz, L, GDN_V)

    xbc, c_ssm_new = causal_conv(xbc, c_ssm, ssm_conv_w)
    xbc = jax.nn.silu(xbc + ssm_conv_b)
    xs_, bm, cm = jnp.split(xbc, [SSM_DINNER, SSM_DINNER + SSM_GROUPS * SSM_DSTATE], axis=-1)
    xs_ = xs_.reshape(bsz, L, SSM_HEADS, SSM_HEADDIM)
    dt = jax.nn.softplus(dt_raw.astype(f32) + ssm_dt_bias.astype(f32))
    a = -jnp.exp(ssm_a_log.astype(f32))
    y, s_ssm_new = ssd_chunked(xs_, dt, a,
                               bm.reshape(bsz, L, SSM_GROUPS, SSM_DSTATE),
                               cm.reshape(bsz, L, SSM_GROUPS, SSM_DSTATE), s_ssm)
    y = y + ssm_d.astype(f32)[:, None] * xs_.astype(f32)
    y = (y.astype(x.dtype) * jax.nn.silu(z_ssm.reshape(bsz, L, SSM_HEADS, SSM_HEADDIM)))
    y = rmsnorm(y.reshape(bsz, L, SSM_GROUPS, SSM_DINNER // SSM_GROUPS), ssm_norm.reshape(SSM_GROUPS, -1))
    y_ssm = y.reshape(bsz, L, SSM_DINNER)

    qm = q_mem.reshape(bsz, L, MEM_HEADS, MEM_HEAD_DIM)
    sc = jnp.einsum('blhd,bmhd->bhlm', qm, mem_k).astype(f32) * (MEM_HEAD_DIM ** -0.5)
    p = jax.nn.softmax(sc, axis=-1)
    y_mem = jnp.einsum('bhlm,bmhd->blhd', p.astype(mem_v.dtype), mem_v).reshape(bsz, L, MEM_WIDTH)

    wb_gdn, wb_ssm, wb_mem = jnp.split(w_branch, [GDN_V, GDN_V + SSM_DINNER], axis=0)
    g_gdn, g_ssm, g_mem = jnp.split(jax.nn.sigmoid(gate_raw), N_BRANCH, axis=-1)
    merged = g_gdn * (y_gdn @ wb_gdn) + g_ssm * (y_ssm @ wb_ssm) + g_mem * (y_mem @ wb_mem)
    h = h + merged @ w_out
    h = h + 0.5 * swiglu(rmsnorm(h, norm_ff2), w_ff2_in, w_ff2_out)
    return h, (s_gdn_new.astype(s_gdn.dtype), c_gdn_new, s_ssm_new.astype(s_ssm.dtype), c_ssm_new)


def setup_inputs(seed: int = 0) -> dict:
    key = jax.random.key(seed)
    ks = iter(jax.random.split(key, 48))
    f32 = jnp.float32

    def nrm(shape, scale):
        return jax.random.normal(next(ks), shape, f32) * scale

    def gain(shape):
        return 1.0 + nrm(shape, 0.02)

    def a_log(nh):
        return jnp.log(jax.random.uniform(next(ks), (DEPTH, nh), f32, 1.0, 16.0))

    def dt_bias(nh):
        dtv = jnp.exp(jax.random.uniform(next(ks), (DEPTH, nh), f32, math.log(1e-3), math.log(1e-1)))
        return dtv + jnp.log(-jnp.expm1(-dtv))

    return {
        'x_prompt': nrm((BATCH, SEQ, D_MODEL), 1.0),
        'x_sample': nrm((DEC_BATCH, DEC_SEQ, D_MODEL), 1.0),
        'mem_prompt': nrm((BATCH, N_MEM, D_MODEL), 1.0),
        'state_gdn': nrm((DEPTH, DEC_BATCH, GDN_HEADS, GDN_DK, GDN_DV), 0.1),
        'state_gdn_conv': nrm((DEPTH, DEC_BATCH, CONV_W - 1, GDN_CONV_CH), 1.0),
        'state_ssm': nrm((DEPTH, DEC_BATCH, SSM_HEADS, SSM_HEADDIM, SSM_DSTATE), 0.1),
        'state_ssm_conv': nrm((DEPTH, DEC_BATCH, CONV_W - 1, SSM_CONV_CH), 1.0),
        'cache_mem_k': nrm((DEPTH, DEC_BATCH, N_MEM, MEM_HEADS, MEM_HEAD_DIM), 1.0),
        'cache_mem_v': nrm((DEPTH, DEC_BATCH, N_MEM, MEM_HEADS, MEM_HEAD_DIM), 1.0),
        'norm_ff1': gain((DEPTH, D_MODEL)),
        'w_ff1_in': nrm((DEPTH, D_MODEL, 2 * FFN_DIM), D_MODEL ** -0.5),
        'w_ff1_out': nrm((DEPTH, FFN_DIM, D_MODEL), FFN_DIM ** -0.5),
        'norm_mix': gain((DEPTH, D_MODEL)),
        'w_in': nrm((DEPTH, D_MODEL, IN_COLS), D_MODEL ** -0.5),
        'gdn_conv_w': nrm((DEPTH, CONV_W, GDN_CONV_CH), 0.5),
        'gdn_a_log': a_log(GDN_HEADS),
        'gdn_dt_bias': dt_bias(GDN_HEADS),
        'gdn_norm': gain((DEPTH, GDN_DV)),
        'ssm_conv_w': nrm((DEPTH, CONV_W, SSM_CONV_CH), 0.5),
        'ssm_conv_b': nrm((DEPTH, SSM_CONV_CH), 0.01),
        'ssm_a_log': a_log(SSM_HEADS),
        'ssm_dt_bias': dt_bias(SSM_HEADS),
        'ssm_d': gain((DEPTH, SSM_HEADS)),
        'ssm_norm': gain((DEPTH, SSM_DINNER)),
        'norm_mem': gain((DEPTH, D_MODEL)),
        'w_mem_kv': nrm((DEPTH, D_MODEL, 2 * MEM_WIDTH), D_MODEL ** -0.5),
        'w_branch': jnp.concatenate([nrm((DEPTH, GDN_V, D_MODEL), GDN_V ** -0.5),
                                     nrm((DEPTH, SSM_DINNER, D_MODEL), SSM_DINNER ** -0.5),
                                     nrm((DEPTH, MEM_WIDTH, D_MODEL), MEM_WIDTH ** -0.5)], axis=1),
        'w_out': nrm((DEPTH, D_MODEL, D_MODEL), D_MODEL ** -0.5),
        'norm_ff2': gain((DEPTH, D_MODEL)),
        'w_ff2_in': nrm((DEPTH, D_MODEL, 2 * FFN_DIM), D_MODEL ** -0.5),
        'w_ff2_out': nrm((DEPTH, FFN_DIM, D_MODEL), FFN_DIM ** -0.5),
        'norm_final': gain((D_MODEL,)),
    }


def reference(x_prompt, x_sample, mem_prompt, state_gdn, state_gdn_conv, state_ssm, state_ssm_conv,
              cache_mem_k, cache_mem_v, norm_ff1, w_ff1_in, w_ff1_out, norm_mix, w_in,
              gdn_conv_w, gdn_a_log, gdn_dt_bias, gdn_norm, ssm_conv_w, ssm_conv_b, ssm_a_log,
              ssm_dt_bias, ssm_d, ssm_norm, norm_mem, w_mem_kv, w_branch, w_out,
              norm_ff2, w_ff2_in, w_ff2_out, norm_final):
    bp = x_prompt.shape[0]
    dtp = x_prompt.dtype
    hp, hs = x_prompt, x_sample
    pst, sst = [], []
    for l in range(DEPTH):
        lw = (norm_ff1[l], w_ff1_in[l], w_ff1_out[l], norm_mix[l], w_in[l],
              gdn_conv_w[l], gdn_a_log[l], gdn_dt_bias[l], gdn_norm[l],
              ssm_conv_w[l], ssm_conv_b[l], ssm_a_log[l], ssm_dt_bias[l], ssm_d[l], ssm_norm[l],
              w_branch[l], w_out[l], norm_ff2[l], w_ff2_in[l], w_ff2_out[l])
        mk, mv = memory_kv(mem_prompt, norm_mem[l], w_mem_kv[l])
        hp, stp = hybrid_layer(hp, mk, mv,
                               jnp.zeros((bp, GDN_HEADS, GDN_DK, GDN_DV), dtp),
                               jnp.zeros((bp, CONV_W - 1, GDN_CONV_CH), dtp),
                               jnp.zeros((bp, SSM_HEADS, SSM_HEADDIM, SSM_DSTATE), dtp),
                               jnp.zeros((bp, CONV_W - 1, SSM_CONV_CH), dtp), *lw)
        pst.append(stp + (mk, mv))
        hs, sts = hybrid_layer(hs, cache_mem_k[l], cache_mem_v[l], state_gdn[l], state_gdn_conv[l],
                               state_ssm[l], state_ssm_conv[l], *lw)
        sst.append(sts)
    y_prompt = rmsnorm(hp, norm_final)
    y_sample = rmsnorm(hs, norm_final)
    new_state_gdn_prompt = jnp.stack([t[0] for t in pst])
    new_state_gdn_conv_prompt = jnp.stack([t[1] for t in pst])
    new_state_ssm_prompt = jnp.stack([t[2] for t in pst])
    new_state_ssm_conv_prompt = jnp.stack([t[3] for t in pst])
    new_cache_mem_k_prompt = jnp.stack([t[4] for t in pst])
    new_cache_mem_v_prompt = jnp.stack([t[5] for t in pst])
    new_state_gdn_sample = jnp.stack([t[0] for t in sst])
    new_state_gdn_conv_sample = jnp.stack([t[1] for t in sst])
    new_state_ssm_sample = jnp.stack([t[2] for t in sst])
    new_state_ssm_conv_sample = jnp.stack([t[3] for t in sst])
    return (y_prompt, y_sample, new_state_gdn_prompt, new_state_gdn_conv_prompt, new_state_ssm_prompt,
            new_state_ssm_conv_prompt, new_cache_mem_k_prompt, new_cache_mem_v_prompt,
            new_state_gdn_sample, new_state_gdn_conv_sample, new_state_ssm_sample, new_state_ssm_conv_sample)
```

```python
import functools

import numpy as np
import jax
import jax.numpy as jnp
from jax import lax
from jax.experimental import pallas as pl
from jax.experimental.pallas import tpu as pltpu

F32 = jnp.float32
BF16 = jnp.bfloat16
EPS = 1e-6

D_MODEL = 1024
FFN_DIM = 2816
CONV_W = 4
CHUNK = 64
LANES = 128
SUBLANES = 8

GDN_HEADS = 8
GDN_DK = 128
GDN_DV = 128
GDN_QK = GDN_HEADS * GDN_DK
GDN_V = GDN_HEADS * GDN_DV
GDN_CONV_CH = 2 * GDN_QK + GDN_V

SSM_DINNER = 2048
SSM_HEADDIM = 64
SSM_HEADS = 32
SSM_GROUPS = 4
SSM_DSTATE = 128
SSM_CONV_CH = SSM_DINNER + 2 * SSM_GROUPS * SSM_DSTATE
SSM_PAIRS = SSM_HEADS // 2
SSM_GROUP_CH = SSM_DINNER // SSM_GROUPS

N_MEM = 256
MEM_HEADS = 4
MEM_HEAD_DIM = 64
MEM_WIDTH = MEM_HEADS * MEM_HEAD_DIM

COL_QKV = 0
COL_XBC = COL_QKV + GDN_CONV_CH
COL_GATE = COL_XBC + SSM_CONV_CH
COL_ZG = COL_GATE + 3 * D_MODEL
COL_ZS = COL_ZG + GDN_V
COL_QM = COL_ZS + SSM_DINNER
COL_SMALL = COL_QM + MEM_WIDTH
PROJ_COLS = COL_SMALL + LANES
SM_A = 0
SM_B = GDN_HEADS
SM_DT = 2 * GDN_HEADS

VMEM_LIMIT = 56 * 1024 * 1024


def _params(semantics):
    return pltpu.CompilerParams(dimension_semantics=semantics, vmem_limit_bytes=VMEM_LIMIT)


def _mm(a, b):
    return jnp.dot(a, b, preferred_element_type=F32)


def _mm_nt(a, b):
    return lax.dot_general(a, b, (((1,), (1,)), ((), ())), preferred_element_type=F32)


def _mm_tn(a, b):
    return lax.dot_general(a, b, (((0,), (0,)), ((), ())), preferred_element_type=F32)


def _split3(x):
    hi = x.astype(BF16)
    r1 = x - hi.astype(F32)
    mid = r1.astype(BF16)
    lo = (r1 - mid.astype(F32)).astype(BF16)
    return hi, mid, lo


def _mm_sel(x, sel):
    hi, mid, lo = _split3(x)
    return _mm(hi, sel) + _mm(mid, sel) + _mm(lo, sel)


def _sel_mm(sel, x):
    hi, mid, lo = _split3(x)
    return _mm(sel, hi) + _mm(sel, mid) + _mm(sel, lo)


def _rms(x, g):
    return x * lax.rsqrt(jnp.mean(x * x, axis=-1, keepdims=True) + EPS) * g


def _sigmoid(x):
    return 1.0 / (1.0 + jnp.exp(-x))


def _silu(x):
    return x * _sigmoid(x)


def _softplus(x):
    return jnp.maximum(x, 0.0) + jnp.log1p(jnp.exp(-jnp.abs(x)))


def _const_spec(shape):
    nd = len(shape)
    return pl.BlockSpec(shape, lambda *_: (0,) * nd, pipeline_mode=pl.Buffered(1))


FFN_TM = 512
FFN_TF = FFN_DIM // 2


def _ffn_compute(x, g_ref, wi_ref, wo_ref):
    xn = _rms(x, g_ref[...]).astype(BF16)
    acc = None
    for c in range(FFN_DIM // FFN_TF):
        lo = c * FFN_TF
        gate = _mm(xn, wi_ref[:, lo:lo + FFN_TF])
        up = _mm(xn, wi_ref[:, FFN_DIM + lo:FFN_DIM + lo + FFN_TF])
        act = (_silu(gate) * up).astype(BF16)
        part = _mm(act, wo_ref[lo:lo + FFN_TF, :])
        acc = part if acc is None else acc + part
    return x + 0.5 * acc


def _ffn1_body(x_ref, g_ref, wi_ref, wo_ref, g2_ref, h_ref, u_ref):
    h = _ffn_compute(x_ref[...], g_ref, wi_ref, wo_ref)
    h_ref[...] = h
    u_ref[...] = _rms(h, g2_ref[...]).astype(BF16)


def _ffn1(x, g, wi, wo, g2):
    t = x.shape[0]
    tm = min(FFN_TM, t)
    row = lambda i: (i, 0)
    return pl.pallas_call(
        _ffn1_body,
        grid=(t // tm,),
        in_specs=[pl.BlockSpec((tm, D_MODEL), row), _const_spec(g.shape), _const_spec(wi.shape),
                  _const_spec(wo.shape), _const_spec(g2.shape)],
        out_specs=[pl.BlockSpec((tm, D_MODEL), row), pl.BlockSpec((tm, D_MODEL), row)],
        out_shape=[jax.ShapeDtypeStruct((t, D_MODEL), F32), jax.ShapeDtypeStruct((t, D_MODEL), BF16)],
        compiler_params=_params(("parallel",)),
        name="ffn1",
    )(x, g, wi, wo, g2)


PROJ_TM = 1024
PROJ_TN = PROJ_COLS // 9


def _proj_body(u_ref, w_ref, o_ref):
    o_ref[...] = _mm(u_ref[...], w_ref[...])


def _in_proj(u, w):
    t = u.shape[0]
    tm = min(PROJ_TM, t)
    return pl.pallas_call(
        _proj_body,
        grid=(PROJ_COLS // PROJ_TN, t // tm),
        in_specs=[pl.BlockSpec((tm, D_MODEL), lambda j, i: (i, 0)),
                  pl.BlockSpec((D_MODEL, PROJ_TN), lambda j, i: (0, j))],
        out_specs=pl.BlockSpec((tm, PROJ_TN), lambda j, i: (i, j)),
        out_shape=jax.ShapeDtypeStruct((t, PROJ_COLS), F32),
        compiler_params=_params(("parallel", "parallel")),
        name="in_proj",
    )(u, w)


def _memkv_body(x_ref, g_ref, w_ref, o_ref):
    o_ref[...] = _mm(_rms(x_ref[...], g_ref[...]).astype(BF16), w_ref[...])


def _memkv(mem, g, w):
    t = mem.shape[0]
    tm = min(512, t)
    row = lambda i: (i, 0)
    return pl.pallas_call(
        _memkv_body,
        grid=(t // tm,),
        in_specs=[pl.BlockSpec((tm, D_MODEL), row), _const_spec(g.shape), _const_spec(w.shape)],
        out_specs=pl.BlockSpec((tm, 2 * MEM_WIDTH), row),
        out_shape=jax.ShapeDtypeStruct((t, 2 * MEM_WIDTH), F32),
        compiler_params=_params(("parallel",)),
        name="memkv",
    )(mem, g, w)


MERGE_TM = 256


def _merge_ffn2_body(h_ref, yg_ref, ys_ref, ym_ref, gate_ref, wb_ref, wout_ref,
                     g_ref, wi_ref, wo_ref, gf_ref, y_ref):
    gates = gate_ref[...]
    merged = (_sigmoid(gates[:, 0:D_MODEL]) * _mm(yg_ref[...], wb_ref[0:GDN_V, :])
              + _sigmoid(gates[:, D_MODEL:2 * D_MODEL])
              * _mm(ys_ref[...], wb_ref[GDN_V:GDN_V + SSM_DINNER, :])
              + _sigmoid(gates[:, 2 * D_MODEL:3 * D_MODEL])
              * _mm(ym_ref[...], wb_ref[GDN_V + SSM_DINNER:, :]))
    h = h_ref[...] + _mm(merged.astype(BF16), wout_ref[...])
    h = _ffn_compute(h, g_ref, wi_ref, wo_ref)
    y_ref[...] = _rms(h, gf_ref[...])


def _merge_ffn2(h, yg, ys, ym, proj, wb, wout, g, wi, wo, gf):
    t = h.shape[0]
    tm = min(MERGE_TM, t)
    row = lambda i: (i, 0)
    gate_blk = COL_GATE // (3 * D_MODEL)
    return pl.pallas_call(
        _merge_ffn2_body,
        grid=(t // tm,),
        in_specs=[pl.BlockSpec((tm, D_MODEL), row), pl.BlockSpec((tm, GDN_V), row),
                  pl.BlockSpec((tm, SSM_DINNER), row), pl.BlockSpec((tm, MEM_WIDTH), row),
                  pl.BlockSpec((tm, 3 * D_MODEL), lambda i: (i, gate_blk)),
                  _const_spec(wb.shape), _const_spec(wout.shape), _const_spec(g.shape),
                  _const_spec(wi.shape), _const_spec(wo.shape), _const_spec(gf.shape)],
        out_specs=pl.BlockSpec((tm, D_MODEL), row),
        out_shape=jax.ShapeDtypeStruct((t, D_MODEL), F32),
        compiler_params=_params(("parallel",)),
        name="merge_ffn2",
    )(h, yg, ys, ym, proj, wb, wout, g, wi, wo, gf)


def _memattn_body(q_ref, k_ref, v_ref, y_ref):
    q = q_ref[...]
    k = k_ref[0].astype(BF16)
    v = v_ref[0].astype(BF16)
    lane = lax.broadcasted_iota(jnp.int32, q.shape, 1)
    out = jnp.zeros(q.shape, F32)
    for hh in range(MEM_HEADS):
        in_head = (lane >= hh * MEM_HEAD_DIM) & (lane < (hh + 1) * MEM_HEAD_DIM)
        s = _mm_nt(jnp.where(in_head, q, 0.0).astype(BF16), k) * (MEM_HEAD_DIM ** -0.5)
        s = s - jnp.max(s, axis=-1, keepdims=True)
        p = jnp.exp(s)
        p = p / jnp.sum(p, axis=-1, keepdims=True)
        out = out + jnp.where(in_head, _mm(p.astype(BF16), v), 0.0)
    y_ref[...] = out.astype(BF16)


def _memattn(proj, k, v, bsz, seq):
    tl = min(512, seq)
    nt = seq // tl
    qm_blk = COL_QM // MEM_WIDTH
    return pl.pallas_call(
        _memattn_body,
        grid=(bsz, nt),
        in_specs=[pl.BlockSpec((tl, MEM_WIDTH), lambda b, i: (b * nt + i, qm_blk)),
                  pl.BlockSpec((1, N_MEM, MEM_WIDTH), lambda b, i: (b, 0, 0)),
                  pl.BlockSpec((1, N_MEM, MEM_WIDTH), lambda b, i: (b, 0, 0))],
        out_specs=pl.BlockSpec((tl, MEM_WIDTH), lambda b, i: (b * nt + i, 0)),
        out_shape=jax.ShapeDtypeStruct((bsz * seq, MEM_WIDTH), BF16),
        compiler_params=_params(("parallel", "arbitrary")),
        name="memattn",
    )(proj, k, v)


SEQ_TL = 256
CONV_PAD = SUBLANES


def _conv_silu(x_ref, cst_ref, cw_ref, cb_ref, xbuf, act, tlv, first_tile):
    ch = x_ref.shape[1]

    @pl.when(first_tile)
    def _():
        xbuf[0:CONV_PAD, :] = cst_ref[0]

    @pl.when(jnp.logical_not(first_tile))
    def _():
        xbuf[0:CONV_PAD, :] = xbuf[tlv:tlv + CONV_PAD, :]

    xbuf[CONV_PAD:CONV_PAD + tlv, :] = x_ref[...]

    def col_block(cb, carry):
        c0 = pl.multiple_of(cb * LANES, LANES)
        w = cw_ref[:, pl.ds(c0, LANES)]
        y = None
        for j in range(CONV_W):
            start = CONV_PAD - (CONV_W - 1) + j
            term = xbuf[start:start + tlv, pl.ds(c0, LANES)] * w[j:j + 1, :]
            y = term if y is None else y + term
        if cb_ref is not None:
            y = y + cb_ref[:, pl.ds(c0, LANES)]
        act[0:tlv, pl.ds(c0, LANES)] = _silu(y)
        return carry

    lax.fori_loop(0, ch // LANES, col_block, 0)


def _pair_masks():
    row = lax.broadcasted_iota(jnp.int32, (CHUNK, LANES), 0)
    lane = lax.broadcasted_iota(jnp.int32, (CHUNK, LANES), 1)
    col = jnp.where(lane < CHUNK, lane, lane - CHUNK)
    return row, lane, col


def _block_diag(x, left):
    zero = jnp.zeros_like(x)
    return jnp.concatenate([jnp.where(left, x, zero), jnp.where(left, zero, x)], axis=0)


def _row_values(tile, lane, base):
    hi, mid, lo = _split3(tile)
    zero = jnp.zeros_like(hi)
    return jnp.where(lane == base, hi, jnp.where(lane == base + 1, mid,
                                                 jnp.where(lane == base + 2, lo, zero)))


def _gdn_body(x_ref, sm_ref, z_ref, cst_ref, cw_ref, s0_ref, par_ref, gn_ref, sela_ref, selb_ref,
              y_ref, sout_ref, xbuf, act, gates, s_scr, *, tlv, nch):
    tile = pl.program_id(1)
    ntile = pl.num_programs(1)
    first = tile == 0

    if tlv < CHUNK:
        act[...] = jnp.zeros_like(act)
    _conv_silu(x_ref, cst_ref, cw_ref, None, xbuf, act, tlv, first)

    @pl.when(first)
    def _():
        s_scr[...] = s0_ref[0]

    rows_g = gates.shape[0]
    sm = sm_ref[...]
    lane_g = lax.broadcasted_iota(jnp.int32, sm.shape, 1)
    neg_a = -jnp.exp(par_ref[0:1, :])
    log_decay = jnp.where(lane_g < SM_B, neg_a * _softplus(sm + par_ref[1:2, :]), 0.0)
    beta = jnp.where((lane_g >= SM_B) & (lane_g < SM_DT), _sigmoid(sm), 0.0)
    if tlv < rows_g:
        gates[...] = jnp.zeros_like(gates)
    gates[0:tlv, 0:LANES] = log_decay
    gates[0:tlv, LANES:2 * LANES] = beta

    row, lane, col = _pair_masks()
    left = lane < CHUNK
    incl = row >= col
    strict = row > col
    eye = jnp.where(row == col, 1.0, 0.0).astype(F32)
    tril = jnp.where(lax.broadcasted_iota(jnp.int32, (CHUNK, CHUNK), 0)
                     >= lax.broadcasted_iota(jnp.int32, (CHUNK, CHUNK), 1), 1.0, 0.0).astype(BF16)
    ones = jnp.ones((CHUNK, LANES), BF16)
    lane2 = lax.broadcasted_iota(jnp.int32, (2 * CHUNK, LANES), 1)
    zeros_h = jnp.zeros((CHUNK, LANES), F32)

    def chunk_step(c, carry):
        r0 = pl.multiple_of(c * CHUNK, CHUNK)
        g_c = gates[pl.ds(r0, CHUNK), 0:LANES]
        b_c = gates[pl.ds(r0, CHUNK), LANES:2 * LANES]
        gc = _sel_mm(tril, g_c)
        gc_e = _mm_sel(gc, sela_ref[...])
        beta_e = _mm_sel(b_c, selb_ref[...])
        eg_e = jnp.exp(gc_e)
        gl_e = jnp.broadcast_to(gc_e[CHUNK - 1:CHUNK, :], gc_e.shape)
        kdec_e = jnp.exp(gl_e - gc_e)
        egl_e = jnp.exp(gc_e[CHUNK - 1:CHUNK, :])

        for p in range(GDN_HEADS // 2):
            h1, h2 = 2 * p, 2 * p + 1
            hs = lambda a, h: a[:, h * LANES:(h + 1) * LANES]
            stack = lambda a: jnp.concatenate([hs(a, h1), hs(a, h2)], axis=0)

            def load(base):
                return [act[pl.ds(r0, CHUNK), base + h * LANES:base + (h + 1) * LANES]
                        for h in (h1, h2)]

            def l2n(t):
                return t * lax.rsqrt(jnp.sum(t * t, axis=-1, keepdims=True) + EPS)

            q1, q2 = [l2n(t) * (GDN_DK ** -0.5) for t in load(0)]
            k1, k2 = [l2n(t) for t in load(GDN_QK)]
            v1, v2 = load(2 * GDN_QK)
            beta2 = stack(beta_e)
            gc2 = stack(gc_e)
            eg2 = stack(eg_e)
            k2s = jnp.concatenate([k1, k2], axis=0)
            kb2 = k2s * beta2

            r_pair = _mm_nt(ones, _row_values(gc2, lane2, 0))
            gci_pair = jnp.where(left, hs(gc_e, h1), hs(gc_e, h2))
            decay = jnp.where(incl, jnp.exp(jnp.where(incl, gci_pair - r_pair, 0.0)), 0.0)

            kbd = jnp.concatenate([jnp.concatenate([k1, zeros_h], axis=1),
                                   jnp.concatenate([zeros_h, k2], axis=1)], axis=0).astype(BF16)
            kb_cat = jnp.concatenate([kb2[0:CHUNK], kb2[CHUNK:]], axis=1).astype(BF16)
            q_cat = jnp.concatenate([q1, q2], axis=1).astype(BF16)
            a_mat = jnp.where(strict, _mm_nt(kb_cat, kbd) * decay, 0.0)
            qk = _mm_nt(q_cat, kbd) * decay

            n_pow = -a_mat
            t_inv = eye + n_pow
            for _ in range(int(np.log2(CHUNK)) - 1):
                n_bd = _block_diag(n_pow, left).astype(BF16)
                n_pow = _mm(n_pow.astype(BF16), n_bd)
                t_inv = t_inv + _mm(t_inv.astype(BF16), _block_diag(n_pow, left).astype(BF16))

            t_bd = _block_diag(t_inv, left).astype(BF16)
            v2s = jnp.concatenate([v1, v2], axis=0)
            rhs = jnp.concatenate([v2s * beta2, kb2 * eg2], axis=1).astype(BF16)
            uw = _mm(t_bd, rhs)
            u2 = uw[:, 0:LANES]
            w2 = uw[:, LANES:]
            qd2 = jnp.concatenate([q1, q2], axis=0) * eg2
            kdec2 = k2s * stack(kdec_e)

            vnew = []
            qs = []
            for idx, h in enumerate((h1, h2)):
                sl = slice(idx * CHUNK, (idx + 1) * CHUNK)
                s_h = s_scr[h]
                wq = jnp.concatenate([w2[sl], qd2[sl]], axis=0).astype(BF16)
                wq_s = _mm(wq, s_h.astype(BF16))
                vn = u2[sl] - wq_s[0:CHUNK]
                vnew.append(vn)
                qs.append(wq_s[CHUNK:])
                s_scr[h] = (s_h * hs(egl_e, h)
                            + _mm_tn(kdec2[sl].astype(BF16), vn.astype(BF16)))
            vnew2 = jnp.concatenate(vnew, axis=0)
            o2 = jnp.concatenate(qs, axis=0) + _mm(_block_diag(qk, left).astype(BF16),
                                                   vnew2.astype(BF16))

            for idx, h in enumerate((h1, h2)):
                o = o2[idx * CHUNK:(idx + 1) * CHUNK]
                z = z_ref[pl.ds(r0, min(CHUNK, tlv)), h * LANES:(h + 1) * LANES]
                yv = _rms(o[0:min(CHUNK, tlv)], gn_ref[...]) * _silu(z)
                y_ref[pl.ds(r0, min(CHUNK, tlv)), h * LANES:(h + 1) * LANES] = yv.astype(BF16)
        return carry

    lax.fori_loop(0, nch, chunk_step, 0)

    @pl.when(tile == ntile - 1)
    def _():
        sout_ref[0] = s_scr[...]


def _gdn(proj, zg_src, cst8, cw, s0, par, gn, sela, selb, bsz, seq):
    tlv = min(SEQ_TL, seq)
    nt = seq // tlv
    nch = max(1, tlv // CHUNK)
    rows = nch * CHUNK
    qkv_blk = COL_QKV // GDN_CONV_CH
    zg_blk = COL_ZG // GDN_V
    sm_blk = COL_SMALL // LANES
    tok = lambda blk: (lambda b, i: (b * nt + i, blk))
    return pl.pallas_call(
        functools.partial(_gdn_body, tlv=tlv, nch=nch),
        grid=(bsz, nt),
        in_specs=[pl.BlockSpec((tlv, GDN_CONV_CH), tok(qkv_blk)),
                  pl.BlockSpec((tlv, LANES), tok(sm_blk)),
                  pl.BlockSpec((tlv, GDN_V), tok(zg_blk)),
                  pl.BlockSpec((1, CONV_PAD, GDN_CONV_CH), lambda b, i: (b, 0, 0)),
                  _const_spec(cw.shape),
                  pl.BlockSpec((1, GDN_HEADS, GDN_DK, GDN_DV), lambda b, i: (b, 0, 0, 0)),
                  _const_spec(par.shape), _const_spec(gn.shape),
                  _const_spec(sela.shape), _const_spec(selb.shape)],
        out_specs=[pl.BlockSpec((tlv, GDN_V), lambda b, i: (b * nt + i, 0)),
                   pl.BlockSpec((1, GDN_HEADS, GDN_DK, GDN_DV), lambda b, i: (b, 0, 0, 0))],
        out_shape=[jax.ShapeDtypeStruct((bsz * seq, GDN_V), BF16),
                   jax.ShapeDtypeStruct((bsz, GDN_HEADS, GDN_DK, GDN_DV), F32)],
        scratch_shapes=[pltpu.VMEM((tlv + CONV_PAD, GDN_CONV_CH), F32),
                        pltpu.VMEM((rows, GDN_CONV_CH), F32),
                        pltpu.VMEM((rows, 2 * LANES), F32),
                        pltpu.VMEM((GDN_HEADS, GDN_DK, GDN_DV), F32)],
        compiler_params=_params(("parallel", "arbitrary")),
        name="gdn",
    )(proj, proj, zg_src, cst8, cw, s0, par, gn, sela, selb)


def _ssd_body(x_ref, sm_ref, z_ref, cst_ref, cw_ref, cb_ref, h0_ref, par_ref, dch_ref, nw_ref,
              sel_ref, y_ref, hout_ref, xbuf, act, gates, h_scr, *, tlv, nch):
    tile = pl.program_id(1)
    ntile = pl.num_programs(1)
    first = tile == 0

    if tlv < CHUNK:
        act[...] = jnp.zeros_like(act)
    _conv_silu(x_ref, cst_ref, cw_ref, cb_ref, xbuf, act, tlv, first)

    @pl.when(first)
    def _():
        for p in range(SSM_PAIRS):
            h_scr[p] = h0_ref[0, p].T

    rows_g = gates.shape[0]
    sm = sm_ref[...]
    lane_g = lax.broadcasted_iota(jnp.int32, sm.shape, 1)
    is_dt = (lane_g >= SM_DT) & (lane_g < SM_DT + SSM_HEADS)
    dt = jnp.where(is_dt, _softplus(sm + par_ref[1:2, :]), 0.0)
    log_decay = dt * jnp.where(is_dt, -jnp.exp(par_ref[0:1, :]), 0.0)
    if tlv < rows_g:
        gates[...] = jnp.zeros_like(gates)
    gates[0:tlv, 0:LANES] = log_decay
    gates[0:tlv, LANES:2 * LANES] = dt

    row, lane, col = _pair_masks()
    left = lane < CHUNK
    incl = row >= col
    tril = jnp.where(lax.broadcasted_iota(jnp.int32, (CHUNK, CHUNK), 0)
                     >= lax.broadcasted_iota(jnp.int32, (CHUNK, CHUNK), 1), 1.0, 0.0).astype(BF16)
    ones = jnp.ones((CHUNK, LANES), BF16)
    nrows = min(CHUNK, tlv)

    def chunk_step(c, carry):
        r0 = pl.multiple_of(c * CHUNK, CHUNK)
        g_c = gates[pl.ds(r0, CHUNK), 0:LANES]
        dt_c = gates[pl.ds(r0, CHUNK), LANES:2 * LANES]
        gc = _sel_mm(tril, g_c)

        for g in range(SSM_GROUPS):
            b_g = act[pl.ds(r0, CHUNK), SSM_DINNER + g * SSM_DSTATE:SSM_DINNER + (g + 1) * SSM_DSTATE]
            c_off = SSM_DINNER + SSM_GROUPS * SSM_DSTATE
            c_g = act[pl.ds(r0, CHUNK), c_off + g * SSM_DSTATE:c_off + (g + 1) * SSM_DSTATE]
            b_bf = b_g.astype(BF16)
            c_bf = c_g.astype(BF16)
            cb2 = _mm_nt(c_bf, jnp.concatenate([b_bf, b_bf], axis=0))

            ys = []
            ssq = None
            for pp in range(SSM_PAIRS // SSM_GROUPS):
                p = g * (SSM_PAIRS // SSM_GROUPS) + pp
                cols = slice(p * LANES, (p + 1) * LANES)
                sel = sel_ref[:, cols]
                gc_p = _mm_sel(gc, sel)
                dt_p = _mm_sel(dt_c, sel)
                x_p = act[pl.ds(r0, CHUNK), cols]
                xdt = x_p * dt_p
                gl_p = jnp.broadcast_to(gc_p[CHUNK - 1:CHUNK, :], gc_p.shape)
                xw = xdt * jnp.exp(gl_p - gc_p)

                yrow = jnp.concatenate([_row_values(gc_p, lane, 0),
                                        _row_values(gc_p, lane, CHUNK)], axis=0)
                r_pair = _mm_nt(ones, yrow)
                decay = jnp.where(incl, jnp.exp(jnp.where(incl, gc_p - r_pair, 0.0)), 0.0)
                m_pair = (cb2 * decay).astype(BF16)
                y_p = _mm(m_pair, _block_diag(xdt, left).astype(BF16))

                h_t = h_scr[p]
                y_p = y_p + _mm(c_bf, h_t.astype(BF16)) * jnp.exp(gc_p)
                h_scr[p] = h_t * jnp.exp(gc_p[CHUNK - 1:CHUNK, :]) + _mm_tn(b_bf, xw.astype(BF16))

                y_p = y_p + dch_ref[:, cols] * x_p
                z = z_ref[pl.ds(r0, nrows), cols]
                y_p = y_p[0:nrows] * _silu(z)
                ys.append(y_p)
                sq = jnp.sum(y_p * y_p, axis=-1, keepdims=True)
                ssq = sq if ssq is None else ssq + sq

            inv = lax.rsqrt(ssq * (1.0 / SSM_GROUP_CH) + EPS)
            for pp, y_p in enumerate(ys):
                p = g * (SSM_PAIRS // SSM_GROUPS) + pp
                cols = slice(p * LANES, (p + 1) * LANES)
                y_ref[pl.ds(r0, nrows), cols] = (y_p * inv * nw_ref[:, cols]).astype(BF16)
        return carry

    lax.fori_loop(0, nch, chunk_step, 0)

    @pl.when(tile == ntile - 1)
    def _():
        for p in range(SSM_PAIRS):
            hout_ref[0, p] = h_scr[p].T


def _ssd(proj, zs_src, cst8, cw, cb, h0, par, dch, nw, sel, bsz, seq):
    tlv = min(SEQ_TL, seq)
    nt = seq // tlv
    nch = max(1, tlv // CHUNK)
    rows = nch * CHUNK
    xbc_blk = COL_XBC // SSM_CONV_CH
    zs_blk = COL_ZS // SSM_DINNER
    sm_blk = COL_SMALL // LANES
    tok = lambda blk: (lambda b, i: (b * nt + i, blk))
    st_shape = (1, SSM_PAIRS, 2 * SSM_HEADDIM, SSM_DSTATE)
    return pl.pallas_call(
        functools.partial(_ssd_body, tlv=tlv, nch=nch),
        grid=(bsz, nt),
        in_specs=[pl.BlockSpec((tlv, SSM_CONV_CH), tok(xbc_blk)),
                  pl.BlockSpec((tlv, LANES), tok(sm_blk)),
                  pl.BlockSpec((tlv, SSM_DINNER), tok(zs_blk)),
                  pl.BlockSpec((1, CONV_PAD, SSM_CONV_CH), lambda b, i: (b, 0, 0)),
                  _const_spec(cw.shape), _const_spec(cb.shape),
                  pl.BlockSpec(st_shape, lambda b, i: (b, 0, 0, 0)),
                  _const_spec(par.shape), _const_spec(dch.shape), _const_spec(nw.shape),
                  _const_spec(sel.shape)],
        out_specs=[pl.BlockSpec((tlv, SSM_DINNER), lambda b, i: (b * nt + i, 0)),
                   pl.BlockSpec(st_shape, lambda b, i: (b, 0, 0, 0))],
        out_shape=[jax.ShapeDtypeStruct((bsz * seq, SSM_DINNER), BF16),
                   jax.ShapeDtypeStruct((bsz,) + st_shape[1:], F32)],
        scratch_shapes=[pltpu.VMEM((tlv + CONV_PAD, SSM_CONV_CH), F32),
                        pltpu.VMEM((rows, SSM_CONV_CH), F32),
                        pltpu.VMEM((rows, 2 * LANES), F32),
                        pltpu.VMEM((SSM_PAIRS, SSM_DSTATE, 2 * SSM_HEADDIM), F32)],
        compiler_params=_params(("parallel", "arbitrary")),
        name="ssd",
    )(proj, proj, zs_src, cst8, cw, cb, h0, par, dch, nw, sel)


def _pad_lanes(v, offset):
    out = jnp.zeros((LANES,), F32)
    return out.at[offset:offset + v.shape[0]].set(v.astype(F32))


def _selection(n_rows, offset, n_heads, width):
    sel = np.zeros((n_rows, n_heads * width), np.float32)
    for h in range(n_heads):
        sel[offset + h, h * width:(h + 1) * width] = 1.0
    return jnp.asarray(sel, BF16)


def _pad_conv_state(st):
    bsz, _, ch = st.shape
    return jnp.concatenate([jnp.zeros((bsz, CONV_PAD - (CONV_W - 1), ch), F32), st.astype(F32)], axis=1)


def _layer(x, mem_k, mem_v, s_gdn, c_gdn, s_ssm, c_ssm, w):
    bsz, seq, _ = x.shape
    t = bsz * seq
    xf = x.reshape(t, D_MODEL)
    h, u = _ffn1(xf, w["norm_ff1"], w["w_ff1_in"], w["w_ff1_out"], w["norm_mix"])
    proj = _in_proj(u, w["w_in"])

    yg, s_gdn_new = _gdn(proj, proj, _pad_conv_state(c_gdn), w["gdn_conv_w"], s_gdn.astype(F32),
                         w["gdn_par"], w["gdn_norm"], w["sel_a"], w["sel_b"], bsz, seq)
    h0 = s_ssm.astype(F32).reshape(bsz, SSM_PAIRS, 2 * SSM_HEADDIM, SSM_DSTATE)
    ys, s_ssm_new = _ssd(proj, proj, _pad_conv_state(c_ssm), w["ssm_conv_w"], w["ssm_conv_b"], h0,
                         w["ssm_par"], w["ssm_d_ch"], w["ssm_norm"], w["sel_s"], bsz, seq)
    ym = _memattn(proj, mem_k.reshape(bsz, N_MEM, MEM_WIDTH), mem_v.reshape(bsz, N_MEM, MEM_WIDTH),
                  bsz, seq)
    y = _merge_ffn2(h, yg, ys, ym, proj, w["w_branch"], w["w_out"], w["norm_ff2"],
                    w["w_ff2_in"], w["w_ff2_out"], w["norm_final"])

    proj3 = proj.reshape(bsz, seq, PROJ_COLS)
    c_gdn_new = proj3[:, seq - (CONV_W - 1):, COL_QKV:COL_QKV + GDN_CONV_CH]
    c_ssm_new = proj3[:, seq - (CONV_W - 1):, COL_XBC:COL_XBC + SSM_CONV_CH]
    return (y.reshape(bsz, seq, D_MODEL), s_gdn_new, c_gdn_new,
            s_ssm_new.reshape(bsz, SSM_HEADS, SSM_HEADDIM, SSM_DSTATE), c_ssm_new)


def _row(v):
    return v.astype(F32).reshape(1, -1)


def _prep_weights(l, norm_ff1, w_ff1_in, w_ff1_out, norm_mix, w_in, gdn_conv_w, gdn_a_log,
                  gdn_dt_bias, gdn_norm, ssm_conv_w, ssm_conv_b, ssm_a_log, ssm_dt_bias, ssm_d,
                  ssm_norm, w_branch, w_out, norm_ff2, w_ff2_in, w_ff2_out, norm_final):
    row = _row

    wi = w_in[l]
    o = np.cumsum([0, GDN_CONV_CH, GDN_HEADS, GDN_HEADS, GDN_V, SSM_DINNER, SSM_CONV_CH, SSM_HEADS,
                   MEM_WIDTH, 3 * D_MODEL]).tolist()
    sec = [wi[:, o[i]:o[i + 1]] for i in range(9)]
    qkv_w, a_w, b_w, zg_w, zs_w, xbc_w, dt_w, qm_w, gate_w = sec
    small_w = jnp.concatenate(
        [a_w, b_w, dt_w, jnp.zeros((D_MODEL, LANES - 2 * GDN_HEADS - SSM_HEADS), wi.dtype)], axis=1)
    w_in_packed = jnp.concatenate([qkv_w, xbc_w, gate_w, zg_w, zs_w, qm_w, small_w], axis=1).astype(BF16)

    return {
        "norm_ff1": row(norm_ff1[l]), "w_ff1_in": w_ff1_in[l].astype(BF16),
        "w_ff1_out": w_ff1_out[l].astype(BF16), "norm_mix": row(norm_mix[l]),
        "w_in": w_in_packed,
        "gdn_conv_w": gdn_conv_w[l].astype(F32),
        "gdn_par": jnp.zeros((SUBLANES, LANES), F32).at[0].set(_pad_lanes(gdn_a_log[l], SM_A))
                      .at[1].set(_pad_lanes(gdn_dt_bias[l], SM_A)),
        "gdn_norm": row(gdn_norm[l]),
        "sel_a": _selection(LANES, SM_A, GDN_HEADS, LANES),
        "sel_b": _selection(LANES, SM_B, GDN_HEADS, LANES),
        "ssm_conv_w": ssm_conv_w[l].astype(F32), "ssm_conv_b": row(ssm_conv_b[l]),
        "ssm_par": jnp.zeros((SUBLANES, LANES), F32).at[0].set(_pad_lanes(ssm_a_log[l], SM_DT))
                      .at[1].set(_pad_lanes(ssm_dt_bias[l], SM_DT)),
        "ssm_d_ch": jnp.repeat(ssm_d[l].astype(F32), SSM_HEADDIM).reshape(1, -1),
        "ssm_norm": row(ssm_norm[l]),
        "sel_s": _selection(LANES, SM_DT, SSM_HEADS, SSM_HEADDIM),
        "w_branch": w_branch[l].astype(BF16), "w_out": w_out[l].astype(BF16),
        "norm_ff2": row(norm_ff2[l]), "w_ff2_in": w_ff2_in[l].astype(BF16),
        "w_ff2_out": w_ff2_out[l].astype(BF16), "norm_final": row(norm_final),
    }


def kernel(x_prompt, x_sample, mem_prompt, state_gdn, state_gdn_conv, state_ssm, state_ssm_conv,
           cache_mem_k, cache_mem_v, norm_ff1, w_ff1_in, w_ff1_out, norm_mix, w_in,
           gdn_conv_w, gdn_a_log, gdn_dt_bias, gdn_norm, ssm_conv_w, ssm_conv_b, ssm_a_log,
           ssm_dt_bias, ssm_d, ssm_norm, norm_mem, w_mem_kv, w_branch, w_out,
           norm_ff2, w_ff2_in, w_ff2_out, norm_final):
    assert w_in.shape[0] == 1, "the kernels implement the single-layer configuration"
    l = 0
    bp = x_prompt.shape[0]
    w = _prep_weights(l, norm_ff1, w_ff1_in, w_ff1_out, norm_mix, w_in, gdn_conv_w, gdn_a_log,
                      gdn_dt_bias, gdn_norm, ssm_conv_w, ssm_conv_b, ssm_a_log, ssm_dt_bias, ssm_d,
                      ssm_norm, w_branch, w_out, norm_ff2, w_ff2_in, w_ff2_out, norm_final)

    n_mem_tok = mem_prompt.shape[0] * mem_prompt.shape[1]
    kv = _memkv(mem_prompt.reshape(n_mem_tok, D_MODEL), _row(norm_mem[l]), w_mem_kv[l].astype(BF16))
    mk = kv[:, :MEM_WIDTH].reshape(bp, N_MEM, MEM_HEADS, MEM_HEAD_DIM)
    mv = kv[:, MEM_WIDTH:].reshape(bp, N_MEM, MEM_HEADS, MEM_HEAD_DIM)
    dtp = x_prompt.dtype
    yp, sgp, cgp, ssp, csp = _layer(
        x_prompt, mk, mv,
        jnp.zeros((bp, GDN_HEADS, GDN_DK, GDN_DV), dtp), jnp.zeros((bp, CONV_W - 1, GDN_CONV_CH), dtp),
        jnp.zeros((bp, SSM_HEADS, SSM_HEADDIM, SSM_DSTATE), dtp),
        jnp.zeros((bp, CONV_W - 1, SSM_CONV_CH), dtp), w)
    ys_, sgs, cgs, sss, css = _layer(
        x_sample, cache_mem_k[l], cache_mem_v[l], state_gdn[l], state_gdn_conv[l],
        state_ssm[l], state_ssm_conv[l], w)

    lead = lambda a: a[None]
    return (yp, ys_, lead(sgp), lead(cgp), lead(ssp), lead(csp), lead(mk), lead(mv),
            lead(sgs), lead(cgs), lead(sss), lead(css))
```

```python
import collections
import functools

import numpy as np
import jax
import jax.numpy as jnp
from jax import lax
from jax.experimental import pallas as pl
from jax.experimental.pallas import tpu as pltpu

F32 = jnp.float32
BF16 = jnp.bfloat16
EPS = 1e-6

D_MODEL = 1024
FFN_DIM = 2816
CONV_W = 4
CHUNK = 64
LANES = 128
SUBLANES = 8

GDN_HEADS = 8
GDN_PAIRS = GDN_HEADS // 2
GDN_DK = 128
GDN_DV = 128
GDN_QK = GDN_HEADS * GDN_DK
GDN_V = GDN_HEADS * GDN_DV
GDN_CONV_CH = 2 * GDN_QK + GDN_V

SSM_DINNER = 2048
SSM_HEADDIM = 64
SSM_HEADS = 32
SSM_GROUPS = 4
SSM_DSTATE = 128
SSM_CONV_CH = SSM_DINNER + 2 * SSM_GROUPS * SSM_DSTATE
SSM_PAIRS = SSM_HEADS // 2
SSM_GROUP_PAIRS = SSM_PAIRS // SSM_GROUPS
SSM_GROUP_CH = SSM_DINNER // SSM_GROUPS

N_MEM = 256
MEM_HEADS = 4
MEM_HEAD_DIM = 64
MEM_WIDTH = MEM_HEADS * MEM_HEAD_DIM

COL_QKV = 0
COL_XBC = COL_QKV + GDN_CONV_CH
COL_GATE = COL_XBC + SSM_CONV_CH
COL_ZG = COL_GATE + 3 * D_MODEL
COL_ZS = COL_ZG + GDN_V
COL_QM = COL_ZS + SSM_DINNER
COL_SMALL = COL_QM + MEM_WIDTH
PROJ_COLS = COL_SMALL + LANES
SM_A = 0
SM_B = GDN_HEADS
SM_DT = 2 * GDN_HEADS

VMEM_LIMIT = 56 * 1024 * 1024


def _params(semantics):
    return pltpu.CompilerParams(dimension_semantics=semantics, vmem_limit_bytes=VMEM_LIMIT)


def _mm(a, b):
    return jnp.dot(a, b, preferred_element_type=F32)


def _mm_nt(a, b):
    return lax.dot_general(a, b, (((1,), (1,)), ((), ())), preferred_element_type=F32)


def _mm_tn(a, b):
    return lax.dot_general(a, b, (((0,), (0,)), ((), ())), preferred_element_type=F32)


def _split3(x):
    hi = x.astype(BF16)
    r1 = x - hi.astype(F32)
    mid = r1.astype(BF16)
    lo = (r1 - mid.astype(F32)).astype(BF16)
    return hi, mid, lo


def _split2_lanes(x):
    hi = x.astype(BF16)
    lo = (x - hi.astype(F32)).astype(BF16)
    return jnp.concatenate([hi, lo], axis=1)


def _rms(x, g):
    return x * lax.rsqrt(jnp.mean(x * x, axis=-1, keepdims=True) + EPS) * g


def _sigmoid(x):
    return 1.0 / (1.0 + jnp.exp(-x))


def _silu(x):
    return x * _sigmoid(x)


def _softplus(x):
    return jnp.maximum(x, 0.0) + jnp.log1p(jnp.exp(-jnp.abs(x)))


def _const_spec(shape):
    nd = len(shape)
    return pl.BlockSpec(shape, lambda *_: (0,) * nd, pipeline_mode=pl.Buffered(1))


FFN_TM = 512
FFN_TF = FFN_DIM // 2


def _ffn_compute(x, g_ref, wi_ref, wo_ref):
    xn = _rms(x, g_ref[...]).astype(BF16)
    acc = None
    for c in range(FFN_DIM // FFN_TF):
        lo = c * FFN_TF
        gate = _mm(xn, wi_ref[:, lo:lo + FFN_TF])
        up = _mm(xn, wi_ref[:, FFN_DIM + lo:FFN_DIM + lo + FFN_TF])
        act = (_silu(gate) * up).astype(BF16)
        part = _mm(act, wo_ref[lo:lo + FFN_TF, :])
        acc = part if acc is None else acc + part
    return x + 0.5 * acc


def _ffn1_body(x_ref, g_ref, wi_ref, wo_ref, g2_ref, h_ref, u_ref):
    h = _ffn_compute(x_ref[...], g_ref, wi_ref, wo_ref)
    h_ref[...] = h
    u_ref[...] = _rms(h, g2_ref[...]).astype(BF16)


def _ffn1(x, g, wi, wo, g2):
    t = x.shape[0]
    tm = min(FFN_TM, t)
    row = lambda i: (i, 0)
    return pl.pallas_call(
        _ffn1_body,
        grid=(t // tm,),
        in_specs=[pl.BlockSpec((tm, D_MODEL), row), _const_spec(g.shape), _const_spec(wi.shape),
                  _const_spec(wo.shape), _const_spec(g2.shape)],
        out_specs=[pl.BlockSpec((tm, D_MODEL), row), pl.BlockSpec((tm, D_MODEL), row)],
        out_shape=[jax.ShapeDtypeStruct((t, D_MODEL), F32), jax.ShapeDtypeStruct((t, D_MODEL), BF16)],
        compiler_params=_params(("parallel",)),
        name="ffn1",
    )(x, g, wi, wo, g2)


PROJ_TM = 1024
PROJ_TN = PROJ_COLS // 9


def _proj_body(u_ref, w_ref, o_ref):
    o_ref[...] = _mm(u_ref[...], w_ref[...])


def _in_proj(u, w):
    t = u.shape[0]
    tm = min(PROJ_TM, t)
    return pl.pallas_call(
        _proj_body,
        grid=(PROJ_COLS // PROJ_TN, t // tm),
        in_specs=[pl.BlockSpec((tm, D_MODEL), lambda j, i: (i, 0)),
                  pl.BlockSpec((D_MODEL, PROJ_TN), lambda j, i: (0, j))],
        out_specs=pl.BlockSpec((tm, PROJ_TN), lambda j, i: (i, j)),
        out_shape=jax.ShapeDtypeStruct((t, PROJ_COLS), F32),
        compiler_params=_params(("parallel", "parallel")),
        name="in_proj",
    )(u, w)


def _memkv_body(x_ref, g_ref, w_ref, o_ref):
    o_ref[...] = _mm(_rms(x_ref[...], g_ref[...]).astype(BF16), w_ref[...])


def _memkv(mem, g, w):
    t = mem.shape[0]
    tm = min(512, t)
    row = lambda i: (i, 0)
    return pl.pallas_call(
        _memkv_body,
        grid=(t // tm,),
        in_specs=[pl.BlockSpec((tm, D_MODEL), row), _const_spec(g.shape), _const_spec(w.shape)],
        out_specs=pl.BlockSpec((tm, 2 * MEM_WIDTH), row),
        out_shape=jax.ShapeDtypeStruct((t, 2 * MEM_WIDTH), F32),
        compiler_params=_params(("parallel",)),
        name="memkv",
    )(mem, g, w)


MERGE_TM = 256


def _merge_ffn2_body(h_ref, yg_ref, ys_ref, ym_ref, gate_ref, wb_ref, wout_ref,
                     g_ref, wi_ref, wo_ref, gf_ref, y_ref):
    gates = gate_ref[...]
    merged = (_sigmoid(gates[:, 0:D_MODEL]) * _mm(yg_ref[...], wb_ref[0:GDN_V, :])
              + _sigmoid(gates[:, D_MODEL:2 * D_MODEL])
              * _mm(ys_ref[...], wb_ref[GDN_V:GDN_V + SSM_DINNER, :])
              + _sigmoid(gates[:, 2 * D_MODEL:3 * D_MODEL])
              * _mm(ym_ref[...], wb_ref[GDN_V + SSM_DINNER:, :]))
    h = h_ref[...] + _mm(merged.astype(BF16), wout_ref[...])
    h = _ffn_compute(h, g_ref, wi_ref, wo_ref)
    y_ref[...] = _rms(h, gf_ref[...])


def _merge_ffn2(h, yg, ys, ym, proj, wb, wout, g, wi, wo, gf):
    t = h.shape[0]
    tm = min(MERGE_TM, t)
    row = lambda i: (i, 0)
    gate_blk = COL_GATE // (3 * D_MODEL)
    return pl.pallas_call(
        _merge_ffn2_body,
        grid=(t // tm,),
        in_specs=[pl.BlockSpec((tm, D_MODEL), row), pl.BlockSpec((tm, GDN_V), row),
                  pl.BlockSpec((tm, SSM_DINNER), row), pl.BlockSpec((tm, MEM_WIDTH), row),
                  pl.BlockSpec((tm, 3 * D_MODEL), lambda i: (i, gate_blk)),
                  _const_spec(wb.shape), _const_spec(wout.shape), _const_spec(g.shape),
                  _const_spec(wi.shape), _const_spec(wo.shape), _const_spec(gf.shape)],
        out_specs=pl.BlockSpec((tm, D_MODEL), row),
        out_shape=jax.ShapeDtypeStruct((t, D_MODEL), F32),
        compiler_params=_params(("parallel",)),
        name="merge_ffn2",
    )(h, yg, ys, ym, proj, wb, wout, g, wi, wo, gf)


def _memattn_body(q_ref, k_ref, v_ref, y_ref):
    q = q_ref[...]
    k = k_ref[0].astype(BF16)
    v = v_ref[0].astype(BF16)
    lane = lax.broadcasted_iota(jnp.int32, q.shape, 1)
    out = jnp.zeros(q.shape, F32)
    for hh in range(MEM_HEADS):
        in_head = (lane >= hh * MEM_HEAD_DIM) & (lane < (hh + 1) * MEM_HEAD_DIM)
        s = _mm_nt(jnp.where(in_head, q, 0.0).astype(BF16), k) * (MEM_HEAD_DIM ** -0.5)
        s = s - jnp.max(s, axis=-1, keepdims=True)
        p = jnp.exp(s)
        p = p / jnp.sum(p, axis=-1, keepdims=True)
        out = out + jnp.where(in_head, _mm(p.astype(BF16), v), 0.0)
    y_ref[...] = out.astype(BF16)


def _memattn(proj, k, v, bsz, seq):
    tl = min(512, seq)
    nt = seq // tl
    qm_blk = COL_QM // MEM_WIDTH
    return pl.pallas_call(
        _memattn_body,
        grid=(bsz, nt),
        in_specs=[pl.BlockSpec((tl, MEM_WIDTH), lambda b, i: (b * nt + i, qm_blk)),
                  pl.BlockSpec((1, N_MEM, MEM_WIDTH), lambda b, i: (b, 0, 0)),
                  pl.BlockSpec((1, N_MEM, MEM_WIDTH), lambda b, i: (b, 0, 0))],
        out_specs=pl.BlockSpec((tl, MEM_WIDTH), lambda b, i: (b * nt + i, 0)),
        out_shape=jax.ShapeDtypeStruct((bsz * seq, MEM_WIDTH), BF16),
        compiler_params=_params(("parallel", "arbitrary")),
        name="memattn",
    )(proj, k, v)


SEQ_TL = 256
SHORT_NSEQ = 4
CONV_PAD = SUBLANES

Geom = collections.namedtuple("Geom", ["nseq", "nch", "vrows"])


def _geom(bsz, seq):
    if seq >= CHUNK:
        assert seq % SEQ_TL == 0
        return Geom(1, SEQ_TL // CHUNK, CHUNK), seq // SEQ_TL
    assert bsz % SHORT_NSEQ == 0 and seq % SUBLANES == 0
    return Geom(SHORT_NSEQ, 1, seq), 1


def _conv_silu(x_ref, cst_ref, cw_ref, cb_ref, xbuf, act, geom, first_tile):
    ch = x_ref.shape[1]
    tls = geom.nch * geom.vrows
    stride = tls + CONV_PAD

    @pl.when(first_tile)
    def _():
        for s in range(geom.nseq):
            xbuf[s * stride:s * stride + CONV_PAD, :] = cst_ref[s]

    @pl.when(jnp.logical_not(first_tile))
    def _():
        for s in range(geom.nseq):
            xbuf[s * stride:s * stride + CONV_PAD, :] = xbuf[s * stride + tls:(s + 1) * stride, :]

    for s in range(geom.nseq):
        xbuf[s * stride + CONV_PAD:(s + 1) * stride, :] = x_ref[s * tls:(s + 1) * tls, :]

    if geom.vrows < CHUNK:
        act[...] = jnp.zeros_like(act)

    def col_block(cb, carry):
        c0 = pl.multiple_of(cb * LANES, LANES)
        w = cw_ref[:, pl.ds(c0, LANES)]
        for s in range(geom.nseq):
            y = None
            for j in range(CONV_W):
                start = s * stride + CONV_PAD - (CONV_W - 1) + j
                term = xbuf[start:start + tls, pl.ds(c0, LANES)] * w[j:j + 1, :]
                y = term if y is None else y + term
            if cb_ref is not None:
                y = y + cb_ref[:, pl.ds(c0, LANES)]
            a0 = s * geom.nch * CHUNK
            act[a0:a0 + tls, pl.ds(c0, LANES)] = _silu(y)
        return carry

    lax.fori_loop(0, ch // LANES, col_block, 0)


def _store_gates(gates, geom, first, second):
    tls = geom.nch * geom.vrows
    if geom.vrows < CHUNK:
        gates[...] = jnp.zeros_like(gates)
    for s in range(geom.nseq):
        g0 = s * geom.nch * CHUNK
        gates[g0:g0 + tls, 0:LANES] = first[s * tls:(s + 1) * tls]
        gates[g0:g0 + tls, LANES:2 * LANES] = second[s * tls:(s + 1) * tls]


def _pair_masks():
    row = lax.broadcasted_iota(jnp.int32, (CHUNK, LANES), 0)
    lane = lax.broadcasted_iota(jnp.int32, (CHUNK, LANES), 1)
    col = jnp.where(lane < CHUNK, lane, lane - CHUNK)
    return row, lane, col


def _block_diag(x, left):
    zero = jnp.zeros_like(x)
    return jnp.concatenate([jnp.where(left, x, zero), jnp.where(left, zero, x)], axis=0)


def _chunk_cumsums(gates, nprob):
    tril = jnp.where(lax.broadcasted_iota(jnp.int32, (CHUNK, CHUNK), 0)
                     >= lax.broadcasted_iota(jnp.int32, (CHUNK, CHUNK), 1), 1.0, 0.0).astype(BF16)
    tril3 = jnp.concatenate([tril, tril, tril], axis=1)
    pieces = [jnp.concatenate(_split3(gates[pb * CHUNK:(pb + 1) * CHUNK, 0:LANES]), axis=0)
              for pb in range(nprob)]
    gc_all = _mm(tril3, jnp.concatenate(pieces, axis=1))
    return [gc_all[:, pb * LANES:(pb + 1) * LANES] for pb in range(nprob)]


def _row_form(gc):
    return jnp.concatenate([gc, pltpu.roll(gc, LANES - 1, axis=1)], axis=0).T


def _gdn_body(x_ref, sm_ref, z_ref, cst_ref, cw_ref, s0_ref, par_ref, gn_ref, selp_ref,
              y_ref, sout_ref, xbuf, act, gates, s_scr, u_scr, wq_scr, kd_scr, qk_scr, egl_scr,
              o_scr, *, geom):
    tile = pl.program_id(1)
    first = tile == 0
    nprob = geom.nseq * geom.nch
    vr = geom.vrows

    _conv_silu(x_ref, cst_ref, cw_ref, None, xbuf, act, geom, first)

    @pl.when(first)
    def _():
        s_scr[...] = s0_ref[...]

    sm = sm_ref[...]
    lane_g = lax.broadcasted_iota(jnp.int32, sm.shape, 1)
    head_lane = lane_g < GDN_HEADS
    log_decay = jnp.where(head_lane, -jnp.exp(par_ref[0:1, :]) * _softplus(sm + par_ref[1:2, :]), 0.0)
    beta = jnp.where(head_lane, pltpu.roll(_sigmoid(sm), LANES - SM_B, axis=1), 0.0)
    _store_gates(gates, geom, log_decay, beta)

    row, lane, col = _pair_masks()
    left = lane < CHUNK
    incl = row >= col
    strict = row > col
    eye = jnp.where(row == col, 1.0, 0.0).astype(F32)
    zeros_h = jnp.zeros((CHUNK, LANES), F32)
    hcols = lambda h: slice(h * LANES, (h + 1) * LANES)
    prows = lambda pb, n=CHUNK: slice(pb * n, (pb + 1) * n)

    gcs = _chunk_cumsums(gates, nprob)
    lhs, xts = [], []
    for pb in range(nprob):
        gc = gcs[pb]
        gl = jnp.broadcast_to(gc[CHUNK - 1:CHUNK, :], gc.shape)
        quantities = jnp.concatenate(
            [gc, gates[prows(pb), LANES:2 * LANES], jnp.exp(gc), jnp.exp(gl - gc)], axis=0)
        lhs.append(_split2_lanes(quantities))
        xts.append(_row_form(gc))

    items = [(pb, p) for pb in range(nprob) for p in range(GDN_PAIRS)]
    exp_ = [_mm(lhs[pb], selp_ref[p]) for pb, p in items]

    def l2n(t):
        return t * lax.rsqrt(jnp.sum(t * t, axis=-1, keepdims=True) + EPS)

    def stacked(e, qi):
        return jnp.concatenate([e[qi * CHUNK:(qi + 1) * CHUNK, 0:LANES],
                                e[qi * CHUNK:(qi + 1) * CHUNK, LANES:]], axis=0)

    decay_l, kq_l, rhs_l, qd_l, kd_l = [], [], [], [], []
    for (pb, p), e in zip(items, exp_):
        ha, hb = 2 * p, 2 * p + 1
        qa, qb = [l2n(act[prows(pb), hcols(h)]) * (GDN_DK ** -0.5) for h in (ha, hb)]
        ka, kb = [l2n(act[prows(pb), hcols(GDN_HEADS + h)]) for h in (ha, hb)]
        va, vb = [act[prows(pb), hcols(2 * GDN_HEADS + h)] for h in (ha, hb)]
        beta2, eg2, kdec2 = stacked(e, 1), stacked(e, 2), stacked(e, 3)
        k2 = jnp.concatenate([ka, kb], axis=0)
        kbeta2 = k2 * beta2
        gci = jnp.where(left, e[0:CHUNK, 0:LANES], e[0:CHUNK, LANES:])
        gcj = jnp.broadcast_to(xts[pb][ha:ha + 1, :], (CHUNK, LANES))
        decay_l.append(jnp.where(incl, jnp.exp(jnp.where(incl, gci - gcj, 0.0)), 0.0))
        k_bd = jnp.concatenate([jnp.concatenate([ka, zeros_h], axis=1),
                                jnp.concatenate([zeros_h, kb], axis=1)], axis=0).astype(BF16)
        kbq = jnp.concatenate([jnp.concatenate([kbeta2[0:CHUNK], kbeta2[CHUNK:]], axis=1),
                               jnp.concatenate([qa, qb], axis=1)], axis=0).astype(BF16)
        kq_l.append(_mm_nt(kbq, k_bd))
        rhs_l.append(jnp.concatenate(
            [jnp.concatenate([va, vb], axis=0) * beta2, kbeta2 * eg2], axis=1).astype(BF16))
        qd_l.append(jnp.concatenate([qa, qb], axis=0) * eg2)
        kd_l.append(k2 * kdec2)
        egl_scr[prows(pb, SUBLANES), hcols(ha)] = jnp.broadcast_to(
            e[3 * CHUNK - 1:3 * CHUNK, 0:LANES], (SUBLANES, LANES))
        egl_scr[prows(pb, SUBLANES), hcols(hb)] = jnp.broadcast_to(
            e[3 * CHUNK - 1:3 * CHUNK, LANES:], (SUBLANES, LANES))

    n_pow = [-jnp.where(strict, kq[0:CHUNK] * d, 0.0) for kq, d in zip(kq_l, decay_l)]
    t_inv = [eye + n for n in n_pow]
    for _ in range(int(np.log2(CHUNK)) - 1):
        n_pow = [_mm(n.astype(BF16), _block_diag(n, left).astype(BF16)) for n in n_pow]
        t_inv = [t + _mm(t.astype(BF16), _block_diag(n, left).astype(BF16))
                 for t, n in zip(t_inv, n_pow)]

    uw_l = [_mm(_block_diag(t, left).astype(BF16), rhs) for t, rhs in zip(t_inv, rhs_l)]
    for (pb, p), uw, kq, d, qd, kd in zip(items, uw_l, kq_l, decay_l, qd_l, kd_l):
        qk_scr[prows(pb, 2 * CHUNK), hcols(p)] = _block_diag(kq[CHUNK:] * d, left).astype(BF16)
        for idx, h in enumerate((2 * p, 2 * p + 1)):
            sl = slice(idx * CHUNK, (idx + 1) * CHUNK)
            u_scr[prows(pb), hcols(h)] = uw[sl, 0:LANES]
            wq_scr[prows(pb, 2 * CHUNK), hcols(h)] = jnp.concatenate(
                [uw[sl, LANES:], qd[sl]], axis=0).astype(BF16)
            kd_scr[prows(pb), hcols(h)] = kd[sl].astype(BF16)

    for c in range(geom.nch):
        sh = [(s, h) for s in range(geom.nseq) for h in range(GDN_HEADS)]
        pbs = lambda s: s * geom.nch + c
        wqs = [_mm(wq_scr[prows(pbs(s), 2 * CHUNK), hcols(h)], s_scr[s, h].astype(BF16))
               for s, h in sh]
        vnew = [u_scr[prows(pbs(s)), hcols(h)] - w[0:CHUNK] for (s, h), w in zip(sh, wqs)]
        for (s, h), vn in zip(sh, vnew):
            egl = egl_scr[pbs(s) * SUBLANES:pbs(s) * SUBLANES + 1, hcols(h)]
            s_scr[s, h] = s_scr[s, h] * egl + _mm_tn(kd_scr[prows(pbs(s)), hcols(h)], vn.astype(BF16))
        for s in range(geom.nseq):
            for p in range(GDN_PAIRS):
                i0 = s * GDN_HEADS + 2 * p
                vn2 = jnp.concatenate([vnew[i0], vnew[i0 + 1]], axis=0).astype(BF16)
                intra = _mm(qk_scr[prows(pbs(s), 2 * CHUNK), hcols(p)], vn2)
                for idx in range(2):
                    o = wqs[i0 + idx][CHUNK:] + intra[idx * CHUNK:(idx + 1) * CHUNK]
                    o_scr[prows(pbs(s), vr), hcols(2 * p + idx)] = o[0:vr]

    for h in range(GDN_HEADS):
        y = _rms(o_scr[:, hcols(h)], gn_ref[...]) * _silu(z_ref[:, hcols(h)])
        y_ref[:, hcols(h)] = y.astype(BF16)

    @pl.when(tile == pl.num_programs(1) - 1)
    def _():
        sout_ref[...] = s_scr[...]


def _gdn(proj, cst8, cw, s0, par, gn, selp, bsz, seq):
    geom, nt = _geom(bsz, seq)
    nprob = geom.nseq * geom.nch
    tl = nprob * geom.vrows
    tls = geom.nch * geom.vrows
    tok = lambda blk: (lambda b, i: (b * nt + i, blk))
    st_spec = pl.BlockSpec((geom.nseq, GDN_HEADS, GDN_DK, GDN_DV), lambda b, i: (b, 0, 0, 0))
    return pl.pallas_call(
        functools.partial(_gdn_body, geom=geom),
        grid=(bsz // geom.nseq, nt),
        in_specs=[pl.BlockSpec((tl, GDN_CONV_CH), tok(COL_QKV // GDN_CONV_CH)),
                  pl.BlockSpec((tl, LANES), tok(COL_SMALL // LANES)),
                  pl.BlockSpec((tl, GDN_V), tok(COL_ZG // GDN_V)),
                  pl.BlockSpec((geom.nseq, CONV_PAD, GDN_CONV_CH), lambda b, i: (b, 0, 0)),
                  _const_spec(cw.shape), st_spec,
                  _const_spec(par.shape), _const_spec(gn.shape), _const_spec(selp.shape)],
        out_specs=[pl.BlockSpec((tl, GDN_V), lambda b, i: (b * nt + i, 0)), st_spec],
        out_shape=[jax.ShapeDtypeStruct((bsz * seq, GDN_V), BF16),
                   jax.ShapeDtypeStruct((bsz, GDN_HEADS, GDN_DK, GDN_DV), F32)],
        scratch_shapes=[pltpu.VMEM((geom.nseq * (tls + CONV_PAD), GDN_CONV_CH), F32),
                        pltpu.VMEM((nprob * CHUNK, GDN_CONV_CH), F32),
                        pltpu.VMEM((nprob * CHUNK, 2 * LANES), F32),
                        pltpu.VMEM((geom.nseq, GDN_HEADS, GDN_DK, GDN_DV), F32),
                        pltpu.VMEM((nprob * CHUNK, GDN_V), F32),
                        pltpu.VMEM((nprob * 2 * CHUNK, GDN_V), BF16),
                        pltpu.VMEM((nprob * CHUNK, GDN_V), BF16),
                        pltpu.VMEM((nprob * 2 * CHUNK, GDN_PAIRS * LANES), BF16),
                        pltpu.VMEM((nprob * SUBLANES, GDN_V), F32),
                        pltpu.VMEM((tl, GDN_V), F32)],
        compiler_params=_params(("parallel", "arbitrary")),
        name="gdn",
    )(proj, proj, proj, cst8, cw, s0, par, gn, selp)


def _ssd_body(x_ref, sm_ref, z_ref, cst_ref, cw_ref, cb_ref, h0_ref, par_ref, dch_ref, nw_ref,
              seld_ref, y_ref, hout_ref, xbuf, act, gates, h_scr, o_scr, *, geom):
    tile = pl.program_id(1)
    first = tile == 0
    nprob = geom.nseq * geom.nch
    vr = geom.vrows

    _conv_silu(x_ref, cst_ref, cw_ref, cb_ref, xbuf, act, geom, first)

    @pl.when(first)
    def _():
        for s in range(geom.nseq):
            for p in range(SSM_PAIRS):
                h_scr[s, p] = h0_ref[s, p].T

    sm = sm_ref[...]
    lane_g = lax.broadcasted_iota(jnp.int32, sm.shape, 1)
    is_dt = (lane_g >= SM_DT) & (lane_g < SM_DT + SSM_HEADS)
    dt = jnp.where(is_dt, _softplus(sm + par_ref[1:2, :]), 0.0)
    log_decay = dt * jnp.where(is_dt, -jnp.exp(par_ref[0:1, :]), 0.0)
    _store_gates(gates, geom, log_decay, dt)

    row, lane, col = _pair_masks()
    left = lane < CHUNK
    incl = row >= col
    pcols = lambda p: slice(p * LANES, (p + 1) * LANES)
    prows = lambda pb, n=CHUNK: slice(pb * n, (pb + 1) * n)
    quarter = lambda e, qi, j: e[qi * CHUNK:(qi + 1) * CHUNK, j * LANES:(j + 1) * LANES]
    b_off = SSM_DINNER
    c_off = SSM_DINNER + SSM_GROUPS * SSM_DSTATE

    gcs = _chunk_cumsums(gates, nprob)
    for pb in range(nprob):
        s = pb // geom.nch
        gc = gcs[pb]
        dt_c = gates[prows(pb), LANES:2 * LANES]
        gl = jnp.broadcast_to(gc[CHUNK - 1:CHUNK, :], gc.shape)
        lhs = _split2_lanes(jnp.concatenate([gc, dt_c, jnp.exp(gc), jnp.exp(gl - gc) * dt_c], axis=0))
        xt = _row_form(gc)
        exp_ = [_mm(lhs, seld_ref[d]) for d in range(SSM_PAIRS // 2)]

        b_bf = [act[prows(pb), b_off + g * SSM_DSTATE:b_off + (g + 1) * SSM_DSTATE].astype(BF16)
                for g in range(SSM_GROUPS)]
        c_bf = [act[prows(pb), c_off + g * SSM_DSTATE:c_off + (g + 1) * SSM_DSTATE].astype(BF16)
                for g in range(SSM_GROUPS)]
        cb2 = [_mm_nt(c, jnp.concatenate([b, b], axis=0)) for b, c in zip(b_bf, c_bf)]

        pairs = range(SSM_PAIRS)
        grp = lambda p: p // SSM_GROUP_PAIRS
        x_l = [act[prows(pb), pcols(p)] for p in pairs]
        gc_l = [quarter(exp_[p // 2], 0, p % 2) for p in pairs]
        eg_l = [quarter(exp_[p // 2], 2, p % 2) for p in pairs]
        xdt_l = [x * quarter(exp_[p // 2], 1, p % 2) for p, x in zip(pairs, x_l)]
        xw_l = [x * quarter(exp_[p // 2], 3, p % 2) for p, x in zip(pairs, x_l)]
        m_l = []
        for p in pairs:
            r = SM_DT + 2 * p
            gcj = jnp.broadcast_to(xt[r:r + 1, :], (CHUNK, LANES))
            decay = jnp.where(incl, jnp.exp(jnp.where(incl, gc_l[p] - gcj, 0.0)), 0.0)
            m_l.append((cb2[grp(p)] * decay).astype(BF16))
        intra = [_mm(m, _block_diag(xdt, left).astype(BF16)) for m, xdt in zip(m_l, xdt_l)]
        dstate = [_mm_tn(b_bf[grp(p)], xw_l[p].astype(BF16)) for p in pairs]
        inter = [_mm(c_bf[grp(p)], h_scr[s, p].astype(BF16)) for p in pairs]
        for p in pairs:
            h_scr[s, p] = h_scr[s, p] * eg_l[p][CHUNK - 1:CHUNK, :] + dstate[p]

        for g in range(SSM_GROUPS):
            ys = []
            ssq = None
            for p in range(g * SSM_GROUP_PAIRS, (g + 1) * SSM_GROUP_PAIRS):
                y = intra[p] + inter[p] * eg_l[p] + dch_ref[:, pcols(p)] * x_l[p]
                y = y[0:vr] * _silu(z_ref[prows(pb, vr), pcols(p)])
                ys.append(y)
                sq = jnp.sum(y * y, axis=-1, keepdims=True)
                ssq = sq if ssq is None else ssq + sq
            inv = lax.rsqrt(ssq * (1.0 / SSM_GROUP_CH) + EPS)
            for pp, y in enumerate(ys):
                o_scr[prows(pb, vr), pcols(g * SSM_GROUP_PAIRS + pp)] = y * inv

    y_ref[...] = (o_scr[...] * nw_ref[...]).astype(BF16)

    @pl.when(tile == pl.num_programs(1) - 1)
    def _():
        for s in range(geom.nseq):
            for p in range(SSM_PAIRS):
                hout_ref[s, p] = h_scr[s, p].T


def _ssd(proj, cst8, cw, cb, h0, par, dch, nw, seld, bsz, seq):
    geom, nt = _geom(bsz, seq)
    nprob = geom.nseq * geom.nch
    tl = nprob * geom.vrows
    tls = geom.nch * geom.vrows
    tok = lambda blk: (lambda b, i: (b * nt + i, blk))
    st_shape = (geom.nseq, SSM_PAIRS, 2 * SSM_HEADDIM, SSM_DSTATE)
    st_spec = pl.BlockSpec(st_shape, lambda b, i: (b, 0, 0, 0))
    return pl.pallas_call(
        functools.partial(_ssd_body, geom=geom),
        grid=(bsz // geom.nseq, nt),
        in_specs=[pl.BlockSpec((tl, SSM_CONV_CH), tok(COL_XBC // SSM_CONV_CH)),
                  pl.BlockSpec((tl, LANES), tok(COL_SMALL // LANES)),
                  pl.BlockSpec((tl, SSM_DINNER), tok(COL_ZS // SSM_DINNER)),
                  pl.BlockSpec((geom.nseq, CONV_PAD, SSM_CONV_CH), lambda b, i: (b, 0, 0)),
                  _const_spec(cw.shape), _const_spec(cb.shape), st_spec,
                  _const_spec(par.shape), _const_spec(dch.shape), _const_spec(nw.shape),
                  _const_spec(seld.shape)],
        out_specs=[pl.BlockSpec((tl, SSM_DINNER), lambda b, i: (b * nt + i, 0)), st_spec],
        out_shape=[jax.ShapeDtypeStruct((bsz * seq, SSM_DINNER), BF16),
                   jax.ShapeDtypeStruct((bsz,) + st_shape[1:], F32)],
        scratch_shapes=[pltpu.VMEM((geom.nseq * (tls + CONV_PAD), SSM_CONV_CH), F32),
                        pltpu.VMEM((nprob * CHUNK, SSM_CONV_CH), F32),
                        pltpu.VMEM((nprob * CHUNK, 2 * LANES), F32),
                        pltpu.VMEM((geom.nseq, SSM_PAIRS, SSM_DSTATE, 2 * SSM_HEADDIM), F32),
                        pltpu.VMEM((tl, SSM_DINNER), F32)],
        compiler_params=_params(("parallel", "arbitrary")),
        name="ssd",
    )(proj, proj, proj, cst8, cw, cb, h0, par, dch, nw, seld)


def _pad_lanes(v, offset):
    out = jnp.zeros((LANES,), F32)
    return out.at[offset:offset + v.shape[0]].set(v.astype(F32))


def _pair_selection(first_lane, lanes_per_head, n_mats):
    heads_per_mat = 2 * LANES // lanes_per_head
    sel = np.zeros((n_mats, 2 * LANES, 2 * LANES), np.float32)
    for m in range(n_mats):
        for j in range(heads_per_mat):
            src = first_lane + m * heads_per_mat + j
            sel[m, src, j * lanes_per_head:(j + 1) * lanes_per_head] = 1.0
            sel[m, LANES + src, j * lanes_per_head:(j + 1) * lanes_per_head] = 1.0
    return jnp.asarray(sel, BF16)


def _pad_conv_state(st):
    bsz, _, ch = st.shape
    return jnp.concatenate([jnp.zeros((bsz, CONV_PAD - (CONV_W - 1), ch), F32), st.astype(F32)], axis=1)


def _layer(x, mem_k, mem_v, s_gdn, c_gdn, s_ssm, c_ssm, w):
    bsz, seq, _ = x.shape
    t = bsz * seq
    xf = x.reshape(t, D_MODEL)
    h, u = _ffn1(xf, w["norm_ff1"], w["w_ff1_in"], w["w_ff1_out"], w["norm_mix"])
    proj = _in_proj(u, w["w_in"])

    yg, s_gdn_new = _gdn(proj, _pad_conv_state(c_gdn), w["gdn_conv_w"], s_gdn.astype(F32),
                         w["gdn_par"], w["gdn_norm"], w["sel_gdn"], bsz, seq)
    h0 = s_ssm.astype(F32).reshape(bsz, SSM_PAIRS, 2 * SSM_HEADDIM, SSM_DSTATE)
    ys, s_ssm_new = _ssd(proj, _pad_conv_state(c_ssm), w["ssm_conv_w"], w["ssm_conv_b"], h0,
                         w["ssm_par"], w["ssm_d_ch"], w["ssm_norm"], w["sel_ssm"], bsz, seq)
    ym = _memattn(proj, mem_k.reshape(bsz, N_MEM, MEM_WIDTH), mem_v.reshape(bsz, N_MEM, MEM_WIDTH),
                  bsz, seq)
    y = _merge_ffn2(h, yg, ys, ym, proj, w["w_branch"], w["w_out"], w["norm_ff2"],
                    w["w_ff2_in"], w["w_ff2_out"], w["norm_final"])

    proj3 = proj.reshape(bsz, seq, PROJ_COLS)
    c_gdn_new = proj3[:, seq - (CONV_W - 1):, COL_QKV:COL_QKV + GDN_CONV_CH]
    c_ssm_new = proj3[:, seq - (CONV_W - 1):, COL_XBC:COL_XBC + SSM_CONV_CH]
    return (y.reshape(bsz, seq, D_MODEL), s_gdn_new, c_gdn_new,
            s_ssm_new.reshape(bsz, SSM_HEADS, SSM_HEADDIM, SSM_DSTATE), c_ssm_new)


def _row(v):
    return v.astype(F32).reshape(1, -1)


def _prep_weights(l, norm_ff1, w_ff1_in, w_ff1_out, norm_mix, w_in, gdn_conv_w, gdn_a_log,
                  gdn_dt_bias, gdn_norm, ssm_conv_w, ssm_conv_b, ssm_a_log, ssm_dt_bias, ssm_d,
                  ssm_norm, w_branch, w_out, norm_ff2, w_ff2_in, w_ff2_out, norm_final):
    row = _row

    wi = w_in[l]
    o = np.cumsum([0, GDN_CONV_CH, GDN_HEADS, GDN_HEADS, GDN_V, SSM_DINNER, SSM_CONV_CH, SSM_HEADS,
                   MEM_WIDTH, 3 * D_MODEL]).tolist()
    sec = [wi[:, o[i]:o[i + 1]] for i in range(9)]
    qkv_w, a_w, b_w, zg_w, zs_w, xbc_w, dt_w, qm_w, gate_w = sec
    small_w = jnp.concatenate(
        [a_w, b_w, dt_w, jnp.zeros((D_MODEL, LANES - 2 * GDN_HEADS - SSM_HEADS), wi.dtype)], axis=1)
    w_in_packed = jnp.concatenate([qkv_w, xbc_w, gate_w, zg_w, zs_w, qm_w, small_w], axis=1)

    def two_rows(a, b, offset):
        par = jnp.zeros((SUBLANES, LANES), F32)
        return par.at[0].set(_pad_lanes(a, offset)).at[1].set(_pad_lanes(b, offset))

    return {
        "norm_ff1": row(norm_ff1[l]), "w_ff1_in": w_ff1_in[l].astype(BF16),
        "w_ff1_out": w_ff1_out[l].astype(BF16), "norm_mix": row(norm_mix[l]),
        "w_in": w_in_packed.astype(BF16),
        "gdn_conv_w": gdn_conv_w[l].astype(F32),
        "gdn_par": two_rows(gdn_a_log[l], gdn_dt_bias[l], SM_A),
        "gdn_norm": row(gdn_norm[l]),
        "sel_gdn": _pair_selection(0, LANES, GDN_PAIRS),
        "ssm_conv_w": ssm_conv_w[l].astype(F32), "ssm_conv_b": row(ssm_conv_b[l]),
        "ssm_par": two_rows(ssm_a_log[l], ssm_dt_bias[l], SM_DT),
        "ssm_d_ch": jnp.repeat(ssm_d[l].astype(F32), SSM_HEADDIM).reshape(1, -1),
        "ssm_norm": row(ssm_norm[l]),
        "sel_ssm": _pair_selection(SM_DT, SSM_HEADDIM, SSM_PAIRS // 2),
        "w_branch": w_branch[l].astype(BF16), "w_out": w_out[l].astype(BF16),
        "norm_ff2": row(norm_ff2[l]), "w_ff2_in": w_ff2_in[l].astype(BF16),
        "w_ff2_out": w_ff2_out[l].astype(BF16), "norm_final": row(norm_final),
    }


def kernel(x_prompt, x_sample, mem_prompt, state_gdn, state_gdn_conv, state_ssm, state_ssm_conv,
           cache_mem_k, cache_mem_v, norm_ff1, w_ff1_in, w_ff1_out, norm_mix, w_in,
           gdn_conv_w, gdn_a_log, gdn_dt_bias, gdn_norm, ssm_conv_w, ssm_conv_b, ssm_a_log,
           ssm_dt_bias, ssm_d, ssm_norm, norm_mem, w_mem_kv, w_branch, w_out,
           norm_ff2, w_ff2_in, w_ff2_out, norm_final):
    assert w_in.shape[0] == 1, "the kernels implement the single-layer configuration"
    l = 0
    bp = x_prompt.shape[0]
    w = _prep_weights(l, norm_ff1, w_ff1_in, w_ff1_out, norm_mix, w_in, gdn_conv_w, gdn_a_log,
                      gdn_dt_bias, gdn_norm, ssm_conv_w, ssm_conv_b, ssm_a_log, ssm_dt_bias, ssm_d,
                      ssm_norm, w_branch, w_out, norm_ff2, w_ff2_in, w_ff2_out, norm_final)

    n_mem_tok = mem_prompt.shape[0] * mem_prompt.shape[1]
    kv = _memkv(mem_prompt.reshape(n_mem_tok, D_MODEL), _row(norm_mem[l]), w_mem_kv[l].astype(BF16))
    mk = kv[:, :MEM_WIDTH].reshape(bp, N_MEM, MEM_HEADS, MEM_HEAD_DIM)
    mv = kv[:, MEM_WIDTH:].reshape(bp, N_MEM, MEM_HEADS, MEM_HEAD_DIM)
    dtp = x_prompt.dtype
    yp, sgp, cgp, ssp, csp = _layer(
        x_prompt, mk, mv,
        jnp.zeros((bp, GDN_HEADS, GDN_DK, GDN_DV), dtp), jnp.zeros((bp, CONV_W - 1, GDN_CONV_CH), dtp),
        jnp.zeros((bp, SSM_HEADS, SSM_HEADDIM, SSM_DSTATE), dtp),
        jnp.zeros((bp, CONV_W - 1, SSM_CONV_CH), dtp), w)
    ys_, sgs, cgs, sss, css = _layer(
        x_sample, cache_mem_k[l], cache_mem_v[l], state_gdn[l], state_gdn_conv[l],
        state_ssm[l], state_ssm_conv[l], w)

    lead = lambda a: a[None]
    return (yp, ys_, lead(sgp), lead(cgp), lead(ssp), lead(csp), lead(mk), lead(mv),
            lead(sgs), lead(cgs), lead(sss), lead(css))
```

```python
import collections
import functools

import numpy as np
import jax
import jax.numpy as jnp
from jax import lax
from jax.experimental import pallas as pl
from jax.experimental.pallas import tpu as pltpu

F32 = jnp.float32
BF16 = jnp.bfloat16
EPS = 1e-6

D_MODEL = 1024
FFN_DIM = 2816
CONV_W = 4
CHUNK = 64
LANES = 128
SUBLANES = 8

GDN_HEADS = 8
GDN_PAIRS = GDN_HEADS // 2
GDN_DK = 128
GDN_DV = 128
GDN_QK = GDN_HEADS * GDN_DK
GDN_V = GDN_HEADS * GDN_DV
GDN_CONV_CH = 2 * GDN_QK + GDN_V

SSM_DINNER = 2048
SSM_HEADDIM = 64
SSM_HEADS = 32
SSM_GROUPS = 4
SSM_DSTATE = 128
SSM_CONV_CH = SSM_DINNER + 2 * SSM_GROUPS * SSM_DSTATE
SSM_PAIRS = SSM_HEADS // 2
SSM_GROUP_PAIRS = SSM_PAIRS // SSM_GROUPS
SSM_GROUP_CH = SSM_DINNER // SSM_GROUPS

N_MEM = 256
MEM_HEADS = 4
MEM_HEAD_DIM = 64
MEM_WIDTH = MEM_HEADS * MEM_HEAD_DIM

COL_QKV = 0
COL_XBC = COL_QKV + GDN_CONV_CH
COL_GATE = COL_XBC + SSM_CONV_CH
COL_ZG = COL_GATE + 3 * D_MODEL
COL_ZS = COL_ZG + GDN_V
COL_QM = COL_ZS + SSM_DINNER
COL_SMALL = COL_QM + MEM_WIDTH
PROJ_COLS = COL_SMALL + LANES
SM_A = 0
SM_B = GDN_HEADS
SM_DT = 2 * GDN_HEADS

VMEM_LIMIT = 56 * 1024 * 1024


def _params(semantics):
    return pltpu.CompilerParams(dimension_semantics=semantics, vmem_limit_bytes=VMEM_LIMIT)


def _mm(a, b):
    return jnp.dot(a, b, preferred_element_type=F32)


def _mm_nt(a, b):
    return lax.dot_general(a, b, (((1,), (1,)), ((), ())), preferred_element_type=F32)


def _mm_tn(a, b):
    return lax.dot_general(a, b, (((0,), (0,)), ((), ())), preferred_element_type=F32)


def _split3(x):
    hi = x.astype(BF16)
    r1 = x - hi.astype(F32)
    mid = r1.astype(BF16)
    lo = (r1 - mid.astype(F32)).astype(BF16)
    return hi, mid, lo


def _split2_lanes(x):
    hi = x.astype(BF16)
    lo = (x - hi.astype(F32)).astype(BF16)
    return jnp.concatenate([hi, lo], axis=1)


def _rms(x, g):
    return x * lax.rsqrt(jnp.mean(x * x, axis=-1, keepdims=True) + EPS) * g


def _sigmoid(x):
    return 1.0 / (1.0 + jnp.exp(-x))


def _silu(x):
    return x * _sigmoid(x)


def _softplus(x):
    return jnp.maximum(x, 0.0) + jnp.log1p(jnp.exp(-jnp.abs(x)))


def _const_spec(shape):
    nd = len(shape)
    return pl.BlockSpec(shape, lambda *_: (0,) * nd, pipeline_mode=pl.Buffered(1))


FFN_TM = 512
FFN_TF = FFN_DIM // 2


def _ffn_compute(x, g_ref, wi_ref, wo_ref):
    xn = _rms(x, g_ref[...]).astype(BF16)
    acc = None
    for c in range(FFN_DIM // FFN_TF):
        lo = c * FFN_TF
        gate = _mm(xn, wi_ref[:, lo:lo + FFN_TF])
        up = _mm(xn, wi_ref[:, FFN_DIM + lo:FFN_DIM + lo + FFN_TF])
        act = (_silu(gate) * up).astype(BF16)
        part = _mm(act, wo_ref[lo:lo + FFN_TF, :])
        acc = part if acc is None else acc + part
    return x + 0.5 * acc


def _ffn1_body(x_ref, g_ref, wi_ref, wo_ref, g2_ref, h_ref, u_ref):
    h = _ffn_compute(x_ref[...], g_ref, wi_ref, wo_ref)
    h_ref[...] = h
    u_ref[...] = _rms(h, g2_ref[...]).astype(BF16)


def _ffn1(x, g, wi, wo, g2):
    t = x.shape[0]
    tm = min(FFN_TM, t)
    row = lambda i: (i, 0)
    return pl.pallas_call(
        _ffn1_body,
        grid=(t // tm,),
        in_specs=[pl.BlockSpec((tm, D_MODEL), row), _const_spec(g.shape), _const_spec(wi.shape),
                  _const_spec(wo.shape), _const_spec(g2.shape)],
        out_specs=[pl.BlockSpec((tm, D_MODEL), row), pl.BlockSpec((tm, D_MODEL), row)],
        out_shape=[jax.ShapeDtypeStruct((t, D_MODEL), F32), jax.ShapeDtypeStruct((t, D_MODEL), BF16)],
        compiler_params=_params(("parallel",)),
        name="ffn1",
    )(x, g, wi, wo, g2)


PROJ_TM = 1024
PROJ_TN = PROJ_COLS // 9


def _proj_body(u_ref, w_ref, o_ref):
    o_ref[...] = _mm(u_ref[...], w_ref[...])


def _in_proj(u, w):
    t = u.shape[0]
    tm = min(PROJ_TM, t)
    return pl.pallas_call(
        _proj_body,
        grid=(PROJ_COLS // PROJ_TN, t // tm),
        in_specs=[pl.BlockSpec((tm, D_MODEL), lambda j, i: (i, 0)),
                  pl.BlockSpec((D_MODEL, PROJ_TN), lambda j, i: (0, j))],
        out_specs=pl.BlockSpec((tm, PROJ_TN), lambda j, i: (i, j)),
        out_shape=jax.ShapeDtypeStruct((t, PROJ_COLS), F32),
        compiler_params=_params(("parallel", "parallel")),
        name="in_proj",
    )(u, w)


def _memkv_body(x_ref, g_ref, w_ref, o_ref):
    o_ref[...] = _mm(_rms(x_ref[...], g_ref[...]).astype(BF16), w_ref[...])


def _memkv(mem, g, w):
    t = mem.shape[0]
    tm = min(512, t)
    row = lambda i: (i, 0)
    return pl.pallas_call(
        _memkv_body,
        grid=(t // tm,),
        in_specs=[pl.BlockSpec((tm, D_MODEL), row), _const_spec(g.shape), _const_spec(w.shape)],
        out_specs=pl.BlockSpec((tm, 2 * MEM_WIDTH), row),
        out_shape=jax.ShapeDtypeStruct((t, 2 * MEM_WIDTH), F32),
        compiler_params=_params(("parallel",)),
        name="memkv",
    )(mem, g, w)


MERGE_TM = 256


def _merge_ffn2_body(h_ref, yg_ref, ys_ref, ym_ref, gate_ref, wb_ref, wout_ref,
                     g_ref, wi_ref, wo_ref, gf_ref, y_ref):
    gates = gate_ref[...]
    merged = (_sigmoid(gates[:, 0:D_MODEL]) * _mm(yg_ref[...], wb_ref[0:GDN_V, :])
              + _sigmoid(gates[:, D_MODEL:2 * D_MODEL])
              * _mm(ys_ref[...], wb_ref[GDN_V:GDN_V + SSM_DINNER, :])
              + _sigmoid(gates[:, 2 * D_MODEL:3 * D_MODEL])
              * _mm(ym_ref[...], wb_ref[GDN_V + SSM_DINNER:, :]))
    h = h_ref[...] + _mm(merged.astype(BF16), wout_ref[...])
    h = _ffn_compute(h, g_ref, wi_ref, wo_ref)
    y_ref[...] = _rms(h, gf_ref[...])


def _merge_ffn2(h, yg, ys, ym, proj, wb, wout, g, wi, wo, gf):
    t = h.shape[0]
    tm = min(MERGE_TM, t)
    row = lambda i: (i, 0)
    gate_blk = COL_GATE // (3 * D_MODEL)
    return pl.pallas_call(
        _merge_ffn2_body,
        grid=(t // tm,),
        in_specs=[pl.BlockSpec((tm, D_MODEL), row), pl.BlockSpec((tm, GDN_V), row),
                  pl.BlockSpec((tm, SSM_DINNER), row), pl.BlockSpec((tm, MEM_WIDTH), row),
                  pl.BlockSpec((tm, 3 * D_MODEL), lambda i: (i, gate_blk)),
                  _const_spec(wb.shape), _const_spec(wout.shape), _const_spec(g.shape),
                  _const_spec(wi.shape), _const_spec(wo.shape), _const_spec(gf.shape)],
        out_specs=pl.BlockSpec((tm, D_MODEL), row),
        out_shape=jax.ShapeDtypeStruct((t, D_MODEL), F32),
        compiler_params=_params(("parallel",)),
        name="merge_ffn2",
    )(h, yg, ys, ym, proj, wb, wout, g, wi, wo, gf)


MEMATTN_TL = 512
MEMATTN_NSEQ = 16


def _memattn_body(q_ref, k_ref, v_ref, y_ref, *, nseq, rows):
    items = [(s, hh) for s in range(nseq) for hh in range(MEM_HEADS)]
    hrows = lambda ref, s, hh: ref[s, pl.ds(hh, N_MEM, stride=MEM_HEADS), :].astype(BF16)
    q = q_ref[...].astype(BF16)
    sc = [_mm_nt(q[s * rows:(s + 1) * rows, hh * MEM_HEAD_DIM:(hh + 1) * MEM_HEAD_DIM],
                 hrows(k_ref, s, hh)) * (MEM_HEAD_DIM ** -0.5) for s, hh in items]
    p = [jnp.exp(x - jnp.max(x, axis=-1, keepdims=True)) for x in sc]
    p = [(x / jnp.sum(x, axis=-1, keepdims=True)).astype(BF16) for x in p]
    o = [_mm(x, hrows(v_ref, s, hh)) for x, (s, hh) in zip(p, items)]
    ys = [jnp.concatenate(o[s * MEM_HEADS:(s + 1) * MEM_HEADS], axis=1) for s in range(nseq)]
    y_ref[...] = jnp.concatenate(ys, axis=0).astype(BF16)


def _memattn(proj, k, v, bsz, seq):
    if seq >= MEMATTN_TL:
        nseq, rows, nt = 1, MEMATTN_TL, seq // MEMATTN_TL
    else:
        nseq, rows, nt = MEMATTN_NSEQ, seq, 1
    assert bsz % nseq == 0 and seq % rows == 0
    tl = nseq * rows
    qm_blk = COL_QM // MEM_WIDTH
    kv_spec = pl.BlockSpec((nseq, N_MEM * MEM_HEADS, MEM_HEAD_DIM), lambda b, i: (b, 0, 0))
    return pl.pallas_call(
        functools.partial(_memattn_body, nseq=nseq, rows=rows),
        grid=(bsz // nseq, nt),
        in_specs=[pl.BlockSpec((tl, MEM_WIDTH), lambda b, i: (b * nt + i, qm_blk)), kv_spec, kv_spec],
        out_specs=pl.BlockSpec((tl, MEM_WIDTH), lambda b, i: (b * nt + i, 0)),
        out_shape=jax.ShapeDtypeStruct((bsz * seq, MEM_WIDTH), BF16),
        compiler_params=_params(("parallel", "arbitrary")),
        name="memattn",
    )(proj, k, v)


SEQ_TL = 256
SHORT_NSEQ = 4
CONV_PAD = SUBLANES

Geom = collections.namedtuple("Geom", ["nseq", "nch", "vrows"])


def _geom(bsz, seq):
    if seq >= CHUNK:
        assert seq % SEQ_TL == 0
        return Geom(1, SEQ_TL // CHUNK, CHUNK), seq // SEQ_TL
    assert bsz % SHORT_NSEQ == 0 and seq % SUBLANES == 0
    return Geom(SHORT_NSEQ, 1, seq), 1


def _conv_silu(x_ref, cst_ref, cw_ref, cb_ref, xbuf, act, geom, first_tile):
    ch = x_ref.shape[1]
    tls = geom.nch * geom.vrows
    stride = tls + CONV_PAD

    @pl.when(first_tile)
    def _():
        for s in range(geom.nseq):
            xbuf[s * stride:s * stride + CONV_PAD, :] = cst_ref[s]

    @pl.when(jnp.logical_not(first_tile))
    def _():
        for s in range(geom.nseq):
            xbuf[s * stride:s * stride + CONV_PAD, :] = xbuf[s * stride + tls:(s + 1) * stride, :]

    for s in range(geom.nseq):
        xbuf[s * stride + CONV_PAD:(s + 1) * stride, :] = x_ref[s * tls:(s + 1) * tls, :]

    if geom.vrows < CHUNK:
        act[...] = jnp.zeros_like(act)

    for cb in range(ch // LANES):
        cols = slice(cb * LANES, (cb + 1) * LANES)
        w = cw_ref[:, cols]
        for s in range(geom.nseq):
            xa = xbuf[s * stride:(s + 1) * stride, cols]
            y = xa * w[CONV_W - 1:CONV_W, :]
            for k in range(1, CONV_W):
                y = y + pltpu.roll(xa, k, axis=0) * w[CONV_W - 1 - k:CONV_W - k, :]
            if cb_ref is not None:
                y = y + cb_ref[:, cols]
            a0 = s * geom.nch * CHUNK
            act[a0:a0 + tls, cols] = _silu(y[CONV_PAD:, :])


def _store_gates(gates, geom, first, second):
    tls = geom.nch * geom.vrows
    if geom.vrows < CHUNK:
        gates[...] = jnp.zeros_like(gates)
    for s in range(geom.nseq):
        g0 = s * geom.nch * CHUNK
        gates[g0:g0 + tls, 0:LANES] = first[s * tls:(s + 1) * tls]
        gates[g0:g0 + tls, LANES:2 * LANES] = second[s * tls:(s + 1) * tls]


def _pair_masks():
    row = lax.broadcasted_iota(jnp.int32, (CHUNK, LANES), 0)
    lane = lax.broadcasted_iota(jnp.int32, (CHUNK, LANES), 1)
    col = jnp.where(lane < CHUNK, lane, lane - CHUNK)
    return row, lane, col


def _block_diag(x, left):
    zero = jnp.zeros_like(x)
    return jnp.concatenate([jnp.where(left, x, zero), jnp.where(left, zero, x)], axis=0)


def _chunk_cumsums(gates, nprob):
    tril = jnp.where(lax.broadcasted_iota(jnp.int32, (CHUNK, CHUNK), 0)
                     >= lax.broadcasted_iota(jnp.int32, (CHUNK, CHUNK), 1), 1.0, 0.0).astype(BF16)
    tril3 = jnp.concatenate([tril, tril, tril], axis=1)
    pieces = [jnp.concatenate(_split3(gates[pb * CHUNK:(pb + 1) * CHUNK, 0:LANES]), axis=0)
              for pb in range(nprob)]
    gc_all = _mm(tril3, jnp.concatenate(pieces, axis=1))
    return [gc_all[:, pb * LANES:(pb + 1) * LANES] for pb in range(nprob)]


def _row_form(gc):
    return jnp.concatenate([gc, pltpu.roll(gc, LANES - 1, axis=1)], axis=0).T


def _gdn_body(x_ref, sm_ref, z_ref, cst_ref, cw_ref, s0_ref, par_ref, gn_ref, selp_ref,
              y_ref, sout_ref, xbuf, act, gates, s_scr, u_scr, wq_scr, kd_scr, qk_scr, egl_scr,
              o_scr, *, geom):
    tile = pl.program_id(1)
    first = tile == 0
    nprob = geom.nseq * geom.nch
    vr = geom.vrows

    _conv_silu(x_ref, cst_ref, cw_ref, None, xbuf, act, geom, first)

    @pl.when(first)
    def _():
        s_scr[...] = s0_ref[...]

    sm = sm_ref[...]
    lane_g = lax.broadcasted_iota(jnp.int32, sm.shape, 1)
    head_lane = lane_g < GDN_HEADS
    log_decay = jnp.where(head_lane, -jnp.exp(par_ref[0:1, :]) * _softplus(sm + par_ref[1:2, :]), 0.0)
    beta = jnp.where(head_lane, pltpu.roll(_sigmoid(sm), LANES - SM_B, axis=1), 0.0)
    _store_gates(gates, geom, log_decay, beta)

    row, lane, col = _pair_masks()
    left = lane < CHUNK
    incl = row >= col
    strict = row > col
    eye = jnp.where(row == col, 1.0, 0.0).astype(F32)
    zeros_h = jnp.zeros((CHUNK, LANES), F32)
    hcols = lambda h: slice(h * LANES, (h + 1) * LANES)
    prows = lambda pb, n=CHUNK: slice(pb * n, (pb + 1) * n)

    gcs = _chunk_cumsums(gates, nprob)
    lhs, xts = [], []
    for pb in range(nprob):
        gc = gcs[pb]
        gl = jnp.broadcast_to(gc[CHUNK - 1:CHUNK, :], gc.shape)
        quantities = jnp.concatenate(
            [gc, gates[prows(pb), LANES:2 * LANES], jnp.exp(gc), jnp.exp(gl - gc)], axis=0)
        lhs.append(_split2_lanes(quantities))
        xts.append(_row_form(gc))

    items = [(pb, p) for pb in range(nprob) for p in range(GDN_PAIRS)]
    exp_ = [_mm(lhs[pb], selp_ref[p]) for pb, p in items]

    def l2n(t):
        return t * lax.rsqrt(jnp.sum(t * t, axis=-1, keepdims=True) + EPS)

    def stacked(e, qi):
        return jnp.concatenate([e[qi * CHUNK:(qi + 1) * CHUNK, 0:LANES],
                                e[qi * CHUNK:(qi + 1) * CHUNK, LANES:]], axis=0)

    decay_l, kq_l, rhs_l, qd_l, kd_l = [], [], [], [], []
    for (pb, p), e in zip(items, exp_):
        ha, hb = 2 * p, 2 * p + 1
        qa, qb = [l2n(act[prows(pb), hcols(h)]) * (GDN_DK ** -0.5) for h in (ha, hb)]
        ka, kb = [l2n(act[prows(pb), hcols(GDN_HEADS + h)]) for h in (ha, hb)]
        va, vb = [act[prows(pb), hcols(2 * GDN_HEADS + h)] for h in (ha, hb)]
        beta2, eg2, kdec2 = stacked(e, 1), stacked(e, 2), stacked(e, 3)
        k2 = jnp.concatenate([ka, kb], axis=0)
        kbeta2 = k2 * beta2
        gci = jnp.where(left, e[0:CHUNK, 0:LANES], e[0:CHUNK, LANES:])
        gcj = jnp.broadcast_to(xts[pb][ha:ha + 1, :], (CHUNK, LANES))
        decay_l.append(jnp.where(incl, jnp.exp(jnp.where(incl, gci - gcj, 0.0)), 0.0))
        k_bd = jnp.concatenate([jnp.concatenate([ka, zeros_h], axis=1),
                                jnp.concatenate([zeros_h, kb], axis=1)], axis=0).astype(BF16)
        kbq = jnp.concatenate([jnp.concatenate([kbeta2[0:CHUNK], kbeta2[CHUNK:]], axis=1),
                               jnp.concatenate([qa, qb], axis=1)], axis=0).astype(BF16)
        kq_l.append(_mm_nt(kbq, k_bd))
        rhs_l.append(jnp.concatenate(
            [jnp.concatenate([va, vb], axis=0) * beta2, kbeta2 * eg2], axis=1).astype(BF16))
        qd_l.append(jnp.concatenate([qa, qb], axis=0) * eg2)
        kd_l.append(k2 * kdec2)
        egl_scr[prows(pb, SUBLANES), hcols(ha)] = jnp.broadcast_to(
            e[3 * CHUNK - 1:3 * CHUNK, 0:LANES], (SUBLANES, LANES))
        egl_scr[prows(pb, SUBLANES), hcols(hb)] = jnp.broadcast_to(
            e[3 * CHUNK - 1:3 * CHUNK, LANES:], (SUBLANES, LANES))

    n_pow = [-jnp.where(strict, kq[0:CHUNK] * d, 0.0) for kq, d in zip(kq_l, decay_l)]
    t_inv = [eye + n for n in n_pow]
    for _ in range(int(np.log2(CHUNK)) - 1):
        n_pow = [_mm(n.astype(BF16), _block_diag(n, left).astype(BF16)) for n in n_pow]
        t_inv = [t + _mm(t.astype(BF16), _block_diag(n, left).astype(BF16))
                 for t, n in zip(t_inv, n_pow)]

    uw_l = [_mm(_block_diag(t, left).astype(BF16), rhs) for t, rhs in zip(t_inv, rhs_l)]
    for (pb, p), uw, kq, d, qd, kd in zip(items, uw_l, kq_l, decay_l, qd_l, kd_l):
        qk_scr[prows(pb, 2 * CHUNK), hcols(p)] = _block_diag(kq[CHUNK:] * d, left).astype(BF16)
        for idx, h in enumerate((2 * p, 2 * p + 1)):
            sl = slice(idx * CHUNK, (idx + 1) * CHUNK)
            u_scr[prows(pb), hcols(h)] = uw[sl, 0:LANES]
            wq_scr[prows(pb, 2 * CHUNK), hcols(h)] = jnp.concatenate(
                [uw[sl, LANES:], qd[sl]], axis=0).astype(BF16)
            kd_scr[prows(pb), hcols(h)] = kd[sl].astype(BF16)

    for c in range(geom.nch):
        sh = [(s, h) for s in range(geom.nseq) for h in range(GDN_HEADS)]
        pbs = lambda s: s * geom.nch + c
        wqs = [_mm(wq_scr[prows(pbs(s), 2 * CHUNK), hcols(h)], s_scr[s, h].astype(BF16))
               for s, h in sh]
        vnew = [u_scr[prows(pbs(s)), hcols(h)] - w[0:CHUNK] for (s, h), w in zip(sh, wqs)]
        for (s, h), vn in zip(sh, vnew):
            egl = egl_scr[pbs(s) * SUBLANES:pbs(s) * SUBLANES + 1, hcols(h)]
            s_scr[s, h] = s_scr[s, h] * egl + _mm_tn(kd_scr[prows(pbs(s)), hcols(h)], vn.astype(BF16))
        for s in range(geom.nseq):
            for p in range(GDN_PAIRS):
                i0 = s * GDN_HEADS + 2 * p
                vn2 = jnp.concatenate([vnew[i0], vnew[i0 + 1]], axis=0).astype(BF16)
                intra = _mm(qk_scr[prows(pbs(s), 2 * CHUNK), hcols(p)], vn2)
                for idx in range(2):
                    o = wqs[i0 + idx][CHUNK:] + intra[idx * CHUNK:(idx + 1) * CHUNK]
                    o_scr[prows(pbs(s), vr), hcols(2 * p + idx)] = o[0:vr]

    for h in range(GDN_HEADS):
        y = _rms(o_scr[:, hcols(h)], gn_ref[...]) * _silu(z_ref[:, hcols(h)])
        y_ref[:, hcols(h)] = y.astype(BF16)

    @pl.when(tile == pl.num_programs(1) - 1)
    def _():
        sout_ref[...] = s_scr[...]


def _gdn(proj, cst8, cw, s0, par, gn, selp, bsz, seq):
    geom, nt = _geom(bsz, seq)
    nprob = geom.nseq * geom.nch
    tl = nprob * geom.vrows
    tls = geom.nch * geom.vrows
    tok = lambda blk: (lambda b, i: (b * nt + i, blk))
    st_spec = pl.BlockSpec((geom.nseq, GDN_HEADS, GDN_DK, GDN_DV), lambda b, i: (b, 0, 0, 0))
    return pl.pallas_call(
        functools.partial(_gdn_body, geom=geom),
        grid=(bsz // geom.nseq, nt),
        in_specs=[pl.BlockSpec((tl, GDN_CONV_CH), tok(COL_QKV // GDN_CONV_CH)),
                  pl.BlockSpec((tl, LANES), tok(COL_SMALL // LANES)),
                  pl.BlockSpec((tl, GDN_V), tok(COL_ZG // GDN_V)),
                  pl.BlockSpec((geom.nseq, CONV_PAD, GDN_CONV_CH), lambda b, i: (b, 0, 0)),
                  _const_spec(cw.shape), st_spec,
                  _const_spec(par.shape), _const_spec(gn.shape), _const_spec(selp.shape)],
        out_specs=[pl.BlockSpec((tl, GDN_V), lambda b, i: (b * nt + i, 0)), st_spec],
        out_shape=[jax.ShapeDtypeStruct((bsz * seq, GDN_V), BF16),
                   jax.ShapeDtypeStruct((bsz, GDN_HEADS, GDN_DK, GDN_DV), F32)],
        scratch_shapes=[pltpu.VMEM((geom.nseq * (tls + CONV_PAD), GDN_CONV_CH), F32),
                        pltpu.VMEM((nprob * CHUNK, GDN_CONV_CH), F32),
                        pltpu.VMEM((nprob * CHUNK, 2 * LANES), F32),
                        pltpu.VMEM((geom.nseq, GDN_HEADS, GDN_DK, GDN_DV), F32),
                        pltpu.VMEM((nprob * CHUNK, GDN_V), F32),
                        pltpu.VMEM((nprob * 2 * CHUNK, GDN_V), BF16),
                        pltpu.VMEM((nprob * CHUNK, GDN_V), BF16),
                        pltpu.VMEM((nprob * 2 * CHUNK, GDN_PAIRS * LANES), BF16),
                        pltpu.VMEM((nprob * SUBLANES, GDN_V), F32),
                        pltpu.VMEM((tl, GDN_V), F32)],
        compiler_params=_params(("parallel", "arbitrary")),
        name="gdn",
    )(proj, proj, proj, cst8, cw, s0, par, gn, selp)


def _ssd_body(x_ref, sm_ref, z_ref, cst_ref, cw_ref, cb_ref, h0_ref, par_ref, dch_ref, nw_ref,
              seld_ref, y_ref, hout_ref, xbuf, act, gates, h_scr, o_scr, *, geom):
    tile = pl.program_id(1)
    first = tile == 0
    nprob = geom.nseq * geom.nch
    vr = geom.vrows

    _conv_silu(x_ref, cst_ref, cw_ref, cb_ref, xbuf, act, geom, first)

    @pl.when(first)
    def _():
        for s in range(geom.nseq):
            for p in range(SSM_PAIRS):
                h_scr[s, p] = h0_ref[s, p].T

    sm = sm_ref[...]
    lane_g = lax.broadcasted_iota(jnp.int32, sm.shape, 1)
    is_dt = (lane_g >= SM_DT) & (lane_g < SM_DT + SSM_HEADS)
    dt = jnp.where(is_dt, _softplus(sm + par_ref[1:2, :]), 0.0)
    log_decay = dt * jnp.where(is_dt, -jnp.exp(par_ref[0:1, :]), 0.0)
    _store_gates(gates, geom, log_decay, dt)

    row, lane, col = _pair_masks()
    left = lane < CHUNK
    incl = row >= col
    pcols = lambda p: slice(p * LANES, (p + 1) * LANES)
    prows = lambda pb, n=CHUNK: slice(pb * n, (pb + 1) * n)
    quarter = lambda e, qi, j: e[qi * CHUNK:(qi + 1) * CHUNK, j * LANES:(j + 1) * LANES]
    b_off = SSM_DINNER
    c_off = SSM_DINNER + SSM_GROUPS * SSM_DSTATE

    gcs = _chunk_cumsums(gates, nprob)
    for pb in range(nprob):
        s = pb // geom.nch
        gc = gcs[pb]
        dt_c = gates[prows(pb), LANES:2 * LANES]
        gl = jnp.broadcast_to(gc[CHUNK - 1:CHUNK, :], gc.shape)
        lhs = _split2_lanes(jnp.concatenate([gc, dt_c, jnp.exp(gc), jnp.exp(gl - gc) * dt_c], axis=0))
        xt = _row_form(gc)
        exp_ = [_mm(lhs, seld_ref[d]) for d in range(SSM_PAIRS // 2)]

        b_bf = [act[prows(pb), b_off + g * SSM_DSTATE:b_off + (g + 1) * SSM_DSTATE].astype(BF16)
                for g in range(SSM_GROUPS)]
        c_bf = [act[prows(pb), c_off + g * SSM_DSTATE:c_off + (g + 1) * SSM_DSTATE].astype(BF16)
                for g in range(SSM_GROUPS)]
        cb2 = [_mm_nt(c, jnp.concatenate([b, b], axis=0)) for b, c in zip(b_bf, c_bf)]

        pairs = range(SSM_PAIRS)
        grp = lambda p: p // SSM_GROUP_PAIRS
        x_l = [act[prows(pb), pcols(p)] for p in pairs]
        gc_l = [quarter(exp_[p // 2], 0, p % 2) for p in pairs]
        eg_l = [quarter(exp_[p // 2], 2, p % 2) for p in pairs]
        xdt_l = [x * quarter(exp_[p // 2], 1, p % 2) for p, x in zip(pairs, x_l)]
        xw_l = [x * quarter(exp_[p // 2], 3, p % 2) for p, x in zip(pairs, x_l)]
        m_l = []
        for p in pairs:
            r = SM_DT + 2 * p
            gcj = jnp.broadcast_to(xt[r:r + 1, :], (CHUNK, LANES))
            decay = jnp.where(incl, jnp.exp(jnp.where(incl, gc_l[p] - gcj, 0.0)), 0.0)
            m_l.append((cb2[grp(p)] * decay).astype(BF16))
        intra = [_mm(m, _block_diag(xdt, left).astype(BF16)) for m, xdt in zip(m_l, xdt_l)]
        dstate = [_mm_tn(b_bf[grp(p)], xw_l[p].astype(BF16)) for p in pairs]
        inter = [_mm(c_bf[grp(p)], h_scr[s, p].astype(BF16)) for p in pairs]
        for p in pairs:
            h_scr[s, p] = h_scr[s, p] * eg_l[p][CHUNK - 1:CHUNK, :] + dstate[p]

        for g in range(SSM_GROUPS):
            ys = []
            ssq = None
            for p in range(g * SSM_GROUP_PAIRS, (g + 1) * SSM_GROUP_PAIRS):
                y = intra[p] + inter[p] * eg_l[p] + dch_ref[:, pcols(p)] * x_l[p]
                y = y[0:vr] * _silu(z_ref[prows(pb, vr), pcols(p)])
                ys.append(y)
                sq = jnp.sum(y * y, axis=-1, keepdims=True)
                ssq = sq if ssq is None else ssq + sq
            inv = lax.rsqrt(ssq * (1.0 / SSM_GROUP_CH) + EPS)
            for pp, y in enumerate(ys):
                o_scr[prows(pb, vr), pcols(g * SSM_GROUP_PAIRS + pp)] = y * inv

    y_ref[...] = (o_scr[...] * nw_ref[...]).astype(BF16)

    @pl.when(tile == pl.num_programs(1) - 1)
    def _():
        for s in range(geom.nseq):
            for p in range(SSM_PAIRS):
                hout_ref[s, p] = h_scr[s, p].T


def _ssd(proj, cst8, cw, cb, h0, par, dch, nw, seld, bsz, seq):
    geom, nt = _geom(bsz, seq)
    nprob = geom.nseq * geom.nch
    tl = nprob * geom.vrows
    tls = geom.nch * geom.vrows
    tok = lambda blk: (lambda b, i: (b * nt + i, blk))
    st_shape = (geom.nseq, SSM_PAIRS, 2 * SSM_HEADDIM, SSM_DSTATE)
    st_spec = pl.BlockSpec(st_shape, lambda b, i: (b, 0, 0, 0))
    return pl.pallas_call(
        functools.partial(_ssd_body, geom=geom),
        grid=(bsz // geom.nseq, nt),
        in_specs=[pl.BlockSpec((tl, SSM_CONV_CH), tok(COL_XBC // SSM_CONV_CH)),
                  pl.BlockSpec((tl, LANES), tok(COL_SMALL // LANES)),
                  pl.BlockSpec((tl, SSM_DINNER), tok(COL_ZS // SSM_DINNER)),
                  pl.BlockSpec((geom.nseq, CONV_PAD, SSM_CONV_CH), lambda b, i: (b, 0, 0)),
                  _const_spec(cw.shape), _const_spec(cb.shape), st_spec,
                  _const_spec(par.shape), _const_spec(dch.shape), _const_spec(nw.shape),
                  _const_spec(seld.shape)],
        out_specs=[pl.BlockSpec((tl, SSM_DINNER), lambda b, i: (b * nt + i, 0)), st_spec],
        out_shape=[jax.ShapeDtypeStruct((bsz * seq, SSM_DINNER), BF16),
                   jax.ShapeDtypeStruct((bsz,) + st_shape[1:], F32)],
        scratch_shapes=[pltpu.VMEM((geom.nseq * (tls + CONV_PAD), SSM_CONV_CH), F32),
                        pltpu.VMEM((nprob * CHUNK, SSM_CONV_CH), F32),
                        pltpu.VMEM((nprob * CHUNK, 2 * LANES), F32),
                        pltpu.VMEM((geom.nseq, SSM_PAIRS, SSM_DSTATE, 2 * SSM_HEADDIM), F32),
                        pltpu.VMEM((tl, SSM_DINNER), F32)],
        compiler_params=_params(("parallel", "arbitrary")),
        name="ssd",
    )(proj, proj, proj, cst8, cw, cb, h0, par, dch, nw, seld)


def _pad_lanes(v, offset):
    out = jnp.zeros((LANES,), F32)
    return out.at[offset:offset + v.shape[0]].set(v.astype(F32))


def _pair_selection(first_lane, lanes_per_head, n_mats):
    heads_per_mat = 2 * LANES // lanes_per_head
    sel = np.zeros((n_mats, 2 * LANES, 2 * LANES), np.float32)
    for m in range(n_mats):
        for j in range(heads_per_mat):
            src = first_lane + m * heads_per_mat + j
            sel[m, src, j * lanes_per_head:(j + 1) * lanes_per_head] = 1.0
            sel[m, LANES + src, j * lanes_per_head:(j + 1) * lanes_per_head] = 1.0
    return jnp.asarray(sel, BF16)


def _pad_conv_state(st):
    bsz, _, ch = st.shape
    return jnp.concatenate([jnp.zeros((bsz, CONV_PAD - (CONV_W - 1), ch), F32), st.astype(F32)], axis=1)


def _layer(x, mem_k, mem_v, s_gdn, c_gdn, s_ssm, c_ssm, w):
    bsz, seq, _ = x.shape
    t = bsz * seq
    xf = x.reshape(t, D_MODEL)
    h, u = _ffn1(xf, w["norm_ff1"], w["w_ff1_in"], w["w_ff1_out"], w["norm_mix"])
    proj = _in_proj(u, w["w_in"])

    yg, s_gdn_new = _gdn(proj, _pad_conv_state(c_gdn), w["gdn_conv_w"], s_gdn.astype(F32),
                         w["gdn_par"], w["gdn_norm"], w["sel_gdn"], bsz, seq)
    h0 = s_ssm.astype(F32).reshape(bsz, SSM_PAIRS, 2 * SSM_HEADDIM, SSM_DSTATE)
    ys, s_ssm_new = _ssd(proj, _pad_conv_state(c_ssm), w["ssm_conv_w"], w["ssm_conv_b"], h0,
                         w["ssm_par"], w["ssm_d_ch"], w["ssm_norm"], w["sel_ssm"], bsz, seq)
    flat = (bsz, N_MEM * MEM_HEADS, MEM_HEAD_DIM)
    ym = _memattn(proj, mem_k.reshape(flat), mem_v.reshape(flat), bsz, seq)
    y = _merge_ffn2(h, yg, ys, ym, proj, w["w_branch"], w["w_out"], w["norm_ff2"],
                    w["w_ff2_in"], w["w_ff2_out"], w["norm_final"])

    proj3 = proj.reshape(bsz, seq, PROJ_COLS)
    c_gdn_new = proj3[:, seq - (CONV_W - 1):, COL_QKV:COL_QKV + GDN_CONV_CH]
    c_ssm_new = proj3[:, seq - (CONV_W - 1):, COL_XBC:COL_XBC + SSM_CONV_CH]
    return (y.reshape(bsz, seq, D_MODEL), s_gdn_new, c_gdn_new,
            s_ssm_new.reshape(bsz, SSM_HEADS, SSM_HEADDIM, SSM_DSTATE), c_ssm_new)


def _row(v):
    return v.astype(F32).reshape(1, -1)


def _prep_weights(l, norm_ff1, w_ff1_in, w_ff1_out, norm_mix, w_in, gdn_conv_w, gdn_a_log,
                  gdn_dt_bias, gdn_norm, ssm_conv_w, ssm_conv_b, ssm_a_log, ssm_dt_bias, ssm_d,
                  ssm_norm, w_branch, w_out, norm_ff2, w_ff2_in, w_ff2_out, norm_final):
    row = _row

    wi = w_in[l]
    o = np.cumsum([0, GDN_CONV_CH, GDN_HEADS, GDN_HEADS, GDN_V, SSM_DINNER, SSM_CONV_CH, SSM_HEADS,
                   MEM_WIDTH, 3 * D_MODEL]).tolist()
    sec = [wi[:, o[i]:o[i + 1]] for i in range(9)]
    qkv_w, a_w, b_w, zg_w, zs_w, xbc_w, dt_w, qm_w, gate_w = sec
    small_w = jnp.concatenate(
        [a_w, b_w, dt_w, jnp.zeros((D_MODEL, LANES - 2 * GDN_HEADS - SSM_HEADS), wi.dtype)], axis=1)
    w_in_packed = jnp.concatenate([qkv_w, xbc_w, gate_w, zg_w, zs_w, qm_w, small_w], axis=1)

    def two_rows(a, b, offset):
        par = jnp.zeros((SUBLANES, LANES), F32)
        return par.at[0].set(_pad_lanes(a, offset)).at[1].set(_pad_lanes(b, offset))

    return {
        "norm_ff1": row(norm_ff1[l]), "w_ff1_in": w_ff1_in[l].astype(BF16),
        "w_ff1_out": w_ff1_out[l].astype(BF16), "norm_mix": row(norm_mix[l]),
        "w_in": w_in_packed.astype(BF16),
        "gdn_conv_w": gdn_conv_w[l].astype(F32),
        "gdn_par": two_rows(gdn_a_log[l], gdn_dt_bias[l], SM_A),
        "gdn_norm": row(gdn_norm[l]),
        "sel_gdn": _pair_selection(0, LANES, GDN_PAIRS),
        "ssm_conv_w": ssm_conv_w[l].astype(F32), "ssm_conv_b": row(ssm_conv_b[l]),
        "ssm_par": two_rows(ssm_a_log[l], ssm_dt_bias[l], SM_DT),
        "ssm_d_ch": jnp.repeat(ssm_d[l].astype(F32), SSM_HEADDIM).reshape(1, -1),
        "ssm_norm": row(ssm_norm[l]),
        "sel_ssm": _pair_selection(SM_DT, SSM_HEADDIM, SSM_PAIRS // 2),
        "w_branch": w_branch[l].astype(BF16), "w_out": w_out[l].astype(BF16),
        "norm_ff2": row(norm_ff2[l]), "w_ff2_in": w_ff2_in[l].astype(BF16),
        "w_ff2_out": w_ff2_out[l].astype(BF16), "norm_final": row(norm_final),
    }


def kernel(x_prompt, x_sample, mem_prompt, state_gdn, state_gdn_conv, state_ssm, state_ssm_conv,
           cache_mem_k, cache_mem_v, norm_ff1, w_ff1_in, w_ff1_out, norm_mix, w_in,
           gdn_conv_w, gdn_a_log, gdn_dt_bias, gdn_norm, ssm_conv_w, ssm_conv_b, ssm_a_log,
           ssm_dt_bias, ssm_d, ssm_norm, norm_mem, w_mem_kv, w_branch, w_out,
           norm_ff2, w_ff2_in, w_ff2_out, norm_final):
    assert w_in.shape[0] == 1, "the kernels implement the single-layer configuration"
    l = 0
    bp = x_prompt.shape[0]
    w = _prep_weights(l, norm_ff1, w_ff1_in, w_ff1_out, norm_mix, w_in, gdn_conv_w, gdn_a_log,
                      gdn_dt_bias, gdn_norm, ssm_conv_w, ssm_conv_b, ssm_a_log, ssm_dt_bias, ssm_d,
                      ssm_norm, w_branch, w_out, norm_ff2, w_ff2_in, w_ff2_out, norm_final)

    n_mem_tok = mem_prompt.shape[0] * mem_prompt.shape[1]
    kv = _memkv(mem_prompt.reshape(n_mem_tok, D_MODEL), _row(norm_mem[l]), w_mem_kv[l].astype(BF16))
    mk = kv[:, :MEM_WIDTH].reshape(bp, N_MEM, MEM_HEADS, MEM_HEAD_DIM)
    mv = kv[:, MEM_WIDTH:].reshape(bp, N_MEM, MEM_HEADS, MEM_HEAD_DIM)
    dtp = x_prompt.dtype
    yp, sgp, cgp, ssp, csp = _layer(
        x_prompt, mk, mv,
        jnp.zeros((bp, GDN_HEADS, GDN_DK, GDN_DV), dtp), jnp.zeros((bp, CONV_W - 1, GDN_CONV_CH), dtp),
        jnp.zeros((bp, SSM_HEADS, SSM_HEADDIM, SSM_DSTATE), dtp),
        jnp.zeros((bp, CONV_W - 1, SSM_CONV_CH), dtp), w)
    ys_, sgs, cgs, sss, css = _layer(
        x_sample, cache_mem_k[l], cache_mem_v[l], state_gdn[l], state_gdn_conv[l],
        state_ssm[l], state_ssm_conv[l], w)

    lead = lambda a: a[None]
    return (yp, ys_, lead(sgp), lead(cgp), lead(ssp), lead(csp), lead(mk), lead(mv),
            lead(sgs), lead(cgs), lead(sss), lead(css))
```

```python
import collections
import functools

import numpy as np
import jax
import jax.numpy as jnp
from jax import lax
from jax.experimental import pallas as pl
from jax.experimental.pallas import tpu as pltpu

F32 = jnp.float32
BF16 = jnp.bfloat16
EPS = 1e-6

D_MODEL = 1024
FFN_DIM = 2816
CONV_W = 4
CHUNK = 64
LANES = 128
SUBLANES = 8

GDN_HEADS = 8
GDN_PAIRS = GDN_HEADS // 2
GDN_DK = 128
GDN_DV = 128
GDN_QK = GDN_HEADS * GDN_DK
GDN_V = GDN_HEADS * GDN_DV
GDN_CONV_CH = 2 * GDN_QK + GDN_V

SSM_DINNER = 2048
SSM_HEADDIM = 64
SSM_HEADS = 32
SSM_GROUPS = 4
SSM_DSTATE = 128
SSM_CONV_CH = SSM_DINNER + 2 * SSM_GROUPS * SSM_DSTATE
SSM_PAIRS = SSM_HEADS // 2
SSM_GROUP_PAIRS = SSM_PAIRS // SSM_GROUPS
SSM_GROUP_CH = SSM_DINNER // SSM_GROUPS

N_MEM = 256
MEM_HEADS = 4
MEM_HEAD_DIM = 64
MEM_WIDTH = MEM_HEADS * MEM_HEAD_DIM

MIX_QKV = 0
MIX_XBC = MIX_QKV + GDN_CONV_CH
MIX_COLS = MIX_XBC + SSM_CONV_CH
REST_GATE = 0
REST_ZG = REST_GATE + 3 * D_MODEL
REST_ZS = REST_ZG + GDN_V
REST_QM = REST_ZS + SSM_DINNER
REST_SMALL = REST_QM + MEM_WIDTH
REST_COLS = REST_SMALL + LANES
IN_SPLITS = (GDN_CONV_CH, GDN_HEADS, GDN_HEADS, GDN_V, SSM_DINNER, SSM_CONV_CH, SSM_HEADS, MEM_WIDTH,
             3 * D_MODEL)
SM_A = 0
SM_B = GDN_HEADS
SM_DT = 2 * GDN_HEADS

VMEM_LIMIT = 56 * 1024 * 1024


def _params(semantics):
    return pltpu.CompilerParams(dimension_semantics=semantics, vmem_limit_bytes=VMEM_LIMIT)


def _mm(a, b):
    return jnp.dot(a, b, preferred_element_type=F32)


def _mm_nt(a, b):
    return lax.dot_general(a, b, (((1,), (1,)), ((), ())), preferred_element_type=F32)


def _mm_tn(a, b):
    return lax.dot_general(a, b, (((0,), (0,)), ((), ())), preferred_element_type=F32)


def _split3(x):
    hi = x.astype(BF16)
    r1 = x - hi.astype(F32)
    mid = r1.astype(BF16)
    lo = (r1 - mid.astype(F32)).astype(BF16)
    return hi, mid, lo


def _split2_lanes(x):
    hi = x.astype(BF16)
    lo = (x - hi.astype(F32)).astype(BF16)
    return jnp.concatenate([hi, lo], axis=1)


def _rms(x, g):
    return x * lax.rsqrt(jnp.mean(x * x, axis=-1, keepdims=True) + EPS) * g


def _sigmoid(x):
    return 1.0 / (1.0 + jnp.exp2(x * float(-1.0 / np.log(2.0))))


def _silu(x):
    return x * _sigmoid(x)


def _softplus(x):
    return jnp.maximum(x, 0.0) + jnp.log1p(jnp.exp(-jnp.abs(x)))


def _const_spec(shape):
    nd = len(shape)
    return pl.BlockSpec(shape, lambda *_: (0,) * nd, pipeline_mode=pl.Buffered(1))


FFN_TM = 512
FFN_TF = FFN_DIM // 2


def _ffn_compute(x, g_ref, wi_ref, wo_ref):
    xn = _rms(x, g_ref[...]).astype(BF16)
    acc = None
    for c in range(FFN_DIM // FFN_TF):
        lo = c * FFN_TF
        gate = _mm(xn, wi_ref[:, lo:lo + FFN_TF])
        up = _mm(xn, wi_ref[:, FFN_DIM + lo:FFN_DIM + lo + FFN_TF])
        act = (_silu(gate) * up).astype(BF16)
        part = _mm(act, wo_ref[lo:lo + FFN_TF, :])
        acc = part if acc is None else acc + part
    return x + 0.5 * acc


def _ffn1_body(x_ref, g_ref, wi_ref, wo_ref, g2_ref, h_ref, u_ref):
    h = _ffn_compute(x_ref[...], g_ref, wi_ref, wo_ref)
    h_ref[...] = h
    u_ref[...] = _rms(h, g2_ref[...]).astype(BF16)


def _ffn1(x, g, wi, wo, g2):
    t = x.shape[0]
    tm = min(FFN_TM, t)
    row = lambda i: (i, 0)
    return pl.pallas_call(
        _ffn1_body,
        grid=(t // tm,),
        in_specs=[pl.BlockSpec((tm, D_MODEL), row), _const_spec(g.shape), _const_spec(wi.shape),
                  _const_spec(wo.shape), _const_spec(g2.shape)],
        out_specs=[pl.BlockSpec((tm, D_MODEL), row), pl.BlockSpec((tm, D_MODEL), row)],
        out_shape=[jax.ShapeDtypeStruct((t, D_MODEL), F32), jax.ShapeDtypeStruct((t, D_MODEL), BF16)],
        compiler_params=_params(("parallel",)),
        name="ffn1",
    )(x, g, wi, wo, g2)


PACK_ROWS = 128


def _pack_body(w_ref, small_ref, mix_ref, rest_ref):
    src = np.cumsum((0,) + IN_SPLITS).tolist()
    qkv, _, _, zg, zs, xbc, _, qm, gate = [(src[i], IN_SPLITS[i]) for i in range(len(IN_SPLITS))]
    for ref, dst, (lo, n) in ((mix_ref, MIX_QKV, qkv), (mix_ref, MIX_XBC, xbc),
                              (rest_ref, REST_GATE, gate), (rest_ref, REST_ZG, zg),
                              (rest_ref, REST_ZS, zs), (rest_ref, REST_QM, qm)):
        ref[:, dst:dst + n] = w_ref[:, lo:lo + n].astype(BF16)
    rest_ref[:, REST_SMALL:] = small_ref[...]


def _pack_w_in(w, small):
    rows = lambda i: (i, 0)
    return pl.pallas_call(
        _pack_body,
        grid=(D_MODEL // PACK_ROWS,),
        in_specs=[pl.BlockSpec((PACK_ROWS, w.shape[1]), rows), pl.BlockSpec((PACK_ROWS, LANES), rows)],
        out_specs=[pl.BlockSpec((PACK_ROWS, MIX_COLS), rows), pl.BlockSpec((PACK_ROWS, REST_COLS), rows)],
        out_shape=[jax.ShapeDtypeStruct((D_MODEL, MIX_COLS), BF16),
                   jax.ShapeDtypeStruct((D_MODEL, REST_COLS), BF16)],
        compiler_params=_params(("parallel",)),
        name="pack_w_in",
    )(w, small)


PROJ_TM = 1024
PROJ_ROWS = 256
MIX_TN = MIX_COLS // 4
REST_TN = REST_COLS // 3


def _proj_body(u_ref, w_ref, o_ref):
    o_ref[...] = _mm(u_ref[...], w_ref[...])


def _in_proj(u, w, tn, name):
    t = u.shape[0]
    tm = min(PROJ_TM, t)
    cols = w.shape[1]
    return pl.pallas_call(
        _proj_body,
        grid=(cols // tn, t // tm),
        in_specs=[pl.BlockSpec((tm, D_MODEL), lambda j, i: (i, 0)),
                  pl.BlockSpec((D_MODEL, tn), lambda j, i: (0, j))],
        out_specs=pl.BlockSpec((tm, tn), lambda j, i: (i, j)),
        out_shape=jax.ShapeDtypeStruct((t, cols), F32),
        compiler_params=_params(("parallel", "parallel")),
        name=name,
    )(u, w)


def _causal_conv_silu(xa, w, bias):
    shape = (SUBLANES, xa.shape[1])
    sub = lax.broadcasted_iota(jnp.int32, shape, 0)
    taps = [jnp.broadcast_to(w[j:j + 1, :], shape) for j in range(CONV_W)]
    bias = jnp.broadcast_to(bias, shape)
    groups = [xa[g * SUBLANES:(g + 1) * SUBLANES, :] for g in range(xa.shape[0] // SUBLANES)]
    rolled = [[pltpu.roll(x, k, axis=0) for x in groups] for k in range(1, CONV_W)]
    outs = []
    for g in range(CONV_PAD // SUBLANES, len(groups)):
        y = groups[g] * taps[CONV_W - 1] + bias
        for k in range(1, CONV_W):
            shifted = jnp.where(sub < k, rolled[k - 1][g - 1], rolled[k - 1][g])
            y = y + shifted * taps[CONV_W - 1 - k]
        outs.append(_silu(y))
    return jnp.concatenate(outs, axis=0)


def _proj_conv_body(u_ref, w_ref, cw_ref, cb_ref, cst_ref, o_ref, tail_ref, carry, *, tiles_per_seq):
    i = pl.program_id(1)

    @pl.when(i % tiles_per_seq == 0)
    def _():
        carry[...] = cst_ref[0]

    prev = carry[...]
    w = w_ref[...]
    cw = cw_ref[...]
    bias = cb_ref[...]
    for r in range(u_ref.shape[0] // PROJ_ROWS):
        rows = slice(r * PROJ_ROWS, (r + 1) * PROJ_ROWS)
        raw = _mm(u_ref[rows, :], w)
        o_ref[rows, :] = _causal_conv_silu(jnp.concatenate([prev, raw], axis=0), cw, bias)
        prev = raw[PROJ_ROWS - CONV_PAD:, :]
    carry[...] = prev
    tail_ref[0] = prev


def _in_proj_conv(u, w, cw, cb, cst8, seq):
    t = u.shape[0]
    tm = min(PROJ_TM, seq)
    assert seq % tm == 0 and tm % PROJ_ROWS == 0
    tps = seq // tm
    tn = MIX_TN
    return pl.pallas_call(
        functools.partial(_proj_conv_body, tiles_per_seq=tps),
        grid=(MIX_COLS // tn, t // tm),
        in_specs=[pl.BlockSpec((tm, D_MODEL), lambda j, i: (i, 0)),
                  pl.BlockSpec((D_MODEL, tn), lambda j, i: (0, j)),
                  pl.BlockSpec((CONV_W, tn), lambda j, i: (0, j)),
                  pl.BlockSpec((1, tn), lambda j, i: (0, j)),
                  pl.BlockSpec((1, CONV_PAD, tn), lambda j, i: (i // tps, 0, j))],
        out_specs=[pl.BlockSpec((tm, tn), lambda j, i: (i, j)),
                   pl.BlockSpec((1, CONV_PAD, tn), lambda j, i: (i // tps, 0, j))],
        out_shape=[jax.ShapeDtypeStruct((t, MIX_COLS), F32),
                   jax.ShapeDtypeStruct((t // seq, CONV_PAD, MIX_COLS), F32)],
        scratch_shapes=[pltpu.VMEM((CONV_PAD, tn), F32)],
        compiler_params=_params(("parallel", "arbitrary")),
        name="in_proj_conv",
    )(u, w, cw, cb, cst8)


def _memkv_body(x_ref, g_ref, w_ref, o_ref):
    o_ref[...] = _mm(_rms(x_ref[...], g_ref[...]).astype(BF16), w_ref[...])


def _memkv(mem, g, w):
    t = mem.shape[0]
    tm = min(512, t)
    row = lambda i: (i, 0)
    return pl.pallas_call(
        _memkv_body,
        grid=(t // tm,),
        in_specs=[pl.BlockSpec((tm, D_MODEL), row), _const_spec(g.shape), _const_spec(w.shape)],
        out_specs=pl.BlockSpec((tm, 2 * MEM_WIDTH), row),
        out_shape=jax.ShapeDtypeStruct((t, 2 * MEM_WIDTH), F32),
        compiler_params=_params(("parallel",)),
        name="memkv",
    )(mem, g, w)


MERGE_TM = 256


def _merge_ffn2_body(h_ref, yg_ref, ys_ref, ym_ref, gate_ref, wb_ref, wout_ref,
                     g_ref, wi_ref, wo_ref, gf_ref, y_ref):
    gates = gate_ref[...]
    merged = (_sigmoid(gates[:, 0:D_MODEL]) * _mm(yg_ref[...], wb_ref[0:GDN_V, :])
              + _sigmoid(gates[:, D_MODEL:2 * D_MODEL])
              * _mm(ys_ref[...], wb_ref[GDN_V:GDN_V + SSM_DINNER, :])
              + _sigmoid(gates[:, 2 * D_MODEL:3 * D_MODEL])
              * _mm(ym_ref[...], wb_ref[GDN_V + SSM_DINNER:, :]))
    h = h_ref[...] + _mm(merged.astype(BF16), wout_ref[...])
    h = _ffn_compute(h, g_ref, wi_ref, wo_ref)
    y_ref[...] = _rms(h, gf_ref[...])


def _merge_ffn2(h, yg, ys, ym, proj, wb, wout, g, wi, wo, gf):
    t = h.shape[0]
    tm = min(MERGE_TM, t)
    row = lambda i: (i, 0)
    gate_blk = REST_GATE // (3 * D_MODEL)
    return pl.pallas_call(
        _merge_ffn2_body,
        grid=(t // tm,),
        in_specs=[pl.BlockSpec((tm, D_MODEL), row), pl.BlockSpec((tm, GDN_V), row),
                  pl.BlockSpec((tm, SSM_DINNER), row), pl.BlockSpec((tm, MEM_WIDTH), row),
                  pl.BlockSpec((tm, 3 * D_MODEL), lambda i: (i, gate_blk)),
                  _const_spec(wb.shape), _const_spec(wout.shape), _const_spec(g.shape),
                  _const_spec(wi.shape), _const_spec(wo.shape), _const_spec(gf.shape)],
        out_specs=pl.BlockSpec((tm, D_MODEL), row),
        out_shape=jax.ShapeDtypeStruct((t, D_MODEL), F32),
        compiler_params=_params(("parallel",)),
        name="merge_ffn2",
    )(h, yg, ys, ym, proj, wb, wout, g, wi, wo, gf)


MEMATTN_TL = 512
MEMATTN_NSEQ = 16


def _memattn_body(q_ref, k_ref, v_ref, y_ref, *, nseq, rows):
    lane = lax.broadcasted_iota(jnp.int32, (rows, MEM_WIDTH), 1)
    masks = [(lane >= hh * MEM_HEAD_DIM) & (lane < (hh + 1) * MEM_HEAD_DIM) for hh in range(MEM_HEADS)]
    seqs = range(nseq)
    q = [q_ref[s * rows:(s + 1) * rows, :] for s in seqs]
    q4 = [jnp.concatenate([jnp.where(m, x, 0.0) for m in masks], axis=0).astype(BF16) for x in q]
    sc = [_mm_nt(q4[s], k_ref[s]) * (MEM_HEAD_DIM ** -0.5) for s in seqs]
    p = [jnp.exp(x - jnp.max(x, axis=-1, keepdims=True)) for x in sc]
    p = [(x / jnp.sum(x, axis=-1, keepdims=True)).astype(BF16) for x in p]
    o4 = [_mm(p[s], v_ref[s]) for s in seqs]
    ys = []
    for x in o4:
        y = jnp.where(masks[0], x[0:rows], 0.0)
        for hh in range(1, MEM_HEADS):
            y = y + jnp.where(masks[hh], x[hh * rows:(hh + 1) * rows], 0.0)
        ys.append(y)
    y_ref[...] = jnp.concatenate(ys, axis=0).astype(BF16)


def _memattn(proj, k, v, bsz, seq):
    if seq >= MEMATTN_TL:
        nseq, rows, nt = 1, MEMATTN_TL, seq // MEMATTN_TL
    else:
        nseq, rows, nt = MEMATTN_NSEQ, seq, 1
    assert bsz % nseq == 0 and seq % rows == 0
    tl = nseq * rows
    qm_blk = REST_QM // MEM_WIDTH
    kv_spec = pl.BlockSpec((nseq, N_MEM, MEM_WIDTH), lambda b, i: (b, 0, 0))
    return pl.pallas_call(
        functools.partial(_memattn_body, nseq=nseq, rows=rows),
        grid=(bsz // nseq, nt),
        in_specs=[pl.BlockSpec((tl, MEM_WIDTH), lambda b, i: (b * nt + i, qm_blk)), kv_spec, kv_spec],
        out_specs=pl.BlockSpec((tl, MEM_WIDTH), lambda b, i: (b * nt + i, 0)),
        out_shape=jax.ShapeDtypeStruct((bsz * seq, MEM_WIDTH), BF16),
        compiler_params=_params(("parallel", "arbitrary")),
        name="memattn",
    )(proj, k, v)


SEQ_TL = 256
SHORT_NSEQ = 4
CONV_PAD = SUBLANES

Geom = collections.namedtuple("Geom", ["nseq", "nch", "vrows"])


def _geom(bsz, seq):
    if seq >= CHUNK:
        assert seq % SEQ_TL == 0
        return Geom(1, SEQ_TL // CHUNK, CHUNK), seq // SEQ_TL
    assert bsz % SHORT_NSEQ == 0 and seq % SUBLANES == 0
    return Geom(SHORT_NSEQ, 1, seq), 1


def _short_conv_silu(x_ref, cst_ref, cw_ref, cb_ref, act, geom):
    assert geom.nch == 1 and geom.vrows < CHUNK
    act[...] = jnp.zeros_like(act)
    for cb in range(x_ref.shape[1] // LANES):
        cols = slice(cb * LANES, (cb + 1) * LANES)
        for s in range(geom.nseq):
            xa = jnp.concatenate([cst_ref[s, :, cols], x_ref[s * geom.vrows:(s + 1) * geom.vrows, cols]],
                                 axis=0)
            act[s * CHUNK:s * CHUNK + geom.vrows, cols] = _causal_conv_silu(
                xa, cw_ref[:, cols], cb_ref[:, cols])


def _store_gates(gates, geom, first, second):
    tls = geom.nch * geom.vrows
    if geom.vrows < CHUNK:
        gates[...] = jnp.zeros_like(gates)
    for s in range(geom.nseq):
        g0 = s * geom.nch * CHUNK
        gates[g0:g0 + tls, 0:LANES] = first[s * tls:(s + 1) * tls]
        gates[g0:g0 + tls, LANES:2 * LANES] = second[s * tls:(s + 1) * tls]


def _pair_masks():
    row = lax.broadcasted_iota(jnp.int32, (CHUNK, LANES), 0)
    lane = lax.broadcasted_iota(jnp.int32, (CHUNK, LANES), 1)
    col = jnp.where(lane < CHUNK, lane, lane - CHUNK)
    return row, lane, col


def _block_diag(x, left):
    zero = jnp.zeros_like(x)
    return jnp.concatenate([jnp.where(left, x, zero), jnp.where(left, zero, x)], axis=0)


def _chunk_cumsums(gates, nprob):
    tril = jnp.where(lax.broadcasted_iota(jnp.int32, (CHUNK, CHUNK), 0)
                     >= lax.broadcasted_iota(jnp.int32, (CHUNK, CHUNK), 1), 1.0, 0.0).astype(BF16)
    tril3 = jnp.concatenate([tril, tril, tril], axis=1)
    pieces = [jnp.concatenate(_split3(gates[pb * CHUNK:(pb + 1) * CHUNK, 0:LANES]), axis=0)
              for pb in range(nprob)]
    gc_all = _mm(tril3, jnp.concatenate(pieces, axis=1))
    return [gc_all[:, pb * LANES:(pb + 1) * LANES] for pb in range(nprob)]


def _row_form(gc):
    return jnp.concatenate([gc, pltpu.roll(gc, LANES - 1, axis=1)], axis=0).T


def _gdn_body(*refs, geom):
    if geom.vrows == CHUNK:
        (x_ref, sm_ref, z_ref, s0_ref, par_ref, gn_ref, selp_ref, y_ref, sout_ref,
         gates, s_scr, u_scr, wq_scr, kd_scr, qk_scr, egl_scr, o_scr) = refs
        act = x_ref
    else:
        (x_ref, sm_ref, z_ref, cst_ref, cw_ref, cb_ref, s0_ref, par_ref, gn_ref, selp_ref, y_ref,
         sout_ref, act, gates, s_scr, u_scr, wq_scr, kd_scr, qk_scr, egl_scr, o_scr) = refs
        _short_conv_silu(x_ref, cst_ref, cw_ref, cb_ref, act, geom)
    tile = pl.program_id(1)
    first = tile == 0
    nprob = geom.nseq * geom.nch
    vr = geom.vrows

    @pl.when(first)
    def _():
        s_scr[...] = s0_ref[...]

    sm = sm_ref[...]
    lane_g = lax.broadcasted_iota(jnp.int32, sm.shape, 1)
    head_lane = lane_g < GDN_HEADS
    log_decay = jnp.where(head_lane, -jnp.exp(par_ref[0:1, :]) * _softplus(sm + par_ref[1:2, :]), 0.0)
    beta = jnp.where(head_lane, pltpu.roll(_sigmoid(sm), LANES - SM_B, axis=1), 0.0)
    _store_gates(gates, geom, log_decay, beta)

    row, lane, col = _pair_masks()
    left = lane < CHUNK
    incl = row >= col
    strict = row > col
    eye = jnp.where(row == col, 1.0, 0.0).astype(F32)
    zeros_h = jnp.zeros((CHUNK, LANES), F32)
    hcols = lambda h: slice(h * LANES, (h + 1) * LANES)
    prows = lambda pb, n=CHUNK: slice(pb * n, (pb + 1) * n)

    gcs = _chunk_cumsums(gates, nprob)
    lhs, xts = [], []
    for pb in range(nprob):
        gc = gcs[pb]
        gl = jnp.broadcast_to(gc[CHUNK - 1:CHUNK, :], gc.shape)
        quantities = jnp.concatenate(
            [gc, gates[prows(pb), LANES:2 * LANES], jnp.exp(gc), jnp.exp(gl - gc)], axis=0)
        lhs.append(_split2_lanes(quantities))
        xts.append(_row_form(gc))

    items = [(pb, p) for pb in range(nprob) for p in range(GDN_PAIRS)]
    exp_ = [_mm(lhs[pb], selp_ref[p]) for pb, p in items]

    def l2n(t):
        return t * lax.rsqrt(jnp.sum(t * t, axis=-1, keepdims=True) + EPS)

    def stacked(e, qi):
        return jnp.concatenate([e[qi * CHUNK:(qi + 1) * CHUNK, 0:LANES],
                                e[qi * CHUNK:(qi + 1) * CHUNK, LANES:]], axis=0)

    decay_l, kq_l, rhs_l, qd_l, kd_l = [], [], [], [], []
    for (pb, p), e in zip(items, exp_):
        ha, hb = 2 * p, 2 * p + 1
        qa, qb = [l2n(act[prows(pb), hcols(h)]) * (GDN_DK ** -0.5) for h in (ha, hb)]
        ka, kb = [l2n(act[prows(pb), hcols(GDN_HEADS + h)]) for h in (ha, hb)]
        va, vb = [act[prows(pb), hcols(2 * GDN_HEADS + h)] for h in (ha, hb)]
        beta2, eg2, kdec2 = stacked(e, 1), stacked(e, 2), stacked(e, 3)
        k2 = jnp.concatenate([ka, kb], axis=0)
        kbeta2 = k2 * beta2
        gci = jnp.where(left, e[0:CHUNK, 0:LANES], e[0:CHUNK, LANES:])
        gcj = jnp.broadcast_to(xts[pb][ha:ha + 1, :], (CHUNK, LANES))
        decay_l.append(jnp.where(incl, jnp.exp(jnp.where(incl, gci - gcj, 0.0)), 0.0))
        k_bd = jnp.concatenate([jnp.concatenate([ka, zeros_h], axis=1),
                                jnp.concatenate([zeros_h, kb], axis=1)], axis=0).astype(BF16)
        kbq = jnp.concatenate([jnp.concatenate([kbeta2[0:CHUNK], kbeta2[CHUNK:]], axis=1),
                               jnp.concatenate([qa, qb], axis=1)], axis=0).astype(BF16)
        kq_l.append(_mm_nt(kbq, k_bd))
        rhs_l.append(jnp.concatenate(
            [jnp.concatenate([va, vb], axis=0) * beta2, kbeta2 * eg2], axis=1).astype(BF16))
        qd_l.append(jnp.concatenate([qa, qb], axis=0) * eg2)
        kd_l.append(k2 * kdec2)
        egl_scr[prows(pb, SUBLANES), hcols(ha)] = jnp.broadcast_to(
            e[3 * CHUNK - 1:3 * CHUNK, 0:LANES], (SUBLANES, LANES))
        egl_scr[prows(pb, SUBLANES), hcols(hb)] = jnp.broadcast_to(
            e[3 * CHUNK - 1:3 * CHUNK, LANES:], (SUBLANES, LANES))

    n_pow = [-jnp.where(strict, kq[0:CHUNK] * d, 0.0) for kq, d in zip(kq_l, decay_l)]
    t_inv = [eye + n for n in n_pow]
    for _ in range(int(np.log2(CHUNK)) - 1):
        n_pow = [_mm(n.astype(BF16), _block_diag(n, left).astype(BF16)) for n in n_pow]
        t_inv = [t + _mm(t.astype(BF16), _block_diag(n, left).astype(BF16))
                 for t, n in zip(t_inv, n_pow)]

    uw_l = [_mm(_block_diag(t, left).astype(BF16), rhs) for t, rhs in zip(t_inv, rhs_l)]
    for (pb, p), uw, kq, d, qd, kd in zip(items, uw_l, kq_l, decay_l, qd_l, kd_l):
        qk_scr[prows(pb, 2 * CHUNK), hcols(p)] = _block_diag(kq[CHUNK:] * d, left).astype(BF16)
        for idx, h in enumerate((2 * p, 2 * p + 1)):
            sl = slice(idx * CHUNK, (idx + 1) * CHUNK)
            u_scr[prows(pb), hcols(h)] = uw[sl, 0:LANES]
            wq_scr[prows(pb, 2 * CHUNK), hcols(h)] = jnp.concatenate(
                [uw[sl, LANES:], qd[sl]], axis=0).astype(BF16)
            kd_scr[prows(pb), hcols(h)] = kd[sl].astype(BF16)

    for c in range(geom.nch):
        sh = [(s, h) for s in range(geom.nseq) for h in range(GDN_HEADS)]
        pbs = lambda s: s * geom.nch + c
        wqs = [_mm(wq_scr[prows(pbs(s), 2 * CHUNK), hcols(h)], s_scr[s, h].astype(BF16))
               for s, h in sh]
        vnew = [u_scr[prows(pbs(s)), hcols(h)] - w[0:CHUNK] for (s, h), w in zip(sh, wqs)]
        for (s, h), vn in zip(sh, vnew):
            egl = egl_scr[pbs(s) * SUBLANES:pbs(s) * SUBLANES + 1, hcols(h)]
            s_scr[s, h] = s_scr[s, h] * egl + _mm_tn(kd_scr[prows(pbs(s)), hcols(h)], vn.astype(BF16))
        for s in range(geom.nseq):
            for p in range(GDN_PAIRS):
                i0 = s * GDN_HEADS + 2 * p
                vn2 = jnp.concatenate([vnew[i0], vnew[i0 + 1]], axis=0).astype(BF16)
                intra = _mm(qk_scr[prows(pbs(s), 2 * CHUNK), hcols(p)], vn2)
                for idx in range(2):
                    o = wqs[i0 + idx][CHUNK:] + intra[idx * CHUNK:(idx + 1) * CHUNK]
                    o_scr[prows(pbs(s), vr), hcols(2 * p + idx)] = o[0:vr]

    for h in range(GDN_HEADS):
        y = _rms(o_scr[:, hcols(h)], gn_ref[...]) * _silu(z_ref[:, hcols(h)])
        y_ref[:, hcols(h)] = y.astype(BF16)

    @pl.when(tile == pl.num_programs(1) - 1)
    def _():
        sout_ref[...] = s_scr[...]


def _conv_operands(conv, geom, width, col_blk):
    if conv is None:
        return [], [], []
    cst8, cw, cb = conv
    specs = [pl.BlockSpec((geom.nseq, CONV_PAD, width), lambda b, i: (b, 0, col_blk)),
             pl.BlockSpec((CONV_W, width), lambda b, i: (0, col_blk)),
             pl.BlockSpec((1, width), lambda b, i: (0, col_blk))]
    return [cst8, cw, cb], specs, [pltpu.VMEM((geom.nseq * CHUNK, width), F32)]


def _gdn(mix, rest, conv, s0, par, gn, selp, bsz, seq):
    geom, nt = _geom(bsz, seq)
    nprob = geom.nseq * geom.nch
    tl = nprob * geom.vrows
    tok = lambda blk: (lambda b, i: (b * nt + i, blk))
    st_spec = pl.BlockSpec((geom.nseq, GDN_HEADS, GDN_DK, GDN_DV), lambda b, i: (b, 0, 0, 0))
    conv_ops, conv_specs, conv_scratch = _conv_operands(conv, geom, GDN_CONV_CH, MIX_QKV // GDN_CONV_CH)
    return pl.pallas_call(
        functools.partial(_gdn_body, geom=geom),
        grid=(bsz // geom.nseq, nt),
        in_specs=[pl.BlockSpec((tl, GDN_CONV_CH), tok(MIX_QKV // GDN_CONV_CH)),
                  pl.BlockSpec((tl, LANES), tok(REST_SMALL // LANES)),
                  pl.BlockSpec((tl, GDN_V), tok(REST_ZG // GDN_V))] + conv_specs + [
                  st_spec, _const_spec(par.shape), _const_spec(gn.shape), _const_spec(selp.shape)],
        out_specs=[pl.BlockSpec((tl, GDN_V), lambda b, i: (b * nt + i, 0)), st_spec],
        out_shape=[jax.ShapeDtypeStruct((bsz * seq, GDN_V), BF16),
                   jax.ShapeDtypeStruct((bsz, GDN_HEADS, GDN_DK, GDN_DV), F32)],
        scratch_shapes=conv_scratch + [
            pltpu.VMEM((nprob * CHUNK, 2 * LANES), F32),
            pltpu.VMEM((geom.nseq, GDN_HEADS, GDN_DK, GDN_DV), F32),
            pltpu.VMEM((nprob * CHUNK, GDN_V), F32),
            pltpu.VMEM((nprob * 2 * CHUNK, GDN_V), BF16),
            pltpu.VMEM((nprob * CHUNK, GDN_V), BF16),
            pltpu.VMEM((nprob * 2 * CHUNK, GDN_PAIRS * LANES), BF16),
            pltpu.VMEM((nprob * SUBLANES, GDN_V), F32),
            pltpu.VMEM((tl, GDN_V), F32)],
        compiler_params=_params(("parallel", "arbitrary")),
        name="gdn",
    )(mix, rest, rest, *conv_ops, s0, par, gn, selp)


def _ssd_body(*refs, geom):
    if geom.vrows == CHUNK:
        (x_ref, sm_ref, z_ref, h0_ref, par_ref, dch_ref, nw_ref, seld_ref, y_ref, hout_ref,
         gates, h_scr, o_scr) = refs
        act = x_ref
    else:
        (x_ref, sm_ref, z_ref, cst_ref, cw_ref, cb_ref, h0_ref, par_ref, dch_ref, nw_ref, seld_ref,
         y_ref, hout_ref, act, gates, h_scr, o_scr) = refs
        _short_conv_silu(x_ref, cst_ref, cw_ref, cb_ref, act, geom)
    tile = pl.program_id(1)
    first = tile == 0
    nprob = geom.nseq * geom.nch
    vr = geom.vrows

    @pl.when(first)
    def _():
        for s in range(geom.nseq):
            for p in range(SSM_PAIRS):
                h_scr[s, p] = h0_ref[s, p].T

    sm = sm_ref[...]
    lane_g = lax.broadcasted_iota(jnp.int32, sm.shape, 1)
    is_dt = (lane_g >= SM_DT) & (lane_g < SM_DT + SSM_HEADS)
    dt = jnp.where(is_dt, _softplus(sm + par_ref[1:2, :]), 0.0)
    log_decay = dt * jnp.where(is_dt, -jnp.exp(par_ref[0:1, :]), 0.0)
    _store_gates(gates, geom, log_decay, dt)

    row, lane, col = _pair_masks()
    left = lane < CHUNK
    incl = row >= col
    pcols = lambda p: slice(p * LANES, (p + 1) * LANES)
    prows = lambda pb, n=CHUNK: slice(pb * n, (pb + 1) * n)
    quarter = lambda e, qi, j: e[qi * CHUNK:(qi + 1) * CHUNK, j * LANES:(j + 1) * LANES]
    b_off = SSM_DINNER
    c_off = SSM_DINNER + SSM_GROUPS * SSM_DSTATE

    gcs = _chunk_cumsums(gates, nprob)
    for pb in range(nprob):
        s = pb // geom.nch
        gc = gcs[pb]
        dt_c = gates[prows(pb), LANES:2 * LANES]
        gl = jnp.broadcast_to(gc[CHUNK - 1:CHUNK, :], gc.shape)
        lhs = _split2_lanes(jnp.concatenate([gc, dt_c, jnp.exp(gc), jnp.exp(gl - gc) * dt_c], axis=0))
        xt = _row_form(gc)
        exp_ = [_mm(lhs, seld_ref[d]) for d in range(SSM_PAIRS // 2)]

        b_bf = [act[prows(pb), b_off + g * SSM_DSTATE:b_off + (g + 1) * SSM_DSTATE].astype(BF16)
                for g in range(SSM_GROUPS)]
        c_bf = [act[prows(pb), c_off + g * SSM_DSTATE:c_off + (g + 1) * SSM_DSTATE].astype(BF16)
                for g in range(SSM_GROUPS)]
        cb2 = [_mm_nt(c, jnp.concatenate([b, b], axis=0)) for b, c in zip(b_bf, c_bf)]

        pairs = range(SSM_PAIRS)
        grp = lambda p: p // SSM_GROUP_PAIRS
        x_l = [act[prows(pb), pcols(p)] for p in pairs]
        gc_l = [quarter(exp_[p // 2], 0, p % 2) for p in pairs]
        eg_l = [quarter(exp_[p // 2], 2, p % 2) for p in pairs]
        xdt_l = [x * quarter(exp_[p // 2], 1, p % 2) for p, x in zip(pairs, x_l)]
        xw_l = [x * quarter(exp_[p // 2], 3, p % 2) for p, x in zip(pairs, x_l)]
        m_l = []
        for p in pairs:
            r = SM_DT + 2 * p
            gcj = jnp.broadcast_to(xt[r:r + 1, :], (CHUNK, LANES))
            decay = jnp.where(incl, jnp.exp(jnp.where(incl, gc_l[p] - gcj, 0.0)), 0.0)
            m_l.append((cb2[grp(p)] * decay).astype(BF16))
        intra = [_mm(m, _block_diag(xdt, left).astype(BF16)) for m, xdt in zip(m_l, xdt_l)]
        dstate = [_mm_tn(b_bf[grp(p)], xw_l[p].astype(BF16)) for p in pairs]
        inter = [_mm(c_bf[grp(p)], h_scr[s, p].astype(BF16)) for p in pairs]
        for p in pairs:
            h_scr[s, p] = h_scr[s, p] * eg_l[p][CHUNK - 1:CHUNK, :] + dstate[p]

        for g in range(SSM_GROUPS):
            ys = []
            ssq = None
            for p in range(g * SSM_GROUP_PAIRS, (g + 1) * SSM_GROUP_PAIRS):
                y = intra[p] + inter[p] * eg_l[p] + dch_ref[:, pcols(p)] * x_l[p]
                y = y[0:vr] * _silu(z_ref[prows(pb, vr), pcols(p)])
                ys.append(y)
                sq = jnp.sum(y * y, axis=-1, keepdims=True)
                ssq = sq if ssq is None else ssq + sq
            inv = lax.rsqrt(ssq * (1.0 / SSM_GROUP_CH) + EPS)
            for pp, y in enumerate(ys):
                o_scr[prows(pb, vr), pcols(g * SSM_GROUP_PAIRS + pp)] = y * inv

    y_ref[...] = (o_scr[...] * nw_ref[...]).astype(BF16)

    @pl.when(tile == pl.num_programs(1) - 1)
    def _():
        for s in range(geom.nseq):
            for p in range(SSM_PAIRS):
                hout_ref[s, p] = h_scr[s, p].T


def _ssd(mix, rest, conv, h0, par, dch, nw, seld, bsz, seq):
    geom, nt = _geom(bsz, seq)
    nprob = geom.nseq * geom.nch
    tl = nprob * geom.vrows
    tok = lambda blk: (lambda b, i: (b * nt + i, blk))
    st_shape = (geom.nseq, SSM_PAIRS, 2 * SSM_HEADDIM, SSM_DSTATE)
    st_spec = pl.BlockSpec(st_shape, lambda b, i: (b, 0, 0, 0))
    conv_ops, conv_specs, conv_scratch = _conv_operands(conv, geom, SSM_CONV_CH, MIX_XBC // SSM_CONV_CH)
    return pl.pallas_call(
        functools.partial(_ssd_body, geom=geom),
        grid=(bsz // geom.nseq, nt),
        in_specs=[pl.BlockSpec((tl, SSM_CONV_CH), tok(MIX_XBC // SSM_CONV_CH)),
                  pl.BlockSpec((tl, LANES), tok(REST_SMALL // LANES)),
                  pl.BlockSpec((tl, SSM_DINNER), tok(REST_ZS // SSM_DINNER))] + conv_specs + [
                  st_spec, _const_spec(par.shape), _const_spec(dch.shape), _const_spec(nw.shape),
                  _const_spec(seld.shape)],
        out_specs=[pl.BlockSpec((tl, SSM_DINNER), lambda b, i: (b * nt + i, 0)), st_spec],
        out_shape=[jax.ShapeDtypeStruct((bsz * seq, SSM_DINNER), BF16),
                   jax.ShapeDtypeStruct((bsz,) + st_shape[1:], F32)],
        scratch_shapes=conv_scratch + [
            pltpu.VMEM((nprob * CHUNK, 2 * LANES), F32),
            pltpu.VMEM((geom.nseq, SSM_PAIRS, SSM_DSTATE, 2 * SSM_HEADDIM), F32),
            pltpu.VMEM((tl, SSM_DINNER), F32)],
        compiler_params=_params(("parallel", "arbitrary")),
        name="ssd",
    )(mix, rest, rest, *conv_ops, h0, par, dch, nw, seld)


def _pad_lanes(v, offset):
    out = jnp.zeros((LANES,), F32)
    return out.at[offset:offset + v.shape[0]].set(v.astype(F32))


def _pair_selection(first_lane, lanes_per_head, n_mats):
    heads_per_mat = 2 * LANES // lanes_per_head
    sel = np.zeros((n_mats, 2 * LANES, 2 * LANES), np.float32)
    for m in range(n_mats):
        for j in range(heads_per_mat):
            src = first_lane + m * heads_per_mat + j
            sel[m, src, j * lanes_per_head:(j + 1) * lanes_per_head] = 1.0
            sel[m, LANES + src, j * lanes_per_head:(j + 1) * lanes_per_head] = 1.0
    return jnp.asarray(sel, BF16)


def _pad_conv_state(st):
    bsz, _, ch = st.shape
    return jnp.concatenate([jnp.zeros((bsz, CONV_PAD - (CONV_W - 1), ch), F32), st.astype(F32)], axis=1)


def _layer(x, mem_k, mem_v, s_gdn, c_gdn, s_ssm, c_ssm, w):
    bsz, seq, _ = x.shape
    t = bsz * seq
    xf = x.reshape(t, D_MODEL)
    h, u = _ffn1(xf, w["norm_ff1"], w["w_ff1_in"], w["w_ff1_out"], w["norm_mix"])
    rest = _in_proj(u, w["w_rest"], REST_TN, "in_proj_rest")
    cst8 = jnp.concatenate([_pad_conv_state(c_gdn), _pad_conv_state(c_ssm)], axis=2)
    if seq >= CHUNK:
        mix, tail = _in_proj_conv(u, w["w_mix"], w["conv_w"], w["conv_b"], cst8, seq)
        conv = None
    else:
        mix = _in_proj(u, w["w_mix"], MIX_TN, "in_proj_mix")
        tail = mix.reshape(bsz, seq, MIX_COLS)[:, seq - CONV_PAD:, :]
        conv = (cst8, w["conv_w"], w["conv_b"])

    yg, s_gdn_new = _gdn(mix, rest, conv, s_gdn.astype(F32), w["gdn_par"], w["gdn_norm"],
                         w["sel_gdn"], bsz, seq)
    h0 = s_ssm.astype(F32).reshape(bsz, SSM_PAIRS, 2 * SSM_HEADDIM, SSM_DSTATE)
    ys, s_ssm_new = _ssd(mix, rest, conv, h0, w["ssm_par"], w["ssm_d_ch"], w["ssm_norm"],
                         w["sel_ssm"], bsz, seq)
    ym = _memattn(rest, mem_k.reshape(bsz, N_MEM, MEM_WIDTH).astype(BF16),
                  mem_v.reshape(bsz, N_MEM, MEM_WIDTH).astype(BF16), bsz, seq)
    y = _merge_ffn2(h, yg, ys, ym, rest, w["w_branch"], w["w_out"], w["norm_ff2"],
                    w["w_ff2_in"], w["w_ff2_out"], w["norm_final"])

    c_gdn_new = tail[:, CONV_PAD - (CONV_W - 1):, MIX_QKV:MIX_QKV + GDN_CONV_CH]
    c_ssm_new = tail[:, CONV_PAD - (CONV_W - 1):, MIX_XBC:MIX_XBC + SSM_CONV_CH]
    return (y.reshape(bsz, seq, D_MODEL), s_gdn_new, c_gdn_new,
            s_ssm_new.reshape(bsz, SSM_HEADS, SSM_HEADDIM, SSM_DSTATE), c_ssm_new)


def _row(v):
    return v.astype(F32).reshape(1, -1)


def _prep_weights(l, norm_ff1, w_ff1_in, w_ff1_out, norm_mix, w_in, gdn_conv_w, gdn_a_log,
                  gdn_dt_bias, gdn_norm, ssm_conv_w, ssm_conv_b, ssm_a_log, ssm_dt_bias, ssm_d,
                  ssm_norm, w_branch, w_out, norm_ff2, w_ff2_in, w_ff2_out, norm_final):
    row = _row

    wi = w_in[l]
    o = np.cumsum((0,) + IN_SPLITS).tolist()
    small_w = jnp.concatenate(
        [wi[:, o[1]:o[3]], wi[:, o[6]:o[7]],
         jnp.zeros((D_MODEL, LANES - 2 * GDN_HEADS - SSM_HEADS), wi.dtype)], axis=1).astype(BF16)
    w_mix, w_rest = _pack_w_in(wi, small_w)

    def two_rows(a, b, offset):
        par = jnp.zeros((SUBLANES, LANES), F32)
        return par.at[0].set(_pad_lanes(a, offset)).at[1].set(_pad_lanes(b, offset))

    return {
        "norm_ff1": row(norm_ff1[l]), "w_ff1_in": w_ff1_in[l].astype(BF16),
        "w_ff1_out": w_ff1_out[l].astype(BF16), "norm_mix": row(norm_mix[l]),
        "w_mix": w_mix, "w_rest": w_rest,
        "conv_w": jnp.concatenate([gdn_conv_w[l], ssm_conv_w[l]], axis=1).astype(F32),
        "conv_b": jnp.concatenate([jnp.zeros((1, GDN_CONV_CH), F32), row(ssm_conv_b[l])], axis=1),
        "gdn_par": two_rows(gdn_a_log[l], gdn_dt_bias[l], SM_A),
        "gdn_norm": row(gdn_norm[l]),
        "sel_gdn": _pair_selection(0, LANES, GDN_PAIRS),
        "ssm_par": two_rows(ssm_a_log[l], ssm_dt_bias[l], SM_DT),
        "ssm_d_ch": jnp.repeat(ssm_d[l].astype(F32), SSM_HEADDIM).reshape(1, -1),
        "ssm_norm": row(ssm_norm[l]),
        "sel_ssm": _pair_selection(SM_DT, SSM_HEADDIM, SSM_PAIRS // 2),
        "w_branch": w_branch[l].astype(BF16), "w_out": w_out[l].astype(BF16),
        "norm_ff2": row(norm_ff2[l]), "w_ff2_in": w_ff2_in[l].astype(BF16),
        "w_ff2_out": w_ff2_out[l].astype(BF16), "norm_final": row(norm_final),
    }


def kernel(x_prompt, x_sample, mem_prompt, state_gdn, state_gdn_conv, state_ssm, state_ssm_conv,
           cache_mem_k, cache_mem_v, norm_ff1, w_ff1_in, w_ff1_out, norm_mix, w_in,
           gdn_conv_w, gdn_a_log, gdn_dt_bias, gdn_norm, ssm_conv_w, ssm_conv_b, ssm_a_log,
           ssm_dt_bias, ssm_d, ssm_norm, norm_mem, w_mem_kv, w_branch, w_out,
           norm_ff2, w_ff2_in, w_ff2_out, norm_final):
    assert w_in.shape[0] == 1, "the kernels implement the single-layer configuration"
    l = 0
    bp = x_prompt.shape[0]
    w = _prep_weights(l, norm_ff1, w_ff1_in, w_ff1_out, norm_mix, w_in, gdn_conv_w, gdn_a_log,
                      gdn_dt_bias, gdn_norm, ssm_conv_w, ssm_conv_b, ssm_a_log, ssm_dt_bias, ssm_d,
                      ssm_norm, w_branch, w_out, norm_ff2, w_ff2_in, w_ff2_out, norm_final)

    n_mem_tok = mem_prompt.shape[0] * mem_prompt.shape[1]
    kv = _memkv(mem_prompt.reshape(n_mem_tok, D_MODEL), _row(norm_mem[l]), w_mem_kv[l].astype(BF16))
    mk = kv[:, :MEM_WIDTH].reshape(bp, N_MEM, MEM_HEADS, MEM_HEAD_DIM)
    mv = kv[:, MEM_WIDTH:].reshape(bp, N_MEM, MEM_HEADS, MEM_HEAD_DIM)
    dtp = x_prompt.dtype
    yp, sgp, cgp, ssp, csp = _layer(
        x_prompt, mk, mv,
        jnp.zeros((bp, GDN_HEADS, GDN_DK, GDN_DV), dtp), jnp.zeros((bp, CONV_W - 1, GDN_CONV_CH), dtp),
        jnp.zeros((bp, SSM_HEADS, SSM_HEADDIM, SSM_DSTATE), dtp),
        jnp.zeros((bp, CONV_W - 1, SSM_CONV_CH), dtp), w)
    ys_, sgs, cgs, sss, css = _layer(
        x_sample, cache_mem_k[l], cache_mem_v[l], state_gdn[l], state_gdn_conv[l],
        state_ssm[l], state_ssm_conv[l], w)

    lead = lambda a: a[None]
    return (yp, ys_, lead(sgp), lead(cgp), lead(ssp), lead(csp), lead(mk), lead(mv),
            lead(sgs), lead(cgs), lead(sss), lead(css))
```

```python
import collections
import functools

import numpy as np
import jax
import jax.numpy as jnp
from jax import lax
from jax.experimental import pallas as pl
from jax.experimental.pallas import tpu as pltpu

F32 = jnp.float32
BF16 = jnp.bfloat16
EPS = 1e-6

D_MODEL = 1024
FFN_DIM = 2816
CONV_W = 4
CHUNK = 64
LANES = 128
SUBLANES = 8

GDN_HEADS = 8
GDN_PAIRS = GDN_HEADS // 2
GDN_DK = 128
GDN_DV = 128
GDN_QK = GDN_HEADS * GDN_DK
GDN_V = GDN_HEADS * GDN_DV
GDN_CONV_CH = 2 * GDN_QK + GDN_V

SSM_DINNER = 2048
SSM_HEADDIM = 64
SSM_HEADS = 32
SSM_GROUPS = 4
SSM_DSTATE = 128
SSM_CONV_CH = SSM_DINNER + 2 * SSM_GROUPS * SSM_DSTATE
SSM_PAIRS = SSM_HEADS // 2
SSM_GROUP_PAIRS = SSM_PAIRS // SSM_GROUPS
SSM_GROUP_CH = SSM_DINNER // SSM_GROUPS

N_MEM = 256
MEM_HEADS = 4
MEM_HEAD_DIM = 64
MEM_WIDTH = MEM_HEADS * MEM_HEAD_DIM

MIX_QKV = 0
MIX_XBC = MIX_QKV + GDN_CONV_CH
MIX_COLS = MIX_XBC + SSM_CONV_CH
REST_GATE = 0
REST_ZG = REST_GATE + 3 * D_MODEL
REST_ZS = REST_ZG + GDN_V
REST_QM = REST_ZS + SSM_DINNER
REST_SMALL = REST_QM + MEM_WIDTH
REST_COLS = REST_SMALL + LANES
IN_SPLITS = (GDN_CONV_CH, GDN_HEADS, GDN_HEADS, GDN_V, SSM_DINNER, SSM_CONV_CH, SSM_HEADS, MEM_WIDTH,
             3 * D_MODEL)
SM_A = 0
SM_B = GDN_HEADS
SM_DT = 2 * GDN_HEADS

VMEM_LIMIT = 56 * 1024 * 1024


def _params(semantics):
    return pltpu.CompilerParams(dimension_semantics=semantics, vmem_limit_bytes=VMEM_LIMIT)


def _mm(a, b):
    return jnp.dot(a, b, preferred_element_type=F32)


def _mm_nt(a, b):
    return lax.dot_general(a, b, (((1,), (1,)), ((), ())), preferred_element_type=F32)


def _mm_tn(a, b):
    return lax.dot_general(a, b, (((0,), (0,)), ((), ())), preferred_element_type=F32)


def _split3(x):
    hi = x.astype(BF16)
    r1 = x - hi.astype(F32)
    mid = r1.astype(BF16)
    lo = (r1 - mid.astype(F32)).astype(BF16)
    return hi, mid, lo


def _split2_lanes(x):
    hi = x.astype(BF16)
    lo = (x - hi.astype(F32)).astype(BF16)
    return jnp.concatenate([hi, lo], axis=1)


def _rms(x, g):
    return x * lax.rsqrt(jnp.mean(x * x, axis=-1, keepdims=True) + EPS) * g


def _sigmoid(x):
    return 1.0 / (1.0 + jnp.exp2(x * float(-1.0 / np.log(2.0))))


def _silu(x):
    return x * _sigmoid(x)


def _softplus(x):
    return jnp.maximum(x, 0.0) + jnp.log1p(jnp.exp(-jnp.abs(x)))


def _const_spec(shape):
    nd = len(shape)
    return pl.BlockSpec(shape, lambda *_: (0,) * nd, pipeline_mode=pl.Buffered(1))


FFN_TM = 512
FFN_TF = FFN_DIM // 2


def _ffn_compute(x, g_ref, wi_ref, wo_ref):
    xn = _rms(x, g_ref[...]).astype(BF16)
    acc = None
    for c in range(FFN_DIM // FFN_TF):
        lo = c * FFN_TF
        gate = _mm(xn, wi_ref[:, lo:lo + FFN_TF])
        up = _mm(xn, wi_ref[:, FFN_DIM + lo:FFN_DIM + lo + FFN_TF])
        act = (_silu(gate) * up).astype(BF16)
        part = _mm(act, wo_ref[lo:lo + FFN_TF, :])
        acc = part if acc is None else acc + part
    return x + 0.5 * acc


def _ffn1_body(x_ref, g_ref, wi_ref, wo_ref, g2_ref, h_ref, u_ref):
    h = _ffn_compute(x_ref[...], g_ref, wi_ref, wo_ref)
    h_ref[...] = h
    u_ref[...] = _rms(h, g2_ref[...]).astype(BF16)


def _ffn1(x, g, wi, wo, g2):
    t = x.shape[0]
    tm = min(FFN_TM, t)
    row = lambda i: (i, 0)
    return pl.pallas_call(
        _ffn1_body,
        grid=(t // tm,),
        in_specs=[pl.BlockSpec((tm, D_MODEL), row), _const_spec(g.shape), _const_spec(wi.shape),
                  _const_spec(wo.shape), _const_spec(g2.shape)],
        out_specs=[pl.BlockSpec((tm, D_MODEL), row), pl.BlockSpec((tm, D_MODEL), row)],
        out_shape=[jax.ShapeDtypeStruct((t, D_MODEL), F32), jax.ShapeDtypeStruct((t, D_MODEL), BF16)],
        compiler_params=_params(("parallel",)),
        name="ffn1",
    )(x, g, wi, wo, g2)


PACK_ROWS = 128


def _pack_body(w_ref, mix_ref, rest_ref):
    src = np.cumsum((0,) + IN_SPLITS).tolist()
    qkv, ab, _, zg, zs, xbc, dt, qm, gate = [(src[i], IN_SPLITS[i]) for i in range(len(IN_SPLITS))]
    for ref, dst, (lo, n) in ((mix_ref, MIX_QKV, qkv), (mix_ref, MIX_XBC, xbc),
                              (rest_ref, REST_GATE, gate), (rest_ref, REST_ZG, zg),
                              (rest_ref, REST_ZS, zs), (rest_ref, REST_QM, qm)):
        ref[:, dst:dst + n] = w_ref[:, lo:lo + n].astype(BF16)
    assert ab[0] % LANES == SM_A and dt[0] % LANES == SM_DT
    lane = lax.broadcasted_iota(jnp.int32, (w_ref.shape[0], LANES), 1)
    tile_ab = w_ref[:, ab[0] - SM_A:ab[0] - SM_A + LANES]
    tile_dt = w_ref[:, dt[0] - SM_DT:dt[0] - SM_DT + LANES]
    small = jnp.where(lane < SM_DT, tile_ab, jnp.where(lane < SM_DT + SSM_HEADS, tile_dt, 0.0))
    rest_ref[:, REST_SMALL:] = small.astype(BF16)


def _pack_w_in(w):
    rows = lambda i: (i, 0)
    return pl.pallas_call(
        _pack_body,
        grid=(D_MODEL // PACK_ROWS,),
        in_specs=[pl.BlockSpec((PACK_ROWS, w.shape[1]), rows)],
        out_specs=[pl.BlockSpec((PACK_ROWS, MIX_COLS), rows), pl.BlockSpec((PACK_ROWS, REST_COLS), rows)],
        out_shape=[jax.ShapeDtypeStruct((D_MODEL, MIX_COLS), BF16),
                   jax.ShapeDtypeStruct((D_MODEL, REST_COLS), BF16)],
        compiler_params=_params(("parallel",)),
        name="pack_w_in",
    )(w)


PROJ_TM = 1024
PROJ_ROWS = 256
MIX_TN = MIX_COLS // 4
REST_TN = REST_COLS // 3


def _proj_body(u_ref, w_ref, o_ref):
    o_ref[...] = _mm(u_ref[...], w_ref[...])


def _in_proj(u, w, tn, name):
    t = u.shape[0]
    tm = min(PROJ_TM, t)
    cols = w.shape[1]
    return pl.pallas_call(
        _proj_body,
        grid=(cols // tn, t // tm),
        in_specs=[pl.BlockSpec((tm, D_MODEL), lambda j, i: (i, 0)),
                  pl.BlockSpec((D_MODEL, tn), lambda j, i: (0, j))],
        out_specs=pl.BlockSpec((tm, tn), lambda j, i: (i, j)),
        out_shape=jax.ShapeDtypeStruct((t, cols), F32),
        compiler_params=_params(("parallel", "parallel")),
        name=name,
    )(u, w)


def _causal_conv_silu(xa, w, bias):
    shape = (SUBLANES, LANES)
    sub = lax.broadcasted_iota(jnp.int32, shape, 0)
    columns = []
    for c in range(xa.shape[1] // LANES):
        cols = slice(c * LANES, (c + 1) * LANES)
        taps = [jnp.broadcast_to(w[j:j + 1, cols], shape) for j in range(CONV_W)]
        b = jnp.broadcast_to(bias[:, cols], shape)
        before = xa[CONV_PAD - SUBLANES:CONV_PAD, cols]
        outs = []
        for g in range(CONV_PAD // SUBLANES, xa.shape[0] // SUBLANES):
            x = xa[g * SUBLANES:(g + 1) * SUBLANES, cols]
            y = x * taps[CONV_W - 1] + b
            for k in range(1, CONV_W):
                shifted = pltpu.roll(jnp.where(sub >= SUBLANES - k, before, x), k, axis=0)
                y = y + shifted * taps[CONV_W - 1 - k]
            outs.append(_silu(y))
            before = x
        columns.append(jnp.concatenate(outs, axis=0))
    return jnp.concatenate(columns, axis=1)


CONVPROJ_TM = 512
CONVPROJ_STEPS = 3
PROJ_STRIPS = 8


def _strips(width, n):
    tiles = width // LANES
    assert width % LANES == 0 and tiles >= n
    bounds = [LANES * ((tiles * k) // n) for k in range(n + 1)]
    return [slice(bounds[k], bounds[k + 1]) for k in range(n)]


def _proj_conv_body(u_ref, wm_ref, wr_ref, cw_ref, cb_ref, cst_ref, mix_ref, rest_ref, tail_ref,
                    carry, *, tiles_per_seq):
    i = pl.program_id(1)

    @pl.when(i % tiles_per_seq == 0)
    def _():
        carry[...] = cst_ref[0]

    strips_m = _strips(mix_ref.shape[1], PROJ_STRIPS)
    strips_r = _strips(rest_ref.shape[1], PROJ_STRIPS)
    prev = [carry[:, cm] for cm in strips_m]
    for r in range(u_ref.shape[0] // PROJ_ROWS):
        rows = slice(r * PROJ_ROWS, (r + 1) * PROJ_ROWS)
        u = u_ref[rows, :]
        for s, (cm, cr) in enumerate(zip(strips_m, strips_r)):
            raw = _mm(u, wm_ref[:, cm])
            rest_ref[rows, cr] = _mm(u, wr_ref[:, cr])
            mix_ref[rows, cm] = _causal_conv_silu(jnp.concatenate([prev[s], raw], axis=0),
                                                  cw_ref[:, cm], cb_ref[:, cm])
            prev[s] = raw[PROJ_ROWS - CONV_PAD:, :]
    for s, cm in enumerate(strips_m):
        carry[:, cm] = prev[s]
        tail_ref[0, :, cm] = prev[s]


def _in_proj_conv(u, w_mix, w_rest, cw, cb, cst8, seq):
    t = u.shape[0]
    tm = min(CONVPROJ_TM, seq)
    assert seq % tm == 0 and tm % PROJ_ROWS == 0
    tps = seq // tm
    tn_m = MIX_COLS // CONVPROJ_STEPS
    tn_r = REST_COLS // CONVPROJ_STEPS
    col = lambda j, i: (0, j)
    tile = lambda j, i: (i, j)
    seq_blk = lambda j, i: (i // tps, 0, j)
    return pl.pallas_call(
        functools.partial(_proj_conv_body, tiles_per_seq=tps),
        grid=(CONVPROJ_STEPS, t // tm),
        in_specs=[pl.BlockSpec((tm, D_MODEL), lambda j, i: (i, 0)),
                  pl.BlockSpec((D_MODEL, tn_m), col), pl.BlockSpec((D_MODEL, tn_r), col),
                  pl.BlockSpec((CONV_W, tn_m), col), pl.BlockSpec((1, tn_m), col),
                  pl.BlockSpec((1, CONV_PAD, tn_m), seq_blk)],
        out_specs=[pl.BlockSpec((tm, tn_m), tile), pl.BlockSpec((tm, tn_r), tile),
                   pl.BlockSpec((1, CONV_PAD, tn_m), seq_blk)],
        out_shape=[jax.ShapeDtypeStruct((t, MIX_COLS), F32), jax.ShapeDtypeStruct((t, REST_COLS), F32),
                   jax.ShapeDtypeStruct((t // seq, CONV_PAD, MIX_COLS), F32)],
        scratch_shapes=[pltpu.VMEM((CONV_PAD, tn_m), F32)],
        compiler_params=_params(("parallel", "arbitrary")),
        name="in_proj_conv",
    )(u, w_mix, w_rest, cw, cb, cst8)


def _memkv_body(x_ref, g_ref, w_ref, o_ref):
    o_ref[...] = _mm(_rms(x_ref[...], g_ref[...]).astype(BF16), w_ref[...])


def _memkv(mem, g, w):
    t = mem.shape[0]
    tm = min(512, t)
    row = lambda i: (i, 0)
    return pl.pallas_call(
        _memkv_body,
        grid=(t // tm,),
        in_specs=[pl.BlockSpec((tm, D_MODEL), row), _const_spec(g.shape), _const_spec(w.shape)],
        out_specs=pl.BlockSpec((tm, 2 * MEM_WIDTH), row),
        out_shape=jax.ShapeDtypeStruct((t, 2 * MEM_WIDTH), F32),
        compiler_params=_params(("parallel",)),
        name="memkv",
    )(mem, g, w)


MERGE_TM = 256


def _merge_ffn2_body(h_ref, yg_ref, ys_ref, ym_ref, gate_ref, wb_ref, wout_ref,
                     g_ref, wi_ref, wo_ref, gf_ref, y_ref):
    gates = gate_ref[...]
    merged = (_sigmoid(gates[:, 0:D_MODEL]) * _mm(yg_ref[...], wb_ref[0:GDN_V, :])
              + _sigmoid(gates[:, D_MODEL:2 * D_MODEL])
              * _mm(ys_ref[...], wb_ref[GDN_V:GDN_V + SSM_DINNER, :])
              + _sigmoid(gates[:, 2 * D_MODEL:3 * D_MODEL])
              * _mm(ym_ref[...], wb_ref[GDN_V + SSM_DINNER:, :]))
    h = h_ref[...] + _mm(merged.astype(BF16), wout_ref[...])
    h = _ffn_compute(h, g_ref, wi_ref, wo_ref)
    y_ref[...] = _rms(h, gf_ref[...])


def _merge_ffn2(h, yg, ys, ym, proj, wb, wout, g, wi, wo, gf):
    t = h.shape[0]
    tm = min(MERGE_TM, t)
    row = lambda i: (i, 0)
    gate_blk = REST_GATE // (3 * D_MODEL)
    return pl.pallas_call(
        _merge_ffn2_body,
        grid=(t // tm,),
        in_specs=[pl.BlockSpec((tm, D_MODEL), row), pl.BlockSpec((tm, GDN_V), row),
                  pl.BlockSpec((tm, SSM_DINNER), row), pl.BlockSpec((tm, MEM_WIDTH), row),
                  pl.BlockSpec((tm, 3 * D_MODEL), lambda i: (i, gate_blk)),
                  _const_spec(wb.shape), _const_spec(wout.shape), _const_spec(g.shape),
                  _const_spec(wi.shape), _const_spec(wo.shape), _const_spec(gf.shape)],
        out_specs=pl.BlockSpec((tm, D_MODEL), row),
        out_shape=jax.ShapeDtypeStruct((t, D_MODEL), F32),
        compiler_params=_params(("parallel",)),
        name="merge_ffn2",
    )(h, yg, ys, ym, proj, wb, wout, g, wi, wo, gf)


MEMATTN_TL = 512
MEMATTN_NSEQ = 16


def _memattn_body(q_ref, k_ref, v_ref, y_ref, *, nseq, rows):
    lane = lax.broadcasted_iota(jnp.int32, (rows, MEM_WIDTH), 1)
    masks = [(lane >= hh * MEM_HEAD_DIM) & (lane < (hh + 1) * MEM_HEAD_DIM) for hh in range(MEM_HEADS)]
    seqs = range(nseq)
    q = [q_ref[s * rows:(s + 1) * rows, :] for s in seqs]
    q4 = [jnp.concatenate([jnp.where(m, x, 0.0) for m in masks], axis=0).astype(BF16) for x in q]
    sc = [_mm_nt(q4[s], k_ref[s].astype(BF16)) * (MEM_HEAD_DIM ** -0.5) for s in seqs]
    p = [jnp.exp(x - jnp.max(x, axis=-1, keepdims=True)) for x in sc]
    p = [(x / jnp.sum(x, axis=-1, keepdims=True)).astype(BF16) for x in p]
    o4 = [_mm(p[s], v_ref[s].astype(BF16)) for s in seqs]
    ys = []
    for x in o4:
        y = jnp.where(masks[0], x[0:rows], 0.0)
        for hh in range(1, MEM_HEADS):
            y = y + jnp.where(masks[hh], x[hh * rows:(hh + 1) * rows], 0.0)
        ys.append(y)
    y_ref[...] = jnp.concatenate(ys, axis=0).astype(BF16)


def _memattn(proj, k, v, bsz, seq):
    if seq >= MEMATTN_TL:
        nseq, rows, nt = 1, MEMATTN_TL, seq // MEMATTN_TL
    else:
        nseq, rows, nt = MEMATTN_NSEQ, seq, 1
    assert bsz % nseq == 0 and seq % rows == 0
    tl = nseq * rows
    qm_blk = REST_QM // MEM_WIDTH
    kv_spec = pl.BlockSpec((nseq, N_MEM, MEM_WIDTH), lambda b, i: (b, 0, 0))
    return pl.pallas_call(
        functools.partial(_memattn_body, nseq=nseq, rows=rows),
        grid=(bsz // nseq, nt),
        in_specs=[pl.BlockSpec((tl, MEM_WIDTH), lambda b, i: (b * nt + i, qm_blk)), kv_spec, kv_spec],
        out_specs=pl.BlockSpec((tl, MEM_WIDTH), lambda b, i: (b * nt + i, 0)),
        out_shape=jax.ShapeDtypeStruct((bsz * seq, MEM_WIDTH), BF16),
        compiler_params=_params(("parallel", "arbitrary")),
        name="memattn",
    )(proj, k, v)


SEQ_TL = 256
SHORT_NSEQ = 4
CONV_PAD = SUBLANES

Geom = collections.namedtuple("Geom", ["nseq", "nch", "vrows"])


def _geom(bsz, seq):
    if seq >= CHUNK:
        assert seq % SEQ_TL == 0
        return Geom(1, SEQ_TL // CHUNK, CHUNK), seq // SEQ_TL
    assert bsz % SHORT_NSEQ == 0 and seq % SUBLANES == 0
    return Geom(SHORT_NSEQ, 1, seq), 1


def _short_conv_silu(x_ref, cst_ref, cw_ref, cb_ref, act, geom):
    assert geom.nch == 1 and geom.vrows < CHUNK
    act[...] = jnp.zeros_like(act)
    for cb in range(x_ref.shape[1] // LANES):
        cols = slice(cb * LANES, (cb + 1) * LANES)
        for s in range(geom.nseq):
            xa = jnp.concatenate([cst_ref[s, :, cols], x_ref[s * geom.vrows:(s + 1) * geom.vrows, cols]],
                                 axis=0)
            act[s * CHUNK:s * CHUNK + geom.vrows, cols] = _causal_conv_silu(
                xa, cw_ref[:, cols], cb_ref[:, cols])


def _store_gates(gates, geom, first, second):
    tls = geom.nch * geom.vrows
    if geom.vrows < CHUNK:
        gates[...] = jnp.zeros_like(gates)
    for s in range(geom.nseq):
        g0 = s * geom.nch * CHUNK
        gates[g0:g0 + tls, 0:LANES] = first[s * tls:(s + 1) * tls]
        gates[g0:g0 + tls, LANES:2 * LANES] = second[s * tls:(s + 1) * tls]


def _pair_masks():
    row = lax.broadcasted_iota(jnp.int32, (CHUNK, LANES), 0)
    lane = lax.broadcasted_iota(jnp.int32, (CHUNK, LANES), 1)
    col = jnp.where(lane < CHUNK, lane, lane - CHUNK)
    return row, lane, col


def _block_diag(x, left):
    zero = jnp.zeros_like(x)
    return jnp.concatenate([jnp.where(left, x, zero), jnp.where(left, zero, x)], axis=0)


def _chunk_cumsums(gates, nprob):
    tril = jnp.where(lax.broadcasted_iota(jnp.int32, (CHUNK, CHUNK), 0)
                     >= lax.broadcasted_iota(jnp.int32, (CHUNK, CHUNK), 1), 1.0, 0.0).astype(BF16)
    tril3 = jnp.concatenate([tril, tril, tril], axis=1)
    pieces = [jnp.concatenate(_split3(gates[pb * CHUNK:(pb + 1) * CHUNK, 0:LANES]), axis=0)
              for pb in range(nprob)]
    gc_all = _mm(tril3, jnp.concatenate(pieces, axis=1))
    return [gc_all[:, pb * LANES:(pb + 1) * LANES] for pb in range(nprob)]


def _row_form(gc):
    return jnp.concatenate([gc, pltpu.roll(gc, LANES - 1, axis=1)], axis=0).T


def _gdn_body(*refs, geom):
    if geom.vrows == CHUNK:
        (x_ref, sm_ref, z_ref, s0_ref, par_ref, gn_ref, selp_ref, y_ref, sout_ref,
         gates, s_scr, u_scr, wq_scr, kd_scr, qk_scr, egl_scr, o_scr) = refs
        act = x_ref
    else:
        (x_ref, sm_ref, z_ref, cst_ref, cw_ref, cb_ref, s0_ref, par_ref, gn_ref, selp_ref, y_ref,
         sout_ref, act, gates, s_scr, u_scr, wq_scr, kd_scr, qk_scr, egl_scr, o_scr) = refs
        _short_conv_silu(x_ref, cst_ref, cw_ref, cb_ref, act, geom)
    tile = pl.program_id(1)
    first = tile == 0
    nprob = geom.nseq * geom.nch
    vr = geom.vrows

    @pl.when(first)
    def _():
        s_scr[...] = s0_ref[...]

    sm = sm_ref[...]
    lane_g = lax.broadcasted_iota(jnp.int32, sm.shape, 1)
    head_lane = lane_g < GDN_HEADS
    log_decay = jnp.where(head_lane, -jnp.exp(par_ref[0:1, :]) * _softplus(sm + par_ref[1:2, :]), 0.0)
    beta = jnp.where(head_lane, pltpu.roll(_sigmoid(sm), LANES - SM_B, axis=1), 0.0)
    _store_gates(gates, geom, log_decay, beta)

    row, lane, col = _pair_masks()
    left = lane < CHUNK
    incl = row >= col
    strict = row > col
    eye = jnp.where(row == col, 1.0, 0.0).astype(F32)
    zeros_h = jnp.zeros((CHUNK, LANES), F32)
    hcols = lambda h: slice(h * LANES, (h + 1) * LANES)
    prows = lambda pb, n=CHUNK: slice(pb * n, (pb + 1) * n)

    gcs = _chunk_cumsums(gates, nprob)
    lhs, xts = [], []
    for pb in range(nprob):
        gc = gcs[pb]
        gl = jnp.broadcast_to(gc[CHUNK - 1:CHUNK, :], gc.shape)
        quantities = jnp.concatenate(
            [gc, gates[prows(pb), LANES:2 * LANES], jnp.exp(gc), jnp.exp(gl - gc)], axis=0)
        lhs.append(_split2_lanes(quantities))
        xts.append(_row_form(gc))

    items = [(pb, p) for pb in range(nprob) for p in range(GDN_PAIRS)]
    exp_ = [_mm(lhs[pb], selp_ref[p]) for pb, p in items]

    def l2n(t):
        return t * lax.rsqrt(jnp.sum(t * t, axis=-1, keepdims=True) + EPS)

    def stacked(e, qi):
        return jnp.concatenate([e[qi * CHUNK:(qi + 1) * CHUNK, 0:LANES],
                                e[qi * CHUNK:(qi + 1) * CHUNK, LANES:]], axis=0)

    decay_l, kq_l, rhs_l, qd_l, kd_l = [], [], [], [], []
    for (pb, p), e in zip(items, exp_):
        ha, hb = 2 * p, 2 * p + 1
        qa, qb = [l2n(act[prows(pb), hcols(h)]) * (GDN_DK ** -0.5) for h in (ha, hb)]
        ka, kb = [l2n(act[prows(pb), hcols(GDN_HEADS + h)]) for h in (ha, hb)]
        va, vb = [act[prows(pb), hcols(2 * GDN_HEADS + h)] for h in (ha, hb)]
        beta2, eg2, kdec2 = stacked(e, 1), stacked(e, 2), stacked(e, 3)
        k2 = jnp.concatenate([ka, kb], axis=0)
        kbeta2 = k2 * beta2
        gci = jnp.where(left, e[0:CHUNK, 0:LANES], e[0:CHUNK, LANES:])
        gcj = jnp.broadcast_to(xts[pb][ha:ha + 1, :], (CHUNK, LANES))
        decay_l.append(jnp.where(incl, jnp.exp(jnp.where(incl, gci - gcj, 0.0)), 0.0))
        k_bd = jnp.concatenate([jnp.concatenate([ka, zeros_h], axis=1),
                                jnp.concatenate([zeros_h, kb], axis=1)], axis=0).astype(BF16)
        kbq = jnp.concatenate([jnp.concatenate([kbeta2[0:CHUNK], kbeta2[CHUNK:]], axis=1),
                               jnp.concatenate([qa, qb], axis=1)], axis=0).astype(BF16)
        kq_l.append(_mm_nt(kbq, k_bd))
        rhs_l.append(jnp.concatenate(
            [jnp.concatenate([va, vb], axis=0) * beta2, kbeta2 * eg2], axis=1).astype(BF16))
        qd_l.append(jnp.concatenate([qa, qb], axis=0) * eg2)
        kd_l.append(k2 * kdec2)
        egl_scr[prows(pb, SUBLANES), hcols(ha)] = jnp.broadcast_to(
            e[3 * CHUNK - 1:3 * CHUNK, 0:LANES], (SUBLANES, LANES))
        egl_scr[prows(pb, SUBLANES), hcols(hb)] = jnp.broadcast_to(
            e[3 * CHUNK - 1:3 * CHUNK, LANES:], (SUBLANES, LANES))

    n_pow = [-jnp.where(strict, kq[0:CHUNK] * d, 0.0) for kq, d in zip(kq_l, decay_l)]
    t_inv = [eye + n for n in n_pow]
    for _ in range(int(np.ceil(np.log2(vr))) - 1):
        n_pow = [_mm(n.astype(BF16), _block_diag(n, left).astype(BF16)) for n in n_pow]
        t_inv = [t + _mm(t.astype(BF16), _block_diag(n, left).astype(BF16))
                 for t, n in zip(t_inv, n_pow)]

    uw_l = [_mm(_block_diag(t, left).astype(BF16), rhs) for t, rhs in zip(t_inv, rhs_l)]
    for (pb, p), uw, kq, d, qd, kd in zip(items, uw_l, kq_l, decay_l, qd_l, kd_l):
        qk_scr[prows(pb, 2 * CHUNK), hcols(p)] = _block_diag(kq[CHUNK:] * d, left).astype(BF16)
        for idx, h in enumerate((2 * p, 2 * p + 1)):
            sl = slice(idx * CHUNK, (idx + 1) * CHUNK)
            u_scr[prows(pb), hcols(h)] = uw[sl, 0:LANES]
            wq_scr[prows(pb, 2 * CHUNK), hcols(h)] = jnp.concatenate(
                [uw[sl, LANES:], qd[sl]], axis=0).astype(BF16)
            kd_scr[prows(pb), hcols(h)] = kd[sl].astype(BF16)

    for c in range(geom.nch):
        sh = [(s, h) for s in range(geom.nseq) for h in range(GDN_HEADS)]
        pbs = lambda s: s * geom.nch + c
        wqs = [_mm(wq_scr[prows(pbs(s), 2 * CHUNK), hcols(h)], s_scr[s, h].astype(BF16))
               for s, h in sh]
        vnew = [u_scr[prows(pbs(s)), hcols(h)] - w[0:CHUNK] for (s, h), w in zip(sh, wqs)]
        for (s, h), vn in zip(sh, vnew):
            egl = egl_scr[pbs(s) * SUBLANES:pbs(s) * SUBLANES + 1, hcols(h)]
            s_scr[s, h] = s_scr[s, h] * egl + _mm_tn(kd_scr[prows(pbs(s)), hcols(h)], vn.astype(BF16))
        for s in range(geom.nseq):
            for p in range(GDN_PAIRS):
                i0 = s * GDN_HEADS + 2 * p
                vn2 = jnp.concatenate([vnew[i0], vnew[i0 + 1]], axis=0).astype(BF16)
                intra = _mm(qk_scr[prows(pbs(s), 2 * CHUNK), hcols(p)], vn2)
                for idx in range(2):
                    o = wqs[i0 + idx][CHUNK:] + intra[idx * CHUNK:(idx + 1) * CHUNK]
                    o_scr[prows(pbs(s), vr), hcols(2 * p + idx)] = o[0:vr]

    for h in range(GDN_HEADS):
        y = _rms(o_scr[:, hcols(h)], gn_ref[...]) * _silu(z_ref[:, hcols(h)])
        y_ref[:, hcols(h)] = y.astype(BF16)

    @pl.when(tile == pl.num_programs(1) - 1)
    def _():
        sout_ref[...] = s_scr[...]


def _conv_operands(conv, geom, width, col_blk):
    if conv is None:
        return [], [], []
    cst8, cw, cb = conv
    specs = [pl.BlockSpec((geom.nseq, CONV_PAD, width), lambda b, i: (b, 0, col_blk)),
             pl.BlockSpec((CONV_W, width), lambda b, i: (0, col_blk)),
             pl.BlockSpec((1, width), lambda b, i: (0, col_blk))]
    return [cst8, cw, cb], specs, [pltpu.VMEM((geom.nseq * CHUNK, width), F32)]


def _gdn(mix, rest, conv, s0, par, gn, selp, bsz, seq):
    geom, nt = _geom(bsz, seq)
    nprob = geom.nseq * geom.nch
    tl = nprob * geom.vrows
    tok = lambda blk: (lambda b, i: (b * nt + i, blk))
    st_spec = pl.BlockSpec((geom.nseq, GDN_HEADS, GDN_DK, GDN_DV), lambda b, i: (b, 0, 0, 0))
    conv_ops, conv_specs, conv_scratch = _conv_operands(conv, geom, GDN_CONV_CH, MIX_QKV // GDN_CONV_CH)
    return pl.pallas_call(
        functools.partial(_gdn_body, geom=geom),
        grid=(bsz // geom.nseq, nt),
        in_specs=[pl.BlockSpec((tl, GDN_CONV_CH), tok(MIX_QKV // GDN_CONV_CH)),
                  pl.BlockSpec((tl, LANES), tok(REST_SMALL // LANES)),
                  pl.BlockSpec((tl, GDN_V), tok(REST_ZG // GDN_V))] + conv_specs + [
                  st_spec, _const_spec(par.shape), _const_spec(gn.shape), _const_spec(selp.shape)],
        out_specs=[pl.BlockSpec((tl, GDN_V), lambda b, i: (b * nt + i, 0)), st_spec],
        out_shape=[jax.ShapeDtypeStruct((bsz * seq, GDN_V), BF16),
                   jax.ShapeDtypeStruct((bsz, GDN_HEADS, GDN_DK, GDN_DV), F32)],
        scratch_shapes=conv_scratch + [
            pltpu.VMEM((nprob * CHUNK, 2 * LANES), F32),
            pltpu.VMEM((geom.nseq, GDN_HEADS, GDN_DK, GDN_DV), F32),
            pltpu.VMEM((nprob * CHUNK, GDN_V), F32),
            pltpu.VMEM((nprob * 2 * CHUNK, GDN_V), BF16),
            pltpu.VMEM((nprob * CHUNK, GDN_V), BF16),
            pltpu.VMEM((nprob * 2 * CHUNK, GDN_PAIRS * LANES), BF16),
            pltpu.VMEM((nprob * SUBLANES, GDN_V), F32),
            pltpu.VMEM((tl, GDN_V), F32)],
        compiler_params=_params(("parallel", "arbitrary")),
        name="gdn",
    )(mix, rest, rest, *conv_ops, s0, par, gn, selp)


def _ssd_body(*refs, geom):
    if geom.vrows == CHUNK:
        (x_ref, sm_ref, z_ref, h0_ref, par_ref, dch_ref, nw_ref, seld_ref, y_ref, hout_ref,
         gates, h_scr, o_scr) = refs
        act = x_ref
    else:
        (x_ref, sm_ref, z_ref, cst_ref, cw_ref, cb_ref, h0_ref, par_ref, dch_ref, nw_ref, seld_ref,
         y_ref, hout_ref, act, gates, h_scr, o_scr) = refs
        _short_conv_silu(x_ref, cst_ref, cw_ref, cb_ref, act, geom)
    tile = pl.program_id(1)
    first = tile == 0
    nprob = geom.nseq * geom.nch
    vr = geom.vrows

    @pl.when(first)
    def _():
        for s in range(geom.nseq):
            for p in range(SSM_PAIRS):
                h_scr[s, p] = h0_ref[s, p].T

    sm = sm_ref[...]
    lane_g = lax.broadcasted_iota(jnp.int32, sm.shape, 1)
    is_dt = (lane_g >= SM_DT) & (lane_g < SM_DT + SSM_HEADS)
    dt = jnp.where(is_dt, _softplus(sm + par_ref[1:2, :]), 0.0)
    log_decay = dt * jnp.where(is_dt, -jnp.exp(par_ref[0:1, :]), 0.0)
    _store_gates(gates, geom, log_decay, dt)

    row, lane, col = _pair_masks()
    left = lane < CHUNK
    incl = row >= col
    pcols = lambda p: slice(p * LANES, (p + 1) * LANES)
    prows = lambda pb, n=CHUNK: slice(pb * n, (pb + 1) * n)
    quarter = lambda e, qi, j: e[qi * CHUNK:(qi + 1) * CHUNK, j * LANES:(j + 1) * LANES]
    b_off = SSM_DINNER
    c_off = SSM_DINNER + SSM_GROUPS * SSM_DSTATE

    gcs = _chunk_cumsums(gates, nprob)
    for pb in range(nprob):
        s = pb // geom.nch
        gc = gcs[pb]
        dt_c = gates[prows(pb), LANES:2 * LANES]
        gl = jnp.broadcast_to(gc[CHUNK - 1:CHUNK, :], gc.shape)
        lhs = _split2_lanes(jnp.concatenate([gc, dt_c, jnp.exp(gc), jnp.exp(gl - gc) * dt_c], axis=0))
        xt = _row_form(gc)
        exp_ = [_mm(lhs, seld_ref[d]) for d in range(SSM_PAIRS // 2)]

        b_bf = [act[prows(pb), b_off + g * SSM_DSTATE:b_off + (g + 1) * SSM_DSTATE].astype(BF16)
                for g in range(SSM_GROUPS)]
        c_bf = [act[prows(pb), c_off + g * SSM_DSTATE:c_off + (g + 1) * SSM_DSTATE].astype(BF16)
                for g in range(SSM_GROUPS)]
        cb2 = [_mm_nt(c, jnp.concatenate([b, b], axis=0)) for b, c in zip(b_bf, c_bf)]

        pairs = range(SSM_PAIRS)
        grp = lambda p: p // SSM_GROUP_PAIRS
        x_l = [act[prows(pb), pcols(p)] for p in pairs]
        gc_l = [quarter(exp_[p // 2], 0, p % 2) for p in pairs]
        eg_l = [quarter(exp_[p // 2], 2, p % 2) for p in pairs]
        xdt_l = [x * quarter(exp_[p // 2], 1, p % 2) for p, x in zip(pairs, x_l)]
        xw_l = [x * quarter(exp_[p // 2], 3, p % 2) for p, x in zip(pairs, x_l)]
        m_l = []
        for p in pairs:
            r = SM_DT + 2 * p
            gcj = jnp.broadcast_to(xt[r:r + 1, :], (CHUNK, LANES))
            decay = jnp.where(incl, jnp.exp(jnp.where(incl, gc_l[p] - gcj, 0.0)), 0.0)
            m_l.append((cb2[grp(p)] * decay).astype(BF16))
        intra = [_mm(m, _block_diag(xdt, left).astype(BF16)) for m, xdt in zip(m_l, xdt_l)]
        dstate = [_mm_tn(b_bf[grp(p)], xw_l[p].astype(BF16)) for p in pairs]
        inter = [_mm(c_bf[grp(p)], h_scr[s, p].astype(BF16)) for p in pairs]
        for p in pairs:
            h_scr[s, p] = h_scr[s, p] * eg_l[p][CHUNK - 1:CHUNK, :] + dstate[p]

        for g in range(SSM_GROUPS):
            ys = []
            ssq = None
            for p in range(g * SSM_GROUP_PAIRS, (g + 1) * SSM_GROUP_PAIRS):
                y = intra[p] + inter[p] * eg_l[p] + dch_ref[:, pcols(p)] * x_l[p]
                y = y[0:vr] * _silu(z_ref[prows(pb, vr), pcols(p)])
                ys.append(y)
                sq = jnp.sum(y * y, axis=-1, keepdims=True)
                ssq = sq if ssq is None else ssq + sq
            inv = lax.rsqrt(ssq * (1.0 / SSM_GROUP_CH) + EPS)
            for pp, y in enumerate(ys):
                o_scr[prows(pb, vr), pcols(g * SSM_GROUP_PAIRS + pp)] = y * inv

    y_ref[...] = (o_scr[...] * nw_ref[...]).astype(BF16)

    @pl.when(tile == pl.num_programs(1) - 1)
    def _():
        for s in range(geom.nseq):
            for p in range(SSM_PAIRS):
                hout_ref[s, p] = h_scr[s, p].T


def _ssd(mix, rest, conv, h0, par, dch, nw, seld, bsz, seq):
    geom, nt = _geom(bsz, seq)
    nprob = geom.nseq * geom.nch
    tl = nprob * geom.vrows
    tok = lambda blk: (lambda b, i: (b * nt + i, blk))
    st_shape = (geom.nseq, SSM_PAIRS, 2 * SSM_HEADDIM, SSM_DSTATE)
    st_spec = pl.BlockSpec(st_shape, lambda b, i: (b, 0, 0, 0))
    conv_ops, conv_specs, conv_scratch = _conv_operands(conv, geom, SSM_CONV_CH, MIX_XBC // SSM_CONV_CH)
    return pl.pallas_call(
        functools.partial(_ssd_body, geom=geom),
        grid=(bsz // geom.nseq, nt),
        in_specs=[pl.BlockSpec((tl, SSM_CONV_CH), tok(MIX_XBC // SSM_CONV_CH)),
                  pl.BlockSpec((tl, LANES), tok(REST_SMALL // LANES)),
                  pl.BlockSpec((tl, SSM_DINNER), tok(REST_ZS // SSM_DINNER))] + conv_specs + [
                  st_spec, _const_spec(par.shape), _const_spec(dch.shape), _const_spec(nw.shape),
                  _const_spec(seld.shape)],
        out_specs=[pl.BlockSpec((tl, SSM_DINNER), lambda b, i: (b * nt + i, 0)), st_spec],
        out_shape=[jax.ShapeDtypeStruct((bsz * seq, SSM_DINNER), BF16),
                   jax.ShapeDtypeStruct((bsz,) + st_shape[1:], F32)],
        scratch_shapes=conv_scratch + [
            pltpu.VMEM((nprob * CHUNK, 2 * LANES), F32),
            pltpu.VMEM((geom.nseq, SSM_PAIRS, SSM_DSTATE, 2 * SSM_HEADDIM), F32),
            pltpu.VMEM((tl, SSM_DINNER), F32)],
        compiler_params=_params(("parallel", "arbitrary")),
        name="ssd",
    )(mix, rest, rest, *conv_ops, h0, par, dch, nw, seld)


def _pad_lanes(v, offset):
    out = jnp.zeros((LANES,), F32)
    return out.at[offset:offset + v.shape[0]].set(v.astype(F32))


def _pair_selection(first_lane, lanes_per_head, n_mats):
    heads_per_mat = 2 * LANES // lanes_per_head
    sel = np.zeros((n_mats, 2 * LANES, 2 * LANES), np.float32)
    for m in range(n_mats):
        for j in range(heads_per_mat):
            src = first_lane + m * heads_per_mat + j
            sel[m, src, j * lanes_per_head:(j + 1) * lanes_per_head] = 1.0
            sel[m, LANES + src, j * lanes_per_head:(j + 1) * lanes_per_head] = 1.0
    return jnp.asarray(sel, BF16)


def _pad_conv_state(st):
    bsz, _, ch = st.shape
    return jnp.concatenate([jnp.zeros((bsz, CONV_PAD - (CONV_W - 1), ch), F32), st.astype(F32)], axis=1)


def _layer(x, mem_k, mem_v, s_gdn, c_gdn, s_ssm, c_ssm, w):
    bsz, seq, _ = x.shape
    t = bsz * seq
    xf = x.reshape(t, D_MODEL)
    h, u = _ffn1(xf, w["norm_ff1"], w["w_ff1_in"], w["w_ff1_out"], w["norm_mix"])
    cst8 = jnp.concatenate([_pad_conv_state(c_gdn), _pad_conv_state(c_ssm)], axis=2)
    if seq >= CHUNK:
        mix, rest, tail = _in_proj_conv(u, w["w_mix"], w["w_rest"], w["conv_w"], w["conv_b"], cst8, seq)
        conv = None
    else:
        rest = _in_proj(u, w["w_rest"], REST_TN, "in_proj_rest")
        mix = _in_proj(u, w["w_mix"], MIX_TN, "in_proj_mix")
        tail = mix.reshape(bsz, seq, MIX_COLS)[:, seq - CONV_PAD:, :]
        conv = (cst8, w["conv_w"], w["conv_b"])

    yg, s_gdn_new = _gdn(mix, rest, conv, s_gdn.astype(F32), w["gdn_par"], w["gdn_norm"],
                         w["sel_gdn"], bsz, seq)
    h0 = s_ssm.astype(F32).reshape(bsz, SSM_PAIRS, 2 * SSM_HEADDIM, SSM_DSTATE)
    ys, s_ssm_new = _ssd(mix, rest, conv, h0, w["ssm_par"], w["ssm_d_ch"], w["ssm_norm"],
                         w["sel_ssm"], bsz, seq)
    ym = _memattn(rest, mem_k.reshape(bsz, N_MEM, MEM_WIDTH), mem_v.reshape(bsz, N_MEM, MEM_WIDTH),
                  bsz, seq)
    y = _merge_ffn2(h, yg, ys, ym, rest, w["w_branch"], w["w_out"], w["norm_ff2"],
                    w["w_ff2_in"], w["w_ff2_out"], w["norm_final"])

    c_gdn_new = tail[:, CONV_PAD - (CONV_W - 1):, MIX_QKV:MIX_QKV + GDN_CONV_CH]
    c_ssm_new = tail[:, CONV_PAD - (CONV_W - 1):, MIX_XBC:MIX_XBC + SSM_CONV_CH]
    return (y.reshape(bsz, seq, D_MODEL), s_gdn_new, c_gdn_new,
            s_ssm_new.reshape(bsz, SSM_HEADS, SSM_HEADDIM, SSM_DSTATE), c_ssm_new)


def _row(v):
    return v.astype(F32).reshape(1, -1)


def _prep_weights(l, norm_ff1, w_ff1_in, w_ff1_out, norm_mix, w_in, gdn_conv_w, gdn_a_log,
                  gdn_dt_bias, gdn_norm, ssm_conv_w, ssm_conv_b, ssm_a_log, ssm_dt_bias, ssm_d,
                  ssm_norm, w_branch, w_out, norm_ff2, w_ff2_in, w_ff2_out, norm_final):
    row = _row

    w_mix, w_rest = _pack_w_in(w_in[l])

    def two_rows(a, b, offset):
        par = jnp.zeros((SUBLANES, LANES), F32)
        return par.at[0].set(_pad_lanes(a, offset)).at[1].set(_pad_lanes(b, offset))

    return {
        "norm_ff1": row(norm_ff1[l]), "w_ff1_in": w_ff1_in[l].astype(BF16),
        "w_ff1_out": w_ff1_out[l].astype(BF16), "norm_mix": row(norm_mix[l]),
        "w_mix": w_mix, "w_rest": w_rest,
        "conv_w": jnp.concatenate([gdn_conv_w[l], ssm_conv_w[l]], axis=1).astype(F32),
        "conv_b": jnp.concatenate([jnp.zeros((1, GDN_CONV_CH), F32), row(ssm_conv_b[l])], axis=1),
        "gdn_par": two_rows(gdn_a_log[l], gdn_dt_bias[l], SM_A),
        "gdn_norm": row(gdn_norm[l]),
        "sel_gdn": _pair_selection(0, LANES, GDN_PAIRS),
        "ssm_par": two_rows(ssm_a_log[l], ssm_dt_bias[l], SM_DT),
        "ssm_d_ch": jnp.repeat(ssm_d[l].astype(F32), SSM_HEADDIM).reshape(1, -1),
        "ssm_norm": row(ssm_norm[l]),
        "sel_ssm": _pair_selection(SM_DT, SSM_HEADDIM, SSM_PAIRS // 2),
        "w_branch": w_branch[l].astype(BF16), "w_out": w_out[l].astype(BF16),
        "norm_ff2": row(norm_ff2[l]), "w_ff2_in": w_ff2_in[l].astype(BF16),
        "w_ff2_out": w_ff2_out[l].astype(BF16), "norm_final": row(norm_final),
    }


def kernel(x_prompt, x_sample, mem_prompt, state_gdn, state_gdn_conv, state_ssm, state_ssm_conv,
           cache_mem_k, cache_mem_v, norm_ff1, w_ff1_in, w_ff1_out, norm_mix, w_in,
           gdn_conv_w, gdn_a_log, gdn_dt_bias, gdn_norm, ssm_conv_w, ssm_conv_b, ssm_a_log,
           ssm_dt_bias, ssm_d, ssm_norm, norm_mem, w_mem_kv, w_branch, w_out,
           norm_ff2, w_ff2_in, w_ff2_out, norm_final):
    assert w_in.shape[0] == 1, "the kernels implement the single-layer configuration"
    l = 0
    bp = x_prompt.shape[0]
    w = _prep_weights(l, norm_ff1, w_ff1_in, w_ff1_out, norm_mix, w_in, gdn_conv_w, gdn_a_log,
                      gdn_dt_bias, gdn_norm, ssm_conv_w, ssm_conv_b, ssm_a_log, ssm_dt_bias, ssm_d,
                      ssm_norm, w_branch, w_out, norm_ff2, w_ff2_in, w_ff2_out, norm_final)

    n_mem_tok = mem_prompt.shape[0] * mem_prompt.shape[1]
    kv = _memkv(mem_prompt.reshape(n_mem_tok, D_MODEL), _row(norm_mem[l]), w_mem_kv[l].astype(BF16))
    mk = kv[:, :MEM_WIDTH].reshape(bp, N_MEM, MEM_HEADS, MEM_HEAD_DIM)
    mv = kv[:, MEM_WIDTH:].reshape(bp, N_MEM, MEM_HEADS, MEM_HEAD_DIM)
    dtp = x_prompt.dtype
    yp, sgp, cgp, ssp, csp = _layer(
        x_prompt, mk, mv,
        jnp.zeros((bp, GDN_HEADS, GDN_DK, GDN_DV), dtp), jnp.zeros((bp, CONV_W - 1, GDN_CONV_CH), dtp),
        jnp.zeros((bp, SSM_HEADS, SSM_HEADDIM, SSM_DSTATE), dtp),
        jnp.zeros((bp, CONV_W - 1, SSM_CONV_CH), dtp), w)
    ys_, sgs, cgs, sss, css = _layer(
        x_sample, cache_mem_k[l], cache_mem_v[l], state_gdn[l], state_gdn_conv[l],
        state_ssm[l], state_ssm_conv[l], w)

    lead = lambda a: a[None]
    return (yp, ys_, lead(sgp), lead(cgp), lead(ssp), lead(csp), lead(mk), lead(mv),
            lead(sgs), lead(cgs), lead(sss), lead(css))
```

```python
import collections
import functools

import numpy as np
import jax
import jax.numpy as jnp
from jax import lax
from jax.experimental import pallas as pl
from jax.experimental.pallas import tpu as pltpu

F32 = jnp.float32
BF16 = jnp.bfloat16
EPS = 1e-6

D_MODEL = 1024
FFN_DIM = 2816
CONV_W = 4
CHUNK = 64
LANES = 128
SUBLANES = 8

GDN_HEADS = 8
GDN_PAIRS = GDN_HEADS // 2
GDN_DK = 128
GDN_DV = 128
GDN_QK = GDN_HEADS * GDN_DK
GDN_V = GDN_HEADS * GDN_DV
GDN_CONV_CH = 2 * GDN_QK + GDN_V

SSM_DINNER = 2048
SSM_HEADDIM = 64
SSM_HEADS = 32
SSM_GROUPS = 4
SSM_DSTATE = 128
SSM_CONV_CH = SSM_DINNER + 2 * SSM_GROUPS * SSM_DSTATE
SSM_PAIRS = SSM_HEADS // 2
SSM_GROUP_PAIRS = SSM_PAIRS // SSM_GROUPS
SSM_GROUP_CH = SSM_DINNER // SSM_GROUPS

N_MEM = 256
MEM_HEADS = 4
MEM_HEAD_DIM = 64
MEM_WIDTH = MEM_HEADS * MEM_HEAD_DIM

MIX_QKV = 0
MIX_XBC = MIX_QKV + GDN_CONV_CH
MIX_COLS = MIX_XBC + SSM_CONV_CH
REST_GATE = 0
REST_ZG = REST_GATE + 3 * D_MODEL
REST_ZS = REST_ZG + GDN_V
REST_QM = REST_ZS + SSM_DINNER
REST_SMALL = REST_QM + MEM_WIDTH
REST_COLS = REST_SMALL + LANES
IN_SPLITS = (GDN_CONV_CH, GDN_HEADS, GDN_HEADS, GDN_V, SSM_DINNER, SSM_CONV_CH, SSM_HEADS, MEM_WIDTH,
             3 * D_MODEL)
SM_A = 0
SM_B = GDN_HEADS
SM_DT = 2 * GDN_HEADS

VMEM_LIMIT = 56 * 1024 * 1024


def _params(semantics):
    return pltpu.CompilerParams(dimension_semantics=semantics, vmem_limit_bytes=VMEM_LIMIT)


def _mm(a, b):
    return jnp.dot(a, b, preferred_element_type=F32)


def _mm_nt(a, b):
    return lax.dot_general(a, b, (((1,), (1,)), ((), ())), preferred_element_type=F32)


def _mm_tn(a, b):
    return lax.dot_general(a, b, (((0,), (0,)), ((), ())), preferred_element_type=F32)


def _split3(x):
    hi = x.astype(BF16)
    r1 = x - hi.astype(F32)
    mid = r1.astype(BF16)
    lo = (r1 - mid.astype(F32)).astype(BF16)
    return hi, mid, lo


def _split2_lanes(x):
    hi = x.astype(BF16)
    lo = (x - hi.astype(F32)).astype(BF16)
    return jnp.concatenate([hi, lo], axis=1)


def _rms(x, g):
    return x * lax.rsqrt(jnp.mean(x * x, axis=-1, keepdims=True) + EPS) * g


def _sigmoid(x):
    return 1.0 / (1.0 + jnp.exp2(x * float(-1.0 / np.log(2.0))))


def _silu(x):
    return x * _sigmoid(x)


def _softplus(x):
    return jnp.maximum(x, 0.0) + jnp.log1p(jnp.exp(-jnp.abs(x)))


def _const_spec(shape):
    nd = len(shape)
    return pl.BlockSpec(shape, lambda *_: (0,) * nd, pipeline_mode=pl.Buffered(1))


FFN_TM = 512
MXU_WIDTH = 256
FFN_CHUNKS = ((0, 6 * MXU_WIDTH), (6 * MXU_WIDTH, FFN_DIM))


def _ffn_compute(x, g_ref, wi_ref, wo_ref):
    xn = _rms(x, g_ref[...]).astype(BF16)
    acc = None
    for lo, hi in FFN_CHUNKS:
        gate = _mm(xn, wi_ref[:, lo:hi])
        up = _mm(xn, wi_ref[:, FFN_DIM + lo:FFN_DIM + hi])
        act = (_silu(gate) * up).astype(BF16)
        part = _mm(act, wo_ref[lo:hi, :])
        acc = part if acc is None else acc + part
    return x + 0.5 * acc


def _ffn1_body(x_ref, g_ref, wi_ref, wo_ref, g2_ref, h_ref, u_ref):
    h = _ffn_compute(x_ref[...], g_ref, wi_ref, wo_ref)
    h_ref[...] = h
    u_ref[...] = _rms(h, g2_ref[...]).astype(BF16)


def _ffn1(x, g, wi, wo, g2):
    t = x.shape[0]
    tm = min(FFN_TM, t)
    row = lambda i: (i, 0)
    return pl.pallas_call(
        _ffn1_body,
        grid=(t // tm,),
        in_specs=[pl.BlockSpec((tm, D_MODEL), row), _const_spec(g.shape), _const_spec(wi.shape),
                  _const_spec(wo.shape), _const_spec(g2.shape)],
        out_specs=[pl.BlockSpec((tm, D_MODEL), row), pl.BlockSpec((tm, D_MODEL), row)],
        out_shape=[jax.ShapeDtypeStruct((t, D_MODEL), F32), jax.ShapeDtypeStruct((t, D_MODEL), BF16)],
        compiler_params=_params(("parallel",)),
        name="ffn1",
    )(x, g, wi, wo, g2)


PACK_ROWS = 128


def _pack_body(w_ref, mix_ref, rest_ref):
    src = np.cumsum((0,) + IN_SPLITS).tolist()
    qkv, ab, _, zg, zs, xbc, dt, qm, gate = [(src[i], IN_SPLITS[i]) for i in range(len(IN_SPLITS))]
    for ref, dst, (lo, n) in ((mix_ref, MIX_QKV, qkv), (mix_ref, MIX_XBC, xbc),
                              (rest_ref, REST_GATE, gate), (rest_ref, REST_ZG, zg),
                              (rest_ref, REST_ZS, zs), (rest_ref, REST_QM, qm)):
        ref[:, dst:dst + n] = w_ref[:, lo:lo + n].astype(BF16)
    assert ab[0] % LANES == SM_A and dt[0] % LANES == SM_DT
    lane = lax.broadcasted_iota(jnp.int32, (w_ref.shape[0], LANES), 1)
    tile_ab = w_ref[:, ab[0] - SM_A:ab[0] - SM_A + LANES]
    tile_dt = w_ref[:, dt[0] - SM_DT:dt[0] - SM_DT + LANES]
    small = jnp.where(lane < SM_DT, tile_ab, jnp.where(lane < SM_DT + SSM_HEADS, tile_dt, 0.0))
    rest_ref[:, REST_SMALL:] = small.astype(BF16)


def _pack_w_in(w, l):
    rows = lambda i: (i, 0)
    return pl.pallas_call(
        _pack_body,
        grid=(D_MODEL // PACK_ROWS,),
        in_specs=[pl.BlockSpec((None, PACK_ROWS, w.shape[2]), lambda i: (l, i, 0))],
        out_specs=[pl.BlockSpec((PACK_ROWS, MIX_COLS), rows), pl.BlockSpec((PACK_ROWS, REST_COLS), rows)],
        out_shape=[jax.ShapeDtypeStruct((D_MODEL, MIX_COLS), BF16),
                   jax.ShapeDtypeStruct((D_MODEL, REST_COLS), BF16)],
        compiler_params=_params(("parallel",)),
        name="pack_w_in",
    )(w)


PROJ_TM = 1024
PROJ_ROWS = 256
MIX_TN = MIX_COLS // 4
REST_TN = REST_COLS // 3


def _proj_body(u_ref, w_ref, o_ref):
    o_ref[...] = _mm(u_ref[...], w_ref[...])


def _in_proj(u, w, tn, name):
    t = u.shape[0]
    tm = min(PROJ_TM, t)
    cols = w.shape[1]
    return pl.pallas_call(
        _proj_body,
        grid=(cols // tn, t // tm),
        in_specs=[pl.BlockSpec((tm, D_MODEL), lambda j, i: (i, 0)),
                  pl.BlockSpec((D_MODEL, tn), lambda j, i: (0, j))],
        out_specs=pl.BlockSpec((tm, tn), lambda j, i: (i, j)),
        out_shape=jax.ShapeDtypeStruct((t, cols), F32),
        compiler_params=_params(("parallel", "parallel")),
        name=name,
    )(u, w)


def _causal_conv_silu(xa, w, bias):
    shape = (SUBLANES, LANES)
    sub = lax.broadcasted_iota(jnp.int32, shape, 0)
    columns = []
    for c in range(xa.shape[1] // LANES):
        cols = slice(c * LANES, (c + 1) * LANES)
        taps = [jnp.broadcast_to(w[j:j + 1, cols], shape) for j in range(CONV_W)]
        b = jnp.broadcast_to(bias[:, cols], shape)
        before = xa[CONV_PAD - SUBLANES:CONV_PAD, cols]
        outs = []
        for g in range(CONV_PAD // SUBLANES, xa.shape[0] // SUBLANES):
            x = xa[g * SUBLANES:(g + 1) * SUBLANES, cols]
            y = x * taps[CONV_W - 1] + b
            for k in range(1, CONV_W):
                shifted = pltpu.roll(jnp.where(sub >= SUBLANES - k, before, x), k, axis=0)
                y = y + shifted * taps[CONV_W - 1 - k]
            outs.append(_silu(y))
            before = x
        columns.append(jnp.concatenate(outs, axis=0))
    return jnp.concatenate(columns, axis=1)


CONVPROJ_TM = 512
CONVPROJ_STEPS = 3
PROJ_STRIPS = 8


def _strips(width, n):
    tiles = width // LANES
    assert width % LANES == 0 and tiles >= n
    bounds = [LANES * ((tiles * k) // n) for k in range(n + 1)]
    return [slice(bounds[k], bounds[k + 1]) for k in range(n)]


def _proj_conv_body(u_ref, wm_ref, wr_ref, cw_ref, cb_ref, cst_ref, mix_ref, rest_ref, tail_ref,
                    carry, *, tiles_per_seq):
    i = pl.program_id(1)

    @pl.when(i % tiles_per_seq == 0)
    def _():
        carry[0:CONV_PAD, :] = cst_ref[0]

    strips_m = _strips(mix_ref.shape[1], PROJ_STRIPS)
    strips_r = _strips(rest_ref.shape[1], PROJ_STRIPS)
    for r in range(u_ref.shape[0] // PROJ_ROWS):
        rows = slice(r * PROJ_ROWS, (r + 1) * PROJ_ROWS)
        u = u_ref[rows, :]
        for cm, cr in zip(strips_m, strips_r):
            carry[CONV_PAD:, cm] = _mm(u, wm_ref[:, cm])
            rest_ref[rows, cr] = _mm(u, wr_ref[:, cr])
            mix_ref[rows, cm] = _causal_conv_silu(carry.at[:, cm], cw_ref[:, cm], cb_ref[:, cm])
            carry[0:CONV_PAD, cm] = carry[PROJ_ROWS:, cm]
    tail_ref[0] = carry[0:CONV_PAD, :]


def _in_proj_conv(u, w_mix, w_rest, cw, cb, cst8, seq):
    t = u.shape[0]
    tm = min(CONVPROJ_TM, seq)
    assert seq % tm == 0 and tm % PROJ_ROWS == 0
    tps = seq // tm
    tn_m = MIX_COLS // CONVPROJ_STEPS
    tn_r = REST_COLS // CONVPROJ_STEPS
    col = lambda j, i: (0, j)
    tile = lambda j, i: (i, j)
    seq_blk = lambda j, i: (i // tps, 0, j)
    return pl.pallas_call(
        functools.partial(_proj_conv_body, tiles_per_seq=tps),
        grid=(CONVPROJ_STEPS, t // tm),
        in_specs=[pl.BlockSpec((tm, D_MODEL), lambda j, i: (i, 0)),
                  pl.BlockSpec((D_MODEL, tn_m), col), pl.BlockSpec((D_MODEL, tn_r), col),
                  pl.BlockSpec((CONV_W, tn_m), col), pl.BlockSpec((1, tn_m), col),
                  pl.BlockSpec((1, CONV_PAD, tn_m), seq_blk)],
        out_specs=[pl.BlockSpec((tm, tn_m), tile), pl.BlockSpec((tm, tn_r), tile),
                   pl.BlockSpec((1, CONV_PAD, tn_m), seq_blk)],
        out_shape=[jax.ShapeDtypeStruct((t, MIX_COLS), F32), jax.ShapeDtypeStruct((t, REST_COLS), F32),
                   jax.ShapeDtypeStruct((t // seq, CONV_PAD, MIX_COLS), F32)],
        scratch_shapes=[pltpu.VMEM((CONV_PAD + PROJ_ROWS, tn_m), F32)],
        compiler_params=_params(("parallel", "arbitrary")),
        name="in_proj_conv",
    )(u, w_mix, w_rest, cw, cb, cst8)


def _memkv_body(x_ref, g_ref, w_ref, o_ref):
    o_ref[...] = _mm(_rms(x_ref[...], g_ref[...]).astype(BF16), w_ref[...])


def _memkv(mem, g, w):
    t = mem.shape[0]
    tm = min(512, t)
    row = lambda i: (i, 0)
    return pl.pallas_call(
        _memkv_body,
        grid=(t // tm,),
        in_specs=[pl.BlockSpec((tm, D_MODEL), row), _const_spec(g.shape), _const_spec(w.shape)],
        out_specs=pl.BlockSpec((tm, 2 * MEM_WIDTH), row),
        out_shape=jax.ShapeDtypeStruct((t, 2 * MEM_WIDTH), F32),
        compiler_params=_params(("parallel",)),
        name="memkv",
    )(mem, g, w)


MERGE_TM = 256


def _merge_ffn2_body(h_ref, yg_ref, ys_ref, ym_ref, gate_ref, wb_ref, wout_ref,
                     g_ref, wi_ref, wo_ref, gf_ref, y_ref):
    gates = gate_ref[...]
    merged = (_sigmoid(gates[:, 0:D_MODEL]) * _mm(yg_ref[...], wb_ref[0:GDN_V, :])
              + _sigmoid(gates[:, D_MODEL:2 * D_MODEL])
              * _mm(ys_ref[...], wb_ref[GDN_V:GDN_V + SSM_DINNER, :])
              + _sigmoid(gates[:, 2 * D_MODEL:3 * D_MODEL])
              * _mm(ym_ref[...], wb_ref[GDN_V + SSM_DINNER:, :]))
    h = h_ref[...] + _mm(merged.astype(BF16), wout_ref[...])
    h = _ffn_compute(h, g_ref, wi_ref, wo_ref)
    y_ref[...] = _rms(h, gf_ref[...])


def _merge_ffn2(h, yg, ys, ym, proj, wb, wout, g, wi, wo, gf):
    t = h.shape[0]
    tm = min(MERGE_TM, t)
    row = lambda i: (i, 0)
    gate_blk = REST_GATE // (3 * D_MODEL)
    return pl.pallas_call(
        _merge_ffn2_body,
        grid=(t // tm,),
        in_specs=[pl.BlockSpec((tm, D_MODEL), row), pl.BlockSpec((tm, GDN_V), row),
                  pl.BlockSpec((tm, SSM_DINNER), row), pl.BlockSpec((tm, MEM_WIDTH), row),
                  pl.BlockSpec((tm, 3 * D_MODEL), lambda i: (i, gate_blk)),
                  _const_spec(wb.shape), _const_spec(wout.shape), _const_spec(g.shape),
                  _const_spec(wi.shape), _const_spec(wo.shape), _const_spec(gf.shape)],
        out_specs=pl.BlockSpec((tm, D_MODEL), row),
        out_shape=jax.ShapeDtypeStruct((t, D_MODEL), F32),
        compiler_params=_params(("parallel",)),
        name="merge_ffn2",
    )(h, yg, ys, ym, proj, wb, wout, g, wi, wo, gf)


MEMATTN_TL = 512
MEMATTN_NSEQ = 16


def _memattn_body(q_ref, k_ref, v_ref, y_ref, *, nseq, rows):
    lane = lax.broadcasted_iota(jnp.int32, (rows, MEM_WIDTH), 1)
    masks = [(lane >= hh * MEM_HEAD_DIM) & (lane < (hh + 1) * MEM_HEAD_DIM) for hh in range(MEM_HEADS)]
    seqs = range(nseq)
    q = [q_ref[s * rows:(s + 1) * rows, :] for s in seqs]
    q4 = [jnp.concatenate([jnp.where(m, x, 0.0) for m in masks], axis=0).astype(BF16) for x in q]
    sc = [_mm_nt(q4[s], k_ref[s].astype(BF16)) * (MEM_HEAD_DIM ** -0.5) for s in seqs]
    p = [jnp.exp(x - jnp.max(x, axis=-1, keepdims=True)) for x in sc]
    p = [(x / jnp.sum(x, axis=-1, keepdims=True)).astype(BF16) for x in p]
    o4 = [_mm(p[s], v_ref[s].astype(BF16)) for s in seqs]
    ys = []
    for x in o4:
        y = jnp.where(masks[0], x[0:rows], 0.0)
        for hh in range(1, MEM_HEADS):
            y = y + jnp.where(masks[hh], x[hh * rows:(hh + 1) * rows], 0.0)
        ys.append(y)
    y_ref[...] = jnp.concatenate(ys, axis=0).astype(BF16)


def _memattn(proj, k, v, bsz, seq):
    if seq >= MEMATTN_TL:
        nseq, rows, nt = 1, MEMATTN_TL, seq // MEMATTN_TL
    else:
        nseq, rows, nt = MEMATTN_NSEQ, seq, 1
    assert bsz % nseq == 0 and seq % rows == 0
    tl = nseq * rows
    qm_blk = REST_QM // MEM_WIDTH
    kv_spec = pl.BlockSpec((nseq, N_MEM, MEM_WIDTH), lambda b, i: (b, 0, 0))
    return pl.pallas_call(
        functools.partial(_memattn_body, nseq=nseq, rows=rows),
        grid=(bsz // nseq, nt),
        in_specs=[pl.BlockSpec((tl, MEM_WIDTH), lambda b, i: (b * nt + i, qm_blk)), kv_spec, kv_spec],
        out_specs=pl.BlockSpec((tl, MEM_WIDTH), lambda b, i: (b * nt + i, 0)),
        out_shape=jax.ShapeDtypeStruct((bsz * seq, MEM_WIDTH), BF16),
        compiler_params=_params(("parallel", "arbitrary")),
        name="memattn",
    )(proj, k, v)


SEQ_TL = 256
SHORT_NSEQ = 8
CONV_PAD = SUBLANES
MIN_MXU_ROWS = 16

Geom = collections.namedtuple("Geom", ["nseq", "nch", "vrows"])


def _geom(bsz, seq):
    if seq >= CHUNK:
        assert seq % SEQ_TL == 0
        return Geom(1, SEQ_TL // CHUNK, CHUNK), seq // SEQ_TL
    assert bsz % SHORT_NSEQ == 0 and seq % SUBLANES == 0 and (SHORT_NSEQ * seq) % CHUNK == 0
    return Geom(SHORT_NSEQ, 1, seq), 1


def _short_conv_silu(x_ref, cst_ref, cw_ref, cb_ref, act, geom):
    assert geom.nch == 1 and geom.vrows < CHUNK
    vr = geom.vrows
    for cb in range(x_ref.shape[1] // LANES):
        cols = slice(cb * LANES, (cb + 1) * LANES)
        for s in range(geom.nseq):
            xa = jnp.concatenate([cst_ref[s, :, cols], x_ref[s * vr:(s + 1) * vr, cols]], axis=0)
            act[s * vr:(s + 1) * vr, cols] = _causal_conv_silu(xa, cw_ref[:, cols], cb_ref[:, cols])


def _pair_masks(vr):
    row = lax.broadcasted_iota(jnp.int32, (CHUNK, LANES), 0)
    lane = lax.broadcasted_iota(jnp.int32, (CHUNK, LANES), 1)
    col = jnp.where(lane < CHUNK, lane, lane - CHUNK)
    same = (row // vr) == (col // vr)
    return row, lane, col, same


def _block_diag(x, left):
    zero = jnp.zeros_like(x)
    return jnp.concatenate([jnp.where(left, x, zero), jnp.where(left, zero, x)], axis=0)


def _chunk_cumsums(gates, nprob, vr):
    r = lax.broadcasted_iota(jnp.int32, (CHUNK, CHUNK), 0)
    c = lax.broadcasted_iota(jnp.int32, (CHUNK, CHUNK), 1)
    same = (r // vr) == (c // vr)
    sums = jnp.concatenate([jnp.where(same & (r >= c), 1.0, 0.0), jnp.where(same, 1.0, 0.0)],
                           axis=0).astype(BF16)
    sums3 = jnp.concatenate([sums, sums, sums], axis=1)
    pieces = [jnp.concatenate(_split3(gates[pb * CHUNK:(pb + 1) * CHUNK, 0:LANES]), axis=0)
              for pb in range(nprob)]
    out = _mm(sums3, jnp.concatenate(pieces, axis=1))
    cols = lambda pb: slice(pb * LANES, (pb + 1) * LANES)
    return ([out[0:CHUNK, cols(pb)] for pb in range(nprob)],
            [out[CHUNK:, cols(pb)] for pb in range(nprob)])


def _pad_rows(x):
    if x.shape[0] >= MIN_MXU_ROWS:
        return x
    return jnp.concatenate([x, jnp.zeros((MIN_MXU_ROWS - x.shape[0], x.shape[1]), x.dtype)], axis=0)


def _row_form(gc):
    return jnp.concatenate([gc, pltpu.roll(gc, LANES - 1, axis=1)], axis=0).T


def _gdn_body(*refs, geom):
    if geom.vrows == CHUNK:
        (x_ref, sm_ref, z_ref, s0_ref, par_ref, gn_ref, selp_ref, y_ref, sout_ref,
         gates, s_scr, u_scr, w_scr, qd_scr, kd_scr, qk_scr, egl_scr, o_scr) = refs
        act = x_ref
    else:
        (x_ref, sm_ref, z_ref, cst_ref, cw_ref, cb_ref, s0_ref, par_ref, gn_ref, selp_ref, y_ref,
         sout_ref, act, gates, s_scr, u_scr, w_scr, qd_scr, kd_scr, qk_scr, egl_scr, o_scr) = refs
        _short_conv_silu(x_ref, cst_ref, cw_ref, cb_ref, act, geom)
    tile = pl.program_id(1)
    first = tile == 0
    vr = geom.vrows
    nsub = CHUNK // vr
    nprob = geom.nseq * geom.nch // nsub

    @pl.when(first)
    def _():
        s_scr[...] = s0_ref[...]

    sm = sm_ref[...]
    lane_g = lax.broadcasted_iota(jnp.int32, sm.shape, 1)
    head_lane = lane_g < GDN_HEADS
    gates[:, 0:LANES] = jnp.where(
        head_lane, -jnp.exp(par_ref[0:1, :]) * _softplus(sm + par_ref[1:2, :]), 0.0)
    gates[:, LANES:] = jnp.where(head_lane, pltpu.roll(_sigmoid(sm), LANES - SM_B, axis=1), 0.0)

    row, lane, col, same = _pair_masks(vr)
    left = lane < CHUNK
    incl = same & (row >= col)
    strict = same & (row > col)
    eye = jnp.where(row == col, 1.0, 0.0).astype(F32)
    zeros_h = jnp.zeros((CHUNK, LANES), F32)
    hcols = lambda h: slice(h * LANES, (h + 1) * LANES)
    prows = lambda pb, n=CHUNK: slice(pb * n, (pb + 1) * n)

    gcs, gls = _chunk_cumsums(gates, nprob, vr)
    lhs, xts = [], []
    for pb in range(nprob):
        gc = gcs[pb]
        quantities = jnp.concatenate(
            [gc, gates[prows(pb), LANES:2 * LANES], jnp.exp(gc), jnp.exp(gls[pb] - gc)], axis=0)
        lhs.append(_split2_lanes(quantities))
        xts.append(_row_form(gc))

    items = [(pb, p) for pb in range(nprob) for p in range(GDN_PAIRS)]
    exp_ = [_mm(lhs[pb], selp_ref[p]) for pb, p in items]

    def l2n(t):
        return t * lax.rsqrt(jnp.sum(t * t, axis=-1, keepdims=True) + EPS)

    def stacked(e, qi):
        return jnp.concatenate([e[qi * CHUNK:(qi + 1) * CHUNK, 0:LANES],
                                e[qi * CHUNK:(qi + 1) * CHUNK, LANES:]], axis=0)

    decay_l, kq_l, rhs_l, qd_l, kd_l = [], [], [], [], []
    for (pb, p), e in zip(items, exp_):
        ha, hb = 2 * p, 2 * p + 1
        qa, qb = [l2n(act[prows(pb), hcols(h)]) * (GDN_DK ** -0.5) for h in (ha, hb)]
        ka, kb = [l2n(act[prows(pb), hcols(GDN_HEADS + h)]) for h in (ha, hb)]
        va, vb = [act[prows(pb), hcols(2 * GDN_HEADS + h)] for h in (ha, hb)]
        beta2, eg2, kdec2 = stacked(e, 1), stacked(e, 2), stacked(e, 3)
        k2 = jnp.concatenate([ka, kb], axis=0)
        kbeta2 = k2 * beta2
        gci = jnp.where(left, e[0:CHUNK, 0:LANES], e[0:CHUNK, LANES:])
        gcj = jnp.broadcast_to(xts[pb][ha:ha + 1, :], (CHUNK, LANES))
        decay_l.append(jnp.where(incl, jnp.exp(jnp.where(incl, gci - gcj, 0.0)), 0.0))
        k_bd = jnp.concatenate([jnp.concatenate([ka, zeros_h], axis=1),
                                jnp.concatenate([zeros_h, kb], axis=1)], axis=0).astype(BF16)
        kbq = jnp.concatenate([jnp.concatenate([kbeta2[0:CHUNK], kbeta2[CHUNK:]], axis=1),
                               jnp.concatenate([qa, qb], axis=1)], axis=0).astype(BF16)
        kq_l.append(_mm_nt(kbq, k_bd))
        rhs_l.append(jnp.concatenate(
            [jnp.concatenate([va, vb], axis=0) * beta2, kbeta2 * eg2], axis=1).astype(BF16))
        qd_l.append(jnp.concatenate([qa, qb], axis=0) * eg2)
        kd_l.append(k2 * kdec2)
        for sub in range(nsub):
            last = 2 * CHUNK + (sub + 1) * vr - 1
            slot = prows(pb * nsub + sub, SUBLANES)
            egl_scr[slot, hcols(ha)] = jnp.broadcast_to(e[last:last + 1, 0:LANES], (SUBLANES, LANES))
            egl_scr[slot, hcols(hb)] = jnp.broadcast_to(e[last:last + 1, LANES:], (SUBLANES, LANES))

    n_pow = [-jnp.where(strict, kq[0:CHUNK] * d, 0.0) for kq, d in zip(kq_l, decay_l)]
    t_inv = [eye + n for n in n_pow]
    for _ in range(int(np.ceil(np.log2(vr))) - 1):
        n_pow = [_mm(n.astype(BF16), _block_diag(n, left).astype(BF16)) for n in n_pow]
        t_inv = [t + _mm(t.astype(BF16), _block_diag(n, left).astype(BF16))
                 for t, n in zip(t_inv, n_pow)]

    uw_l = [_mm(_block_diag(t, left).astype(BF16), rhs) for t, rhs in zip(t_inv, rhs_l)]
    for (pb, p), uw, kq, d, qd, kd in zip(items, uw_l, kq_l, decay_l, qd_l, kd_l):
        qk_scr[prows(pb, 2 * CHUNK), hcols(p)] = _block_diag(kq[CHUNK:] * d, left).astype(BF16)
        for idx, h in enumerate((2 * p, 2 * p + 1)):
            sl = slice(idx * CHUNK, (idx + 1) * CHUNK)
            u_scr[prows(pb), hcols(h)] = uw[sl, 0:LANES]
            w_scr[prows(pb), hcols(h)] = uw[sl, LANES:]
            qd_scr[prows(pb), hcols(h)] = qd[sl]
            kd_scr[prows(pb), hcols(h)] = kd[sl]

    for c in range(geom.nch):
        sh = [(s, h) for s in range(geom.nseq) for h in range(GDN_HEADS)]
        trows = lambda s: prows(s * geom.nch + c, vr)
        wqs = [_mm(jnp.concatenate([w_scr[trows(s), hcols(h)], qd_scr[trows(s), hcols(h)]],
                                   axis=0).astype(BF16), s_scr[s, h].astype(BF16)) for s, h in sh]
        vnew = [u_scr[trows(s), hcols(h)] - w[0:vr] for (s, h), w in zip(sh, wqs)]
        for (s, h), vn in zip(sh, vnew):
            slot = (s * geom.nch + c) * SUBLANES
            s_scr[s, h] = (s_scr[s, h] * egl_scr[slot:slot + 1, hcols(h)]
                           + _mm_tn(_pad_rows(kd_scr[trows(s), hcols(h)]).astype(BF16),
                                    _pad_rows(vn).astype(BF16)))
        for pb in range(c * nsub, geom.nseq * geom.nch, geom.nch * nsub):
            subs = [pb // geom.nch + j for j in range(nsub)]
            for p in range(GDN_PAIRS):
                heads = (2 * p, 2 * p + 1)
                vn2 = jnp.concatenate([vnew[s * GDN_HEADS + h] for h in heads for s in subs],
                                      axis=0).astype(BF16)
                intra = _mm(qk_scr[prows((pb // nsub), 2 * CHUNK), hcols(p)], vn2)
                for idx, h in enumerate(heads):
                    for j, s in enumerate(subs):
                        r0 = idx * CHUNK + j * vr
                        o = wqs[s * GDN_HEADS + h][vr:] + intra[r0:r0 + vr]
                        o_scr[trows(s), hcols(h)] = o

    for h in range(GDN_HEADS):
        y = _rms(o_scr[:, hcols(h)], gn_ref[...]) * _silu(z_ref[:, hcols(h)])
        y_ref[:, hcols(h)] = y.astype(BF16)

    @pl.when(tile == pl.num_programs(1) - 1)
    def _():
        sout_ref[...] = s_scr[...]


def _conv_operands(conv, geom, width, col_blk):
    if conv is None:
        return [], [], []
    cst8, cw, cb = conv
    specs = [pl.BlockSpec((geom.nseq, CONV_PAD, width), lambda b, i: (b, 0, col_blk)),
             pl.BlockSpec((CONV_W, width), lambda b, i: (0, col_blk)),
             pl.BlockSpec((1, width), lambda b, i: (0, col_blk))]
    return [cst8, cw, cb], specs, [pltpu.VMEM((geom.nseq * geom.vrows, width), F32)]


def _gdn(mix, rest, conv, s0, par, gn, selp, bsz, seq):
    geom, nt = _geom(bsz, seq)
    tl = geom.nseq * geom.nch * geom.vrows
    nprob = tl // CHUNK
    tok = lambda blk: (lambda b, i: (b * nt + i, blk))
    st_spec = pl.BlockSpec((geom.nseq, GDN_HEADS, GDN_DK, GDN_DV), lambda b, i: (b, 0, 0, 0))
    conv_ops, conv_specs, conv_scratch = _conv_operands(conv, geom, GDN_CONV_CH, MIX_QKV // GDN_CONV_CH)
    return pl.pallas_call(
        functools.partial(_gdn_body, geom=geom),
        grid=(bsz // geom.nseq, nt),
        in_specs=[pl.BlockSpec((tl, GDN_CONV_CH), tok(MIX_QKV // GDN_CONV_CH)),
                  pl.BlockSpec((tl, LANES), tok(REST_SMALL // LANES)),
                  pl.BlockSpec((tl, GDN_V), tok(REST_ZG // GDN_V))] + conv_specs + [
                  st_spec, _const_spec(par.shape), _const_spec(gn.shape), _const_spec(selp.shape)],
        out_specs=[pl.BlockSpec((tl, GDN_V), lambda b, i: (b * nt + i, 0)), st_spec],
        out_shape=[jax.ShapeDtypeStruct((bsz * seq, GDN_V), BF16),
                   jax.ShapeDtypeStruct((bsz, GDN_HEADS, GDN_DK, GDN_DV), F32)],
        scratch_shapes=conv_scratch + [
            pltpu.VMEM((tl, 2 * LANES), F32),
            pltpu.VMEM((geom.nseq, GDN_HEADS, GDN_DK, GDN_DV), F32),
            pltpu.VMEM((tl, GDN_V), F32),
            pltpu.VMEM((tl, GDN_V), F32),
            pltpu.VMEM((tl, GDN_V), F32),
            pltpu.VMEM((tl, GDN_V), F32),
            pltpu.VMEM((nprob * 2 * CHUNK, GDN_PAIRS * LANES), BF16),
            pltpu.VMEM((geom.nseq * geom.nch * SUBLANES, GDN_V), F32),
            pltpu.VMEM((tl, GDN_V), F32)],
        compiler_params=_params(("parallel", "arbitrary")),
        name="gdn",
    )(mix, rest, rest, *conv_ops, s0, par, gn, selp)


def _ssd_body(*refs, geom):
    if geom.vrows == CHUNK:
        (x_ref, sm_ref, z_ref, h0_ref, par_ref, dch_ref, nw_ref, seld_ref, y_ref, hout_ref,
         gates, h_scr, o_scr) = refs
        act = x_ref
        tile = pl.program_id(1)

        @pl.when(tile == 0)
        def _():
            for s in range(geom.nseq):
                for p in range(SSM_PAIRS):
                    h_scr[s, p] = h0_ref[s, p].T

        def get_h(s, p):
            return h_scr[s, p]

        def set_h(s, p, value):
            h_scr[s, p] = value
    else:
        (x_ref, sm_ref, z_ref, cst_ref, cw_ref, cb_ref, h0_ref, par_ref, dch_ref, nw_ref, seld_ref,
         y_ref, hout_ref, act, gates, o_scr) = refs
        _short_conv_silu(x_ref, cst_ref, cw_ref, cb_ref, act, geom)

        def get_h(s, p):
            return h0_ref[s, p].T

        def set_h(s, p, value):
            hout_ref[s, p] = value.T
    vr = geom.vrows
    nsub = CHUNK // vr
    nprob = geom.nseq * geom.nch // nsub

    sm = sm_ref[...]
    lane_g = lax.broadcasted_iota(jnp.int32, sm.shape, 1)
    is_dt = (lane_g >= SM_DT) & (lane_g < SM_DT + SSM_HEADS)
    dt = jnp.where(is_dt, _softplus(sm + par_ref[1:2, :]), 0.0)
    gates[:, 0:LANES] = dt * jnp.where(is_dt, -jnp.exp(par_ref[0:1, :]), 0.0)
    gates[:, LANES:] = dt

    row, lane, col, same = _pair_masks(vr)
    left = lane < CHUNK
    incl = same & (row >= col)
    pcols = lambda p: slice(p * LANES, (p + 1) * LANES)
    prows = lambda pb, n=CHUNK: slice(pb * n, (pb + 1) * n)
    quarter = lambda e, qi, j: e[qi * CHUNK:(qi + 1) * CHUNK, j * LANES:(j + 1) * LANES]
    srows = lambda a, j: a[j * vr:(j + 1) * vr]
    mxu_rows = lambda a: _pad_rows(a).astype(BF16)
    b_off = SSM_DINNER
    c_off = SSM_DINNER + SSM_GROUPS * SSM_DSTATE

    gcs, gls = _chunk_cumsums(gates, nprob, vr)
    for pb in range(nprob):
        seqs = [(pb * nsub + j) // geom.nch for j in range(nsub)]
        gc = gcs[pb]
        dt_c = gates[prows(pb), LANES:2 * LANES]
        lhs = _split2_lanes(jnp.concatenate(
            [gc, dt_c, jnp.exp(gc), jnp.exp(gls[pb] - gc) * dt_c], axis=0))
        xt = _row_form(gc)
        exp_ = [_mm(lhs, seld_ref[d]) for d in range(SSM_PAIRS // 2)]

        b_f = [act[prows(pb), b_off + g * SSM_DSTATE:b_off + (g + 1) * SSM_DSTATE]
               for g in range(SSM_GROUPS)]
        c_f = [act[prows(pb), c_off + g * SSM_DSTATE:c_off + (g + 1) * SSM_DSTATE]
               for g in range(SSM_GROUPS)]
        b_bf = [b.astype(BF16) for b in b_f]
        c_bf = [c.astype(BF16) for c in c_f]
        cb2 = [_mm_nt(c, jnp.concatenate([b, b], axis=0)) for b, c in zip(b_bf, c_bf)]

        pairs = range(SSM_PAIRS)
        grp = lambda p: p // SSM_GROUP_PAIRS
        x_l = [act[prows(pb), pcols(p)] for p in pairs]
        gc_l = [quarter(exp_[p // 2], 0, p % 2) for p in pairs]
        eg_l = [quarter(exp_[p // 2], 2, p % 2) for p in pairs]
        xdt_l = [x * quarter(exp_[p // 2], 1, p % 2) for p, x in zip(pairs, x_l)]
        xw_l = [x * quarter(exp_[p // 2], 3, p % 2) for p, x in zip(pairs, x_l)]
        m_l = []
        for p in pairs:
            r = SM_DT + 2 * p
            gcj = jnp.broadcast_to(xt[r:r + 1, :], (CHUNK, LANES))
            decay = jnp.where(incl, jnp.exp(jnp.where(incl, gc_l[p] - gcj, 0.0)), 0.0)
            m_l.append((cb2[grp(p)] * decay).astype(BF16))
        intra = [_mm(m, _block_diag(xdt, left).astype(BF16)) for m, xdt in zip(m_l, xdt_l)]
        sp = [(j, p) for j in range(nsub) for p in pairs]
        dstate = [_mm_tn(mxu_rows(srows(b_f[grp(p)], j)), mxu_rows(srows(xw_l[p], j))) for j, p in sp]
        h_old = [get_h(seqs[j], p) for j, p in sp]
        inter = [_mm(mxu_rows(srows(c_f[grp(p)], j)), h.astype(BF16))[0:vr] for (j, p), h in zip(sp, h_old)]
        for (j, p), h, ds in zip(sp, h_old, dstate):
            last = (j + 1) * vr - 1
            set_h(seqs[j], p, h * eg_l[p][last:last + 1, :] + ds)
        inter = [jnp.concatenate([inter[j * SSM_PAIRS + p] for j in range(nsub)], axis=0) for p in pairs]

        for g in range(SSM_GROUPS):
            ys = []
            ssq = None
            for p in range(g * SSM_GROUP_PAIRS, (g + 1) * SSM_GROUP_PAIRS):
                y = intra[p] + inter[p] * eg_l[p] + dch_ref[:, pcols(p)] * x_l[p]
                y = y * _silu(z_ref[prows(pb), pcols(p)])
                ys.append(y)
                sq = jnp.sum(y * y, axis=-1, keepdims=True)
                ssq = sq if ssq is None else ssq + sq
            inv = lax.rsqrt(ssq * (1.0 / SSM_GROUP_CH) + EPS)
            for pp, y in enumerate(ys):
                o_scr[prows(pb), pcols(g * SSM_GROUP_PAIRS + pp)] = y * inv

    y_ref[...] = (o_scr[...] * nw_ref[...]).astype(BF16)

    if geom.vrows == CHUNK:
        @pl.when(tile == pl.num_programs(1) - 1)
        def _():
            for s in range(geom.nseq):
                for p in range(SSM_PAIRS):
                    hout_ref[s, p] = h_scr[s, p].T


def _ssd(mix, rest, conv, h0, par, dch, nw, seld, bsz, seq):
    geom, nt = _geom(bsz, seq)
    tl = geom.nseq * geom.nch * geom.vrows
    tok = lambda blk: (lambda b, i: (b * nt + i, blk))
    st_shape = (geom.nseq, SSM_PAIRS, 2 * SSM_HEADDIM, SSM_DSTATE)
    st_spec = pl.BlockSpec(st_shape, lambda b, i: (b, 0, 0, 0))
    conv_ops, conv_specs, conv_scratch = _conv_operands(conv, geom, SSM_CONV_CH, MIX_XBC // SSM_CONV_CH)
    return pl.pallas_call(
        functools.partial(_ssd_body, geom=geom),
        grid=(bsz // geom.nseq, nt),
        in_specs=[pl.BlockSpec((tl, SSM_CONV_CH), tok(MIX_XBC // SSM_CONV_CH)),
                  pl.BlockSpec((tl, LANES), tok(REST_SMALL // LANES)),
                  pl.BlockSpec((tl, SSM_DINNER), tok(REST_ZS // SSM_DINNER))] + conv_specs + [
                  st_spec, _const_spec(par.shape), _const_spec(dch.shape), _const_spec(nw.shape),
                  _const_spec(seld.shape)],
        out_specs=[pl.BlockSpec((tl, SSM_DINNER), lambda b, i: (b * nt + i, 0)), st_spec],
        out_shape=[jax.ShapeDtypeStruct((bsz * seq, SSM_DINNER), BF16),
                   jax.ShapeDtypeStruct((bsz,) + st_shape[1:], F32)],
        scratch_shapes=conv_scratch + [pltpu.VMEM((tl, 2 * LANES), F32)] + (
            [pltpu.VMEM((geom.nseq, SSM_PAIRS, SSM_DSTATE, 2 * SSM_HEADDIM), F32)]
            if conv is None else []) + [
            pltpu.VMEM((tl, SSM_DINNER), F32)],
        compiler_params=_params(("parallel", "arbitrary")),
        name="ssd",
    )(mix, rest, rest, *conv_ops, h0, par, dch, nw, seld)


def _pad_lanes(v, offset):
    out = jnp.zeros((LANES,), F32)
    return out.at[offset:offset + v.shape[0]].set(v.astype(F32))


def _pair_selection(first_lane, lanes_per_head, n_mats):
    heads_per_mat = 2 * LANES // lanes_per_head
    sel = np.zeros((n_mats, 2 * LANES, 2 * LANES), np.float32)
    for m in range(n_mats):
        for j in range(heads_per_mat):
            src = first_lane + m * heads_per_mat + j
            sel[m, src, j * lanes_per_head:(j + 1) * lanes_per_head] = 1.0
            sel[m, LANES + src, j * lanes_per_head:(j + 1) * lanes_per_head] = 1.0
    return jnp.asarray(sel, BF16)


def _pad_conv_state(st):
    bsz, _, ch = st.shape
    return jnp.concatenate([jnp.zeros((bsz, CONV_PAD - (CONV_W - 1), ch), F32), st.astype(F32)], axis=1)


def _layer(x, mem_k, mem_v, s_gdn, c_gdn, s_ssm, c_ssm, w):
    bsz, seq, _ = x.shape
    t = bsz * seq
    xf = x.reshape(t, D_MODEL)
    h, u = _ffn1(xf, w["norm_ff1"], w["w_ff1_in"], w["w_ff1_out"], w["norm_mix"])
    cst8 = jnp.concatenate([_pad_conv_state(c_gdn), _pad_conv_state(c_ssm)], axis=2)
    if seq >= CHUNK:
        mix, rest, tail = _in_proj_conv(u, w["w_mix"], w["w_rest"], w["conv_w"], w["conv_b"], cst8, seq)
        conv = None
    else:
        rest = _in_proj(u, w["w_rest"], REST_TN, "in_proj_rest")
        mix = _in_proj(u, w["w_mix"], MIX_TN, "in_proj_mix")
        tail = mix.reshape(bsz, seq, MIX_COLS)[:, seq - CONV_PAD:, :]
        conv = (cst8, w["conv_w"], w["conv_b"])

    yg, s_gdn_new = _gdn(mix, rest, conv, s_gdn.astype(F32), w["gdn_par"], w["gdn_norm"],
                         w["sel_gdn"], bsz, seq)
    h0 = s_ssm.astype(F32).reshape(bsz, SSM_PAIRS, 2 * SSM_HEADDIM, SSM_DSTATE)
    ys, s_ssm_new = _ssd(mix, rest, conv, h0, w["ssm_par"], w["ssm_d_ch"], w["ssm_norm"],
                         w["sel_ssm"], bsz, seq)
    ym = _memattn(rest, mem_k.reshape(bsz, N_MEM, MEM_WIDTH), mem_v.reshape(bsz, N_MEM, MEM_WIDTH),
                  bsz, seq)
    y = _merge_ffn2(h, yg, ys, ym, rest, w["w_branch"], w["w_out"], w["norm_ff2"],
                    w["w_ff2_in"], w["w_ff2_out"], w["norm_final"])

    c_gdn_new = tail[:, CONV_PAD - (CONV_W - 1):, MIX_QKV:MIX_QKV + GDN_CONV_CH]
    c_ssm_new = tail[:, CONV_PAD - (CONV_W - 1):, MIX_XBC:MIX_XBC + SSM_CONV_CH]
    return (y.reshape(bsz, seq, D_MODEL), s_gdn_new, c_gdn_new,
            s_ssm_new.reshape(bsz, SSM_HEADS, SSM_HEADDIM, SSM_DSTATE), c_ssm_new)


def _row(v):
    return v.astype(F32).reshape(1, -1)


def _prep_weights(l, norm_ff1, w_ff1_in, w_ff1_out, norm_mix, w_in, gdn_conv_w, gdn_a_log,
                  gdn_dt_bias, gdn_norm, ssm_conv_w, ssm_conv_b, ssm_a_log, ssm_dt_bias, ssm_d,
                  ssm_norm, w_branch, w_out, norm_ff2, w_ff2_in, w_ff2_out, norm_final):
    row = _row

    w_mix, w_rest = _pack_w_in(w_in, l)

    def two_rows(a, b, offset):
        par = jnp.zeros((SUBLANES, LANES), F32)
        return par.at[0].set(_pad_lanes(a, offset)).at[1].set(_pad_lanes(b, offset))

    return {
        "norm_ff1": row(norm_ff1[l]), "w_ff1_in": w_ff1_in[l].astype(BF16),
        "w_ff1_out": w_ff1_out[l].astype(BF16), "norm_mix": row(norm_mix[l]),
        "w_mix": w_mix, "w_rest": w_rest,
        "conv_w": jnp.concatenate([gdn_conv_w[l], ssm_conv_w[l]], axis=1).astype(F32),
        "conv_b": jnp.concatenate([jnp.zeros((1, GDN_CONV_CH), F32), row(ssm_conv_b[l])], axis=1),
        "gdn_par": two_rows(gdn_a_log[l], gdn_dt_bias[l], SM_A),
        "gdn_norm": row(gdn_norm[l]),
        "sel_gdn": _pair_selection(0, LANES, GDN_PAIRS),
        "ssm_par": two_rows(ssm_a_log[l], ssm_dt_bias[l], SM_DT),
        "ssm_d_ch": jnp.repeat(ssm_d[l].astype(F32), SSM_HEADDIM).reshape(1, -1),
        "ssm_norm": row(ssm_norm[l]),
        "sel_ssm": _pair_selection(SM_DT, SSM_HEADDIM, SSM_PAIRS // 2),
        "w_branch": w_branch[l].astype(BF16), "w_out": w_out[l].astype(BF16),
        "norm_ff2": row(norm_ff2[l]), "w_ff2_in": w_ff2_in[l].astype(BF16),
        "w_ff2_out": w_ff2_out[l].astype(BF16), "norm_final": row(norm_final),
    }


def kernel(x_prompt, x_sample, mem_prompt, state_gdn, state_gdn_conv, state_ssm, state_ssm_conv,
           cache_mem_k, cache_mem_v, norm_ff1, w_ff1_in, w_ff1_out, norm_mix, w_in,
           gdn_conv_w, gdn_a_log, gdn_dt_bias, gdn_norm, ssm_conv_w, ssm_conv_b, ssm_a_log,
           ssm_dt_bias, ssm_d, ssm_norm, norm_mem, w_mem_kv, w_branch, w_out,
           norm_ff2, w_ff2_in, w_ff2_out, norm_final):
    assert w_in.shape[0] == 1, "the kernels implement the single-layer configuration"
    l = 0
    bp = x_prompt.shape[0]
    w = _prep_weights(l, norm_ff1, w_ff1_in, w_ff1_out, norm_mix, w_in, gdn_conv_w, gdn_a_log,
                      gdn_dt_bias, gdn_norm, ssm_conv_w, ssm_conv_b, ssm_a_log, ssm_dt_bias, ssm_d,
                      ssm_norm, w_branch, w_out, norm_ff2, w_ff2_in, w_ff2_out, norm_final)

    n_mem_tok = mem_prompt.shape[0] * mem_prompt.shape[1]
    kv = _memkv(mem_prompt.reshape(n_mem_tok, D_MODEL), _row(norm_mem[l]), w_mem_kv[l].astype(BF16))
    mk = kv[:, :MEM_WIDTH].reshape(bp, N_MEM, MEM_HEADS, MEM_HEAD_DIM)
    mv = kv[:, MEM_WIDTH:].reshape(bp, N_MEM, MEM_HEADS, MEM_HEAD_DIM)
    dtp = x_prompt.dtype
    yp, sgp, cgp, ssp, csp = _layer(
        x_prompt, mk, mv,
        jnp.zeros((bp, GDN_HEADS, GDN_DK, GDN_DV), dtp), jnp.zeros((bp, CONV_W - 1, GDN_CONV_CH), dtp),
        jnp.zeros((bp, SSM_HEADS, SSM_HEADDIM, SSM_DSTATE), dtp),
        jnp.zeros((bp, CONV_W - 1, SSM_CONV_CH), dtp), w)
    ys_, sgs, cgs, sss, css = _layer(
        x_sample, cache_mem_k[l], cache_mem_v[l], state_gdn[l], state_gdn_conv[l],
        state_ssm[l], state_ssm_conv[l], w)

    lead = lambda a: a[None]
    return (yp, ys_, lead(sgp), lead(cgp), lead(ssp), lead(csp), lead(mk), lead(mv),
            lead(sgs), lead(cgs), lead(sss), lead(css))
```

```python
import collections
import functools

import numpy as np
import jax
import jax.numpy as jnp
from jax import lax
from jax.experimental import pallas as pl
from jax.experimental.pallas import tpu as pltpu

F32 = jnp.float32
BF16 = jnp.bfloat16
EPS = 1e-6

D_MODEL = 1024
FFN_DIM = 2816
CONV_W = 4
CHUNK = 64
LANES = 128
SUBLANES = 8

GDN_HEADS = 8
GDN_PAIRS = GDN_HEADS // 2
GDN_DK = 128
GDN_DV = 128
GDN_QK = GDN_HEADS * GDN_DK
GDN_V = GDN_HEADS * GDN_DV
GDN_CONV_CH = 2 * GDN_QK + GDN_V

SSM_DINNER = 2048
SSM_HEADDIM = 64
SSM_HEADS = 32
SSM_GROUPS = 4
SSM_DSTATE = 128
SSM_CONV_CH = SSM_DINNER + 2 * SSM_GROUPS * SSM_DSTATE
SSM_PAIRS = SSM_HEADS // 2
SSM_GROUP_PAIRS = SSM_PAIRS // SSM_GROUPS
SSM_GROUP_CH = SSM_DINNER // SSM_GROUPS

N_MEM = 256
MEM_HEADS = 4
MEM_HEAD_DIM = 64
MEM_WIDTH = MEM_HEADS * MEM_HEAD_DIM

MIX_QKV = 0
MIX_XBC = MIX_QKV + GDN_CONV_CH
MIX_COLS = MIX_XBC + SSM_CONV_CH
REST_GATE = 0
REST_ZG = REST_GATE + 3 * D_MODEL
REST_ZS = REST_ZG + GDN_V
REST_QM = REST_ZS + SSM_DINNER
REST_SMALL = REST_QM + MEM_WIDTH
REST_COLS = REST_SMALL + LANES
IN_SPLITS = (GDN_CONV_CH, GDN_HEADS, GDN_HEADS, GDN_V, SSM_DINNER, SSM_CONV_CH, SSM_HEADS, MEM_WIDTH,
             3 * D_MODEL)
SM_A = 0
SM_B = GDN_HEADS
SM_DT = 2 * GDN_HEADS

VMEM_LIMIT = 56 * 1024 * 1024


def _params(semantics):
    return pltpu.CompilerParams(dimension_semantics=semantics, vmem_limit_bytes=VMEM_LIMIT)


def _mm(a, b):
    return jnp.dot(a, b, preferred_element_type=F32)


def _mm_nt(a, b):
    return lax.dot_general(a, b, (((1,), (1,)), ((), ())), preferred_element_type=F32)


def _mm_tn(a, b):
    return lax.dot_general(a, b, (((0,), (0,)), ((), ())), preferred_element_type=F32)


def _split3(x):
    hi = x.astype(BF16)
    r1 = x - hi.astype(F32)
    mid = r1.astype(BF16)
    lo = (r1 - mid.astype(F32)).astype(BF16)
    return hi, mid, lo


def _split2_lanes(x):
    hi = x.astype(BF16)
    lo = (x - hi.astype(F32)).astype(BF16)
    return jnp.concatenate([hi, lo], axis=1)


def _rms(x, g):
    return x * lax.rsqrt(jnp.mean(x * x, axis=-1, keepdims=True) + EPS) * g


def _sigmoid(x):
    return 1.0 / (1.0 + jnp.exp2(x * float(-1.0 / np.log(2.0))))


def _silu(x):
    return x * _sigmoid(x)


def _softplus(x):
    return jnp.maximum(x, 0.0) + jnp.log1p(jnp.exp(-jnp.abs(x)))


def _const_spec(shape):
    nd = len(shape)
    return pl.BlockSpec(shape, lambda *_: (0,) * nd, pipeline_mode=pl.Buffered(1))


FFN_TM = 512
MXU_WIDTH = 256
FFN_CHUNKS = ((0, 6 * MXU_WIDTH), (6 * MXU_WIDTH, FFN_DIM))


def _ffn_compute(x, g_ref, wi_ref, wo_ref):
    xn = _rms(x, g_ref[...]).astype(BF16)
    acc = None
    for lo, hi in FFN_CHUNKS:
        gate = _mm(xn, wi_ref[:, lo:hi])
        up = _mm(xn, wi_ref[:, FFN_DIM + lo:FFN_DIM + hi])
        act = (_silu(gate) * up).astype(BF16)
        part = _mm(act, wo_ref[lo:hi, :])
        acc = part if acc is None else acc + part
    return x + 0.5 * acc


def _ffn1_body(x_ref, g_ref, wi_ref, wo_ref, g2_ref, h_ref, u_ref):
    h = _ffn_compute(x_ref[...], g_ref, wi_ref, wo_ref)
    h_ref[...] = h
    u_ref[...] = _rms(h, g2_ref[...]).astype(BF16)


def _ffn1(x, g, wi, wo, g2):
    t = x.shape[0]
    tm = min(FFN_TM, t)
    row = lambda i: (i, 0)
    return pl.pallas_call(
        _ffn1_body,
        grid=(t // tm,),
        in_specs=[pl.BlockSpec((tm, D_MODEL), row), _const_spec(g.shape), _const_spec(wi.shape),
                  _const_spec(wo.shape), _const_spec(g2.shape)],
        out_specs=[pl.BlockSpec((tm, D_MODEL), row), pl.BlockSpec((tm, D_MODEL), row)],
        out_shape=[jax.ShapeDtypeStruct((t, D_MODEL), F32), jax.ShapeDtypeStruct((t, D_MODEL), BF16)],
        compiler_params=_params(("parallel",)),
        name="ffn1",
    )(x, g, wi, wo, g2)


PACK_ROWS = 128


def _pack_body(wt_ref, mix_ref, rest_ref):
    src = np.cumsum((0,) + IN_SPLITS).tolist()
    qkv, ab, _, zg, zs, xbc, dt, qm, gate = [(src[i], IN_SPLITS[i]) for i in range(len(IN_SPLITS))]
    for ref, dst, (lo, n) in ((mix_ref, MIX_QKV, qkv), (mix_ref, MIX_XBC, xbc),
                              (rest_ref, REST_GATE, gate), (rest_ref, REST_ZG, zg),
                              (rest_ref, REST_ZS, zs), (rest_ref, REST_QM, qm)):
        assert lo % SUBLANES == 0 and n % LANES == 0
        for t in range(n // LANES):
            ref[:, dst + t * LANES:dst + (t + 1) * LANES] = (
                wt_ref[lo + t * LANES:lo + (t + 1) * LANES, :].T.astype(BF16))
    assert SM_A == 0 and SM_DT == 2 * GDN_HEADS
    small = jnp.concatenate(
        [wt_ref[ab[0]:ab[0] + 2 * GDN_HEADS, :], wt_ref[dt[0]:dt[0] + SSM_HEADS, :],
         jnp.zeros((LANES - 2 * GDN_HEADS - SSM_HEADS, wt_ref.shape[1]), F32)], axis=0)
    rest_ref[:, REST_SMALL:] = small.T.astype(BF16)


def _pack_w_in(wt):
    rows = lambda i: (i, 0)
    return pl.pallas_call(
        _pack_body,
        grid=(D_MODEL // PACK_ROWS,),
        in_specs=[pl.BlockSpec((wt.shape[0], PACK_ROWS), lambda i: (0, i))],
        out_specs=[pl.BlockSpec((PACK_ROWS, MIX_COLS), rows), pl.BlockSpec((PACK_ROWS, REST_COLS), rows)],
        out_shape=[jax.ShapeDtypeStruct((D_MODEL, MIX_COLS), BF16),
                   jax.ShapeDtypeStruct((D_MODEL, REST_COLS), BF16)],
        compiler_params=_params(("parallel",)),
        name="pack_w_in",
    )(wt)


PROJ_TM = 1024
PROJ_ROWS = 256
MIX_TN = MIX_COLS // 4
REST_TN = REST_COLS // 3


def _proj_body(u_ref, w_ref, o_ref):
    o_ref[...] = _mm(u_ref[...], w_ref[...])


def _in_proj(u, w, tn, name):
    t = u.shape[0]
    tm = min(PROJ_TM, t)
    cols = w.shape[1]
    return pl.pallas_call(
        _proj_body,
        grid=(cols // tn, t // tm),
        in_specs=[pl.BlockSpec((tm, D_MODEL), lambda j, i: (i, 0)),
                  pl.BlockSpec((D_MODEL, tn), lambda j, i: (0, j))],
        out_specs=pl.BlockSpec((tm, tn), lambda j, i: (i, j)),
        out_shape=jax.ShapeDtypeStruct((t, cols), F32),
        compiler_params=_params(("parallel", "parallel")),
        name=name,
    )(u, w)


def _causal_conv_silu(xa, w, bias):
    shape = (SUBLANES, LANES)
    sub = lax.broadcasted_iota(jnp.int32, shape, 0)
    columns = []
    for c in range(xa.shape[1] // LANES):
        cols = slice(c * LANES, (c + 1) * LANES)
        taps = [jnp.broadcast_to(w[j:j + 1, cols], shape) for j in range(CONV_W)]
        b = jnp.broadcast_to(bias[:, cols], shape)
        before = xa[CONV_PAD - SUBLANES:CONV_PAD, cols]
        outs = []
        for g in range(CONV_PAD // SUBLANES, xa.shape[0] // SUBLANES):
            x = xa[g * SUBLANES:(g + 1) * SUBLANES, cols]
            y = x * taps[CONV_W - 1] + b
            for k in range(1, CONV_W):
                shifted = pltpu.roll(jnp.where(sub >= SUBLANES - k, before, x), k, axis=0)
                y = y + shifted * taps[CONV_W - 1 - k]
            outs.append(_silu(y))
            before = x
        columns.append(jnp.concatenate(outs, axis=0))
    return jnp.concatenate(columns, axis=1)


CONVPROJ_TM = 512
CONVPROJ_STEPS = 3
PROJ_STRIPS = 8


def _strips(width, n):
    tiles = width // LANES
    assert width % LANES == 0 and tiles >= n
    bounds = [LANES * ((tiles * k) // n) for k in range(n + 1)]
    return [slice(bounds[k], bounds[k + 1]) for k in range(n)]


def _proj_conv_body(u_ref, wm_ref, wr_ref, cw_ref, cb_ref, cst_ref, mix_ref, rest_ref, tail_ref,
                    carry, *, tiles_per_seq):
    i = pl.program_id(1)

    @pl.when(i % tiles_per_seq == 0)
    def _():
        carry[0:CONV_PAD, :] = cst_ref[0]

    strips_m = _strips(mix_ref.shape[1], PROJ_STRIPS)
    strips_r = _strips(rest_ref.shape[1], PROJ_STRIPS)
    for r in range(u_ref.shape[0] // PROJ_ROWS):
        rows = slice(r * PROJ_ROWS, (r + 1) * PROJ_ROWS)
        u = u_ref[rows, :]
        for cm, cr in zip(strips_m, strips_r):
            carry[CONV_PAD:, cm] = _mm(u, wm_ref[:, cm])
            rest_ref[rows, cr] = _mm(u, wr_ref[:, cr])
            mix_ref[rows, cm] = _causal_conv_silu(carry.at[:, cm], cw_ref[:, cm], cb_ref[:, cm])
            carry[0:CONV_PAD, cm] = carry[PROJ_ROWS:, cm]
    tail_ref[0] = carry[0:CONV_PAD, :]


def _in_proj_conv(u, w_mix, w_rest, cw, cb, cst8, seq):
    t = u.shape[0]
    tm = min(CONVPROJ_TM, seq)
    assert seq % tm == 0 and tm % PROJ_ROWS == 0
    tps = seq // tm
    tn_m = MIX_COLS // CONVPROJ_STEPS
    tn_r = REST_COLS // CONVPROJ_STEPS
    col = lambda j, i: (0, j)
    tile = lambda j, i: (i, j)
    seq_blk = lambda j, i: (i // tps, 0, j)
    return pl.pallas_call(
        functools.partial(_proj_conv_body, tiles_per_seq=tps),
        grid=(CONVPROJ_STEPS, t // tm),
        in_specs=[pl.BlockSpec((tm, D_MODEL), lambda j, i: (i, 0)),
                  pl.BlockSpec((D_MODEL, tn_m), col), pl.BlockSpec((D_MODEL, tn_r), col),
                  pl.BlockSpec((CONV_W, tn_m), col), pl.BlockSpec((1, tn_m), col),
                  pl.BlockSpec((1, CONV_PAD, tn_m), seq_blk)],
        out_specs=[pl.BlockSpec((tm, tn_m), tile), pl.BlockSpec((tm, tn_r), tile),
                   pl.BlockSpec((1, CONV_PAD, tn_m), seq_blk)],
        out_shape=[jax.ShapeDtypeStruct((t, MIX_COLS), F32), jax.ShapeDtypeStruct((t, REST_COLS), F32),
                   jax.ShapeDtypeStruct((t // seq, CONV_PAD, MIX_COLS), F32)],
        scratch_shapes=[pltpu.VMEM((CONV_PAD + PROJ_ROWS, tn_m), F32)],
        compiler_params=_params(("parallel", "arbitrary")),
        name="in_proj_conv",
    )(u, w_mix, w_rest, cw, cb, cst8)


def _memkv_body(x_ref, g_ref, w_ref, o_ref):
    o_ref[...] = _mm(_rms(x_ref[...], g_ref[...]).astype(BF16), w_ref[...])


def _memkv(mem, g, w):
    t = mem.shape[0]
    tm = min(512, t)
    row = lambda i: (i, 0)
    return pl.pallas_call(
        _memkv_body,
        grid=(t // tm,),
        in_specs=[pl.BlockSpec((tm, D_MODEL), row), _const_spec(g.shape), _const_spec(w.shape)],
        out_specs=pl.BlockSpec((tm, 2 * MEM_WIDTH), row),
        out_shape=jax.ShapeDtypeStruct((t, 2 * MEM_WIDTH), F32),
        compiler_params=_params(("parallel",)),
        name="memkv",
    )(mem, g, w)


MERGE_TM = 256


def _merge_ffn2_body(h_ref, yg_ref, ys_ref, ym_ref, gate_ref, wb_ref, wout_ref,
                     g_ref, wi_ref, wo_ref, gf_ref, y_ref):
    gates = gate_ref[...]
    merged = (_sigmoid(gates[:, 0:D_MODEL]) * _mm(yg_ref[...], wb_ref[0:GDN_V, :])
              + _sigmoid(gates[:, D_MODEL:2 * D_MODEL])
              * _mm(ys_ref[...], wb_ref[GDN_V:GDN_V + SSM_DINNER, :])
              + _sigmoid(gates[:, 2 * D_MODEL:3 * D_MODEL])
              * _mm(ym_ref[...], wb_ref[GDN_V + SSM_DINNER:, :]))
    h = h_ref[...] + _mm(merged.astype(BF16), wout_ref[...])
    h = _ffn_compute(h, g_ref, wi_ref, wo_ref)
    y_ref[...] = _rms(h, gf_ref[...])


def _merge_ffn2(h, yg, ys, ym, proj, wb, wout, g, wi, wo, gf):
    t = h.shape[0]
    tm = min(MERGE_TM, t)
    row = lambda i: (i, 0)
    gate_blk = REST_GATE // (3 * D_MODEL)
    return pl.pallas_call(
        _merge_ffn2_body,
        grid=(t // tm,),
        in_specs=[pl.BlockSpec((tm, D_MODEL), row), pl.BlockSpec((tm, GDN_V), row),
                  pl.BlockSpec((tm, SSM_DINNER), row), pl.BlockSpec((tm, MEM_WIDTH), row),
                  pl.BlockSpec((tm, 3 * D_MODEL), lambda i: (i, gate_blk)),
                  _const_spec(wb.shape), _const_spec(wout.shape), _const_spec(g.shape),
                  _const_spec(wi.shape), _const_spec(wo.shape), _const_spec(gf.shape)],
        out_specs=pl.BlockSpec((tm, D_MODEL), row),
        out_shape=jax.ShapeDtypeStruct((t, D_MODEL), F32),
        compiler_params=_params(("parallel",)),
        name="merge_ffn2",
    )(h, yg, ys, ym, proj, wb, wout, g, wi, wo, gf)


MEMATTN_TL = 512
MEMATTN_NSEQ = 16


def _memattn_body(q_ref, k_ref, v_ref, y_ref, *, nseq, rows):
    lane = lax.broadcasted_iota(jnp.int32, (rows, MEM_WIDTH), 1)
    masks = [(lane >= hh * MEM_HEAD_DIM) & (lane < (hh + 1) * MEM_HEAD_DIM) for hh in range(MEM_HEADS)]
    seqs = range(nseq)
    q = [q_ref[s * rows:(s + 1) * rows, :] for s in seqs]
    q4 = [jnp.concatenate([jnp.where(m, x, 0.0) for m in masks], axis=0).astype(BF16) for x in q]
    sc = [_mm(q4[s], k_ref[s].astype(BF16)) * (MEM_HEAD_DIM ** -0.5) for s in seqs]
    p = [jnp.exp(x - jnp.max(x, axis=-1, keepdims=True)) for x in sc]
    p = [(x / jnp.sum(x, axis=-1, keepdims=True)).astype(BF16) for x in p]
    o4 = [_mm_nt(p[s], v_ref[s].astype(BF16)) for s in seqs]
    ys = []
    for x in o4:
        y = jnp.where(masks[0], x[0:rows], 0.0)
        for hh in range(1, MEM_HEADS):
            y = y + jnp.where(masks[hh], x[hh * rows:(hh + 1) * rows], 0.0)
        ys.append(y)
    y_ref[...] = jnp.concatenate(ys, axis=0).astype(BF16)


def _memattn(proj, k, v, bsz, seq):
    if seq >= MEMATTN_TL:
        nseq, rows, nt = 1, MEMATTN_TL, seq // MEMATTN_TL
    else:
        nseq, rows, nt = MEMATTN_NSEQ, seq, 1
    assert bsz % nseq == 0 and seq % rows == 0
    tl = nseq * rows
    qm_blk = REST_QM // MEM_WIDTH
    kv_spec = pl.BlockSpec((nseq, MEM_WIDTH, N_MEM), lambda b, i: (b, 0, 0))
    return pl.pallas_call(
        functools.partial(_memattn_body, nseq=nseq, rows=rows),
        grid=(bsz // nseq, nt),
        in_specs=[pl.BlockSpec((tl, MEM_WIDTH), lambda b, i: (b * nt + i, qm_blk)), kv_spec, kv_spec],
        out_specs=pl.BlockSpec((tl, MEM_WIDTH), lambda b, i: (b * nt + i, 0)),
        out_shape=jax.ShapeDtypeStruct((bsz * seq, MEM_WIDTH), BF16),
        compiler_params=_params(("parallel", "arbitrary")),
        name="memattn",
    )(proj, k, v)


SEQ_TL = 256
SHORT_NSEQ = 8
CONV_PAD = SUBLANES
MIN_MXU_ROWS = 16

Geom = collections.namedtuple("Geom", ["nseq", "nch", "vrows"])


def _geom(bsz, seq):
    if seq >= CHUNK:
        assert seq % SEQ_TL == 0
        return Geom(1, SEQ_TL // CHUNK, CHUNK), seq // SEQ_TL
    assert bsz % SHORT_NSEQ == 0 and seq % SUBLANES == 0 and (SHORT_NSEQ * seq) % CHUNK == 0
    return Geom(SHORT_NSEQ, 1, seq), 1


def _short_conv_silu(x_ref, cst_ref, cw_ref, cb_ref, act, geom):
    assert geom.nch == 1 and geom.vrows < CHUNK and CONV_PAD == SUBLANES
    vr = geom.vrows
    sub = lax.broadcasted_iota(jnp.int32, (SUBLANES, LANES), 0)
    for cb in range(x_ref.shape[1] // LANES):
        cols = slice(cb * LANES, (cb + 1) * LANES)
        planes = [cst_ref[j, :, cols] for j in range(CONV_W - 1)]
        for s in range(geom.nseq):
            before = jnp.zeros((SUBLANES, LANES), F32)
            for j, plane in enumerate(planes):
                before = jnp.where(sub == SUBLANES - (CONV_W - 1) + j,
                                   jnp.broadcast_to(plane[s:s + 1, :], (SUBLANES, LANES)), before)
            xa = jnp.concatenate([before, x_ref[s * vr:(s + 1) * vr, cols]], axis=0)
            act[s * vr:(s + 1) * vr, cols] = _causal_conv_silu(xa, cw_ref[:, cols], cb_ref[:, cols])


def _pair_masks(vr):
    row = lax.broadcasted_iota(jnp.int32, (CHUNK, LANES), 0)
    lane = lax.broadcasted_iota(jnp.int32, (CHUNK, LANES), 1)
    col = jnp.where(lane < CHUNK, lane, lane - CHUNK)
    same = (row // vr) == (col // vr)
    return row, lane, col, same


def _block_diag(x, left):
    zero = jnp.zeros_like(x)
    return jnp.concatenate([jnp.where(left, x, zero), jnp.where(left, zero, x)], axis=0)


def _chunk_cumsums(gates, nprob, vr):
    r = lax.broadcasted_iota(jnp.int32, (CHUNK, CHUNK), 0)
    c = lax.broadcasted_iota(jnp.int32, (CHUNK, CHUNK), 1)
    same = (r // vr) == (c // vr)
    sums = jnp.concatenate([jnp.where(same & (r >= c), 1.0, 0.0), jnp.where(same, 1.0, 0.0)],
                           axis=0).astype(BF16)
    sums3 = jnp.concatenate([sums, sums, sums], axis=1)
    pieces = [jnp.concatenate(_split3(gates[pb * CHUNK:(pb + 1) * CHUNK, 0:LANES]), axis=0)
              for pb in range(nprob)]
    out = _mm(sums3, jnp.concatenate(pieces, axis=1))
    cols = lambda pb: slice(pb * LANES, (pb + 1) * LANES)
    return ([out[0:CHUNK, cols(pb)] for pb in range(nprob)],
            [out[CHUNK:, cols(pb)] for pb in range(nprob)])


def _pad_rows(x):
    if x.shape[0] >= MIN_MXU_ROWS:
        return x
    return jnp.concatenate([x, jnp.zeros((MIN_MXU_ROWS - x.shape[0], x.shape[1]), x.dtype)], axis=0)


def _row_form(gc):
    return jnp.concatenate([gc, pltpu.roll(gc, LANES - 1, axis=1)], axis=0).T


def _gdn_body(*refs, geom):
    if geom.vrows == CHUNK:
        (x_ref, sm_ref, z_ref, s0_ref, par_ref, gn_ref, selp_ref, y_ref, sout_ref,
         gates, s_scr, u_scr, w_scr, qd_scr, kd_scr, qk_scr, egl_scr, o_scr) = refs
        act = x_ref
    else:
        (x_ref, sm_ref, z_ref, cst_ref, cw_ref, cb_ref, s0_ref, par_ref, gn_ref, selp_ref, y_ref,
         sout_ref, act, gates, s_scr, u_scr, w_scr, qd_scr, kd_scr, qk_scr, egl_scr, o_scr) = refs
        _short_conv_silu(x_ref, cst_ref, cw_ref, cb_ref, act, geom)
    tile = pl.program_id(1)
    first = tile == 0
    vr = geom.vrows
    nsub = CHUNK // vr
    nprob = geom.nseq * geom.nch // nsub

    @pl.when(first)
    def _():
        s_scr[...] = s0_ref[...]

    sm = sm_ref[...]
    lane_g = lax.broadcasted_iota(jnp.int32, sm.shape, 1)
    head_lane = lane_g < GDN_HEADS
    gates[:, 0:LANES] = jnp.where(
        head_lane, -jnp.exp(par_ref[0:1, :]) * _softplus(sm + par_ref[1:2, :]), 0.0)
    gates[:, LANES:] = jnp.where(head_lane, pltpu.roll(_sigmoid(sm), LANES - SM_B, axis=1), 0.0)

    row, lane, col, same = _pair_masks(vr)
    left = lane < CHUNK
    incl = same & (row >= col)
    strict = same & (row > col)
    eye = jnp.where(row == col, 1.0, 0.0).astype(F32)
    zeros_h = jnp.zeros((CHUNK, LANES), F32)
    hcols = lambda h: slice(h * LANES, (h + 1) * LANES)
    prows = lambda pb, n=CHUNK: slice(pb * n, (pb + 1) * n)

    gcs, gls = _chunk_cumsums(gates, nprob, vr)
    lhs, xts = [], []
    for pb in range(nprob):
        gc = gcs[pb]
        quantities = jnp.concatenate(
            [gc, gates[prows(pb), LANES:2 * LANES], jnp.exp(gc), jnp.exp(gls[pb] - gc)], axis=0)
        lhs.append(_split2_lanes(quantities))
        xts.append(_row_form(gc))

    items = [(pb, p) for pb in range(nprob) for p in range(GDN_PAIRS)]
    exp_ = [_mm(lhs[pb], selp_ref[p]) for pb, p in items]

    def l2n(t):
        return t * lax.rsqrt(jnp.sum(t * t, axis=-1, keepdims=True) + EPS)

    def stacked(e, qi):
        return jnp.concatenate([e[qi * CHUNK:(qi + 1) * CHUNK, 0:LANES],
                                e[qi * CHUNK:(qi + 1) * CHUNK, LANES:]], axis=0)

    decay_l, kq_l, rhs_l, qd_l, kd_l = [], [], [], [], []
    for (pb, p), e in zip(items, exp_):
        ha, hb = 2 * p, 2 * p + 1
        qa, qb = [l2n(act[prows(pb), hcols(h)]) * (GDN_DK ** -0.5) for h in (ha, hb)]
        ka, kb = [l2n(act[prows(pb), hcols(GDN_HEADS + h)]) for h in (ha, hb)]
        va, vb = [act[prows(pb), hcols(2 * GDN_HEADS + h)] for h in (ha, hb)]
        beta2, eg2, kdec2 = stacked(e, 1), stacked(e, 2), stacked(e, 3)
        k2 = jnp.concatenate([ka, kb], axis=0)
        kbeta2 = k2 * beta2
        gci = jnp.where(left, e[0:CHUNK, 0:LANES], e[0:CHUNK, LANES:])
        gcj = jnp.broadcast_to(xts[pb][ha:ha + 1, :], (CHUNK, LANES))
        decay_l.append(jnp.where(incl, jnp.exp(jnp.where(incl, gci - gcj, 0.0)), 0.0))
        k_bd = jnp.concatenate([jnp.concatenate([ka, zeros_h], axis=1),
                                jnp.concatenate([zeros_h, kb], axis=1)], axis=0).astype(BF16)
        kbq = jnp.concatenate([jnp.concatenate([kbeta2[0:CHUNK], kbeta2[CHUNK:]], axis=1),
                               jnp.concatenate([qa, qb], axis=1)], axis=0).astype(BF16)
        kq_l.append(_mm_nt(kbq, k_bd))
        rhs_l.append(jnp.concatenate(
            [jnp.concatenate([va, vb], axis=0) * beta2, kbeta2 * eg2], axis=1).astype(BF16))
        qd_l.append(jnp.concatenate([qa, qb], axis=0) * eg2)
        kd_l.append(k2 * kdec2)
        for sub in range(nsub):
            last = 2 * CHUNK + (sub + 1) * vr - 1
            slot = prows(pb * nsub + sub, SUBLANES)
            egl_scr[slot, hcols(ha)] = jnp.broadcast_to(e[last:last + 1, 0:LANES], (SUBLANES, LANES))
            egl_scr[slot, hcols(hb)] = jnp.broadcast_to(e[last:last + 1, LANES:], (SUBLANES, LANES))

    n_pow = [-jnp.where(strict, kq[0:CHUNK] * d, 0.0) for kq, d in zip(kq_l, decay_l)]
    t_inv = [eye + n for n in n_pow]
    for _ in range(int(np.ceil(np.log2(vr))) - 1):
        n_pow = [_mm(n.astype(BF16), _block_diag(n, left).astype(BF16)) for n in n_pow]
        t_inv = [t + _mm(t.astype(BF16), _block_diag(n, left).astype(BF16))
                 for t, n in zip(t_inv, n_pow)]

    uw_l = [_mm(_block_diag(t, left).astype(BF16), rhs) for t, rhs in zip(t_inv, rhs_l)]
    for (pb, p), uw, kq, d, qd, kd in zip(items, uw_l, kq_l, decay_l, qd_l, kd_l):
        qk_scr[prows(pb, 2 * CHUNK), hcols(p)] = _block_diag(kq[CHUNK:] * d, left).astype(BF16)
        for idx, h in enumerate((2 * p, 2 * p + 1)):
            sl = slice(idx * CHUNK, (idx + 1) * CHUNK)
            u_scr[prows(pb), hcols(h)] = uw[sl, 0:LANES]
            w_scr[prows(pb), hcols(h)] = uw[sl, LANES:]
            qd_scr[prows(pb), hcols(h)] = qd[sl]
            kd_scr[prows(pb), hcols(h)] = kd[sl]

    for c in range(geom.nch):
        sh = [(s, h) for s in range(geom.nseq) for h in range(GDN_HEADS)]
        trows = lambda s: prows(s * geom.nch + c, vr)
        wqs = [_mm(jnp.concatenate([w_scr[trows(s), hcols(h)], qd_scr[trows(s), hcols(h)]],
                                   axis=0).astype(BF16), s_scr[s, h].astype(BF16)) for s, h in sh]
        vnew = [u_scr[trows(s), hcols(h)] - w[0:vr] for (s, h), w in zip(sh, wqs)]
        for (s, h), vn in zip(sh, vnew):
            slot = (s * geom.nch + c) * SUBLANES
            s_scr[s, h] = (s_scr[s, h] * egl_scr[slot:slot + 1, hcols(h)]
                           + _mm_tn(_pad_rows(kd_scr[trows(s), hcols(h)]).astype(BF16),
                                    _pad_rows(vn).astype(BF16)))
        for pb in range(c * nsub, geom.nseq * geom.nch, geom.nch * nsub):
            subs = [pb // geom.nch + j for j in range(nsub)]
            for p in range(GDN_PAIRS):
                heads = (2 * p, 2 * p + 1)
                vn2 = jnp.concatenate([vnew[s * GDN_HEADS + h] for h in heads for s in subs],
                                      axis=0).astype(BF16)
                intra = _mm(qk_scr[prows((pb // nsub), 2 * CHUNK), hcols(p)], vn2)
                for idx, h in enumerate(heads):
                    for j, s in enumerate(subs):
                        r0 = idx * CHUNK + j * vr
                        o = wqs[s * GDN_HEADS + h][vr:] + intra[r0:r0 + vr]
                        o_scr[trows(s), hcols(h)] = o

    for h in range(GDN_HEADS):
        y = _rms(o_scr[:, hcols(h)], gn_ref[...]) * _silu(z_ref[:, hcols(h)])
        y_ref[:, hcols(h)] = y.astype(BF16)

    @pl.when(tile == pl.num_programs(1) - 1)
    def _():
        sout_ref[...] = s_scr[...]


def _conv_operands(conv, geom, width, col_blk):
    if conv is None:
        return [], [], []
    state, cw, cb = conv
    specs = [pl.BlockSpec((CONV_W - 1, geom.nseq, width), lambda b, i: (0, b, 0)),
             pl.BlockSpec((CONV_W, width), lambda b, i: (0, col_blk)),
             pl.BlockSpec((1, width), lambda b, i: (0, col_blk))]
    return [state, cw, cb], specs, [pltpu.VMEM((geom.nseq * geom.vrows, width), F32)]


def _gdn(mix, rest, conv, s0, par, gn, selp, bsz, seq):
    geom, nt = _geom(bsz, seq)
    tl = geom.nseq * geom.nch * geom.vrows
    nprob = tl // CHUNK
    tok = lambda blk: (lambda b, i: (b * nt + i, blk))
    st_spec = pl.BlockSpec((geom.nseq, GDN_HEADS, GDN_DK, GDN_DV), lambda b, i: (b, 0, 0, 0))
    conv_ops, conv_specs, conv_scratch = _conv_operands(conv, geom, GDN_CONV_CH, MIX_QKV // GDN_CONV_CH)
    return pl.pallas_call(
        functools.partial(_gdn_body, geom=geom),
        grid=(bsz // geom.nseq, nt),
        in_specs=[pl.BlockSpec((tl, GDN_CONV_CH), tok(MIX_QKV // GDN_CONV_CH)),
                  pl.BlockSpec((tl, LANES), tok(REST_SMALL // LANES)),
                  pl.BlockSpec((tl, GDN_V), tok(REST_ZG // GDN_V))] + conv_specs + [
                  st_spec, _const_spec(par.shape), _const_spec(gn.shape), _const_spec(selp.shape)],
        out_specs=[pl.BlockSpec((tl, GDN_V), lambda b, i: (b * nt + i, 0)), st_spec],
        out_shape=[jax.ShapeDtypeStruct((bsz * seq, GDN_V), BF16),
                   jax.ShapeDtypeStruct((bsz, GDN_HEADS, GDN_DK, GDN_DV), F32)],
        scratch_shapes=conv_scratch + [
            pltpu.VMEM((tl, 2 * LANES), F32),
            pltpu.VMEM((geom.nseq, GDN_HEADS, GDN_DK, GDN_DV), F32),
            pltpu.VMEM((tl, GDN_V), F32),
            pltpu.VMEM((tl, GDN_V), F32),
            pltpu.VMEM((tl, GDN_V), F32),
            pltpu.VMEM((tl, GDN_V), F32),
            pltpu.VMEM((nprob * 2 * CHUNK, GDN_PAIRS * LANES), BF16),
            pltpu.VMEM((geom.nseq * geom.nch * SUBLANES, GDN_V), F32),
            pltpu.VMEM((tl, GDN_V), F32)],
        compiler_params=_params(("parallel", "arbitrary")),
        name="gdn",
    )(mix, rest, rest, *conv_ops, s0, par, gn, selp)


def _ssd_body(*refs, geom):
    if geom.vrows == CHUNK:
        (x_ref, sm_ref, z_ref, h0_ref, par_ref, dch_ref, nw_ref, seld_ref, y_ref, hout_ref,
         gates, h_scr, o_scr) = refs
        act = x_ref
        tile = pl.program_id(1)

        @pl.when(tile == 0)
        def _():
            for s in range(geom.nseq):
                for p in range(SSM_PAIRS):
                    h_scr[s, p] = h0_ref[s, p].T

        def get_h(s, p):
            return h_scr[s, p]

        def set_h(s, p, value):
            h_scr[s, p] = value
    else:
        (x_ref, sm_ref, z_ref, cst_ref, cw_ref, cb_ref, h0_ref, par_ref, dch_ref, nw_ref, seld_ref,
         y_ref, hout_ref, act, gates, o_scr) = refs
        _short_conv_silu(x_ref, cst_ref, cw_ref, cb_ref, act, geom)

        def get_h(s, p):
            return h0_ref[s, p].T

        def set_h(s, p, value):
            hout_ref[s, p] = value.T
    vr = geom.vrows
    nsub = CHUNK // vr
    nprob = geom.nseq * geom.nch // nsub

    sm = sm_ref[...]
    lane_g = lax.broadcasted_iota(jnp.int32, sm.shape, 1)
    is_dt = (lane_g >= SM_DT) & (lane_g < SM_DT + SSM_HEADS)
    dt = jnp.where(is_dt, _softplus(sm + par_ref[1:2, :]), 0.0)
    gates[:, 0:LANES] = dt * jnp.where(is_dt, -jnp.exp(par_ref[0:1, :]), 0.0)
    gates[:, LANES:] = dt

    row, lane, col, same = _pair_masks(vr)
    left = lane < CHUNK
    incl = same & (row >= col)
    pcols = lambda p: slice(p * LANES, (p + 1) * LANES)
    prows = lambda pb, n=CHUNK: slice(pb * n, (pb + 1) * n)
    quarter = lambda e, qi, j: e[qi * CHUNK:(qi + 1) * CHUNK, j * LANES:(j + 1) * LANES]
    srows = lambda a, j: a[j * vr:(j + 1) * vr]
    mxu_rows = lambda a: _pad_rows(a).astype(BF16)
    b_off = SSM_DINNER
    c_off = SSM_DINNER + SSM_GROUPS * SSM_DSTATE

    gcs, gls = _chunk_cumsums(gates, nprob, vr)
    for pb in range(nprob):
        seqs = [(pb * nsub + j) // geom.nch for j in range(nsub)]
        gc = gcs[pb]
        dt_c = gates[prows(pb), LANES:2 * LANES]
        lhs = _split2_lanes(jnp.concatenate(
            [gc, dt_c, jnp.exp(gc), jnp.exp(gls[pb] - gc) * dt_c], axis=0))
        xt = _row_form(gc)
        exp_ = [_mm(lhs, seld_ref[d]) for d in range(SSM_PAIRS // 2)]

        b_f = [act[prows(pb), b_off + g * SSM_DSTATE:b_off + (g + 1) * SSM_DSTATE]
               for g in range(SSM_GROUPS)]
        c_f = [act[prows(pb), c_off + g * SSM_DSTATE:c_off + (g + 1) * SSM_DSTATE]
               for g in range(SSM_GROUPS)]
        b_bf = [b.astype(BF16) for b in b_f]
        c_bf = [c.astype(BF16) for c in c_f]
        cb2 = [_mm_nt(c, jnp.concatenate([b, b], axis=0)) for b, c in zip(b_bf, c_bf)]

        pairs = range(SSM_PAIRS)
        grp = lambda p: p // SSM_GROUP_PAIRS
        x_l = [act[prows(pb), pcols(p)] for p in pairs]
        gc_l = [quarter(exp_[p // 2], 0, p % 2) for p in pairs]
        eg_l = [quarter(exp_[p // 2], 2, p % 2) for p in pairs]
        xdt_l = [x * quarter(exp_[p // 2], 1, p % 2) for p, x in zip(pairs, x_l)]
        xw_l = [x * quarter(exp_[p // 2], 3, p % 2) for p, x in zip(pairs, x_l)]
        m_l = []
        for p in pairs:
            r = SM_DT + 2 * p
            gcj = jnp.broadcast_to(xt[r:r + 1, :], (CHUNK, LANES))
            decay = jnp.where(incl, jnp.exp(jnp.where(incl, gc_l[p] - gcj, 0.0)), 0.0)
            m_l.append((cb2[grp(p)] * decay).astype(BF16))
        intra = [_mm(m, _block_diag(xdt, left).astype(BF16)) for m, xdt in zip(m_l, xdt_l)]
        sp = [(j, p) for j in range(nsub) for p in pairs]
        dstate = [_mm_tn(mxu_rows(srows(b_f[grp(p)], j)), mxu_rows(srows(xw_l[p], j))) for j, p in sp]
        h_old = [get_h(seqs[j], p) for j, p in sp]
        inter = [_mm(mxu_rows(srows(c_f[grp(p)], j)), h.astype(BF16))[0:vr] for (j, p), h in zip(sp, h_old)]
        for (j, p), h, ds in zip(sp, h_old, dstate):
            last = (j + 1) * vr - 1
            set_h(seqs[j], p, h * eg_l[p][last:last + 1, :] + ds)
        inter = [jnp.concatenate([inter[j * SSM_PAIRS + p] for j in range(nsub)], axis=0) for p in pairs]

        for g in range(SSM_GROUPS):
            ys = []
            ssq = None
            for p in range(g * SSM_GROUP_PAIRS, (g + 1) * SSM_GROUP_PAIRS):
                y = intra[p] + inter[p] * eg_l[p] + dch_ref[:, pcols(p)] * x_l[p]
                y = y * _silu(z_ref[prows(pb), pcols(p)])
                ys.append(y)
                sq = jnp.sum(y * y, axis=-1, keepdims=True)
                ssq = sq if ssq is None else ssq + sq
            inv = lax.rsqrt(ssq * (1.0 / SSM_GROUP_CH) + EPS)
            for pp, y in enumerate(ys):
                o_scr[prows(pb), pcols(g * SSM_GROUP_PAIRS + pp)] = y * inv

    y_ref[...] = (o_scr[...] * nw_ref[...]).astype(BF16)

    if geom.vrows == CHUNK:
        @pl.when(tile == pl.num_programs(1) - 1)
        def _():
            for s in range(geom.nseq):
                for p in range(SSM_PAIRS):
                    hout_ref[s, p] = h_scr[s, p].T


def _ssd(mix, rest, conv, h0, par, dch, nw, seld, bsz, seq):
    geom, nt = _geom(bsz, seq)
    tl = geom.nseq * geom.nch * geom.vrows
    tok = lambda blk: (lambda b, i: (b * nt + i, blk))
    st_shape = (geom.nseq, SSM_PAIRS, 2 * SSM_HEADDIM, SSM_DSTATE)
    st_spec = pl.BlockSpec(st_shape, lambda b, i: (b, 0, 0, 0))
    conv_ops, conv_specs, conv_scratch = _conv_operands(conv, geom, SSM_CONV_CH, MIX_XBC // SSM_CONV_CH)
    return pl.pallas_call(
        functools.partial(_ssd_body, geom=geom),
        grid=(bsz // geom.nseq, nt),
        in_specs=[pl.BlockSpec((tl, SSM_CONV_CH), tok(MIX_XBC // SSM_CONV_CH)),
                  pl.BlockSpec((tl, LANES), tok(REST_SMALL // LANES)),
                  pl.BlockSpec((tl, SSM_DINNER), tok(REST_ZS // SSM_DINNER))] + conv_specs + [
                  st_spec, _const_spec(par.shape), _const_spec(dch.shape), _const_spec(nw.shape),
                  _const_spec(seld.shape)],
        out_specs=[pl.BlockSpec((tl, SSM_DINNER), lambda b, i: (b * nt + i, 0)), st_spec],
        out_shape=[jax.ShapeDtypeStruct((bsz * seq, SSM_DINNER), BF16),
                   jax.ShapeDtypeStruct((bsz,) + st_shape[1:], F32)],
        scratch_shapes=conv_scratch + [pltpu.VMEM((tl, 2 * LANES), F32)] + (
            [pltpu.VMEM((geom.nseq, SSM_PAIRS, SSM_DSTATE, 2 * SSM_HEADDIM), F32)]
            if conv is None else []) + [
            pltpu.VMEM((tl, SSM_DINNER), F32)],
        compiler_params=_params(("parallel", "arbitrary")),
        name="ssd",
    )(mix, rest, rest, *conv_ops, h0, par, dch, nw, seld)


def _pad_lanes(v, offset):
    out = jnp.zeros((LANES,), F32)
    return out.at[offset:offset + v.shape[0]].set(v.astype(F32))


def _pair_selection(first_lane, lanes_per_head, n_mats):
    heads_per_mat = 2 * LANES // lanes_per_head
    sel = np.zeros((n_mats, 2 * LANES, 2 * LANES), np.float32)
    for m in range(n_mats):
        for j in range(heads_per_mat):
            src = first_lane + m * heads_per_mat + j
            sel[m, src, j * lanes_per_head:(j + 1) * lanes_per_head] = 1.0
            sel[m, LANES + src, j * lanes_per_head:(j + 1) * lanes_per_head] = 1.0
    return jnp.asarray(sel, BF16)


def _pad_conv_state(st):
    bsz, _, ch = st.shape
    return jnp.concatenate([jnp.zeros((bsz, CONV_PAD - (CONV_W - 1), ch), F32), st.astype(F32)], axis=1)


def _layer(x, mem_k, mem_v, s_gdn, c_gdn, s_ssm, c_ssm, w):
    bsz, seq, _ = x.shape
    t = bsz * seq
    xf = x.reshape(t, D_MODEL)
    h, u = _ffn1(xf, w["norm_ff1"], w["w_ff1_in"], w["w_ff1_out"], w["norm_mix"])
    if seq >= CHUNK:
        cst8 = jnp.concatenate([_pad_conv_state(c_gdn), _pad_conv_state(c_ssm)], axis=2)
        mix, rest, tail = _in_proj_conv(u, w["w_mix"], w["w_rest"], w["conv_w"], w["conv_b"], cst8, seq)
        conv_gdn = conv_ssm = None
    else:
        rest = _in_proj(u, w["w_rest"], REST_TN, "in_proj_rest")
        mix = _in_proj(u, w["w_mix"], MIX_TN, "in_proj_mix")
        tail = mix.reshape(bsz, seq, MIX_COLS)[:, seq - CONV_PAD:, :]
        taps_first = lambda c: jnp.swapaxes(c.astype(F32), 0, 1)
        conv_gdn = (taps_first(c_gdn), w["conv_w"], w["conv_b"])
        conv_ssm = (taps_first(c_ssm), w["conv_w"], w["conv_b"])

    yg, s_gdn_new = _gdn(mix, rest, conv_gdn, s_gdn.astype(F32), w["gdn_par"], w["gdn_norm"],
                         w["sel_gdn"], bsz, seq)
    h0 = s_ssm.astype(F32).reshape(bsz, SSM_PAIRS, 2 * SSM_HEADDIM, SSM_DSTATE)
    ys, s_ssm_new = _ssd(mix, rest, conv_ssm, h0, w["ssm_par"], w["ssm_d_ch"], w["ssm_norm"],
                         w["sel_ssm"], bsz, seq)
    transposed = lambda m: jnp.swapaxes(m.reshape(bsz, N_MEM, MEM_WIDTH), 1, 2)
    ym = _memattn(rest, transposed(mem_k), transposed(mem_v), bsz, seq)
    y = _merge_ffn2(h, yg, ys, ym, rest, w["w_branch"], w["w_out"], w["norm_ff2"],
                    w["w_ff2_in"], w["w_ff2_out"], w["norm_final"])

    c_gdn_new = tail[:, CONV_PAD - (CONV_W - 1):, MIX_QKV:MIX_QKV + GDN_CONV_CH]
    c_ssm_new = tail[:, CONV_PAD - (CONV_W - 1):, MIX_XBC:MIX_XBC + SSM_CONV_CH]
    return (y.reshape(bsz, seq, D_MODEL), s_gdn_new, c_gdn_new,
            s_ssm_new.reshape(bsz, SSM_HEADS, SSM_HEADDIM, SSM_DSTATE), c_ssm_new)


def _row(v):
    return v.astype(F32).reshape(1, -1)


def _prep_weights(l, norm_ff1, w_ff1_in, w_ff1_out, norm_mix, w_in, gdn_conv_w, gdn_a_log,
                  gdn_dt_bias, gdn_norm, ssm_conv_w, ssm_conv_b, ssm_a_log, ssm_dt_bias, ssm_d,
                  ssm_norm, w_branch, w_out, norm_ff2, w_ff2_in, w_ff2_out, norm_final):
    row = _row

    w_mix, w_rest = _pack_w_in(jnp.swapaxes(w_in[l], 0, 1))

    def two_rows(a, b, offset):
        par = jnp.zeros((SUBLANES, LANES), F32)
        return par.at[0].set(_pad_lanes(a, offset)).at[1].set(_pad_lanes(b, offset))

    return {
        "norm_ff1": row(norm_ff1[l]), "w_ff1_in": w_ff1_in[l].astype(BF16),
        "w_ff1_out": w_ff1_out[l].astype(BF16), "norm_mix": row(norm_mix[l]),
        "w_mix": w_mix, "w_rest": w_rest,
        "conv_w": jnp.concatenate([gdn_conv_w[l], ssm_conv_w[l]], axis=1).astype(F32),
        "conv_b": jnp.concatenate([jnp.zeros((1, GDN_CONV_CH), F32), row(ssm_conv_b[l])], axis=1),
        "gdn_par": two_rows(gdn_a_log[l], gdn_dt_bias[l], SM_A),
        "gdn_norm": row(gdn_norm[l]),
        "sel_gdn": _pair_selection(0, LANES, GDN_PAIRS),
        "ssm_par": two_rows(ssm_a_log[l], ssm_dt_bias[l], SM_DT),
        "ssm_d_ch": jnp.repeat(ssm_d[l].astype(F32), SSM_HEADDIM).reshape(1, -1),
        "ssm_norm": row(ssm_norm[l]),
        "sel_ssm": _pair_selection(SM_DT, SSM_HEADDIM, SSM_PAIRS // 2),
        "w_branch": w_branch[l].astype(BF16), "w_out": w_out[l].astype(BF16),
        "norm_ff2": row(norm_ff2[l]), "w_ff2_in": w_ff2_in[l].astype(BF16),
        "w_ff2_out": w_ff2_out[l].astype(BF16), "norm_final": row(norm_final),
    }


def kernel(x_prompt, x_sample, mem_prompt, state_gdn, state_gdn_conv, state_ssm, state_ssm_conv,
           cache_mem_k, cache_mem_v, norm_ff1, w_ff1_in, w_ff1_out, norm_mix, w_in,
           gdn_conv_w, gdn_a_log, gdn_dt_bias, gdn_norm, ssm_conv_w, ssm_conv_b, ssm_a_log,
           ssm_dt_bias, ssm_d, ssm_norm, norm_mem, w_mem_kv, w_branch, w_out,
           norm_ff2, w_ff2_in, w_ff2_out, norm_final):
    assert w_in.shape[0] == 1, "the kernels implement the single-layer configuration"
    l = 0
    bp = x_prompt.shape[0]
    w = _prep_weights(l, norm_ff1, w_ff1_in, w_ff1_out, norm_mix, w_in, gdn_conv_w, gdn_a_log,
                      gdn_dt_bias, gdn_norm, ssm_conv_w, ssm_conv_b, ssm_a_log, ssm_dt_bias, ssm_d,
                      ssm_norm, w_branch, w_out, norm_ff2, w_ff2_in, w_ff2_out, norm_final)

    n_mem_tok = mem_prompt.shape[0] * mem_prompt.shape[1]
    kv = _memkv(mem_prompt.reshape(n_mem_tok, D_MODEL), _row(norm_mem[l]), w_mem_kv[l].astype(BF16))
    mk = kv[:, :MEM_WIDTH].reshape(bp, N_MEM, MEM_HEADS, MEM_HEAD_DIM)
    mv = kv[:, MEM_WIDTH:].reshape(bp, N_MEM, MEM_HEADS, MEM_HEAD_DIM)
    dtp = x_prompt.dtype
    yp, sgp, cgp, ssp, csp = _layer(
        x_prompt, mk, mv,
        jnp.zeros((bp, GDN_HEADS, GDN_DK, GDN_DV), dtp), jnp.zeros((bp, CONV_W - 1, GDN_CONV_CH), dtp),
        jnp.zeros((bp, SSM_HEADS, SSM_HEADDIM, SSM_DSTATE), dtp),
        jnp.zeros((bp, CONV_W - 1, SSM_CONV_CH), dtp), w)
    ys_, sgs, cgs, sss, css = _layer(
        x_sample, cache_mem_k[l], cache_mem_v[l], state_gdn[l], state_gdn_conv[l],
        state_ssm[l], state_ssm_conv[l], w)

    lead = lambda a: a[None]
    return (yp, ys_, lead(sgp), lead(cgp), lead(ssp), lead(csp), lead(mk), lead(mv),
            lead(sgs), lead(cgs), lead(sss), lead(css))
```

```python
import collections
import functools

import numpy as np
import jax
import jax.numpy as jnp
from jax import lax
from jax.experimental import pallas as pl
from jax.experimental.pallas import tpu as pltpu

F32 = jnp.float32
BF16 = jnp.bfloat16
EPS = 1e-6

D_MODEL = 1024
FFN_DIM = 2816
CONV_W = 4
CHUNK = 64
LANES = 128
SUBLANES = 8

GDN_HEADS = 8
GDN_PAIRS = GDN_HEADS // 2
GDN_DK = 128
GDN_DV = 128
GDN_QK = GDN_HEADS * GDN_DK
GDN_V = GDN_HEADS * GDN_DV
GDN_CONV_CH = 2 * GDN_QK + GDN_V

SSM_DINNER = 2048
SSM_HEADDIM = 64
SSM_HEADS = 32
SSM_GROUPS = 4
SSM_DSTATE = 128
SSM_CONV_CH = SSM_DINNER + 2 * SSM_GROUPS * SSM_DSTATE
SSM_PAIRS = SSM_HEADS // 2
SSM_GROUP_PAIRS = SSM_PAIRS // SSM_GROUPS
SSM_GROUP_CH = SSM_DINNER // SSM_GROUPS

N_MEM = 256
MEM_HEADS = 4
MEM_HEAD_DIM = 64
MEM_WIDTH = MEM_HEADS * MEM_HEAD_DIM

MIX_QKV = 0
MIX_XBC = MIX_QKV + GDN_CONV_CH
MIX_COLS = MIX_XBC + SSM_CONV_CH
REST_GATE = 0
REST_ZG = REST_GATE + 3 * D_MODEL
REST_ZS = REST_ZG + GDN_V
REST_QM = REST_ZS + SSM_DINNER
REST_SMALL = REST_QM + MEM_WIDTH
REST_COLS = REST_SMALL + LANES
IN_SPLITS = (GDN_CONV_CH, GDN_HEADS, GDN_HEADS, GDN_V, SSM_DINNER, SSM_CONV_CH, SSM_HEADS, MEM_WIDTH,
             3 * D_MODEL)
SM_A = 0
SM_B = GDN_HEADS
SM_DT = 2 * GDN_HEADS

VMEM_LIMIT = 56 * 1024 * 1024


def _params(semantics):
    return pltpu.CompilerParams(dimension_semantics=semantics, vmem_limit_bytes=VMEM_LIMIT)


def _mm(a, b):
    return jnp.dot(a, b, preferred_element_type=F32)


def _mm_nt(a, b):
    return lax.dot_general(a, b, (((1,), (1,)), ((), ())), preferred_element_type=F32)


def _mm_tn(a, b):
    return lax.dot_general(a, b, (((0,), (0,)), ((), ())), preferred_element_type=F32)


def _split3(x):
    hi = x.astype(BF16)
    r1 = x - hi.astype(F32)
    mid = r1.astype(BF16)
    lo = (r1 - mid.astype(F32)).astype(BF16)
    return hi, mid, lo


def _split2_lanes(x):
    hi = x.astype(BF16)
    lo = (x - hi.astype(F32)).astype(BF16)
    return jnp.concatenate([hi, lo], axis=1)


def _rms(x, g):
    return x * lax.rsqrt(jnp.mean(x * x, axis=-1, keepdims=True) + EPS) * g


def _sigmoid(x):
    return 1.0 / (1.0 + jnp.exp2(x * float(-1.0 / np.log(2.0))))


def _silu(x):
    return x * _sigmoid(x)


def _softplus(x):
    return jnp.maximum(x, 0.0) + jnp.log1p(jnp.exp(-jnp.abs(x)))


def _const_spec(shape):
    nd = len(shape)
    return pl.BlockSpec(shape, lambda *_: (0,) * nd, pipeline_mode=pl.Buffered(1))


FFN_TM = 512
MXU_WIDTH = 256
FFN_CHUNKS = ((0, 6 * MXU_WIDTH), (6 * MXU_WIDTH, FFN_DIM))


def _ffn_compute(x, g_ref, wi_ref, wo_ref):
    xn = _rms(x, g_ref[...]).astype(BF16)
    acc = None
    for lo, hi in FFN_CHUNKS:
        gate = _mm(xn, wi_ref[:, lo:hi])
        up = _mm(xn, wi_ref[:, FFN_DIM + lo:FFN_DIM + hi])
        act = (_silu(gate) * up).astype(BF16)
        part = _mm(act, wo_ref[lo:hi, :])
        acc = part if acc is None else acc + part
    return x + 0.5 * acc


def _ffn1_body(x_ref, g_ref, wi_ref, wo_ref, g2_ref, h_ref, u_ref):
    h = _ffn_compute(x_ref[...], g_ref, wi_ref, wo_ref)
    h_ref[...] = h
    u_ref[...] = _rms(h, g2_ref[...]).astype(BF16)


def _ffn1(x, g, wi, wo, g2):
    t = x.shape[0]
    tm = min(FFN_TM, t)
    row = lambda i: (i, 0)
    return pl.pallas_call(
        _ffn1_body,
        grid=(t // tm,),
        in_specs=[pl.BlockSpec((tm, D_MODEL), row), _const_spec(g.shape), _const_spec(wi.shape),
                  _const_spec(wo.shape), _const_spec(g2.shape)],
        out_specs=[pl.BlockSpec((tm, D_MODEL), row), pl.BlockSpec((tm, D_MODEL), row)],
        out_shape=[jax.ShapeDtypeStruct((t, D_MODEL), F32), jax.ShapeDtypeStruct((t, D_MODEL), BF16)],
        compiler_params=_params(("parallel",)),
        name="ffn1",
    )(x, g, wi, wo, g2)


PACK_ROWS = 128


def _pack_body(wt_ref, mix_ref, rest_ref):
    src = np.cumsum((0,) + IN_SPLITS).tolist()
    qkv, ab, _, zg, zs, xbc, dt, qm, gate = [(src[i], IN_SPLITS[i]) for i in range(len(IN_SPLITS))]
    for ref, dst, (lo, n) in ((mix_ref, MIX_QKV, qkv), (mix_ref, MIX_XBC, xbc),
                              (rest_ref, REST_GATE, gate), (rest_ref, REST_ZG, zg),
                              (rest_ref, REST_ZS, zs), (rest_ref, REST_QM, qm)):
        assert lo % SUBLANES == 0 and n % LANES == 0
        for t in range(n // LANES):
            ref[:, dst + t * LANES:dst + (t + 1) * LANES] = (
                wt_ref[lo + t * LANES:lo + (t + 1) * LANES, :].T.astype(BF16))
    assert SM_A == 0 and SM_DT == 2 * GDN_HEADS
    small = jnp.concatenate(
        [wt_ref[ab[0]:ab[0] + 2 * GDN_HEADS, :], wt_ref[dt[0]:dt[0] + SSM_HEADS, :],
         jnp.zeros((LANES - 2 * GDN_HEADS - SSM_HEADS, wt_ref.shape[1]), F32)], axis=0)
    rest_ref[:, REST_SMALL:] = small.T.astype(BF16)


def _pack_w_in(wt):
    rows = lambda i: (i, 0)
    return pl.pallas_call(
        _pack_body,
        grid=(D_MODEL // PACK_ROWS,),
        in_specs=[pl.BlockSpec((wt.shape[0], PACK_ROWS), lambda i: (0, i))],
        out_specs=[pl.BlockSpec((PACK_ROWS, MIX_COLS), rows), pl.BlockSpec((PACK_ROWS, REST_COLS), rows)],
        out_shape=[jax.ShapeDtypeStruct((D_MODEL, MIX_COLS), BF16),
                   jax.ShapeDtypeStruct((D_MODEL, REST_COLS), BF16)],
        compiler_params=_params(("parallel",)),
        name="pack_w_in",
    )(wt)


PROJ_TM = 1024
PROJ_ROWS = 256
MIX_TN = MIX_COLS // 4
REST_TN = REST_COLS // 3


def _proj_body(u_ref, w_ref, o_ref):
    o_ref[...] = _mm(u_ref[...], w_ref[...])


def _in_proj(u, w, tn, name):
    t = u.shape[0]
    tm = min(PROJ_TM, t)
    cols = w.shape[1]
    return pl.pallas_call(
        _proj_body,
        grid=(cols // tn, t // tm),
        in_specs=[pl.BlockSpec((tm, D_MODEL), lambda j, i: (i, 0)),
                  pl.BlockSpec((D_MODEL, tn), lambda j, i: (0, j))],
        out_specs=pl.BlockSpec((tm, tn), lambda j, i: (i, j)),
        out_shape=jax.ShapeDtypeStruct((t, cols), F32),
        compiler_params=_params(("parallel", "parallel")),
        name=name,
    )(u, w)


def _causal_conv_silu(xa, w, bias):
    shape = (SUBLANES, LANES)
    sub = lax.broadcasted_iota(jnp.int32, shape, 0)
    columns = []
    for c in range(xa.shape[1] // LANES):
        cols = slice(c * LANES, (c + 1) * LANES)
        taps = [jnp.broadcast_to(w[j:j + 1, cols], shape) for j in range(CONV_W)]
        b = jnp.broadcast_to(bias[:, cols], shape)
        before = xa[CONV_PAD - SUBLANES:CONV_PAD, cols]
        outs = []
        for g in range(CONV_PAD // SUBLANES, xa.shape[0] // SUBLANES):
            x = xa[g * SUBLANES:(g + 1) * SUBLANES, cols]
            y = x * taps[CONV_W - 1] + b
            for k in range(1, CONV_W):
                shifted = pltpu.roll(jnp.where(sub >= SUBLANES - k, before, x), k, axis=0)
                y = y + shifted * taps[CONV_W - 1 - k]
            outs.append(_silu(y))
            before = x
        columns.append(jnp.concatenate(outs, axis=0))
    return jnp.concatenate(columns, axis=1)


CONVPROJ_TM = 512
CONVPROJ_STEPS = 3
PROJ_STRIPS = 8


def _strips(width, n):
    tiles = width // LANES
    assert width % LANES == 0 and tiles >= n
    bounds = [LANES * ((tiles * k) // n) for k in range(n + 1)]
    return [slice(bounds[k], bounds[k + 1]) for k in range(n)]


def _proj_conv_body(u_ref, wm_ref, wr_ref, cw_ref, cb_ref, cst_ref, mix_ref, rest_ref, tail_ref,
                    carry, *, tiles_per_seq):
    i = pl.program_id(1)

    @pl.when(i % tiles_per_seq == 0)
    def _():
        carry[0:CONV_PAD, :] = cst_ref[0]

    strips_m = _strips(mix_ref.shape[1], PROJ_STRIPS)
    strips_r = _strips(rest_ref.shape[1], PROJ_STRIPS)
    for r in range(u_ref.shape[0] // PROJ_ROWS):
        rows = slice(r * PROJ_ROWS, (r + 1) * PROJ_ROWS)
        u = u_ref[rows, :]
        for cm, cr in zip(strips_m, strips_r):
            carry[CONV_PAD:, cm] = _mm(u, wm_ref[:, cm])
            rest_ref[rows, cr] = _mm(u, wr_ref[:, cr])
            mix_ref[rows, cm] = _causal_conv_silu(carry.at[:, cm], cw_ref[:, cm], cb_ref[:, cm])
            carry[0:CONV_PAD, cm] = carry[PROJ_ROWS:, cm]
    tail_ref[0] = carry[0:CONV_PAD, :]


def _in_proj_conv(u, w_mix, w_rest, cw, cb, cst8, seq):
    t = u.shape[0]
    tm = min(CONVPROJ_TM, seq)
    assert seq % tm == 0 and tm % PROJ_ROWS == 0
    tps = seq // tm
    tn_m = MIX_COLS // CONVPROJ_STEPS
    tn_r = REST_COLS // CONVPROJ_STEPS
    col = lambda j, i: (0, j)
    tile = lambda j, i: (i, j)
    seq_blk = lambda j, i: (i // tps, 0, j)
    return pl.pallas_call(
        functools.partial(_proj_conv_body, tiles_per_seq=tps),
        grid=(CONVPROJ_STEPS, t // tm),
        in_specs=[pl.BlockSpec((tm, D_MODEL), lambda j, i: (i, 0)),
                  pl.BlockSpec((D_MODEL, tn_m), col), pl.BlockSpec((D_MODEL, tn_r), col),
                  pl.BlockSpec((CONV_W, tn_m), col), pl.BlockSpec((1, tn_m), col),
                  pl.BlockSpec((1, CONV_PAD, tn_m), seq_blk)],
        out_specs=[pl.BlockSpec((tm, tn_m), tile), pl.BlockSpec((tm, tn_r), tile),
                   pl.BlockSpec((1, CONV_PAD, tn_m), seq_blk)],
        out_shape=[jax.ShapeDtypeStruct((t, MIX_COLS), F32), jax.ShapeDtypeStruct((t, REST_COLS), F32),
                   jax.ShapeDtypeStruct((t // seq, CONV_PAD, MIX_COLS), F32)],
        scratch_shapes=[pltpu.VMEM((CONV_PAD + PROJ_ROWS, tn_m), F32)],
        compiler_params=_params(("parallel", "arbitrary")),
        name="in_proj_conv",
    )(u, w_mix, w_rest, cw, cb, cst8)


def _memkv_body(x_ref, g_ref, w_ref, o_ref):
    o_ref[...] = _mm(_rms(x_ref[...], g_ref[...]).astype(BF16), w_ref[...])


def _memkv(mem, g, w):
    t = mem.shape[0]
    tm = min(512, t)
    row = lambda i: (i, 0)
    return pl.pallas_call(
        _memkv_body,
        grid=(t // tm,),
        in_specs=[pl.BlockSpec((tm, D_MODEL), row), _const_spec(g.shape), _const_spec(w.shape)],
        out_specs=pl.BlockSpec((tm, 2 * MEM_WIDTH), row),
        out_shape=jax.ShapeDtypeStruct((t, 2 * MEM_WIDTH), F32),
        compiler_params=_params(("parallel",)),
        name="memkv",
    )(mem, g, w)


MERGE_TM = 256


def _merge_ffn2_body(h_ref, yg_ref, ys_ref, ym_ref, gate_ref, wb_ref, wout_ref,
                     g_ref, wi_ref, wo_ref, gf_ref, y_ref):
    gates = gate_ref[...]
    merged = (_sigmoid(gates[:, 0:D_MODEL]) * _mm(yg_ref[...], wb_ref[0:GDN_V, :])
              + _sigmoid(gates[:, D_MODEL:2 * D_MODEL])
              * _mm(ys_ref[...], wb_ref[GDN_V:GDN_V + SSM_DINNER, :])
              + _sigmoid(gates[:, 2 * D_MODEL:3 * D_MODEL])
              * _mm(ym_ref[...], wb_ref[GDN_V + SSM_DINNER:, :]))
    h = h_ref[...] + _mm(merged.astype(BF16), wout_ref[...])
    h = _ffn_compute(h, g_ref, wi_ref, wo_ref)
    y_ref[...] = _rms(h, gf_ref[...])


def _merge_ffn2(h, yg, ys, ym, proj, wb, wout, g, wi, wo, gf):
    t = h.shape[0]
    tm = min(MERGE_TM, t)
    row = lambda i: (i, 0)
    gate_blk = REST_GATE // (3 * D_MODEL)
    return pl.pallas_call(
        _merge_ffn2_body,
        grid=(t // tm,),
        in_specs=[pl.BlockSpec((tm, D_MODEL), row), pl.BlockSpec((tm, GDN_V), row),
                  pl.BlockSpec((tm, SSM_DINNER), row), pl.BlockSpec((tm, MEM_WIDTH), row),
                  pl.BlockSpec((tm, 3 * D_MODEL), lambda i: (i, gate_blk)),
                  _const_spec(wb.shape), _const_spec(wout.shape), _const_spec(g.shape),
                  _const_spec(wi.shape), _const_spec(wo.shape), _const_spec(gf.shape)],
        out_specs=pl.BlockSpec((tm, D_MODEL), row),
        out_shape=jax.ShapeDtypeStruct((t, D_MODEL), F32),
        compiler_params=_params(("parallel",)),
        name="merge_ffn2",
    )(h, yg, ys, ym, proj, wb, wout, g, wi, wo, gf)


MEMATTN_TL = 512
MEMATTN_NSEQ = 16


def _memattn_body(q_ref, k_ref, v_ref, y_ref, *, nseq, rows):
    lane = lax.broadcasted_iota(jnp.int32, (rows, MEM_WIDTH), 1)
    masks = [(lane >= hh * MEM_HEAD_DIM) & (lane < (hh + 1) * MEM_HEAD_DIM) for hh in range(MEM_HEADS)]
    seqs = range(nseq)
    q = [q_ref[s * rows:(s + 1) * rows, :] for s in seqs]
    q4 = [jnp.concatenate([jnp.where(m, x, 0.0) for m in masks], axis=0).astype(BF16) for x in q]
    sc = [_mm(q4[s], k_ref[s].astype(BF16)) * (MEM_HEAD_DIM ** -0.5) for s in seqs]
    p = [jnp.exp(x - jnp.max(x, axis=-1, keepdims=True)) for x in sc]
    p = [(x / jnp.sum(x, axis=-1, keepdims=True)).astype(BF16) for x in p]
    o4 = [_mm_nt(p[s], v_ref[s].astype(BF16)) for s in seqs]
    ys = []
    for x in o4:
        y = jnp.where(masks[0], x[0:rows], 0.0)
        for hh in range(1, MEM_HEADS):
            y = y + jnp.where(masks[hh], x[hh * rows:(hh + 1) * rows], 0.0)
        ys.append(y)
    y_ref[...] = jnp.concatenate(ys, axis=0).astype(BF16)


def _memattn(proj, k, v, bsz, seq):
    if seq >= MEMATTN_TL:
        nseq, rows, nt = 1, MEMATTN_TL, seq // MEMATTN_TL
    else:
        nseq, rows, nt = MEMATTN_NSEQ, seq, 1
    assert bsz % nseq == 0 and seq % rows == 0
    tl = nseq * rows
    qm_blk = REST_QM // MEM_WIDTH
    kv_spec = pl.BlockSpec((nseq, MEM_WIDTH, N_MEM), lambda b, i: (b, 0, 0))
    return pl.pallas_call(
        functools.partial(_memattn_body, nseq=nseq, rows=rows),
        grid=(bsz // nseq, nt),
        in_specs=[pl.BlockSpec((tl, MEM_WIDTH), lambda b, i: (b * nt + i, qm_blk)), kv_spec, kv_spec],
        out_specs=pl.BlockSpec((tl, MEM_WIDTH), lambda b, i: (b * nt + i, 0)),
        out_shape=jax.ShapeDtypeStruct((bsz * seq, MEM_WIDTH), BF16),
        compiler_params=_params(("parallel", "arbitrary")),
        name="memattn",
    )(proj, k, v)


SEQ_TL = 256
LONG_NSEQ = 2
SHORT_NSEQ = 8
CONV_PAD = SUBLANES
MIN_MXU_ROWS = 16

Geom = collections.namedtuple("Geom", ["nseq", "nch", "vrows"])


def _geom(bsz, seq):
    if seq >= CHUNK:
        assert seq % SEQ_TL == 0
        nseq = LONG_NSEQ if bsz % LONG_NSEQ == 0 else 1
        return Geom(nseq, SEQ_TL // CHUNK, CHUNK), seq // SEQ_TL
    assert bsz % SHORT_NSEQ == 0 and seq % SUBLANES == 0 and (SHORT_NSEQ * seq) % CHUNK == 0
    return Geom(SHORT_NSEQ, 1, seq), 1


class _TokenView:
    def __init__(self, ref):
        self.ref = ref

    def _locate(self, rows):
        per = self.ref.shape[1]
        s = rows.start // per
        assert rows.stop <= (s + 1) * per
        return s, slice(rows.start - s * per, rows.stop - s * per)

    def __getitem__(self, idx):
        rows, cols = idx
        if len(self.ref.shape) == 2:
            return self.ref[rows, cols]
        if rows == slice(None):
            v = self.ref[:, :, cols]
            return v.reshape(v.shape[0] * v.shape[1], v.shape[2])
        s, r = self._locate(rows)
        return self.ref[s, r, cols]

    def __setitem__(self, idx, value):
        rows, cols = idx
        if len(self.ref.shape) == 2:
            self.ref[rows, cols] = value
        elif rows == slice(None):
            nseq, per = self.ref.shape[0], self.ref.shape[1]
            self.ref[:, :, cols] = value.reshape(nseq, per, value.shape[1])
        else:
            s, r = self._locate(rows)
            self.ref[s, r, cols] = value


def _short_conv_silu(x_ref, cst_ref, cw_ref, cb_ref, act, geom):
    assert geom.nch == 1 and geom.vrows < CHUNK and CONV_PAD == SUBLANES
    vr = geom.vrows
    sub = lax.broadcasted_iota(jnp.int32, (SUBLANES, LANES), 0)
    for cb in range(x_ref.shape[1] // LANES):
        cols = slice(cb * LANES, (cb + 1) * LANES)
        planes = [cst_ref[j, :, cols] for j in range(CONV_W - 1)]
        for s in range(geom.nseq):
            before = jnp.zeros((SUBLANES, LANES), F32)
            for j, plane in enumerate(planes):
                before = jnp.where(sub == SUBLANES - (CONV_W - 1) + j,
                                   jnp.broadcast_to(plane[s:s + 1, :], (SUBLANES, LANES)), before)
            xa = jnp.concatenate([before, x_ref[s * vr:(s + 1) * vr, cols]], axis=0)
            act[s * vr:(s + 1) * vr, cols] = _causal_conv_silu(xa, cw_ref[:, cols], cb_ref[:, cols])


def _pair_masks(vr):
    row = lax.broadcasted_iota(jnp.int32, (CHUNK, LANES), 0)
    lane = lax.broadcasted_iota(jnp.int32, (CHUNK, LANES), 1)
    col = jnp.where(lane < CHUNK, lane, lane - CHUNK)
    same = (row // vr) == (col // vr)
    return row, lane, col, same


def _block_diag(x, left):
    zero = jnp.zeros_like(x)
    return jnp.concatenate([jnp.where(left, x, zero), jnp.where(left, zero, x)], axis=0)


def _chunk_cumsums(gates, nprob, vr):
    r = lax.broadcasted_iota(jnp.int32, (CHUNK, CHUNK), 0)
    c = lax.broadcasted_iota(jnp.int32, (CHUNK, CHUNK), 1)
    same = (r // vr) == (c // vr)
    sums = jnp.concatenate([jnp.where(same & (r >= c), 1.0, 0.0), jnp.where(same, 1.0, 0.0)],
                           axis=0).astype(BF16)
    sums3 = jnp.concatenate([sums, sums, sums], axis=1)
    pieces = [jnp.concatenate(_split3(gates[pb * CHUNK:(pb + 1) * CHUNK, 0:LANES]), axis=0)
              for pb in range(nprob)]
    out = _mm(sums3, jnp.concatenate(pieces, axis=1))
    cols = lambda pb: slice(pb * LANES, (pb + 1) * LANES)
    return ([out[0:CHUNK, cols(pb)] for pb in range(nprob)],
            [out[CHUNK:, cols(pb)] for pb in range(nprob)])


def _rows_to_columns(e):
    pieces = jnp.concatenate(_split3(e), axis=0)
    pad = jnp.zeros((MIN_MXU_ROWS - pieces.shape[0], e.shape[1]), BF16)
    row = lax.broadcasted_iota(jnp.int32, (MIN_MXU_ROWS, LANES), 0)
    ones = jnp.where(row < pieces.shape[0], 1.0, 0.0).astype(BF16)
    return _mm_tn(jnp.concatenate([pieces, pad], axis=0), ones)


def _pad_rows(x):
    if x.shape[0] >= MIN_MXU_ROWS:
        return x
    return jnp.concatenate([x, jnp.zeros((MIN_MXU_ROWS - x.shape[0], x.shape[1]), x.dtype)], axis=0)


def _row_form(gc):
    return jnp.concatenate([gc, pltpu.roll(gc, LANES - 1, axis=1)], axis=0).T


def _gdn_body(*refs, geom):
    if geom.vrows == CHUNK:
        (x_ref, sm_ref, z_ref, s0_ref, par_ref, gn_ref, selp_ref, y_ref, sout_ref,
         gates, s_scr, u_scr, w_scr, qd_scr, kd_scr, qk_scr, egl_scr, o_scr) = refs
        act = _TokenView(x_ref)
    else:
        (x_ref, sm_ref, z_ref, cst_ref, cw_ref, cb_ref, s0_ref, par_ref, gn_ref, selp_ref, y_ref,
         sout_ref, act, gates, s_scr, u_scr, w_scr, qd_scr, kd_scr, qk_scr, egl_scr, o_scr) = refs
        _short_conv_silu(x_ref, cst_ref, cw_ref, cb_ref, act, geom)
    tile = pl.program_id(1)
    first = tile == 0
    vr = geom.vrows
    nsub = CHUNK // vr
    nprob = geom.nseq * geom.nch // nsub

    @pl.when(first)
    def _():
        s_scr[...] = s0_ref[...]

    sm = _TokenView(sm_ref)[:, :]
    lane_g = lax.broadcasted_iota(jnp.int32, sm.shape, 1)
    head_lane = lane_g < GDN_HEADS
    gates[:, 0:LANES] = jnp.where(
        head_lane, -jnp.exp(par_ref[0:1, :]) * _softplus(sm + par_ref[1:2, :]), 0.0)
    gates[:, LANES:] = jnp.where(head_lane, pltpu.roll(_sigmoid(sm), LANES - SM_B, axis=1), 0.0)

    row, lane, col, same = _pair_masks(vr)
    left = lane < CHUNK
    incl = same & (row >= col)
    strict = same & (row > col)
    eye = jnp.where(row == col, 1.0, 0.0).astype(F32)
    zeros_h = jnp.zeros((CHUNK, LANES), F32)
    hcols = lambda h: slice(h * LANES, (h + 1) * LANES)
    prows = lambda pb, n=CHUNK: slice(pb * n, (pb + 1) * n)

    gcs, gls = _chunk_cumsums(gates, nprob, vr)
    lhs, xts = [], []
    for pb in range(nprob):
        gc = gcs[pb]
        quantities = jnp.concatenate(
            [gc, gates[prows(pb), LANES:2 * LANES], jnp.exp(gc), jnp.exp(gls[pb] - gc)], axis=0)
        lhs.append(_split2_lanes(quantities))
        xts.append(_row_form(gc))

    def l2n(t):
        return t * lax.rsqrt(jnp.sum(t * t, axis=-1, keepdims=True) + EPS)

    def stacked(e, qi):
        return jnp.concatenate([e[qi * CHUNK:(qi + 1) * CHUNK, 0:LANES],
                                e[qi * CHUNK:(qi + 1) * CHUNK, LANES:]], axis=0)

    def phase1(pbs):
        items = [(pb, p) for pb in pbs for p in range(GDN_PAIRS)]
        exp_ = [_mm(lhs[pb], selp_ref[p]) for pb, p in items]
        decay_l, kq_l, rhs_l, qd_l, kd_l = [], [], [], [], []
        for (pb, p), e in zip(items, exp_):
            ha, hb = 2 * p, 2 * p + 1
            qa, qb = [l2n(act[prows(pb), hcols(h)]) * (GDN_DK ** -0.5) for h in (ha, hb)]
            ka, kb = [l2n(act[prows(pb), hcols(GDN_HEADS + h)]) for h in (ha, hb)]
            va, vb = [act[prows(pb), hcols(2 * GDN_HEADS + h)] for h in (ha, hb)]
            beta2, eg2, kdec2 = stacked(e, 1), stacked(e, 2), stacked(e, 3)
            k2 = jnp.concatenate([ka, kb], axis=0)
            kbeta2 = k2 * beta2
            gci = jnp.where(left, e[0:CHUNK, 0:LANES], e[0:CHUNK, LANES:])
            gcj = jnp.broadcast_to(xts[pb][ha:ha + 1, :], (CHUNK, LANES))
            decay_l.append(jnp.where(incl, jnp.exp(jnp.where(incl, gci - gcj, 0.0)), 0.0))
            k_bd = jnp.concatenate([jnp.concatenate([ka, zeros_h], axis=1),
                                    jnp.concatenate([zeros_h, kb], axis=1)], axis=0).astype(BF16)
            kbq = jnp.concatenate([jnp.concatenate([kbeta2[0:CHUNK], kbeta2[CHUNK:]], axis=1),
                                   jnp.concatenate([qa, qb], axis=1)], axis=0).astype(BF16)
            kq_l.append(_mm_nt(kbq, k_bd))
            rhs_l.append(jnp.concatenate(
                [jnp.concatenate([va, vb], axis=0) * beta2, kbeta2 * eg2], axis=1).astype(BF16))
            qd_l.append(jnp.concatenate([qa, qb], axis=0) * eg2)
            kd_l.append(k2 * kdec2)
            for sub in range(nsub):
                last = 2 * CHUNK + (sub + 1) * vr - 1
                slot = prows(pb * nsub + sub, SUBLANES)
                egl_scr[slot, hcols(ha)] = jnp.broadcast_to(e[last:last + 1, 0:LANES], (SUBLANES, LANES))
                egl_scr[slot, hcols(hb)] = jnp.broadcast_to(e[last:last + 1, LANES:], (SUBLANES, LANES))

        n_pow = [-jnp.where(strict, kq[0:CHUNK] * d, 0.0) for kq, d in zip(kq_l, decay_l)]
        t_inv = [eye + n for n in n_pow]
        for _ in range(int(np.ceil(np.log2(vr))) - 1):
            n_pow = [_mm(n.astype(BF16), _block_diag(n, left).astype(BF16)) for n in n_pow]
            t_inv = [t + _mm(t.astype(BF16), _block_diag(n, left).astype(BF16))
                     for t, n in zip(t_inv, n_pow)]

        uw_l = [_mm(_block_diag(t, left).astype(BF16), rhs) for t, rhs in zip(t_inv, rhs_l)]
        for (pb, p), uw, kq, d, qd, kd in zip(items, uw_l, kq_l, decay_l, qd_l, kd_l):
            qk_scr[prows(pb, 2 * CHUNK), hcols(p)] = _block_diag(kq[CHUNK:] * d, left).astype(BF16)
            for idx, h in enumerate((2 * p, 2 * p + 1)):
                sl = slice(idx * CHUNK, (idx + 1) * CHUNK)
                u_scr[prows(pb), hcols(h)] = uw[sl, 0:LANES]
                w_scr[prows(pb), hcols(h)] = uw[sl, LANES:]
                qd_scr[prows(pb), hcols(h)] = qd[sl]
                kd_scr[prows(pb), hcols(h)] = kd[sl]

    def phase2(c):
        sh = [(s, h) for s in range(geom.nseq) for h in range(GDN_HEADS)]
        trows = lambda s: prows(s * geom.nch + c, vr)
        wqs = [_mm(jnp.concatenate([w_scr[trows(s), hcols(h)], qd_scr[trows(s), hcols(h)]],
                                   axis=0).astype(BF16), s_scr[s, h].astype(BF16)) for s, h in sh]
        vnew = [u_scr[trows(s), hcols(h)] - w[0:vr] for (s, h), w in zip(sh, wqs)]
        for (s, h), vn in zip(sh, vnew):
            slot = (s * geom.nch + c) * SUBLANES
            s_scr[s, h] = (s_scr[s, h] * egl_scr[slot:slot + 1, hcols(h)]
                           + _mm_tn(_pad_rows(kd_scr[trows(s), hcols(h)]).astype(BF16),
                                    _pad_rows(vn).astype(BF16)))
        for pb in range(c * nsub, geom.nseq * geom.nch, geom.nch * nsub):
            subs = [pb // geom.nch + j for j in range(nsub)]
            for p in range(GDN_PAIRS):
                heads = (2 * p, 2 * p + 1)
                vn2 = jnp.concatenate([vnew[s * GDN_HEADS + h] for h in heads for s in subs],
                                      axis=0).astype(BF16)
                intra = _mm(qk_scr[prows((pb // nsub), 2 * CHUNK), hcols(p)], vn2)
                for idx, h in enumerate(heads):
                    for j, s in enumerate(subs):
                        r0 = idx * CHUNK + j * vr
                        o = wqs[s * GDN_HEADS + h][vr:] + intra[r0:r0 + vr]
                        o_scr[trows(s), hcols(h)] = o

    phase1(range(nprob))
    for c in range(geom.nch):
        phase2(c)

    for h in range(GDN_HEADS):
        y = _rms(o_scr[:, hcols(h)], gn_ref[...]) * _silu(_TokenView(z_ref)[:, hcols(h)])
        _TokenView(y_ref)[:, hcols(h)] = y.astype(BF16)

    @pl.when(tile == pl.num_programs(1) - 1)
    def _():
        sout_ref[...] = s_scr[...]


def _conv_operands(conv, geom, width, col_blk):
    if conv is None:
        return [], [], []
    state, cw, cb = conv
    specs = [pl.BlockSpec((CONV_W - 1, geom.nseq, width), lambda b, i: (0, b, 0)),
             pl.BlockSpec((CONV_W, width), lambda b, i: (0, col_blk)),
             pl.BlockSpec((1, width), lambda b, i: (0, col_blk))]
    return [state, cw, cb], specs, [pltpu.VMEM((geom.nseq * geom.vrows, width), F32)]


def _token_layout(geom, bsz, seq, nt):
    tls = geom.nch * geom.vrows
    if geom.vrows == CHUNK:
        view = lambda a: a.reshape(bsz, seq, a.shape[-1])
        spec = lambda width, blk: pl.BlockSpec((geom.nseq, tls, width), lambda b, i: (b, i, blk))
        return view, spec, lambda width: (bsz, seq, width)
    spec = lambda width, blk: pl.BlockSpec((geom.nseq * tls, width), lambda b, i: (b * nt + i, blk))
    return (lambda a: a), spec, lambda width: (bsz * seq, width)


def _gdn(mix, rest, conv, s0, par, gn, selp, bsz, seq):
    geom, nt = _geom(bsz, seq)
    tl = geom.nseq * geom.nch * geom.vrows
    nprob = tl // CHUNK
    view, tok, out_shape = _token_layout(geom, bsz, seq, nt)
    st_spec = pl.BlockSpec((geom.nseq, GDN_HEADS, GDN_DK, GDN_DV), lambda b, i: (b, 0, 0, 0))
    conv_ops, conv_specs, conv_scratch = _conv_operands(conv, geom, GDN_CONV_CH, MIX_QKV // GDN_CONV_CH)
    y, s_new = pl.pallas_call(
        functools.partial(_gdn_body, geom=geom),
        grid=(bsz // geom.nseq, nt),
        in_specs=[tok(GDN_CONV_CH, MIX_QKV // GDN_CONV_CH), tok(LANES, REST_SMALL // LANES),
                  tok(GDN_V, REST_ZG // GDN_V)] + conv_specs + [
                  st_spec, _const_spec(par.shape), _const_spec(gn.shape), _const_spec(selp.shape)],
        out_specs=[tok(GDN_V, 0), st_spec],
        out_shape=[jax.ShapeDtypeStruct(out_shape(GDN_V), BF16),
                   jax.ShapeDtypeStruct((bsz, GDN_HEADS, GDN_DK, GDN_DV), F32)],
        scratch_shapes=conv_scratch + [
            pltpu.VMEM((tl, 2 * LANES), F32),
            pltpu.VMEM((geom.nseq, GDN_HEADS, GDN_DK, GDN_DV), F32),
            pltpu.VMEM((tl, GDN_V), F32),
            pltpu.VMEM((tl, GDN_V), F32),
            pltpu.VMEM((tl, GDN_V), F32),
            pltpu.VMEM((tl, GDN_V), F32),
            pltpu.VMEM((nprob * 2 * CHUNK, GDN_PAIRS * LANES), BF16),
            pltpu.VMEM((geom.nseq * geom.nch * SUBLANES, GDN_V), F32),
            pltpu.VMEM((tl, GDN_V), F32)],
        compiler_params=_params(("parallel", "arbitrary")),
        name="gdn",
    )(view(mix), view(rest), view(rest), *conv_ops, s0, par, gn, selp)
    return y.reshape(bsz * seq, GDN_V), s_new


def _ssd_body(*refs, geom):
    if geom.vrows == CHUNK:
        (x_ref, sm_ref, z_ref, h0_ref, par_ref, dch_ref, nw_ref, seld_ref, y_ref, hout_ref,
         gates, h_scr, o_scr) = refs
        act = _TokenView(x_ref)
        tile = pl.program_id(1)

        @pl.when(tile == 0)
        def _():
            for s in range(geom.nseq):
                for p in range(SSM_PAIRS):
                    h_scr[s, p] = h0_ref[s, p].T
    else:
        (x_ref, sm_ref, z_ref, cst_ref, cw_ref, cb_ref, h0_ref, par_ref, dch_ref, nw_ref, seld_ref,
         y_ref, hout_ref, act, gates, o_scr) = refs
        _short_conv_silu(x_ref, cst_ref, cw_ref, cb_ref, act, geom)
    vr = geom.vrows
    nsub = CHUNK // vr
    nprob = geom.nseq * geom.nch // nsub

    sm = _TokenView(sm_ref)[:, :]
    lane_g = lax.broadcasted_iota(jnp.int32, sm.shape, 1)
    is_dt = (lane_g >= SM_DT) & (lane_g < SM_DT + SSM_HEADS)
    dt = jnp.where(is_dt, _softplus(sm + par_ref[1:2, :]), 0.0)
    gates[:, 0:LANES] = dt * jnp.where(is_dt, -jnp.exp(par_ref[0:1, :]), 0.0)
    gates[:, LANES:] = dt

    row, lane, col, same = _pair_masks(vr)
    left = lane < CHUNK
    incl = same & (row >= col)
    pcols = lambda p: slice(p * LANES, (p + 1) * LANES)
    prows = lambda pb, n=CHUNK: slice(pb * n, (pb + 1) * n)
    quarter = lambda e, qi, j: e[qi * CHUNK:(qi + 1) * CHUNK, j * LANES:(j + 1) * LANES]
    srows = lambda a, j: a[j * vr:(j + 1) * vr]
    mxu_rows = lambda a: _pad_rows(a).astype(BF16)
    b_off = SSM_DINNER
    c_off = SSM_DINNER + SSM_GROUPS * SSM_DSTATE

    gcs, gls = _chunk_cumsums(gates, nprob, vr)
    for pb in range(nprob):
        seqs = [(pb * nsub + j) // geom.nch for j in range(nsub)]
        gc = gcs[pb]
        dt_c = gates[prows(pb), LANES:2 * LANES]
        lhs = _split2_lanes(jnp.concatenate(
            [gc, dt_c, jnp.exp(gc), jnp.exp(gls[pb] - gc) * dt_c], axis=0))
        xt = _row_form(gc)
        exp_ = [_mm(lhs, seld_ref[d]) for d in range(SSM_PAIRS // 2)]

        b_f = [act[prows(pb), b_off + g * SSM_DSTATE:b_off + (g + 1) * SSM_DSTATE]
               for g in range(SSM_GROUPS)]
        c_f = [act[prows(pb), c_off + g * SSM_DSTATE:c_off + (g + 1) * SSM_DSTATE]
               for g in range(SSM_GROUPS)]
        b_bf = [b.astype(BF16) for b in b_f]
        c_bf = [c.astype(BF16) for c in c_f]
        cb2 = [_mm_nt(c, jnp.concatenate([b, b], axis=0)) for b, c in zip(b_bf, c_bf)]

        pairs = range(SSM_PAIRS)
        grp = lambda p: p // SSM_GROUP_PAIRS
        x_l = [act[prows(pb), pcols(p)] for p in pairs]
        gc_l = [quarter(exp_[p // 2], 0, p % 2) for p in pairs]
        eg_l = [quarter(exp_[p // 2], 2, p % 2) for p in pairs]
        xdt_l = [x * quarter(exp_[p // 2], 1, p % 2) for p, x in zip(pairs, x_l)]
        xw_l = [x * quarter(exp_[p // 2], 3, p % 2) for p, x in zip(pairs, x_l)]
        m_l = []
        for p in pairs:
            r = SM_DT + 2 * p
            gcj = jnp.broadcast_to(xt[r:r + 1, :], (CHUNK, LANES))
            decay = jnp.where(incl, jnp.exp(jnp.where(incl, gc_l[p] - gcj, 0.0)), 0.0)
            m_l.append((cb2[grp(p)] * decay).astype(BF16))
        intra = [_mm(m, _block_diag(xdt, left).astype(BF16)) for m, xdt in zip(m_l, xdt_l)]
        last_row = lambda a, j: a[(j + 1) * vr - 1:(j + 1) * vr, :]
        if geom.vrows == CHUNK:
            dstate = [_mm_tn(b_bf[grp(p)], xw_l[p].astype(BF16)) for p in pairs]
            inter = [_mm(c_bf[grp(p)], h_scr[seqs[0], p].astype(BF16)) for p in pairs]
            for p in pairs:
                h_scr[seqs[0], p] = h_scr[seqs[0], p] * last_row(eg_l[p], 0) + dstate[p]
        else:
            gp = SSM_GROUP_PAIRS
            jg = [(j, g) for j in range(nsub) for g in range(SSM_GROUPS)]
            gslice = lambda g: slice(g * gp, (g + 1) * gp)
            lanes = lambda a_l, g, f: jnp.concatenate([f(a) for a in a_l[gslice(g)]], axis=1)
            inter_g = []
            for j, g in jg:
                ds = _mm_tn(mxu_rows(lanes(xw_l, g, lambda a: srows(a, j))), mxu_rows(srows(b_f[g], j)))
                sc = _rows_to_columns(lanes(eg_l, g, lambda a: last_row(a, j)))
                h = h0_ref[seqs[j], gslice(g)].reshape(gp * LANES, SSM_DSTATE)
                inter_g.append(_mm_nt(mxu_rows(srows(c_f[g], j)), h.astype(BF16))[0:vr])
                hout_ref[seqs[j], gslice(g)] = (h * sc + ds).reshape(gp, LANES, SSM_DSTATE)
            inter = [jnp.concatenate(
                [inter_g[j * SSM_GROUPS + grp(p)][:, (p % gp) * LANES:(p % gp + 1) * LANES]
                 for j in range(nsub)], axis=0) for p in pairs]

        for g in range(SSM_GROUPS):
            ys = []
            ssq = None
            for p in range(g * SSM_GROUP_PAIRS, (g + 1) * SSM_GROUP_PAIRS):
                y = intra[p] + inter[p] * eg_l[p] + dch_ref[:, pcols(p)] * x_l[p]
                y = y * _silu(_TokenView(z_ref)[prows(pb), pcols(p)])
                ys.append(y)
                sq = jnp.sum(y * y, axis=-1, keepdims=True)
                ssq = sq if ssq is None else ssq + sq
            inv = lax.rsqrt(ssq * (1.0 / SSM_GROUP_CH) + EPS)
            for pp, y in enumerate(ys):
                o_scr[prows(pb), pcols(g * SSM_GROUP_PAIRS + pp)] = y * inv

    _TokenView(y_ref)[:, slice(None)] = (o_scr[...] * nw_ref[...]).astype(BF16)

    if geom.vrows == CHUNK:
        @pl.when(tile == pl.num_programs(1) - 1)
        def _():
            for s in range(geom.nseq):
                for p in range(SSM_PAIRS):
                    hout_ref[s, p] = h_scr[s, p].T


def _ssd(mix, rest, conv, h0, par, dch, nw, seld, bsz, seq):
    geom, nt = _geom(bsz, seq)
    tl = geom.nseq * geom.nch * geom.vrows
    view, tok, out_shape = _token_layout(geom, bsz, seq, nt)
    st_shape = (geom.nseq, SSM_PAIRS, 2 * SSM_HEADDIM, SSM_DSTATE)
    st_spec = pl.BlockSpec(st_shape, lambda b, i: (b, 0, 0, 0))
    conv_ops, conv_specs, conv_scratch = _conv_operands(conv, geom, SSM_CONV_CH, MIX_XBC // SSM_CONV_CH)
    y, h_new = pl.pallas_call(
        functools.partial(_ssd_body, geom=geom),
        grid=(bsz // geom.nseq, nt),
        in_specs=[tok(SSM_CONV_CH, MIX_XBC // SSM_CONV_CH), tok(LANES, REST_SMALL // LANES),
                  tok(SSM_DINNER, REST_ZS // SSM_DINNER)] + conv_specs + [
                  st_spec, _const_spec(par.shape), _const_spec(dch.shape), _const_spec(nw.shape),
                  _const_spec(seld.shape)],
        out_specs=[tok(SSM_DINNER, 0), st_spec],
        out_shape=[jax.ShapeDtypeStruct(out_shape(SSM_DINNER), BF16),
                   jax.ShapeDtypeStruct((bsz,) + st_shape[1:], F32)],
        scratch_shapes=conv_scratch + [pltpu.VMEM((tl, 2 * LANES), F32)] + (
            [pltpu.VMEM((geom.nseq, SSM_PAIRS, SSM_DSTATE, 2 * SSM_HEADDIM), F32)]
            if conv is None else []) + [
            pltpu.VMEM((tl, SSM_DINNER), F32)],
        compiler_params=_params(("parallel", "arbitrary")),
        name="ssd",
    )(view(mix), view(rest), view(rest), *conv_ops, h0, par, dch, nw, seld)
    return y.reshape(bsz * seq, SSM_DINNER), h_new


def _pad_lanes(v, offset):
    out = jnp.zeros((LANES,), F32)
    return out.at[offset:offset + v.shape[0]].set(v.astype(F32))


def _pair_selection(first_lane, lanes_per_head, n_mats):
    heads_per_mat = 2 * LANES // lanes_per_head
    sel = np.zeros((n_mats, 2 * LANES, 2 * LANES), np.float32)
    for m in range(n_mats):
        for j in range(heads_per_mat):
            src = first_lane + m * heads_per_mat + j
            sel[m, src, j * lanes_per_head:(j + 1) * lanes_per_head] = 1.0
            sel[m, LANES + src, j * lanes_per_head:(j + 1) * lanes_per_head] = 1.0
    return jnp.asarray(sel, BF16)


def _pad_conv_state(st):
    bsz, _, ch = st.shape
    return jnp.concatenate([jnp.zeros((bsz, CONV_PAD - (CONV_W - 1), ch), F32), st.astype(F32)], axis=1)


def _layer(x, mem_k, mem_v, s_gdn, c_gdn, s_ssm, c_ssm, w):
    bsz, seq, _ = x.shape
    t = bsz * seq
    xf = x.reshape(t, D_MODEL)
    h, u = _ffn1(xf, w["norm_ff1"], w["w_ff1_in"], w["w_ff1_out"], w["norm_mix"])
    if seq >= CHUNK:
        cst8 = jnp.concatenate([_pad_conv_state(c_gdn), _pad_conv_state(c_ssm)], axis=2)
        mix, rest, tail = _in_proj_conv(u, w["w_mix"], w["w_rest"], w["conv_w"], w["conv_b"], cst8, seq)
        conv_gdn = conv_ssm = None
    else:
        rest = _in_proj(u, w["w_rest"], REST_TN, "in_proj_rest")
        mix = _in_proj(u, w["w_mix"], MIX_TN, "in_proj_mix")
        tail = mix.reshape(bsz, seq, MIX_COLS)[:, seq - CONV_PAD:, :]
        taps_first = lambda c: jnp.swapaxes(c.astype(F32), 0, 1)
        conv_gdn = (taps_first(c_gdn), w["conv_w"], w["conv_b"])
        conv_ssm = (taps_first(c_ssm), w["conv_w"], w["conv_b"])

    yg, s_gdn_new = _gdn(mix, rest, conv_gdn, s_gdn.astype(F32), w["gdn_par"], w["gdn_norm"],
                         w["sel_gdn"], bsz, seq)
    h0 = s_ssm.astype(F32).reshape(bsz, SSM_PAIRS, 2 * SSM_HEADDIM, SSM_DSTATE)
    ys, s_ssm_new = _ssd(mix, rest, conv_ssm, h0, w["ssm_par"], w["ssm_d_ch"], w["ssm_norm"],
                         w["sel_ssm"], bsz, seq)
    transposed = lambda m: jnp.swapaxes(m.reshape(bsz, N_MEM, MEM_WIDTH), 1, 2)
    ym = _memattn(rest, transposed(mem_k), transposed(mem_v), bsz, seq)
    y = _merge_ffn2(h, yg, ys, ym, rest, w["w_branch"], w["w_out"], w["norm_ff2"],
                    w["w_ff2_in"], w["w_ff2_out"], w["norm_final"])

    c_gdn_new = tail[:, CONV_PAD - (CONV_W - 1):, MIX_QKV:MIX_QKV + GDN_CONV_CH]
    c_ssm_new = tail[:, CONV_PAD - (CONV_W - 1):, MIX_XBC:MIX_XBC + SSM_CONV_CH]
    return (y.reshape(bsz, seq, D_MODEL), s_gdn_new, c_gdn_new,
            s_ssm_new.reshape(bsz, SSM_HEADS, SSM_HEADDIM, SSM_DSTATE), c_ssm_new)


def _row(v):
    return v.astype(F32).reshape(1, -1)


def _prep_weights(l, norm_ff1, w_ff1_in, w_ff1_out, norm_mix, w_in, gdn_conv_w, gdn_a_log,
                  gdn_dt_bias, gdn_norm, ssm_conv_w, ssm_conv_b, ssm_a_log, ssm_dt_bias, ssm_d,
                  ssm_norm, w_branch, w_out, norm_ff2, w_ff2_in, w_ff2_out, norm_final):
    row = _row

    w_mix, w_rest = _pack_w_in(jnp.swapaxes(w_in[l], 0, 1))

    def two_rows(a, b, offset):
        par = jnp.zeros((SUBLANES, LANES), F32)
        return par.at[0].set(_pad_lanes(a, offset)).at[1].set(_pad_lanes(b, offset))

    return {
        "norm_ff1": row(norm_ff1[l]), "w_ff1_in": w_ff1_in[l].astype(BF16),
        "w_ff1_out": w_ff1_out[l].astype(BF16), "norm_mix": row(norm_mix[l]),
        "w_mix": w_mix, "w_rest": w_rest,
        "conv_w": jnp.concatenate([gdn_conv_w[l], ssm_conv_w[l]], axis=1).astype(F32),
        "conv_b": jnp.concatenate([jnp.zeros((1, GDN_CONV_CH), F32), row(ssm_conv_b[l])], axis=1),
        "gdn_par": two_rows(gdn_a_log[l], gdn_dt_bias[l], SM_A),
        "gdn_norm": row(gdn_norm[l]),
        "sel_gdn": _pair_selection(0, LANES, GDN_PAIRS),
        "ssm_par": two_rows(ssm_a_log[l], ssm_dt_bias[l], SM_DT),
        "ssm_d_ch": jnp.repeat(ssm_d[l].astype(F32), SSM_HEADDIM).reshape(1, -1),
        "ssm_norm": row(ssm_norm[l]),
        "sel_ssm": _pair_selection(SM_DT, SSM_HEADDIM, SSM_PAIRS // 2),
        "w_branch": w_branch[l].astype(BF16), "w_out": w_out[l].astype(BF16),
        "norm_ff2": row(norm_ff2[l]), "w_ff2_in": w_ff2_in[l].astype(BF16),
        "w_ff2_out": w_ff2_out[l].astype(BF16), "norm_final": row(norm_final),
    }


def kernel(x_prompt, x_sample, mem_prompt, state_gdn, state_gdn_conv, state_ssm, state_ssm_conv,
           cache_mem_k, cache_mem_v, norm_ff1, w_ff1_in, w_ff1_out, norm_mix, w_in,
           gdn_conv_w, gdn_a_log, gdn_dt_bias, gdn_norm, ssm_conv_w, ssm_conv_b, ssm_a_log,
           ssm_dt_bias, ssm_d, ssm_norm, norm_mem, w_mem_kv, w_branch, w_out,
           norm_ff2, w_ff2_in, w_ff2_out, norm_final):
    assert w_in.shape[0] == 1, "the kernels implement the single-layer configuration"
    l = 0
    bp = x_prompt.shape[0]
    w = _prep_weights(l, norm_ff1, w_ff1_in, w_ff1_out, norm_mix, w_in, gdn_conv_w, gdn_a_log,
                      gdn_dt_bias, gdn_norm, ssm_conv_w, ssm_conv_b, ssm_a_log, ssm_dt_bias, ssm_d,
                      ssm_norm, w_branch, w_out, norm_ff2, w_ff2_in, w_ff2_out, norm_final)

    n_mem_tok = mem_prompt.shape[0] * mem_prompt.shape[1]
    kv = _memkv(mem_prompt.reshape(n_mem_tok, D_MODEL), _row(norm_mem[l]), w_mem_kv[l].astype(BF16))
    mk = kv[:, :MEM_WIDTH].reshape(bp, N_MEM, MEM_HEADS, MEM_HEAD_DIM)
    mv = kv[:, MEM_WIDTH:].reshape(bp, N_MEM, MEM_HEADS, MEM_HEAD_DIM)
    dtp = x_prompt.dtype
    yp, sgp, cgp, ssp, csp = _layer(
        x_prompt, mk, mv,
        jnp.zeros((bp, GDN_HEADS, GDN_DK, GDN_DV), dtp), jnp.zeros((bp, CONV_W - 1, GDN_CONV_CH), dtp),
        jnp.zeros((bp, SSM_HEADS, SSM_HEADDIM, SSM_DSTATE), dtp),
        jnp.zeros((bp, CONV_W - 1, SSM_CONV_CH), dtp), w)
    ys_, sgs, cgs, sss, css = _layer(
        x_sample, cache_mem_k[l], cache_mem_v[l], state_gdn[l], state_gdn_conv[l],
        state_ssm[l], state_ssm_conv[l], w)

    lead = lambda a: a[None]
    return (yp, ys_, lead(sgp), lead(cgp), lead(ssp), lead(csp), lead(mk), lead(mv),
            lead(sgs), lead(cgs), lead(sss), lead(css))
```

```python
import collections
import functools

import numpy as np
import jax
import jax.numpy as jnp
from jax import lax
from jax.experimental import pallas as pl
from jax.experimental.pallas import tpu as pltpu

F32 = jnp.float32
BF16 = jnp.bfloat16
EPS = 1e-6

D_MODEL = 1024
FFN_DIM = 2816
CONV_W = 4
CHUNK = 64
LANES = 128
SUBLANES = 8

GDN_HEADS = 8
GDN_PAIRS = GDN_HEADS // 2
GDN_DK = 128
GDN_DV = 128
GDN_QK = GDN_HEADS * GDN_DK
GDN_V = GDN_HEADS * GDN_DV
GDN_CONV_CH = 2 * GDN_QK + GDN_V

SSM_DINNER = 2048
SSM_HEADDIM = 64
SSM_HEADS = 32
SSM_GROUPS = 4
SSM_DSTATE = 128
SSM_CONV_CH = SSM_DINNER + 2 * SSM_GROUPS * SSM_DSTATE
SSM_PAIRS = SSM_HEADS // 2
SSM_GROUP_PAIRS = SSM_PAIRS // SSM_GROUPS
SSM_GROUP_CH = SSM_DINNER // SSM_GROUPS

N_MEM = 256
MEM_HEADS = 4
MEM_HEAD_DIM = 64
MEM_WIDTH = MEM_HEADS * MEM_HEAD_DIM

MIX_QKV = 0
MIX_XBC = MIX_QKV + GDN_CONV_CH
MIX_COLS = MIX_XBC + SSM_CONV_CH
REST_GATE = 0
REST_ZG = REST_GATE + 3 * D_MODEL
REST_ZS = REST_ZG + GDN_V
REST_QM = REST_ZS + SSM_DINNER
REST_SMALL = REST_QM + MEM_WIDTH
REST_COLS = REST_SMALL + LANES
IN_SPLITS = (GDN_CONV_CH, GDN_HEADS, GDN_HEADS, GDN_V, SSM_DINNER, SSM_CONV_CH, SSM_HEADS, MEM_WIDTH,
             3 * D_MODEL)
SM_A = 0
SM_B = GDN_HEADS
SM_DT = 2 * GDN_HEADS

VMEM_LIMIT = 56 * 1024 * 1024


def _params(semantics):
    return pltpu.CompilerParams(dimension_semantics=semantics, vmem_limit_bytes=VMEM_LIMIT)


def _mm(a, b):
    return jnp.dot(a, b, preferred_element_type=F32)


def _mm_nt(a, b):
    return lax.dot_general(a, b, (((1,), (1,)), ((), ())), preferred_element_type=F32)


def _mm_tn(a, b):
    return lax.dot_general(a, b, (((0,), (0,)), ((), ())), preferred_element_type=F32)


def _split3(x):
    hi = x.astype(BF16)
    r1 = x - hi.astype(F32)
    mid = r1.astype(BF16)
    lo = (r1 - mid.astype(F32)).astype(BF16)
    return hi, mid, lo


def _split2_lanes(x):
    hi = x.astype(BF16)
    lo = (x - hi.astype(F32)).astype(BF16)
    return jnp.concatenate([hi, lo], axis=1)


def _rms(x, g):
    return x * lax.rsqrt(jnp.mean(x * x, axis=-1, keepdims=True) + EPS) * g


def _sigmoid(x):
    return 1.0 / (1.0 + jnp.exp2(x * float(-1.0 / np.log(2.0))))


def _silu(x):
    return x * _sigmoid(x)


def _softplus(x):
    return jnp.maximum(x, 0.0) + jnp.log1p(jnp.exp(-jnp.abs(x)))


def _const_spec(shape):
    nd = len(shape)
    return pl.BlockSpec(shape, lambda *_: (0,) * nd, pipeline_mode=pl.Buffered(1))


FFN_TM = 512
MXU_WIDTH = 256
FFN_CHUNKS = ((0, 6 * MXU_WIDTH), (6 * MXU_WIDTH, FFN_DIM))


def _ffn_compute(x, g_ref, wi_ref, wo_ref):
    xn = _rms(x, g_ref[...]).astype(BF16)
    acc = None
    for lo, hi in FFN_CHUNKS:
        gate = _mm(xn, wi_ref[:, lo:hi])
        up = _mm(xn, wi_ref[:, FFN_DIM + lo:FFN_DIM + hi])
        act = (_silu(gate) * up).astype(BF16)
        part = _mm(act, wo_ref[lo:hi, :])
        acc = part if acc is None else acc + part
    return x + 0.5 * acc


def _ffn1_body(x_ref, g_ref, wi_ref, wo_ref, g2_ref, h_ref, u_ref):
    h = _ffn_compute(x_ref[...], g_ref, wi_ref, wo_ref)
    h_ref[...] = h
    u_ref[...] = _rms(h, g2_ref[...]).astype(BF16)


def _ffn1(x, g, wi, wo, g2):
    t = x.shape[0]
    tm = min(FFN_TM, t)
    row = lambda i: (i, 0)
    return pl.pallas_call(
        _ffn1_body,
        grid=(t // tm,),
        in_specs=[pl.BlockSpec((tm, D_MODEL), row), _const_spec(g.shape), _const_spec(wi.shape),
                  _const_spec(wo.shape), _const_spec(g2.shape)],
        out_specs=[pl.BlockSpec((tm, D_MODEL), row), pl.BlockSpec((tm, D_MODEL), row)],
        out_shape=[jax.ShapeDtypeStruct((t, D_MODEL), F32), jax.ShapeDtypeStruct((t, D_MODEL), BF16)],
        compiler_params=_params(("parallel",)),
        name="ffn1",
    )(x, g, wi, wo, g2)


PACK_ROWS = 128


def _pack_body(wt_ref, mix_ref, rest_ref):
    src = np.cumsum((0,) + IN_SPLITS).tolist()
    qkv, ab, _, zg, zs, xbc, dt, qm, gate = [(src[i], IN_SPLITS[i]) for i in range(len(IN_SPLITS))]
    for ref, dst, (lo, n) in ((mix_ref, MIX_QKV, qkv), (mix_ref, MIX_XBC, xbc),
                              (rest_ref, REST_GATE, gate), (rest_ref, REST_ZG, zg),
                              (rest_ref, REST_ZS, zs), (rest_ref, REST_QM, qm)):
        assert lo % SUBLANES == 0 and n % LANES == 0
        for t in range(n // LANES):
            ref[:, dst + t * LANES:dst + (t + 1) * LANES] = (
                wt_ref[lo + t * LANES:lo + (t + 1) * LANES, :].T.astype(BF16))
    assert SM_A == 0 and SM_DT == 2 * GDN_HEADS
    small = jnp.concatenate(
        [wt_ref[ab[0]:ab[0] + 2 * GDN_HEADS, :], wt_ref[dt[0]:dt[0] + SSM_HEADS, :],
         jnp.zeros((LANES - 2 * GDN_HEADS - SSM_HEADS, wt_ref.shape[1]), F32)], axis=0)
    rest_ref[:, REST_SMALL:] = small.T.astype(BF16)


def _pack_w_in(wt):
    rows = lambda i: (i, 0)
    return pl.pallas_call(
        _pack_body,
        grid=(D_MODEL // PACK_ROWS,),
        in_specs=[pl.BlockSpec((wt.shape[0], PACK_ROWS), lambda i: (0, i))],
        out_specs=[pl.BlockSpec((PACK_ROWS, MIX_COLS), rows), pl.BlockSpec((PACK_ROWS, REST_COLS), rows)],
        out_shape=[jax.ShapeDtypeStruct((D_MODEL, MIX_COLS), BF16),
                   jax.ShapeDtypeStruct((D_MODEL, REST_COLS), BF16)],
        compiler_params=_params(("parallel",)),
        name="pack_w_in",
    )(wt)


PROJ_TM = 1024
PROJ_ROWS = 256
MIX_TN = MIX_COLS // 4
REST_TN = REST_COLS // 3


def _proj_body(u_ref, w_ref, o_ref):
    o_ref[...] = _mm(u_ref[...], w_ref[...])


def _in_proj(u, w, tn, name):
    t = u.shape[0]
    tm = min(PROJ_TM, t)
    cols = w.shape[1]
    return pl.pallas_call(
        _proj_body,
        grid=(cols // tn, t // tm),
        in_specs=[pl.BlockSpec((tm, D_MODEL), lambda j, i: (i, 0)),
                  pl.BlockSpec((D_MODEL, tn), lambda j, i: (0, j))],
        out_specs=pl.BlockSpec((tm, tn), lambda j, i: (i, j)),
        out_shape=jax.ShapeDtypeStruct((t, cols), F32),
        compiler_params=_params(("parallel", "parallel")),
        name=name,
    )(u, w)


def _causal_conv_silu(xa, w, bias):
    shape = (SUBLANES, LANES)
    sub = lax.broadcasted_iota(jnp.int32, shape, 0)
    columns = []
    for c in range(xa.shape[1] // LANES):
        cols = slice(c * LANES, (c + 1) * LANES)
        taps = [jnp.broadcast_to(w[j:j + 1, cols], shape) for j in range(CONV_W)]
        b = jnp.broadcast_to(bias[:, cols], shape)
        before = xa[CONV_PAD - SUBLANES:CONV_PAD, cols]
        outs = []
        for g in range(CONV_PAD // SUBLANES, xa.shape[0] // SUBLANES):
            x = xa[g * SUBLANES:(g + 1) * SUBLANES, cols]
            y = x * taps[CONV_W - 1] + b
            for k in range(1, CONV_W):
                shifted = pltpu.roll(jnp.where(sub >= SUBLANES - k, before, x), k, axis=0)
                y = y + shifted * taps[CONV_W - 1 - k]
            outs.append(_silu(y))
            before = x
        columns.append(jnp.concatenate(outs, axis=0))
    return jnp.concatenate(columns, axis=1)


CONVPROJ_TM = 512
CONVPROJ_STEPS = 3
PROJ_STRIPS = 8


def _strips(width, n):
    tiles = width // LANES
    assert width % LANES == 0 and tiles >= n
    bounds = [LANES * ((tiles * k) // n) for k in range(n + 1)]
    return [slice(bounds[k], bounds[k + 1]) for k in range(n)]


def _proj_conv_body(u_ref, wm_ref, wr_ref, cw_ref, cb_ref, cst_ref, mix_ref, rest_ref, tail_ref,
                    carry, *, tiles_per_seq):
    i = pl.program_id(1)

    @pl.when(i % tiles_per_seq == 0)
    def _():
        carry[0:CONV_PAD, :] = cst_ref[0]

    strips_m = _strips(mix_ref.shape[1], PROJ_STRIPS)
    strips_r = _strips(rest_ref.shape[1], PROJ_STRIPS)
    for r in range(u_ref.shape[0] // PROJ_ROWS):
        rows = slice(r * PROJ_ROWS, (r + 1) * PROJ_ROWS)
        u = u_ref[rows, :]
        for cm, cr in zip(strips_m, strips_r):
            carry[CONV_PAD:, cm] = _mm(u, wm_ref[:, cm])
            rest_ref[rows, cr] = _mm(u, wr_ref[:, cr])
            mix_ref[rows, cm] = _causal_conv_silu(carry.at[:, cm], cw_ref[:, cm], cb_ref[:, cm])
            carry[0:CONV_PAD, cm] = carry[PROJ_ROWS:, cm]
    tail_ref[0] = carry[0:CONV_PAD, :]


def _in_proj_conv(u, w_mix, w_rest, cw, cb, cst8, seq):
    t = u.shape[0]
    tm = min(CONVPROJ_TM, seq)
    assert seq % tm == 0 and tm % PROJ_ROWS == 0
    tps = seq // tm
    tn_m = MIX_COLS // CONVPROJ_STEPS
    tn_r = REST_COLS // CONVPROJ_STEPS
    col = lambda j, i: (0, j)
    tile = lambda j, i: (i, j)
    seq_blk = lambda j, i: (i // tps, 0, j)
    return pl.pallas_call(
        functools.partial(_proj_conv_body, tiles_per_seq=tps),
        grid=(CONVPROJ_STEPS, t // tm),
        in_specs=[pl.BlockSpec((tm, D_MODEL), lambda j, i: (i, 0)),
                  pl.BlockSpec((D_MODEL, tn_m), col), pl.BlockSpec((D_MODEL, tn_r), col),
                  pl.BlockSpec((CONV_W, tn_m), col), pl.BlockSpec((1, tn_m), col),
                  pl.BlockSpec((1, CONV_PAD, tn_m), seq_blk)],
        out_specs=[pl.BlockSpec((tm, tn_m), tile), pl.BlockSpec((tm, tn_r), tile),
                   pl.BlockSpec((1, CONV_PAD, tn_m), seq_blk)],
        out_shape=[jax.ShapeDtypeStruct((t, MIX_COLS), F32), jax.ShapeDtypeStruct((t, REST_COLS), F32),
                   jax.ShapeDtypeStruct((t // seq, CONV_PAD, MIX_COLS), F32)],
        scratch_shapes=[pltpu.VMEM((CONV_PAD + PROJ_ROWS, tn_m), F32)],
        compiler_params=_params(("parallel", "arbitrary")),
        name="in_proj_conv",
    )(u, w_mix, w_rest, cw, cb, cst8)


def _memkv_body(x_ref, g_ref, w_ref, o_ref):
    o_ref[...] = _mm(_rms(x_ref[...], g_ref[...]).astype(BF16), w_ref[...])


MEMKV_TM = 512


def _memkv(mem, g, w):
    t = mem.shape[0]
    tm = min(MEMKV_TM, t)
    row = lambda i: (i, 0)
    return pl.pallas_call(
        _memkv_body,
        grid=(t // tm,),
        in_specs=[pl.BlockSpec((tm, D_MODEL), row), _const_spec(g.shape), _const_spec(w.shape)],
        out_specs=pl.BlockSpec((tm, 2 * MEM_WIDTH), row),
        out_shape=jax.ShapeDtypeStruct((t, 2 * MEM_WIDTH), F32),
        compiler_params=_params(("parallel",)),
        name="memkv",
    )(mem, g, w)


MERGE_TM = 256


def _merge_ffn2_body(h_ref, yg_ref, ys_ref, ym_ref, gate_ref, wb_ref, wout_ref,
                     g_ref, wi_ref, wo_ref, gf_ref, y_ref):
    gates = gate_ref[...]
    merged = (_sigmoid(gates[:, 0:D_MODEL]) * _mm(yg_ref[...], wb_ref[0:GDN_V, :])
              + _sigmoid(gates[:, D_MODEL:2 * D_MODEL])
              * _mm(ys_ref[...], wb_ref[GDN_V:GDN_V + SSM_DINNER, :])
              + _sigmoid(gates[:, 2 * D_MODEL:3 * D_MODEL])
              * _mm(ym_ref[...], wb_ref[GDN_V + SSM_DINNER:, :]))
    h = h_ref[...] + _mm(merged.astype(BF16), wout_ref[...])
    h = _ffn_compute(h, g_ref, wi_ref, wo_ref)
    y_ref[...] = _rms(h, gf_ref[...])


def _merge_ffn2(h, yg, ys, ym, proj, wb, wout, g, wi, wo, gf):
    t = h.shape[0]
    tm = min(MERGE_TM, t)
    row = lambda i: (i, 0)
    gate_blk = REST_GATE // (3 * D_MODEL)
    return pl.pallas_call(
        _merge_ffn2_body,
        grid=(t // tm,),
        in_specs=[pl.BlockSpec((tm, D_MODEL), row), pl.BlockSpec((tm, GDN_V), row),
                  pl.BlockSpec((tm, SSM_DINNER), row), pl.BlockSpec((tm, MEM_WIDTH), row),
                  pl.BlockSpec((tm, 3 * D_MODEL), lambda i: (i, gate_blk)),
                  _const_spec(wb.shape), _const_spec(wout.shape), _const_spec(g.shape),
                  _const_spec(wi.shape), _const_spec(wo.shape), _const_spec(gf.shape)],
        out_specs=pl.BlockSpec((tm, D_MODEL), row),
        out_shape=jax.ShapeDtypeStruct((t, D_MODEL), F32),
        compiler_params=_params(("parallel",)),
        name="merge_ffn2",
    )(h, yg, ys, ym, proj, wb, wout, g, wi, wo, gf)


MEMATTN_TL = 512
MEMATTN_NSEQ = 16


def _memattn_body(q_ref, k_ref, v_ref, y_ref, *, nseq, rows):
    lane = lax.broadcasted_iota(jnp.int32, (rows, MEM_WIDTH), 1)
    masks = [(lane >= hh * MEM_HEAD_DIM) & (lane < (hh + 1) * MEM_HEAD_DIM) for hh in range(MEM_HEADS)]
    seqs = range(nseq)
    q = [q_ref[s * rows:(s + 1) * rows, :] for s in seqs]
    q4 = [jnp.concatenate([jnp.where(m, x, 0.0) for m in masks], axis=0).astype(BF16) for x in q]
    sc = [_mm(q4[s], k_ref[s].astype(BF16)) * (MEM_HEAD_DIM ** -0.5) for s in seqs]
    p = [jnp.exp(x - jnp.max(x, axis=-1, keepdims=True)) for x in sc]
    p = [(x / jnp.sum(x, axis=-1, keepdims=True)).astype(BF16) for x in p]
    o4 = [_mm_nt(p[s], v_ref[s].astype(BF16)) for s in seqs]
    ys = []
    for x in o4:
        y = jnp.where(masks[0], x[0:rows], 0.0)
        for hh in range(1, MEM_HEADS):
            y = y + jnp.where(masks[hh], x[hh * rows:(hh + 1) * rows], 0.0)
        ys.append(y)
    y_ref[...] = jnp.concatenate(ys, axis=0).astype(BF16)


def _memattn(proj, k, v, bsz, seq):
    if seq >= MEMATTN_TL:
        nseq, rows, nt = 1, MEMATTN_TL, seq // MEMATTN_TL
    else:
        nseq, rows, nt = MEMATTN_NSEQ, seq, 1
    assert bsz % nseq == 0 and seq % rows == 0
    tl = nseq * rows
    qm_blk = REST_QM // MEM_WIDTH
    kv_spec = pl.BlockSpec((nseq, MEM_WIDTH, N_MEM), lambda b, i: (b, 0, 0))
    return pl.pallas_call(
        functools.partial(_memattn_body, nseq=nseq, rows=rows),
        grid=(bsz // nseq, nt),
        in_specs=[pl.BlockSpec((tl, MEM_WIDTH), lambda b, i: (b * nt + i, qm_blk)), kv_spec, kv_spec],
        out_specs=pl.BlockSpec((tl, MEM_WIDTH), lambda b, i: (b * nt + i, 0)),
        out_shape=jax.ShapeDtypeStruct((bsz * seq, MEM_WIDTH), BF16),
        compiler_params=_params(("parallel", "arbitrary")),
        name="memattn",
    )(proj, k, v)


SEQ_TL = 256
LONG_NSEQ = 2
SHORT_NSEQ = 8
CONV_PAD = SUBLANES
MIN_MXU_ROWS = 16

Geom = collections.namedtuple("Geom", ["nseq", "nch", "vrows"])


def _geom(bsz, seq):
    if seq >= CHUNK:
        assert seq % SEQ_TL == 0
        nseq = LONG_NSEQ if bsz % LONG_NSEQ == 0 else 1
        return Geom(nseq, SEQ_TL // CHUNK, CHUNK), seq // SEQ_TL
    assert bsz % SHORT_NSEQ == 0 and seq % SUBLANES == 0 and (SHORT_NSEQ * seq) % CHUNK == 0
    return Geom(SHORT_NSEQ, 1, seq), 1


class _TokenView:
    def __init__(self, ref):
        self.ref = ref

    def _locate(self, rows):
        per = self.ref.shape[1]
        s = rows.start // per
        assert rows.stop <= (s + 1) * per
        return s, slice(rows.start - s * per, rows.stop - s * per)

    def __getitem__(self, idx):
        rows, cols = idx
        if len(self.ref.shape) == 2:
            return self.ref[rows, cols]
        if rows == slice(None):
            v = self.ref[:, :, cols]
            return v.reshape(v.shape[0] * v.shape[1], v.shape[2])
        s, r = self._locate(rows)
        return self.ref[s, r, cols]

    def __setitem__(self, idx, value):
        rows, cols = idx
        if len(self.ref.shape) == 2:
            self.ref[rows, cols] = value
        elif rows == slice(None):
            nseq, per = self.ref.shape[0], self.ref.shape[1]
            self.ref[:, :, cols] = value.reshape(nseq, per, value.shape[1])
        else:
            s, r = self._locate(rows)
            self.ref[s, r, cols] = value


def _short_conv_silu(x_ref, cst_ref, cw_ref, cb_ref, act, geom):
    assert geom.nch == 1 and geom.vrows < CHUNK and CONV_PAD == SUBLANES
    vr = geom.vrows
    sub = lax.broadcasted_iota(jnp.int32, (SUBLANES, LANES), 0)
    for cb in range(x_ref.shape[1] // LANES):
        cols = slice(cb * LANES, (cb + 1) * LANES)
        planes = [cst_ref[j, :, cols] for j in range(CONV_W - 1)]
        for s in range(geom.nseq):
            before = jnp.zeros((SUBLANES, LANES), F32)
            for j, plane in enumerate(planes):
                before = jnp.where(sub == SUBLANES - (CONV_W - 1) + j,
                                   jnp.broadcast_to(plane[s:s + 1, :], (SUBLANES, LANES)), before)
            xa = jnp.concatenate([before, x_ref[s * vr:(s + 1) * vr, cols]], axis=0)
            act[s * vr:(s + 1) * vr, cols] = _causal_conv_silu(xa, cw_ref[:, cols], cb_ref[:, cols])


def _pair_masks(vr):
    row = lax.broadcasted_iota(jnp.int32, (CHUNK, LANES), 0)
    lane = lax.broadcasted_iota(jnp.int32, (CHUNK, LANES), 1)
    col = jnp.where(lane < CHUNK, lane, lane - CHUNK)
    same = (row // vr) == (col // vr)
    return row, lane, col, same


def _block_diag(x, left):
    zero = jnp.zeros_like(x)
    return jnp.concatenate([jnp.where(left, x, zero), jnp.where(left, zero, x)], axis=0)


def _chunk_cumsums(gates, nprob, vr):
    r = lax.broadcasted_iota(jnp.int32, (CHUNK, CHUNK), 0)
    c = lax.broadcasted_iota(jnp.int32, (CHUNK, CHUNK), 1)
    tril = jnp.where(((r // vr) == (c // vr)) & (r >= c), 1.0, 0.0).astype(BF16)
    tril3 = jnp.concatenate([tril, tril, tril], axis=1)
    pieces = [jnp.concatenate(_split3(gates[pb * CHUNK:(pb + 1) * CHUNK, 0:LANES]), axis=0)
              for pb in range(nprob)]
    out = _mm(tril3, jnp.concatenate(pieces, axis=1))
    return [out[:, pb * LANES:(pb + 1) * LANES] for pb in range(nprob)]


def _sub_totals(gc, vr):
    if vr == CHUNK:
        return jnp.broadcast_to(gc[CHUNK - 1:CHUNK, :], gc.shape)
    g3 = gc.reshape(CHUNK // vr, vr, gc.shape[1])
    return jnp.broadcast_to(g3[:, vr - 1:vr, :], g3.shape).reshape(gc.shape)


def _rows_to_columns(e):
    pieces = jnp.concatenate(_split3(e), axis=0)
    pad = jnp.zeros((MIN_MXU_ROWS - pieces.shape[0], e.shape[1]), BF16)
    row = lax.broadcasted_iota(jnp.int32, (MIN_MXU_ROWS, LANES), 0)
    ones = jnp.where(row < pieces.shape[0], 1.0, 0.0).astype(BF16)
    return _mm_tn(jnp.concatenate([pieces, pad], axis=0), ones)


def _pad_rows(x):
    if x.shape[0] >= MIN_MXU_ROWS:
        return x
    return jnp.concatenate([x, jnp.zeros((MIN_MXU_ROWS - x.shape[0], x.shape[1]), x.dtype)], axis=0)


def _row_form(gc):
    return jnp.concatenate([gc, pltpu.roll(gc, LANES - 1, axis=1)], axis=0).T


def _gdn_body(*refs, geom):
    if geom.vrows == CHUNK:
        (x_ref, sm_ref, z_ref, s0_ref, par_ref, gn_ref, selp_ref, y_ref, sout_ref,
         gates, s_scr, u_scr, w_scr, qd_scr, kd_scr, qk_scr, egl_scr, o_scr) = refs
        act = _TokenView(x_ref)
    else:
        (x_ref, sm_ref, z_ref, cst_ref, cw_ref, cb_ref, s0_ref, par_ref, gn_ref, selp_ref, y_ref,
         sout_ref, act, gates, s_scr, u_scr, w_scr, qd_scr, kd_scr, qk_scr, egl_scr, o_scr) = refs
        _short_conv_silu(x_ref, cst_ref, cw_ref, cb_ref, act, geom)
    tile = pl.program_id(1)
    first = tile == 0
    vr = geom.vrows
    nsub = CHUNK // vr
    nprob = geom.nseq * geom.nch // nsub

    @pl.when(first)
    def _():
        s_scr[...] = s0_ref[...]

    sm = _TokenView(sm_ref)[:, :]
    lane_g = lax.broadcasted_iota(jnp.int32, sm.shape, 1)
    head_lane = lane_g < GDN_HEADS
    gates[:, 0:LANES] = jnp.where(
        head_lane, -jnp.exp(par_ref[0:1, :]) * _softplus(sm + par_ref[1:2, :]), 0.0)
    gates[:, LANES:] = jnp.where(head_lane, pltpu.roll(_sigmoid(sm), LANES - SM_B, axis=1), 0.0)

    row, lane, col, same = _pair_masks(vr)
    left = lane < CHUNK
    incl = same & (row >= col)
    strict = same & (row > col)
    eye = jnp.where(row == col, 1.0, 0.0).astype(F32)
    zeros_h = jnp.zeros((CHUNK, LANES), F32)
    hcols = lambda h: slice(h * LANES, (h + 1) * LANES)
    prows = lambda pb, n=CHUNK: slice(pb * n, (pb + 1) * n)

    gcs = _chunk_cumsums(gates, nprob, vr)
    lhs, xts = [], []
    for pb in range(nprob):
        gc = gcs[pb]
        lhs.append(_split2_lanes(jnp.concatenate([gc, gates[prows(pb), LANES:2 * LANES]], axis=0)))
        xts.append(_row_form(gc))

    def l2n(t):
        return t * lax.rsqrt(jnp.sum(t * t, axis=-1, keepdims=True) + EPS)

    def stacked(e, qi):
        return jnp.concatenate([e[qi * CHUNK:(qi + 1) * CHUNK, 0:LANES],
                                e[qi * CHUNK:(qi + 1) * CHUNK, LANES:]], axis=0)

    def phase1(pbs):
        items = [(pb, p) for pb in pbs for p in range(GDN_PAIRS)]
        exp_ = [_mm(lhs[pb], selp_ref[p]) for pb, p in items]
        decay_l, kq_l, rhs_l, qd_l, kd_l = [], [], [], [], []
        for (pb, p), e in zip(items, exp_):
            ha, hb = 2 * p, 2 * p + 1
            qa, qb = [l2n(act[prows(pb), hcols(h)]) * (GDN_DK ** -0.5) for h in (ha, hb)]
            ka, kb = [l2n(act[prows(pb), hcols(GDN_HEADS + h)]) for h in (ha, hb)]
            va, vb = [act[prows(pb), hcols(2 * GDN_HEADS + h)] for h in (ha, hb)]
            gc2, beta2 = stacked(e, 0), stacked(e, 1)
            gl2 = jnp.concatenate([_sub_totals(gc2[0:CHUNK], vr), _sub_totals(gc2[CHUNK:], vr)], axis=0)
            eg2 = jnp.exp(gc2)
            kdec2 = jnp.exp(gl2 - gc2)
            k2 = jnp.concatenate([ka, kb], axis=0)
            kbeta2 = k2 * beta2
            gci = jnp.where(left, e[0:CHUNK, 0:LANES], e[0:CHUNK, LANES:])
            gcj = jnp.broadcast_to(xts[pb][ha:ha + 1, :], (CHUNK, LANES))
            decay_l.append(jnp.where(incl, jnp.exp(jnp.where(incl, gci - gcj, 0.0)), 0.0))
            k_bd = jnp.concatenate([jnp.concatenate([ka, zeros_h], axis=1),
                                    jnp.concatenate([zeros_h, kb], axis=1)], axis=0).astype(BF16)
            kbq = jnp.concatenate([jnp.concatenate([kbeta2[0:CHUNK], kbeta2[CHUNK:]], axis=1),
                                   jnp.concatenate([qa, qb], axis=1)], axis=0).astype(BF16)
            kq_l.append(_mm_nt(kbq, k_bd))
            rhs_l.append(jnp.concatenate(
                [jnp.concatenate([va, vb], axis=0) * beta2, kbeta2 * eg2], axis=1).astype(BF16))
            qd_l.append(jnp.concatenate([qa, qb], axis=0) * eg2)
            kd_l.append(k2 * kdec2)
            for sub in range(nsub):
                last = (sub + 1) * vr - 1
                slot = prows(pb * nsub + sub, SUBLANES)
                egl_scr[slot, hcols(ha)] = jnp.broadcast_to(eg2[last:last + 1, :], (SUBLANES, LANES))
                egl_scr[slot, hcols(hb)] = jnp.broadcast_to(
                    eg2[CHUNK + last:CHUNK + last + 1, :], (SUBLANES, LANES))

        n_pow = [-jnp.where(strict, kq[0:CHUNK] * d, 0.0) for kq, d in zip(kq_l, decay_l)]
        t_inv = [eye + n for n in n_pow]
        for _ in range(int(np.ceil(np.log2(vr))) - 1):
            n_pow = [_mm(n.astype(BF16), _block_diag(n, left).astype(BF16)) for n in n_pow]
            t_inv = [t + _mm(t.astype(BF16), _block_diag(n, left).astype(BF16))
                     for t, n in zip(t_inv, n_pow)]

        uw_l = [_mm(_block_diag(t, left).astype(BF16), rhs) for t, rhs in zip(t_inv, rhs_l)]
        for (pb, p), uw, kq, d, qd, kd in zip(items, uw_l, kq_l, decay_l, qd_l, kd_l):
            qk_scr[prows(pb, 2 * CHUNK), hcols(p)] = _block_diag(kq[CHUNK:] * d, left).astype(BF16)
            for idx, h in enumerate((2 * p, 2 * p + 1)):
                sl = slice(idx * CHUNK, (idx + 1) * CHUNK)
                u_scr[prows(pb), hcols(h)] = uw[sl, 0:LANES]
                w_scr[prows(pb), hcols(h)] = uw[sl, LANES:]
                qd_scr[prows(pb), hcols(h)] = qd[sl]
                kd_scr[prows(pb), hcols(h)] = kd[sl]

    def phase2(c):
        sh = [(s, h) for s in range(geom.nseq) for h in range(GDN_HEADS)]
        trows = lambda s: prows(s * geom.nch + c, vr)
        wqs = [_mm(jnp.concatenate([w_scr[trows(s), hcols(h)], qd_scr[trows(s), hcols(h)]],
                                   axis=0).astype(BF16), s_scr[s, h].astype(BF16)) for s, h in sh]
        vnew = [u_scr[trows(s), hcols(h)] - w[0:vr] for (s, h), w in zip(sh, wqs)]
        for (s, h), vn in zip(sh, vnew):
            slot = (s * geom.nch + c) * SUBLANES
            s_scr[s, h] = (s_scr[s, h] * egl_scr[slot:slot + 1, hcols(h)]
                           + _mm_tn(_pad_rows(kd_scr[trows(s), hcols(h)]).astype(BF16),
                                    _pad_rows(vn).astype(BF16)))
        for pb in range(c * nsub, geom.nseq * geom.nch, geom.nch * nsub):
            subs = [pb // geom.nch + j for j in range(nsub)]
            for p in range(GDN_PAIRS):
                heads = (2 * p, 2 * p + 1)
                vn2 = jnp.concatenate([vnew[s * GDN_HEADS + h] for h in heads for s in subs],
                                      axis=0).astype(BF16)
                intra = _mm(qk_scr[prows((pb // nsub), 2 * CHUNK), hcols(p)], vn2)
                for idx, h in enumerate(heads):
                    for j, s in enumerate(subs):
                        r0 = idx * CHUNK + j * vr
                        o = wqs[s * GDN_HEADS + h][vr:] + intra[r0:r0 + vr]
                        o_scr[trows(s), hcols(h)] = o

    phase1(range(nprob))
    for c in range(geom.nch):
        phase2(c)

    for h in range(GDN_HEADS):
        y = _rms(o_scr[:, hcols(h)], gn_ref[...]) * _silu(_TokenView(z_ref)[:, hcols(h)])
        _TokenView(y_ref)[:, hcols(h)] = y.astype(BF16)

    @pl.when(tile == pl.num_programs(1) - 1)
    def _():
        sout_ref[...] = s_scr[...]


def _conv_operands(conv, geom, width, col_blk):
    if conv is None:
        return [], [], []
    state, cw, cb = conv
    specs = [pl.BlockSpec((CONV_W - 1, geom.nseq, width), lambda b, i: (0, b, 0)),
             pl.BlockSpec((CONV_W, width), lambda b, i: (0, col_blk)),
             pl.BlockSpec((1, width), lambda b, i: (0, col_blk))]
    return [state, cw, cb], specs, [pltpu.VMEM((geom.nseq * geom.vrows, width), F32)]


def _token_layout(geom, bsz, seq, nt):
    tls = geom.nch * geom.vrows
    if geom.vrows == CHUNK:
        view = lambda a: a.reshape(bsz, seq, a.shape[-1])
        spec = lambda width, blk: pl.BlockSpec((geom.nseq, tls, width), lambda b, i: (b, i, blk))
        return view, spec, lambda width: (bsz, seq, width)
    spec = lambda width, blk: pl.BlockSpec((geom.nseq * tls, width), lambda b, i: (b * nt + i, blk))
    return (lambda a: a), spec, lambda width: (bsz * seq, width)


def _gdn(mix, rest, conv, s0, par, gn, selp, bsz, seq):
    geom, nt = _geom(bsz, seq)
    tl = geom.nseq * geom.nch * geom.vrows
    nprob = tl // CHUNK
    view, tok, out_shape = _token_layout(geom, bsz, seq, nt)
    st_spec = pl.BlockSpec((geom.nseq, GDN_HEADS, GDN_DK, GDN_DV), lambda b, i: (b, 0, 0, 0))
    conv_ops, conv_specs, conv_scratch = _conv_operands(conv, geom, GDN_CONV_CH, MIX_QKV // GDN_CONV_CH)
    y, s_new = pl.pallas_call(
        functools.partial(_gdn_body, geom=geom),
        grid=(bsz // geom.nseq, nt),
        in_specs=[tok(GDN_CONV_CH, MIX_QKV // GDN_CONV_CH), tok(LANES, REST_SMALL // LANES),
                  tok(GDN_V, REST_ZG // GDN_V)] + conv_specs + [
                  st_spec, _const_spec(par.shape), _const_spec(gn.shape), _const_spec(selp.shape)],
        out_specs=[tok(GDN_V, 0), st_spec],
        out_shape=[jax.ShapeDtypeStruct(out_shape(GDN_V), BF16),
                   jax.ShapeDtypeStruct((bsz, GDN_HEADS, GDN_DK, GDN_DV), F32)],
        scratch_shapes=conv_scratch + [
            pltpu.VMEM((tl, 2 * LANES), F32),
            pltpu.VMEM((geom.nseq, GDN_HEADS, GDN_DK, GDN_DV), F32),
            pltpu.VMEM((tl, GDN_V), F32),
            pltpu.VMEM((tl, GDN_V), F32),
            pltpu.VMEM((tl, GDN_V), F32),
            pltpu.VMEM((tl, GDN_V), F32),
            pltpu.VMEM((nprob * 2 * CHUNK, GDN_PAIRS * LANES), BF16),
            pltpu.VMEM((geom.nseq * geom.nch * SUBLANES, GDN_V), F32),
            pltpu.VMEM((tl, GDN_V), F32)],
        compiler_params=_params(("parallel", "arbitrary")),
        name="gdn",
    )(view(mix), view(rest), view(rest), *conv_ops, s0, par, gn, selp)
    return y.reshape(bsz * seq, GDN_V), s_new


def _ssd_body(*refs, geom):
    if geom.vrows == CHUNK:
        (x_ref, sm_ref, z_ref, h0_ref, par_ref, dch_ref, nw_ref, seld_ref, y_ref, hout_ref,
         gates, h_scr, o_scr) = refs
        act = _TokenView(x_ref)
        tile = pl.program_id(1)

        @pl.when(tile == 0)
        def _():
            for s in range(geom.nseq):
                for p in range(SSM_PAIRS):
                    h_scr[s, p] = h0_ref[s, p].T
    else:
        (x_ref, sm_ref, z_ref, cst_ref, cw_ref, cb_ref, h0_ref, par_ref, dch_ref, nw_ref, seld_ref,
         y_ref, hout_ref, act, gates, o_scr) = refs
        _short_conv_silu(x_ref, cst_ref, cw_ref, cb_ref, act, geom)
    vr = geom.vrows
    nsub = CHUNK // vr
    nprob = geom.nseq * geom.nch // nsub

    sm = _TokenView(sm_ref)[:, :]
    lane_g = lax.broadcasted_iota(jnp.int32, sm.shape, 1)
    is_dt = (lane_g >= SM_DT) & (lane_g < SM_DT + SSM_HEADS)
    dt = jnp.where(is_dt, _softplus(sm + par_ref[1:2, :]), 0.0)
    gates[:, 0:LANES] = dt * jnp.where(is_dt, -jnp.exp(par_ref[0:1, :]), 0.0)
    gates[:, LANES:] = dt

    row, lane, col, same = _pair_masks(vr)
    left = lane < CHUNK
    incl = same & (row >= col)
    pcols = lambda p: slice(p * LANES, (p + 1) * LANES)
    prows = lambda pb, n=CHUNK: slice(pb * n, (pb + 1) * n)
    quarter = lambda e, qi, j: e[qi * CHUNK:(qi + 1) * CHUNK, j * LANES:(j + 1) * LANES]
    srows = lambda a, j: a[j * vr:(j + 1) * vr]
    mxu_rows = lambda a: _pad_rows(a).astype(BF16)
    b_off = SSM_DINNER
    c_off = SSM_DINNER + SSM_GROUPS * SSM_DSTATE

    gcs = _chunk_cumsums(gates, nprob, vr)
    for pb in range(nprob):
        seqs = [(pb * nsub + j) // geom.nch for j in range(nsub)]
        gc = gcs[pb]
        dt_c = gates[prows(pb), LANES:2 * LANES]
        lhs = _split2_lanes(jnp.concatenate([gc, dt_c], axis=0))
        xt = _row_form(gc)
        exp_ = [_mm(lhs, seld_ref[d]) for d in range(SSM_PAIRS // 2)]

        b_f = [act[prows(pb), b_off + g * SSM_DSTATE:b_off + (g + 1) * SSM_DSTATE]
               for g in range(SSM_GROUPS)]
        c_f = [act[prows(pb), c_off + g * SSM_DSTATE:c_off + (g + 1) * SSM_DSTATE]
               for g in range(SSM_GROUPS)]
        b_bf = [b.astype(BF16) for b in b_f]
        c_bf = [c.astype(BF16) for c in c_f]
        cb2 = [_mm_nt(c, jnp.concatenate([b, b], axis=0)) for b, c in zip(b_bf, c_bf)]

        pairs = range(SSM_PAIRS)
        grp = lambda p: p // SSM_GROUP_PAIRS
        x_l = [act[prows(pb), pcols(p)] for p in pairs]
        gc_l = [quarter(exp_[p // 2], 0, p % 2) for p in pairs]
        eg_l = [jnp.exp(g) for g in gc_l]
        xdt_l = [x * quarter(exp_[p // 2], 1, p % 2) for p, x in zip(pairs, x_l)]
        xw_l = [xdt * jnp.exp(_sub_totals(g, vr) - g) for xdt, g in zip(xdt_l, gc_l)]
        m_l = []
        for p in pairs:
            r = SM_DT + 2 * p
            gcj = jnp.broadcast_to(xt[r:r + 1, :], (CHUNK, LANES))
            decay = jnp.where(incl, jnp.exp(jnp.where(incl, gc_l[p] - gcj, 0.0)), 0.0)
            m_l.append((cb2[grp(p)] * decay).astype(BF16))
        intra = [_mm(m, _block_diag(xdt, left).astype(BF16)) for m, xdt in zip(m_l, xdt_l)]
        last_row = lambda a, j: a[(j + 1) * vr - 1:(j + 1) * vr, :]
        if geom.vrows == CHUNK:
            dstate = [_mm_tn(b_bf[grp(p)], xw_l[p].astype(BF16)) for p in pairs]
            inter = [_mm(c_bf[grp(p)], h_scr[seqs[0], p].astype(BF16)) for p in pairs]
            for p in pairs:
                h_scr[seqs[0], p] = h_scr[seqs[0], p] * last_row(eg_l[p], 0) + dstate[p]
        else:
            gp = SSM_GROUP_PAIRS
            jg = [(j, g) for j in range(nsub) for g in range(SSM_GROUPS)]
            gslice = lambda g: slice(g * gp, (g + 1) * gp)
            lanes = lambda a_l, g, f: jnp.concatenate([f(a) for a in a_l[gslice(g)]], axis=1)
            inter_g = []
            for j, g in jg:
                ds = _mm_tn(mxu_rows(lanes(xw_l, g, lambda a: srows(a, j))), mxu_rows(srows(b_f[g], j)))
                sc = _rows_to_columns(lanes(eg_l, g, lambda a: last_row(a, j)))
                h = h0_ref[seqs[j], gslice(g)].reshape(gp * LANES, SSM_DSTATE)
                inter_g.append(_mm_nt(mxu_rows(srows(c_f[g], j)), h.astype(BF16))[0:vr])
                hout_ref[seqs[j], gslice(g)] = (h * sc + ds).reshape(gp, LANES, SSM_DSTATE)
            inter = [jnp.concatenate(
                [inter_g[j * SSM_GROUPS + grp(p)][:, (p % gp) * LANES:(p % gp + 1) * LANES]
                 for j in range(nsub)], axis=0) for p in pairs]

        for g in range(SSM_GROUPS):
            ys = []
            ssq = None
            for p in range(g * SSM_GROUP_PAIRS, (g + 1) * SSM_GROUP_PAIRS):
                y = intra[p] + inter[p] * eg_l[p] + dch_ref[:, pcols(p)] * x_l[p]
                y = y * _silu(_TokenView(z_ref)[prows(pb), pcols(p)])
                ys.append(y)
                sq = jnp.sum(y * y, axis=-1, keepdims=True)
                ssq = sq if ssq is None else ssq + sq
            inv = lax.rsqrt(ssq * (1.0 / SSM_GROUP_CH) + EPS)
            for pp, y in enumerate(ys):
                o_scr[prows(pb), pcols(g * SSM_GROUP_PAIRS + pp)] = y * inv

    _TokenView(y_ref)[:, slice(None)] = (o_scr[...] * nw_ref[...]).astype(BF16)

    if geom.vrows == CHUNK:
        @pl.when(tile == pl.num_programs(1) - 1)
        def _():
            for s in range(geom.nseq):
                for p in range(SSM_PAIRS):
                    hout_ref[s, p] = h_scr[s, p].T


def _ssd(mix, rest, conv, h0, par, dch, nw, seld, bsz, seq):
    geom, nt = _geom(bsz, seq)
    tl = geom.nseq * geom.nch * geom.vrows
    view, tok, out_shape = _token_layout(geom, bsz, seq, nt)
    st_shape = (geom.nseq, SSM_PAIRS, 2 * SSM_HEADDIM, SSM_DSTATE)
    st_spec = pl.BlockSpec(st_shape, lambda b, i: (b, 0, 0, 0))
    conv_ops, conv_specs, conv_scratch = _conv_operands(conv, geom, SSM_CONV_CH, MIX_XBC // SSM_CONV_CH)
    y, h_new = pl.pallas_call(
        functools.partial(_ssd_body, geom=geom),
        grid=(bsz // geom.nseq, nt),
        in_specs=[tok(SSM_CONV_CH, MIX_XBC // SSM_CONV_CH), tok(LANES, REST_SMALL // LANES),
                  tok(SSM_DINNER, REST_ZS // SSM_DINNER)] + conv_specs + [
                  st_spec, _const_spec(par.shape), _const_spec(dch.shape), _const_spec(nw.shape),
                  _const_spec(seld.shape)],
        out_specs=[tok(SSM_DINNER, 0), st_spec],
        out_shape=[jax.ShapeDtypeStruct(out_shape(SSM_DINNER), BF16),
                   jax.ShapeDtypeStruct((bsz,) + st_shape[1:], F32)],
        scratch_shapes=conv_scratch + [pltpu.VMEM((tl, 2 * LANES), F32)] + (
            [pltpu.VMEM((geom.nseq, SSM_PAIRS, SSM_DSTATE, 2 * SSM_HEADDIM), F32)]
            if conv is None else []) + [
            pltpu.VMEM((tl, SSM_DINNER), F32)],
        compiler_params=_params(("parallel", "arbitrary")),
        name="ssd",
    )(view(mix), view(rest), view(rest), *conv_ops, h0, par, dch, nw, seld)
    return y.reshape(bsz * seq, SSM_DINNER), h_new


def _pad_lanes(v, offset):
    out = jnp.zeros((LANES,), F32)
    return out.at[offset:offset + v.shape[0]].set(v.astype(F32))


def _pair_selection(first_lane, lanes_per_head, n_mats):
    heads_per_mat = 2 * LANES // lanes_per_head
    sel = np.zeros((n_mats, 2 * LANES, 2 * LANES), np.float32)
    for m in range(n_mats):
        for j in range(heads_per_mat):
            src = first_lane + m * heads_per_mat + j
            sel[m, src, j * lanes_per_head:(j + 1) * lanes_per_head] = 1.0
            sel[m, LANES + src, j * lanes_per_head:(j + 1) * lanes_per_head] = 1.0
    return jnp.asarray(sel, BF16)


def _pad_conv_state(st):
    bsz, _, ch = st.shape
    return jnp.concatenate([jnp.zeros((bsz, CONV_PAD - (CONV_W - 1), ch), F32), st.astype(F32)], axis=1)


def _layer(x, mem_k, mem_v, s_gdn, c_gdn, s_ssm, c_ssm, w):
    bsz, seq, _ = x.shape
    t = bsz * seq
    xf = x.reshape(t, D_MODEL)
    h, u = _ffn1(xf, w["norm_ff1"], w["w_ff1_in"], w["w_ff1_out"], w["norm_mix"])
    if seq >= CHUNK:
        cst8 = jnp.concatenate([_pad_conv_state(c_gdn), _pad_conv_state(c_ssm)], axis=2)
        mix, rest, tail = _in_proj_conv(u, w["w_mix"], w["w_rest"], w["conv_w"], w["conv_b"], cst8, seq)
        conv_gdn = conv_ssm = None
    else:
        rest = _in_proj(u, w["w_rest"], REST_TN, "in_proj_rest")
        mix = _in_proj(u, w["w_mix"], MIX_TN, "in_proj_mix")
        tail = mix.reshape(bsz, seq, MIX_COLS)[:, seq - CONV_PAD:, :]
        taps_first = lambda c: jnp.swapaxes(c.astype(F32), 0, 1)
        conv_gdn = (taps_first(c_gdn), w["conv_w"], w["conv_b"])
        conv_ssm = (taps_first(c_ssm), w["conv_w"], w["conv_b"])

    yg, s_gdn_new = _gdn(mix, rest, conv_gdn, s_gdn.astype(F32), w["gdn_par"], w["gdn_norm"],
                         w["sel_gdn"], bsz, seq)
    h0 = s_ssm.astype(F32).reshape(bsz, SSM_PAIRS, 2 * SSM_HEADDIM, SSM_DSTATE)
    ys, s_ssm_new = _ssd(mix, rest, conv_ssm, h0, w["ssm_par"], w["ssm_d_ch"], w["ssm_norm"],
                         w["sel_ssm"], bsz, seq)
    transposed = lambda m: jnp.swapaxes(m.reshape(bsz, N_MEM, MEM_WIDTH), 1, 2)
    ym = _memattn(rest, transposed(mem_k), transposed(mem_v), bsz, seq)
    y = _merge_ffn2(h, yg, ys, ym, rest, w["w_branch"], w["w_out"], w["norm_ff2"],
                    w["w_ff2_in"], w["w_ff2_out"], w["norm_final"])

    c_gdn_new = tail[:, CONV_PAD - (CONV_W - 1):, MIX_QKV:MIX_QKV + GDN_CONV_CH]
    c_ssm_new = tail[:, CONV_PAD - (CONV_W - 1):, MIX_XBC:MIX_XBC + SSM_CONV_CH]
    return (y.reshape(bsz, seq, D_MODEL), s_gdn_new, c_gdn_new,
            s_ssm_new.reshape(bsz, SSM_HEADS, SSM_HEADDIM, SSM_DSTATE), c_ssm_new)


def _row(v):
    return v.astype(F32).reshape(1, -1)


def _prep_weights(l, norm_ff1, w_ff1_in, w_ff1_out, norm_mix, w_in, gdn_conv_w, gdn_a_log,
                  gdn_dt_bias, gdn_norm, ssm_conv_w, ssm_conv_b, ssm_a_log, ssm_dt_bias, ssm_d,
                  ssm_norm, w_branch, w_out, norm_ff2, w_ff2_in, w_ff2_out, norm_final):
    row = _row

    w_mix, w_rest = _pack_w_in(jnp.swapaxes(w_in[l], 0, 1))

    def two_rows(a, b, offset):
        par = jnp.zeros((SUBLANES, LANES), F32)
        return par.at[0].set(_pad_lanes(a, offset)).at[1].set(_pad_lanes(b, offset))

    return {
        "norm_ff1": row(norm_ff1[l]), "w_ff1_in": w_ff1_in[l].astype(BF16),
        "w_ff1_out": w_ff1_out[l].astype(BF16), "norm_mix": row(norm_mix[l]),
        "w_mix": w_mix, "w_rest": w_rest,
        "conv_w": jnp.concatenate([gdn_conv_w[l], ssm_conv_w[l]], axis=1).astype(F32),
        "conv_b": jnp.concatenate([jnp.zeros((1, GDN_CONV_CH), F32), row(ssm_conv_b[l])], axis=1),
        "gdn_par": two_rows(gdn_a_log[l], gdn_dt_bias[l], SM_A),
        "gdn_norm": row(gdn_norm[l]),
        "sel_gdn": _pair_selection(0, LANES, GDN_PAIRS),
        "ssm_par": two_rows(ssm_a_log[l], ssm_dt_bias[l], SM_DT),
        "ssm_d_ch": jnp.repeat(ssm_d[l].astype(F32), SSM_HEADDIM).reshape(1, -1),
        "ssm_norm": row(ssm_norm[l]),
        "sel_ssm": _pair_selection(SM_DT, SSM_HEADDIM, SSM_PAIRS // 2),
        "w_branch": w_branch[l].astype(BF16), "w_out": w_out[l].astype(BF16),
        "norm_ff2": row(norm_ff2[l]), "w_ff2_in": w_ff2_in[l].astype(BF16),
        "w_ff2_out": w_ff2_out[l].astype(BF16), "norm_final": row(norm_final),
    }


def kernel(x_prompt, x_sample, mem_prompt, state_gdn, state_gdn_conv, state_ssm, state_ssm_conv,
           cache_mem_k, cache_mem_v, norm_ff1, w_ff1_in, w_ff1_out, norm_mix, w_in,
           gdn_conv_w, gdn_a_log, gdn_dt_bias, gdn_norm, ssm_conv_w, ssm_conv_b, ssm_a_log,
           ssm_dt_bias, ssm_d, ssm_norm, norm_mem, w_mem_kv, w_branch, w_out,
           norm_ff2, w_ff2_in, w_ff2_out, norm_final):
    assert w_in.shape[0] == 1, "the kernels implement the single-layer configuration"
    l = 0
    bp = x_prompt.shape[0]
    w = _prep_weights(l, norm_ff1, w_ff1_in, w_ff1_out, norm_mix, w_in, gdn_conv_w, gdn_a_log,
                      gdn_dt_bias, gdn_norm, ssm_conv_w, ssm_conv_b, ssm_a_log, ssm_dt_bias, ssm_d,
                      ssm_norm, w_branch, w_out, norm_ff2, w_ff2_in, w_ff2_out, norm_final)

    n_mem_tok = mem_prompt.shape[0] * mem_prompt.shape[1]
    kv = _memkv(mem_prompt.reshape(n_mem_tok, D_MODEL), _row(norm_mem[l]), w_mem_kv[l].astype(BF16))
    mk = kv[:, :MEM_WIDTH].reshape(bp, N_MEM, MEM_HEADS, MEM_HEAD_DIM)
    mv = kv[:, MEM_WIDTH:].reshape(bp, N_MEM, MEM_HEADS, MEM_HEAD_DIM)
    dtp = x_prompt.dtype
    yp, sgp, cgp, ssp, csp = _layer(
        x_prompt, mk, mv,
        jnp.zeros((bp, GDN_HEADS, GDN_DK, GDN_DV), dtp), jnp.zeros((bp, CONV_W - 1, GDN_CONV_CH), dtp),
        jnp.zeros((bp, SSM_HEADS, SSM_HEADDIM, SSM_DSTATE), dtp),
        jnp.zeros((bp, CONV_W - 1, SSM_CONV_CH), dtp), w)
    ys_, sgs, cgs, sss, css = _layer(
        x_sample, cache_mem_k[l], cache_mem_v[l], state_gdn[l], state_gdn_conv[l],
        state_ssm[l], state_ssm_conv[l], w)

    lead = lambda a: a[None]
    return (yp, ys_, lead(sgp), lead(cgp), lead(ssp), lead(csp), lead(mk), lead(mv),
            lead(sgs), lead(cgs), lead(sss), lead(css))
```

```python
import collections
import functools

import numpy as np
import jax
import jax.numpy as jnp
from jax import lax
from jax.experimental import pallas as pl
from jax.experimental.pallas import tpu as pltpu

F32 = jnp.float32
BF16 = jnp.bfloat16
EPS = 1e-6

D_MODEL = 1024
FFN_DIM = 2816
CONV_W = 4
CHUNK = 64
LANES = 128
SUBLANES = 8

GDN_HEADS = 8
GDN_PAIRS = GDN_HEADS // 2
GDN_DK = 128
GDN_DV = 128
GDN_QK = GDN_HEADS * GDN_DK
GDN_V = GDN_HEADS * GDN_DV
GDN_CONV_CH = 2 * GDN_QK + GDN_V

SSM_DINNER = 2048
SSM_HEADDIM = 64
SSM_HEADS = 32
SSM_GROUPS = 4
SSM_DSTATE = 128
SSM_CONV_CH = SSM_DINNER + 2 * SSM_GROUPS * SSM_DSTATE
SSM_PAIRS = SSM_HEADS // 2
SSM_GROUP_PAIRS = SSM_PAIRS // SSM_GROUPS
SSM_GROUP_CH = SSM_DINNER // SSM_GROUPS

N_MEM = 256
MEM_HEADS = 4
MEM_HEAD_DIM = 64
MEM_WIDTH = MEM_HEADS * MEM_HEAD_DIM

MIX_QKV = 0
MIX_XBC = MIX_QKV + GDN_CONV_CH
MIX_COLS = MIX_XBC + SSM_CONV_CH
REST_GATE = 0
REST_ZG = REST_GATE + 3 * D_MODEL
REST_ZS = REST_ZG + GDN_V
REST_QM = REST_ZS + SSM_DINNER
REST_SMALL = REST_QM + MEM_WIDTH
REST_COLS = REST_SMALL + LANES
IN_SPLITS = (GDN_CONV_CH, GDN_HEADS, GDN_HEADS, GDN_V, SSM_DINNER, SSM_CONV_CH, SSM_HEADS, MEM_WIDTH,
             3 * D_MODEL)
SM_A = 0
SM_B = GDN_HEADS
SM_DT = 2 * GDN_HEADS

VMEM_LIMIT = 56 * 1024 * 1024


def _params(semantics):
    return pltpu.CompilerParams(dimension_semantics=semantics, vmem_limit_bytes=VMEM_LIMIT)


def _mm(a, b):
    return jnp.dot(a, b, preferred_element_type=F32)


def _mm_nt(a, b):
    return lax.dot_general(a, b, (((1,), (1,)), ((), ())), preferred_element_type=F32)


def _mm_tn(a, b):
    return lax.dot_general(a, b, (((0,), (0,)), ((), ())), preferred_element_type=F32)


def _split3(x):
    hi = x.astype(BF16)
    r1 = x - hi.astype(F32)
    mid = r1.astype(BF16)
    lo = (r1 - mid.astype(F32)).astype(BF16)
    return hi, mid, lo


def _split2_lanes(x):
    hi = x.astype(BF16)
    lo = (x - hi.astype(F32)).astype(BF16)
    return jnp.concatenate([hi, lo], axis=1)


def _rms(x, g):
    return x * lax.rsqrt(jnp.mean(x * x, axis=-1, keepdims=True) + EPS) * g


def _sigmoid(x):
    return 1.0 / (1.0 + jnp.exp2(x * float(-1.0 / np.log(2.0))))


def _silu(x):
    return x * _sigmoid(x)


def _softplus(x):
    return jnp.maximum(x, 0.0) + jnp.log1p(jnp.exp(-jnp.abs(x)))


def _const_spec(shape):
    nd = len(shape)
    return pl.BlockSpec(shape, lambda *_: (0,) * nd, pipeline_mode=pl.Buffered(1))


FFN_TM = 512
MXU_WIDTH = 256
FFN_CHUNKS = ((0, 6 * MXU_WIDTH), (6 * MXU_WIDTH, FFN_DIM))


def _ffn_compute(x, g_ref, wi_ref, wo_ref):
    xn = _rms(x, g_ref[...]).astype(BF16)
    acc = None
    for lo, hi in FFN_CHUNKS:
        gate = _mm(xn, wi_ref[:, lo:hi])
        up = _mm(xn, wi_ref[:, FFN_DIM + lo:FFN_DIM + hi])
        act = (_silu(gate) * up).astype(BF16)
        part = _mm(act, wo_ref[lo:hi, :])
        acc = part if acc is None else acc + part
    return x + 0.5 * acc


def _ffn1_body(x_ref, g_ref, wi_ref, wo_ref, g2_ref, h_ref, u_ref):
    h = _ffn_compute(x_ref[...], g_ref, wi_ref, wo_ref)
    h_ref[...] = h
    u_ref[...] = _rms(h, g2_ref[...]).astype(BF16)


def _ffn1(x, g, wi, wo, g2):
    t = x.shape[0]
    tm = min(FFN_TM, t)
    row = lambda i: (i, 0)
    return pl.pallas_call(
        _ffn1_body,
        grid=(t // tm,),
        in_specs=[pl.BlockSpec((tm, D_MODEL), row), _const_spec(g.shape), _const_spec(wi.shape),
                  _const_spec(wo.shape), _const_spec(g2.shape)],
        out_specs=[pl.BlockSpec((tm, D_MODEL), row), pl.BlockSpec((tm, D_MODEL), row)],
        out_shape=[jax.ShapeDtypeStruct((t, D_MODEL), F32), jax.ShapeDtypeStruct((t, D_MODEL), BF16)],
        compiler_params=_params(("parallel",)),
        name="ffn1",
    )(x, g, wi, wo, g2)


PACK_ROWS = 128


def _pack_body(wt_ref, mix_ref, rest_ref):
    src = np.cumsum((0,) + IN_SPLITS).tolist()
    qkv, ab, _, zg, zs, xbc, dt, qm, gate = [(src[i], IN_SPLITS[i]) for i in range(len(IN_SPLITS))]
    for ref, dst, (lo, n) in ((mix_ref, MIX_QKV, qkv), (mix_ref, MIX_XBC, xbc),
                              (rest_ref, REST_GATE, gate), (rest_ref, REST_ZG, zg),
                              (rest_ref, REST_ZS, zs), (rest_ref, REST_QM, qm)):
        assert lo % SUBLANES == 0 and n % LANES == 0
        for t in range(n // LANES):
            ref[:, dst + t * LANES:dst + (t + 1) * LANES] = (
                wt_ref[lo + t * LANES:lo + (t + 1) * LANES, :].T.astype(BF16))
    assert SM_A == 0 and SM_DT == 2 * GDN_HEADS
    small = jnp.concatenate(
        [wt_ref[ab[0]:ab[0] + 2 * GDN_HEADS, :], wt_ref[dt[0]:dt[0] + SSM_HEADS, :],
         jnp.zeros((LANES - 2 * GDN_HEADS - SSM_HEADS, wt_ref.shape[1]), F32)], axis=0)
    rest_ref[:, REST_SMALL:] = small.T.astype(BF16)


def _pack_w_in(wt):
    rows = lambda i: (i, 0)
    return pl.pallas_call(
        _pack_body,
        grid=(D_MODEL // PACK_ROWS,),
        in_specs=[pl.BlockSpec((wt.shape[0], PACK_ROWS), lambda i: (0, i))],
        out_specs=[pl.BlockSpec((PACK_ROWS, MIX_COLS), rows), pl.BlockSpec((PACK_ROWS, REST_COLS), rows)],
        out_shape=[jax.ShapeDtypeStruct((D_MODEL, MIX_COLS), BF16),
                   jax.ShapeDtypeStruct((D_MODEL, REST_COLS), BF16)],
        compiler_params=_params(("parallel",)),
        name="pack_w_in",
    )(wt)


PROJ_TM = 1024
PROJ_ROWS = 256
MIX_TN = MIX_COLS // 4
REST_TN = REST_COLS // 3


def _proj_body(u_ref, w_ref, o_ref):
    o_ref[...] = _mm(u_ref[...], w_ref[...])


def _in_proj(u, w, tn, name):
    t = u.shape[0]
    tm = min(PROJ_TM, t)
    cols = w.shape[1]
    return pl.pallas_call(
        _proj_body,
        grid=(cols // tn, t // tm),
        in_specs=[pl.BlockSpec((tm, D_MODEL), lambda j, i: (i, 0)),
                  pl.BlockSpec((D_MODEL, tn), lambda j, i: (0, j))],
        out_specs=pl.BlockSpec((tm, tn), lambda j, i: (i, j)),
        out_shape=jax.ShapeDtypeStruct((t, cols), F32),
        compiler_params=_params(("parallel", "parallel")),
        name=name,
    )(u, w)


def _causal_conv_silu(xa, w, bias):
    shape = (SUBLANES, LANES)
    sub = lax.broadcasted_iota(jnp.int32, shape, 0)
    columns = []
    for c in range(xa.shape[1] // LANES):
        cols = slice(c * LANES, (c + 1) * LANES)
        taps = [jnp.broadcast_to(w[j:j + 1, cols], shape) for j in range(CONV_W)]
        b = jnp.broadcast_to(bias[:, cols], shape)
        before = xa[CONV_PAD - SUBLANES:CONV_PAD, cols]
        outs = []
        for g in range(CONV_PAD // SUBLANES, xa.shape[0] // SUBLANES):
            x = xa[g * SUBLANES:(g + 1) * SUBLANES, cols]
            y = x * taps[CONV_W - 1] + b
            for k in range(1, CONV_W):
                shifted = pltpu.roll(jnp.where(sub >= SUBLANES - k, before, x), k, axis=0)
                y = y + shifted * taps[CONV_W - 1 - k]
            outs.append(_silu(y))
            before = x
        columns.append(jnp.concatenate(outs, axis=0))
    return jnp.concatenate(columns, axis=1)


CONVPROJ_TM = 512
CONVPROJ_STEPS = 3
PROJ_STRIPS = 8


def _strips(width, n):
    tiles = width // LANES
    assert width % LANES == 0 and tiles >= n
    bounds = [LANES * ((tiles * k) // n) for k in range(n + 1)]
    return [slice(bounds[k], bounds[k + 1]) for k in range(n)]


def _proj_conv_body(u_ref, wm_ref, wr_ref, cw_ref, cb_ref, cst_ref, mix_ref, rest_ref, tail_ref,
                    carry, *, tiles_per_seq):
    i = pl.program_id(1)

    @pl.when(i % tiles_per_seq == 0)
    def _():
        carry[0:CONV_PAD, :] = cst_ref[0]

    strips_m = _strips(mix_ref.shape[1], PROJ_STRIPS)
    strips_r = _strips(rest_ref.shape[1], PROJ_STRIPS)
    for r in range(u_ref.shape[0] // PROJ_ROWS):
        rows = slice(r * PROJ_ROWS, (r + 1) * PROJ_ROWS)
        u = u_ref[rows, :]
        for cm, cr in zip(strips_m, strips_r):
            carry[CONV_PAD:, cm] = _mm(u, wm_ref[:, cm])
            rest_ref[rows, cr] = _mm(u, wr_ref[:, cr])
            mix_ref[rows, cm] = _causal_conv_silu(carry.at[:, cm], cw_ref[:, cm], cb_ref[:, cm])
            carry[0:CONV_PAD, cm] = carry[PROJ_ROWS:, cm]
    tail_ref[0] = carry[0:CONV_PAD, :]


def _in_proj_conv(u, w_mix, w_rest, cw, cb, cst8, seq):
    t = u.shape[0]
    tm = min(CONVPROJ_TM, seq)
    assert seq % tm == 0 and tm % PROJ_ROWS == 0
    tps = seq // tm
    tn_m = MIX_COLS // CONVPROJ_STEPS
    tn_r = REST_COLS // CONVPROJ_STEPS
    col = lambda j, i: (0, j)
    tile = lambda j, i: (i, j)
    seq_blk = lambda j, i: (i // tps, 0, j)
    return pl.pallas_call(
        functools.partial(_proj_conv_body, tiles_per_seq=tps),
        grid=(CONVPROJ_STEPS, t // tm),
        in_specs=[pl.BlockSpec((tm, D_MODEL), lambda j, i: (i, 0)),
                  pl.BlockSpec((D_MODEL, tn_m), col), pl.BlockSpec((D_MODEL, tn_r), col),
                  pl.BlockSpec((CONV_W, tn_m), col), pl.BlockSpec((1, tn_m), col),
                  pl.BlockSpec((1, CONV_PAD, tn_m), seq_blk)],
        out_specs=[pl.BlockSpec((tm, tn_m), tile), pl.BlockSpec((tm, tn_r), tile),
                   pl.BlockSpec((1, CONV_PAD, tn_m), seq_blk)],
        out_shape=[jax.ShapeDtypeStruct((t, MIX_COLS), F32), jax.ShapeDtypeStruct((t, REST_COLS), F32),
                   jax.ShapeDtypeStruct((t // seq, CONV_PAD, MIX_COLS), F32)],
        scratch_shapes=[pltpu.VMEM((CONV_PAD + PROJ_ROWS, tn_m), F32)],
        compiler_params=_params(("parallel", "arbitrary")),
        name="in_proj_conv",
    )(u, w_mix, w_rest, cw, cb, cst8)


def _memkv_body(x_ref, g_ref, w_ref, o_ref):
    o_ref[...] = _mm(_rms(x_ref[...], g_ref[...]).astype(BF16), w_ref[...])


MEMKV_TM = 512


def _memkv(mem, g, w):
    t = mem.shape[0]
    tm = min(MEMKV_TM, t)
    row = lambda i: (i, 0)
    return pl.pallas_call(
        _memkv_body,
        grid=(t // tm,),
        in_specs=[pl.BlockSpec((tm, D_MODEL), row), _const_spec(g.shape), _const_spec(w.shape)],
        out_specs=pl.BlockSpec((tm, 2 * MEM_WIDTH), row),
        out_shape=jax.ShapeDtypeStruct((t, 2 * MEM_WIDTH), F32),
        compiler_params=_params(("parallel",)),
        name="memkv",
    )(mem, g, w)


MERGE_TM = 256


def _merge_ffn2_body(h_ref, yg_ref, ys_ref, ym_ref, gate_ref, wb_ref, wout_ref,
                     g_ref, wi_ref, wo_ref, gf_ref, y_ref):
    gates = gate_ref[...]
    merged = (_sigmoid(gates[:, 0:D_MODEL]) * _mm(yg_ref[...], wb_ref[0:GDN_V, :])
              + _sigmoid(gates[:, D_MODEL:2 * D_MODEL])
              * _mm(ys_ref[...], wb_ref[GDN_V:GDN_V + SSM_DINNER, :])
              + _sigmoid(gates[:, 2 * D_MODEL:3 * D_MODEL])
              * _mm(ym_ref[...], wb_ref[GDN_V + SSM_DINNER:, :]))
    h = h_ref[...] + _mm(merged.astype(BF16), wout_ref[...])
    h = _ffn_compute(h, g_ref, wi_ref, wo_ref)
    y_ref[...] = _rms(h, gf_ref[...])


def _merge_ffn2(h, yg, ys, ym, proj, wb, wout, g, wi, wo, gf):
    t = h.shape[0]
    tm = min(MERGE_TM, t)
    row = lambda i: (i, 0)
    gate_blk = REST_GATE // (3 * D_MODEL)
    return pl.pallas_call(
        _merge_ffn2_body,
        grid=(t // tm,),
        in_specs=[pl.BlockSpec((tm, D_MODEL), row), pl.BlockSpec((tm, GDN_V), row),
                  pl.BlockSpec((tm, SSM_DINNER), row), pl.BlockSpec((tm, MEM_WIDTH), row),
                  pl.BlockSpec((tm, 3 * D_MODEL), lambda i: (i, gate_blk)),
                  _const_spec(wb.shape), _const_spec(wout.shape), _const_spec(g.shape),
                  _const_spec(wi.shape), _const_spec(wo.shape), _const_spec(gf.shape)],
        out_specs=pl.BlockSpec((tm, D_MODEL), row),
        out_shape=jax.ShapeDtypeStruct((t, D_MODEL), F32),
        compiler_params=_params(("parallel",)),
        name="merge_ffn2",
    )(h, yg, ys, ym, proj, wb, wout, g, wi, wo, gf)


MEMATTN_TL = 512
MEMATTN_NSEQ = 16


def _memattn_body(q_ref, k_ref, v_ref, y_ref, *, nseq, rows):
    lane = lax.broadcasted_iota(jnp.int32, (rows, MEM_WIDTH), 1)
    masks = [(lane >= hh * MEM_HEAD_DIM) & (lane < (hh + 1) * MEM_HEAD_DIM) for hh in range(MEM_HEADS)]
    seqs = range(nseq)
    q = [q_ref[s * rows:(s + 1) * rows, :] for s in seqs]
    q4 = [jnp.concatenate([jnp.where(m, x, 0.0) for m in masks], axis=0).astype(BF16) for x in q]
    sc = [_mm(q4[s], k_ref[s].astype(BF16)) * (MEM_HEAD_DIM ** -0.5) for s in seqs]
    p = [jnp.exp(x - jnp.max(x, axis=-1, keepdims=True)) for x in sc]
    p = [(x / jnp.sum(x, axis=-1, keepdims=True)).astype(BF16) for x in p]
    o4 = [_mm_nt(p[s], v_ref[s].astype(BF16)) for s in seqs]
    ys = []
    for x in o4:
        y = jnp.where(masks[0], x[0:rows], 0.0)
        for hh in range(1, MEM_HEADS):
            y = y + jnp.where(masks[hh], x[hh * rows:(hh + 1) * rows], 0.0)
        ys.append(y)
    y_ref[...] = jnp.concatenate(ys, axis=0).astype(BF16)


def _memattn(proj, k, v, bsz, seq):
    if seq >= MEMATTN_TL:
        nseq, rows, nt = 1, MEMATTN_TL, seq // MEMATTN_TL
    else:
        nseq, rows, nt = MEMATTN_NSEQ, seq, 1
    assert bsz % nseq == 0 and seq % rows == 0
    tl = nseq * rows
    qm_blk = REST_QM // MEM_WIDTH
    kv_spec = pl.BlockSpec((nseq, MEM_WIDTH, N_MEM), lambda b, i: (b, 0, 0))
    return pl.pallas_call(
        functools.partial(_memattn_body, nseq=nseq, rows=rows),
        grid=(bsz // nseq, nt),
        in_specs=[pl.BlockSpec((tl, MEM_WIDTH), lambda b, i: (b * nt + i, qm_blk)), kv_spec, kv_spec],
        out_specs=pl.BlockSpec((tl, MEM_WIDTH), lambda b, i: (b * nt + i, 0)),
        out_shape=jax.ShapeDtypeStruct((bsz * seq, MEM_WIDTH), BF16),
        compiler_params=_params(("parallel", "arbitrary")),
        name="memattn",
    )(proj, k, v)


SEQ_TL = 256
LONG_NSEQ = 2
SHORT_NSEQ = 8
CONV_PAD = SUBLANES
MIN_MXU_ROWS = 16

Geom = collections.namedtuple("Geom", ["nseq", "nch", "vrows"])


def _geom(bsz, seq):
    if seq >= CHUNK:
        assert seq % SEQ_TL == 0
        nseq = LONG_NSEQ if bsz % LONG_NSEQ == 0 else 1
        return Geom(nseq, SEQ_TL // CHUNK, CHUNK), seq // SEQ_TL
    assert bsz % SHORT_NSEQ == 0 and seq % SUBLANES == 0 and (SHORT_NSEQ * seq) % CHUNK == 0
    return Geom(SHORT_NSEQ, 1, seq), 1


class _TokenView:
    def __init__(self, ref):
        self.ref = ref

    def _locate(self, rows):
        per = self.ref.shape[1]
        s = rows.start // per
        assert rows.stop <= (s + 1) * per
        return s, slice(rows.start - s * per, rows.stop - s * per)

    def __getitem__(self, idx):
        rows, cols = idx
        if len(self.ref.shape) == 2:
            return self.ref[rows, cols]
        if rows == slice(None):
            v = self.ref[:, :, cols]
            return v.reshape(v.shape[0] * v.shape[1], v.shape[2])
        s, r = self._locate(rows)
        return self.ref[s, r, cols]

    def __setitem__(self, idx, value):
        rows, cols = idx
        if len(self.ref.shape) == 2:
            self.ref[rows, cols] = value
        elif rows == slice(None):
            nseq, per = self.ref.shape[0], self.ref.shape[1]
            self.ref[:, :, cols] = value.reshape(nseq, per, value.shape[1])
        else:
            s, r = self._locate(rows)
            self.ref[s, r, cols] = value


def _short_conv_silu(x_ref, cst_ref, cw_ref, cb_ref, act, geom):
    assert geom.nch == 1 and geom.vrows < CHUNK and CONV_PAD == SUBLANES
    vr = geom.vrows
    sub = lax.broadcasted_iota(jnp.int32, (SUBLANES, LANES), 0)
    for cb in range(x_ref.shape[1] // LANES):
        cols = slice(cb * LANES, (cb + 1) * LANES)
        planes = [cst_ref[j, :, cols] for j in range(CONV_W - 1)]
        for s in range(geom.nseq):
            before = jnp.zeros((SUBLANES, LANES), F32)
            for j, plane in enumerate(planes):
                before = jnp.where(sub == SUBLANES - (CONV_W - 1) + j,
                                   jnp.broadcast_to(plane[s:s + 1, :], (SUBLANES, LANES)), before)
            xa = jnp.concatenate([before, x_ref[s * vr:(s + 1) * vr, cols]], axis=0)
            act[s * vr:(s + 1) * vr, cols] = _causal_conv_silu(xa, cw_ref[:, cols], cb_ref[:, cols])


def _pair_masks(vr):
    row = lax.broadcasted_iota(jnp.int32, (CHUNK, LANES), 0)
    lane = lax.broadcasted_iota(jnp.int32, (CHUNK, LANES), 1)
    col = jnp.where(lane < CHUNK, lane, lane - CHUNK)
    same = (row // vr) == (col // vr)
    return row, lane, col, same


def _block_diag(x, left):
    zero = jnp.zeros_like(x)
    return jnp.concatenate([jnp.where(left, x, zero), jnp.where(left, zero, x)], axis=0)


def _chunk_cumsums(gates, nprob, vr):
    r = lax.broadcasted_iota(jnp.int32, (CHUNK, CHUNK), 0)
    c = lax.broadcasted_iota(jnp.int32, (CHUNK, CHUNK), 1)
    same = (r // vr) == (c // vr)
    sums = jnp.concatenate([jnp.where(same & (r >= c), 1.0, 0.0), jnp.where(same, 1.0, 0.0)],
                           axis=0).astype(BF16)
    sums3 = jnp.concatenate([sums, sums, sums], axis=1)
    pieces = [jnp.concatenate(_split3(gates[pb * CHUNK:(pb + 1) * CHUNK, 0:LANES]), axis=0)
              for pb in range(nprob)]
    out = _mm(sums3, jnp.concatenate(pieces, axis=1))
    cols = lambda pb: slice(pb * LANES, (pb + 1) * LANES)
    return ([out[0:CHUNK, cols(pb)] for pb in range(nprob)],
            [out[CHUNK:, cols(pb)] for pb in range(nprob)])


def _rows_to_columns(e):
    pieces = jnp.concatenate(_split3(e), axis=0)
    pad = jnp.zeros((MIN_MXU_ROWS - pieces.shape[0], e.shape[1]), BF16)
    row = lax.broadcasted_iota(jnp.int32, (MIN_MXU_ROWS, LANES), 0)
    ones = jnp.where(row < pieces.shape[0], 1.0, 0.0).astype(BF16)
    return _mm_tn(jnp.concatenate([pieces, pad], axis=0), ones)


def _pad_rows(x):
    if x.shape[0] >= MIN_MXU_ROWS:
        return x
    return jnp.concatenate([x, jnp.zeros((MIN_MXU_ROWS - x.shape[0], x.shape[1]), x.dtype)], axis=0)


def _row_form(gc):
    return jnp.concatenate([gc, pltpu.roll(gc, LANES - 1, axis=1)], axis=0).T


def _gdn_body(*refs, geom):
    if geom.vrows == CHUNK:
        (x_ref, sm_ref, z_ref, s0_ref, par_ref, gn_ref, selp_ref, y_ref, sout_ref,
         gates, s_scr, u_scr, w_scr, qd_scr, kd_scr, qk_scr, egl_scr, o_scr) = refs
        act = _TokenView(x_ref)
    else:
        (x_ref, sm_ref, z_ref, cst_ref, cw_ref, cb_ref, s0_ref, par_ref, gn_ref, selp_ref, y_ref,
         sout_ref, act, gates, s_scr, u_scr, w_scr, qd_scr, kd_scr, qk_scr, egl_scr, o_scr) = refs
        _short_conv_silu(x_ref, cst_ref, cw_ref, cb_ref, act, geom)
    tile = pl.program_id(1)
    first = tile == 0
    vr = geom.vrows
    nsub = CHUNK // vr
    nprob = geom.nseq * geom.nch // nsub

    @pl.when(first)
    def _():
        s_scr[...] = s0_ref[...]

    sm = _TokenView(sm_ref)[:, :]
    lane_g = lax.broadcasted_iota(jnp.int32, sm.shape, 1)
    head_lane = lane_g < GDN_HEADS
    gates[:, 0:LANES] = jnp.where(
        head_lane, -jnp.exp(par_ref[0:1, :]) * _softplus(sm + par_ref[1:2, :]), 0.0)
    gates[:, LANES:] = jnp.where(head_lane, pltpu.roll(_sigmoid(sm), LANES - SM_B, axis=1), 0.0)

    row, lane, col, same = _pair_masks(vr)
    left = lane < CHUNK
    incl = same & (row >= col)
    strict = same & (row > col)
    eye = jnp.where(row == col, 1.0, 0.0).astype(F32)
    zeros_h = jnp.zeros((CHUNK, LANES), F32)
    hcols = lambda h: slice(h * LANES, (h + 1) * LANES)
    prows = lambda pb, n=CHUNK: slice(pb * n, (pb + 1) * n)

    gcs, gls = _chunk_cumsums(gates, nprob, vr)
    lhs, xts = [], []
    for pb in range(nprob):
        gc = gcs[pb]
        quantities = jnp.concatenate(
            [gc, gates[prows(pb), LANES:2 * LANES], jnp.exp(gc), jnp.exp(gls[pb] - gc)], axis=0)
        lhs.append(_split2_lanes(quantities))
        xts.append(_row_form(gc))

    def l2n(t):
        return t * lax.rsqrt(jnp.sum(t * t, axis=-1, keepdims=True) + EPS)

    def stacked(e, qi):
        return jnp.concatenate([e[qi * CHUNK:(qi + 1) * CHUNK, 0:LANES],
                                e[qi * CHUNK:(qi + 1) * CHUNK, LANES:]], axis=0)

    def phase1(pbs):
        items = [(pb, p) for pb in pbs for p in range(GDN_PAIRS)]
        exp_ = [_mm(lhs[pb], selp_ref[p]) for pb, p in items]
        decay_l, kq_l, rhs_l, qd_l, kd_l = [], [], [], [], []
        for (pb, p), e in zip(items, exp_):
            ha, hb = 2 * p, 2 * p + 1
            qa, qb = [l2n(act[prows(pb), hcols(h)]) * (GDN_DK ** -0.5) for h in (ha, hb)]
            ka, kb = [l2n(act[prows(pb), hcols(GDN_HEADS + h)]) for h in (ha, hb)]
            va, vb = [act[prows(pb), hcols(2 * GDN_HEADS + h)] for h in (ha, hb)]
            beta2, eg2, kdec2 = stacked(e, 1), stacked(e, 2), stacked(e, 3)
            k2 = jnp.concatenate([ka, kb], axis=0)
            kbeta2 = k2 * beta2
            gci = jnp.where(left, e[0:CHUNK, 0:LANES], e[0:CHUNK, LANES:])
            gcj = jnp.broadcast_to(xts[pb][ha:ha + 1, :], (CHUNK, LANES))
            decay_l.append(jnp.where(incl, jnp.exp(jnp.where(incl, gci - gcj, 0.0)), 0.0))
            k_bd = jnp.concatenate([jnp.concatenate([ka, zeros_h], axis=1),
                                    jnp.concatenate([zeros_h, kb], axis=1)], axis=0).astype(BF16)
            kbq = jnp.concatenate([jnp.concatenate([kbeta2[0:CHUNK], kbeta2[CHUNK:]], axis=1),
                                   jnp.concatenate([qa, qb], axis=1)], axis=0).astype(BF16)
            kq_l.append(_mm_nt(kbq, k_bd))
            rhs_l.append(jnp.concatenate(
                [jnp.concatenate([va, vb], axis=0) * beta2, kbeta2 * eg2], axis=1).astype(BF16))
            qd_l.append(jnp.concatenate([qa, qb], axis=0) * eg2)
            kd_l.append(k2 * kdec2)
            for sub in range(nsub):
                last = 2 * CHUNK + (sub + 1) * vr - 1
                slot = prows(pb * nsub + sub, SUBLANES)
                egl_scr[slot, hcols(ha)] = jnp.broadcast_to(e[last:last + 1, 0:LANES], (SUBLANES, LANES))
                egl_scr[slot, hcols(hb)] = jnp.broadcast_to(e[last:last + 1, LANES:], (SUBLANES, LANES))

        n_pow = [-jnp.where(strict, kq[0:CHUNK] * d, 0.0) for kq, d in zip(kq_l, decay_l)]
        t_inv = [eye + n for n in n_pow]
        for _ in range(int(np.ceil(np.log2(vr))) - 1):
            n_pow = [_mm(n.astype(BF16), _block_diag(n, left).astype(BF16)) for n in n_pow]
            t_inv = [t + _mm(t.astype(BF16), _block_diag(n, left).astype(BF16))
                     for t, n in zip(t_inv, n_pow)]

        uw_l = [_mm(_block_diag(t, left).astype(BF16), rhs) for t, rhs in zip(t_inv, rhs_l)]
        for (pb, p), uw, kq, d, qd, kd in zip(items, uw_l, kq_l, decay_l, qd_l, kd_l):
            qk_scr[prows(pb, 2 * CHUNK), hcols(p)] = _block_diag(kq[CHUNK:] * d, left).astype(BF16)
            for idx, h in enumerate((2 * p, 2 * p + 1)):
                sl = slice(idx * CHUNK, (idx + 1) * CHUNK)
                u_scr[prows(pb), hcols(h)] = uw[sl, 0:LANES]
                w_scr[prows(pb), hcols(h)] = uw[sl, LANES:]
                qd_scr[prows(pb), hcols(h)] = qd[sl]
                kd_scr[prows(pb), hcols(h)] = kd[sl]

    def phase2(c):
        sh = [(s, h) for s in range(geom.nseq) for h in range(GDN_HEADS)]
        trows = lambda s: prows(s * geom.nch + c, vr)
        wqs = [_mm(jnp.concatenate([w_scr[trows(s), hcols(h)], qd_scr[trows(s), hcols(h)]],
                                   axis=0).astype(BF16), s_scr[s, h].astype(BF16)) for s, h in sh]
        vnew = [u_scr[trows(s), hcols(h)] - w[0:vr] for (s, h), w in zip(sh, wqs)]
        for (s, h), vn in zip(sh, vnew):
            slot = (s * geom.nch + c) * SUBLANES
            s_scr[s, h] = (s_scr[s, h] * egl_scr[slot:slot + 1, hcols(h)]
                           + _mm_tn(_pad_rows(kd_scr[trows(s), hcols(h)]).astype(BF16),
                                    _pad_rows(vn).astype(BF16)))
        for pb in range(c * nsub, geom.nseq * geom.nch, geom.nch * nsub):
            subs = [pb // geom.nch + j for j in range(nsub)]
            for p in range(GDN_PAIRS):
                heads = (2 * p, 2 * p + 1)
                vn2 = jnp.concatenate([vnew[s * GDN_HEADS + h] for h in heads for s in subs],
                                      axis=0).astype(BF16)
                intra = _mm(qk_scr[prows((pb // nsub), 2 * CHUNK), hcols(p)], vn2)
                for idx, h in enumerate(heads):
                    for j, s in enumerate(subs):
                        r0 = idx * CHUNK + j * vr
                        o = wqs[s * GDN_HEADS + h][vr:] + intra[r0:r0 + vr]
                        o_scr[trows(s), hcols(h)] = o

    phase1(range(nprob))
    for c in range(geom.nch):
        phase2(c)

    for h in range(GDN_HEADS):
        y = _rms(o_scr[:, hcols(h)], gn_ref[...]) * _silu(_TokenView(z_ref)[:, hcols(h)])
        _TokenView(y_ref)[:, hcols(h)] = y.astype(BF16)

    @pl.when(tile == pl.num_programs(1) - 1)
    def _():
        sout_ref[...] = s_scr[...]


def _conv_operands(conv, geom, width, col_blk):
    if conv is None:
        return [], [], []
    state, cw, cb = conv
    specs = [pl.BlockSpec((CONV_W - 1, geom.nseq, width), lambda b, i: (0, b, 0)),
             pl.BlockSpec((CONV_W, width), lambda b, i: (0, col_blk)),
             pl.BlockSpec((1, width), lambda b, i: (0, col_blk))]
    return [state, cw, cb], specs, [pltpu.VMEM((geom.nseq * geom.vrows, width), F32)]


def _token_layout(geom, bsz, seq, nt):
    tls = geom.nch * geom.vrows
    if geom.vrows == CHUNK:
        view = lambda a: a.reshape(bsz, seq, a.shape[-1])
        spec = lambda width, blk: pl.BlockSpec((geom.nseq, tls, width), lambda b, i: (b, i, blk))
        return view, spec, lambda width: (bsz, seq, width)
    spec = lambda width, blk: pl.BlockSpec((geom.nseq * tls, width), lambda b, i: (b * nt + i, blk))
    return (lambda a: a), spec, lambda width: (bsz * seq, width)


def _gdn(mix, rest, conv, s0, par, gn, selp, bsz, seq):
    geom, nt = _geom(bsz, seq)
    tl = geom.nseq * geom.nch * geom.vrows
    nprob = tl // CHUNK
    view, tok, out_shape = _token_layout(geom, bsz, seq, nt)
    st_spec = pl.BlockSpec((geom.nseq, GDN_HEADS, GDN_DK, GDN_DV), lambda b, i: (b, 0, 0, 0))
    conv_ops, conv_specs, conv_scratch = _conv_operands(conv, geom, GDN_CONV_CH, MIX_QKV // GDN_CONV_CH)
    y, s_new = pl.pallas_call(
        functools.partial(_gdn_body, geom=geom),
        grid=(bsz // geom.nseq, nt),
        in_specs=[tok(GDN_CONV_CH, MIX_QKV // GDN_CONV_CH), tok(LANES, REST_SMALL // LANES),
                  tok(GDN_V, REST_ZG // GDN_V)] + conv_specs + [
                  st_spec, _const_spec(par.shape), _const_spec(gn.shape), _const_spec(selp.shape)],
        out_specs=[tok(GDN_V, 0), st_spec],
        out_shape=[jax.ShapeDtypeStruct(out_shape(GDN_V), BF16),
                   jax.ShapeDtypeStruct((bsz, GDN_HEADS, GDN_DK, GDN_DV), F32)],
        scratch_shapes=conv_scratch + [
            pltpu.VMEM((tl, 2 * LANES), F32),
            pltpu.VMEM((geom.nseq, GDN_HEADS, GDN_DK, GDN_DV), F32),
            pltpu.VMEM((tl, GDN_V), F32),
            pltpu.VMEM((tl, GDN_V), F32),
            pltpu.VMEM((tl, GDN_V), F32),
            pltpu.VMEM((tl, GDN_V), F32),
            pltpu.VMEM((nprob * 2 * CHUNK, GDN_PAIRS * LANES), BF16),
            pltpu.VMEM((geom.nseq * geom.nch * SUBLANES, GDN_V), F32),
            pltpu.VMEM((tl, GDN_V), F32)],
        compiler_params=_params(("parallel", "arbitrary")),
        name="gdn",
    )(view(mix), view(rest), view(rest), *conv_ops, s0, par, gn, selp)
    return y.reshape(bsz * seq, GDN_V), s_new


def _ssd_body(*refs, geom):
    if geom.vrows == CHUNK:
        (x_ref, sm_ref, z_ref, h0_ref, par_ref, dch_ref, nw_ref, seld_ref, y_ref, hout_ref,
         gates, h_scr, o_scr) = refs
        act = _TokenView(x_ref)
        tile = pl.program_id(1)

        @pl.when(tile == 0)
        def _():
            for s in range(geom.nseq):
                for p in range(SSM_PAIRS):
                    h_scr[s, p] = h0_ref[s, p].T
    else:
        (x_ref, sm_ref, z_ref, cst_ref, cw_ref, cb_ref, h0_ref, par_ref, dch_ref, nw_ref, seld_ref,
         y_ref, hout_ref, act, gates, o_scr) = refs
        _short_conv_silu(x_ref, cst_ref, cw_ref, cb_ref, act, geom)
    vr = geom.vrows
    nsub = CHUNK // vr
    nprob = geom.nseq * geom.nch // nsub

    sm = _TokenView(sm_ref)[:, :]
    lane_g = lax.broadcasted_iota(jnp.int32, sm.shape, 1)
    is_dt = (lane_g >= SM_DT) & (lane_g < SM_DT + SSM_HEADS)
    dt = jnp.where(is_dt, _softplus(sm + par_ref[1:2, :]), 0.0)
    gates[:, 0:LANES] = dt * jnp.where(is_dt, -jnp.exp(par_ref[0:1, :]), 0.0)
    gates[:, LANES:] = dt

    row, lane, col, same = _pair_masks(vr)
    left = lane < CHUNK
    incl = same & (row >= col)
    pcols = lambda p: slice(p * LANES, (p + 1) * LANES)
    prows = lambda pb, n=CHUNK: slice(pb * n, (pb + 1) * n)
    quarter = lambda e, qi, j: e[qi * CHUNK:(qi + 1) * CHUNK, j * LANES:(j + 1) * LANES]
    srows = lambda a, j: a[j * vr:(j + 1) * vr]
    mxu_rows = lambda a: _pad_rows(a).astype(BF16)
    b_off = SSM_DINNER
    c_off = SSM_DINNER + SSM_GROUPS * SSM_DSTATE

    gcs, gls = _chunk_cumsums(gates, nprob, vr)
    for pb in range(nprob):
        seqs = [(pb * nsub + j) // geom.nch for j in range(nsub)]
        gc = gcs[pb]
        dt_c = gates[prows(pb), LANES:2 * LANES]
        lhs = _split2_lanes(jnp.concatenate(
            [gc, dt_c, jnp.exp(gc), jnp.exp(gls[pb] - gc) * dt_c], axis=0))
        xt = _row_form(gc)
        exp_ = [_mm(lhs, seld_ref[d]) for d in range(SSM_PAIRS // 2)]

        b_f = [act[prows(pb), b_off + g * SSM_DSTATE:b_off + (g + 1) * SSM_DSTATE]
               for g in range(SSM_GROUPS)]
        c_f = [act[prows(pb), c_off + g * SSM_DSTATE:c_off + (g + 1) * SSM_DSTATE]
               for g in range(SSM_GROUPS)]
        b_bf = [b.astype(BF16) for b in b_f]
        c_bf = [c.astype(BF16) for c in c_f]
        cb2 = [_mm_nt(c, jnp.concatenate([b, b], axis=0)) for b, c in zip(b_bf, c_bf)]

        pairs = range(SSM_PAIRS)
        grp = lambda p: p // SSM_GROUP_PAIRS
        x_l = [act[prows(pb), pcols(p)] for p in pairs]
        gc_l = [quarter(exp_[p // 2], 0, p % 2) for p in pairs]
        eg_l = [quarter(exp_[p // 2], 2, p % 2) for p in pairs]
        xdt_l = [x * quarter(exp_[p // 2], 1, p % 2) for p, x in zip(pairs, x_l)]
        xw_l = [x * quarter(exp_[p // 2], 3, p % 2) for p, x in zip(pairs, x_l)]
        m_l = []
        for p in pairs:
            r = SM_DT + 2 * p
            gcj = jnp.broadcast_to(xt[r:r + 1, :], (CHUNK, LANES))
            decay = jnp.where(incl, jnp.exp(jnp.where(incl, gc_l[p] - gcj, 0.0)), 0.0)
            m_l.append((cb2[grp(p)] * decay).astype(BF16))
        intra = [_mm(m, _block_diag(xdt, left).astype(BF16)) for m, xdt in zip(m_l, xdt_l)]
        last_row = lambda a, j: a[(j + 1) * vr - 1:(j + 1) * vr, :]
        if geom.vrows == CHUNK:
            dstate = [_mm_tn(b_bf[grp(p)], xw_l[p].astype(BF16)) for p in pairs]
            inter = [_mm(c_bf[grp(p)], h_scr[seqs[0], p].astype(BF16)) for p in pairs]
            for p in pairs:
                h_scr[seqs[0], p] = h_scr[seqs[0], p] * last_row(eg_l[p], 0) + dstate[p]
        else:
            gp = SSM_GROUP_PAIRS
            jg = [(j, g) for j in range(nsub) for g in range(SSM_GROUPS)]
            gslice = lambda g: slice(g * gp, (g + 1) * gp)
            lanes = lambda a_l, g, f: jnp.concatenate([f(a) for a in a_l[gslice(g)]], axis=1)
            inter_g = []
            for j, g in jg:
                ds = _mm_tn(mxu_rows(lanes(xw_l, g, lambda a: srows(a, j))), mxu_rows(srows(b_f[g], j)))
                sc = _rows_to_columns(lanes(eg_l, g, lambda a: last_row(a, j)))
                h = h0_ref[seqs[j], gslice(g)].reshape(gp * LANES, SSM_DSTATE)
                inter_g.append(_mm_nt(mxu_rows(srows(c_f[g], j)), h.astype(BF16))[0:vr])
                hout_ref[seqs[j], gslice(g)] = (h * sc + ds).reshape(gp, LANES, SSM_DSTATE)
            inter = [jnp.concatenate(
                [inter_g[j * SSM_GROUPS + grp(p)][:, (p % gp) * LANES:(p % gp + 1) * LANES]
                 for j in range(nsub)], axis=0) for p in pairs]

        for g in range(SSM_GROUPS):
            ys = []
            ssq = None
            for p in range(g * SSM_GROUP_PAIRS, (g + 1) * SSM_GROUP_PAIRS):
                y = intra[p] + inter[p] * eg_l[p] + dch_ref[:, pcols(p)] * x_l[p]
                y = y * _silu(_TokenView(z_ref)[prows(pb), pcols(p)])
                ys.append(y)
                sq = jnp.sum(y * y, axis=-1, keepdims=True)
                ssq = sq if ssq is None else ssq + sq
            inv = lax.rsqrt(ssq * (1.0 / SSM_GROUP_CH) + EPS)
            for pp, y in enumerate(ys):
                o_scr[prows(pb), pcols(g * SSM_GROUP_PAIRS + pp)] = y * inv

    _TokenView(y_ref)[:, slice(None)] = (o_scr[...] * nw_ref[...]).astype(BF16)

    if geom.vrows == CHUNK:
        @pl.when(tile == pl.num_programs(1) - 1)
        def _():
            for s in range(geom.nseq):
                for p in range(SSM_PAIRS):
                    hout_ref[s, p] = h_scr[s, p].T


def _ssd(mix, rest, conv, h0, par, dch, nw, seld, bsz, seq):
    geom, nt = _geom(bsz, seq)
    tl = geom.nseq * geom.nch * geom.vrows
    view, tok, out_shape = _token_layout(geom, bsz, seq, nt)
    st_shape = (geom.nseq, SSM_PAIRS, 2 * SSM_HEADDIM, SSM_DSTATE)
    st_spec = pl.BlockSpec(st_shape, lambda b, i: (b, 0, 0, 0))
    conv_ops, conv_specs, conv_scratch = _conv_operands(conv, geom, SSM_CONV_CH, MIX_XBC // SSM_CONV_CH)
    y, h_new = pl.pallas_call(
        functools.partial(_ssd_body, geom=geom),
        grid=(bsz // geom.nseq, nt),
        in_specs=[tok(SSM_CONV_CH, MIX_XBC // SSM_CONV_CH), tok(LANES, REST_SMALL // LANES),
                  tok(SSM_DINNER, REST_ZS // SSM_DINNER)] + conv_specs + [
                  st_spec, _const_spec(par.shape), _const_spec(dch.shape), _const_spec(nw.shape),
                  _const_spec(seld.shape)],
        out_specs=[tok(SSM_DINNER, 0), st_spec],
        out_shape=[jax.ShapeDtypeStruct(out_shape(SSM_DINNER), BF16),
                   jax.ShapeDtypeStruct((bsz,) + st_shape[1:], F32)],
        scratch_shapes=conv_scratch + [pltpu.VMEM((tl, 2 * LANES), F32)] + (
            [pltpu.VMEM((geom.nseq, SSM_PAIRS, SSM_DSTATE, 2 * SSM_HEADDIM), F32)]
            if conv is None else []) + [
            pltpu.VMEM((tl, SSM_DINNER), F32)],
        compiler_params=_params(("parallel", "arbitrary")),
        name="ssd",
    )(view(mix), view(rest), view(rest), *conv_ops, h0, par, dch, nw, seld)
    return y.reshape(bsz * seq, SSM_DINNER), h_new


def _pad_lanes(v, offset):
    out = jnp.zeros((LANES,), F32)
    return out.at[offset:offset + v.shape[0]].set(v.astype(F32))


def _pair_selection(first_lane, lanes_per_head, n_mats):
    heads_per_mat = 2 * LANES // lanes_per_head
    sel = np.zeros((n_mats, 2 * LANES, 2 * LANES), np.float32)
    for m in range(n_mats):
        for j in range(heads_per_mat):
            src = first_lane + m * heads_per_mat + j
            sel[m, src, j * lanes_per_head:(j + 1) * lanes_per_head] = 1.0
            sel[m, LANES + src, j * lanes_per_head:(j + 1) * lanes_per_head] = 1.0
    return jnp.asarray(sel, BF16)


def _pad_conv_state(st):
    bsz, _, ch = st.shape
    return jnp.concatenate([jnp.zeros((bsz, CONV_PAD - (CONV_W - 1), ch), F32), st.astype(F32)], axis=1)


def _layer(x, mem_k, mem_v, s_gdn, c_gdn, s_ssm, c_ssm, w):
    bsz, seq, _ = x.shape
    t = bsz * seq
    xf = x.reshape(t, D_MODEL)
    h, u = _ffn1(xf, w["norm_ff1"], w["w_ff1_in"], w["w_ff1_out"], w["norm_mix"])
    if seq >= CHUNK:
        cst8 = jnp.concatenate([_pad_conv_state(c_gdn), _pad_conv_state(c_ssm)], axis=2)
        mix, rest, tail = _in_proj_conv(u, w["w_mix"], w["w_rest"], w["conv_w"], w["conv_b"], cst8, seq)
        conv_gdn = conv_ssm = None
    else:
        rest = _in_proj(u, w["w_rest"], REST_TN, "in_proj_rest")
        mix = _in_proj(u, w["w_mix"], MIX_TN, "in_proj_mix")
        tail = mix.reshape(bsz, seq, MIX_COLS)[:, seq - CONV_PAD:, :]
        taps_first = lambda c: jnp.swapaxes(c.astype(F32), 0, 1)
        conv_gdn = (taps_first(c_gdn), w["conv_w"], w["conv_b"])
        conv_ssm = (taps_first(c_ssm), w["conv_w"], w["conv_b"])

    yg, s_gdn_new = _gdn(mix, rest, conv_gdn, s_gdn.astype(F32), w["gdn_par"], w["gdn_norm"],
                         w["sel_gdn"], bsz, seq)
    h0 = s_ssm.astype(F32).reshape(bsz, SSM_PAIRS, 2 * SSM_HEADDIM, SSM_DSTATE)
    ys, s_ssm_new = _ssd(mix, rest, conv_ssm, h0, w["ssm_par"], w["ssm_d_ch"], w["ssm_norm"],
                         w["sel_ssm"], bsz, seq)
    transposed = lambda m: jnp.swapaxes(m.reshape(bsz, N_MEM, MEM_WIDTH), 1, 2)
    ym = _memattn(rest, transposed(mem_k), transposed(mem_v), bsz, seq)
    y = _merge_ffn2(h, yg, ys, ym, rest, w["w_branch"], w["w_out"], w["norm_ff2"],
                    w["w_ff2_in"], w["w_ff2_out"], w["norm_final"])

    c_gdn_new = tail[:, CONV_PAD - (CONV_W - 1):, MIX_QKV:MIX_QKV + GDN_CONV_CH]
    c_ssm_new = tail[:, CONV_PAD - (CONV_W - 1):, MIX_XBC:MIX_XBC + SSM_CONV_CH]
    return (y.reshape(bsz, seq, D_MODEL), s_gdn_new, c_gdn_new,
            s_ssm_new.reshape(bsz, SSM_HEADS, SSM_HEADDIM, SSM_DSTATE), c_ssm_new)


def _row(v):
    return v.astype(F32).reshape(1, -1)


def _prep_weights(l, norm_ff1, w_ff1_in, w_ff1_out, norm_mix, w_in, gdn_conv_w, gdn_a_log,
                  gdn_dt_bias, gdn_norm, ssm_conv_w, ssm_conv_b, ssm_a_log, ssm_dt_bias, ssm_d,
                  ssm_norm, w_branch, w_out, norm_ff2, w_ff2_in, w_ff2_out, norm_final):
    row = _row

    w_mix, w_rest = _pack_w_in(jnp.swapaxes(w_in[l], 0, 1))

    def two_rows(a, b, offset):
        par = jnp.zeros((SUBLANES, LANES), F32)
        return par.at[0].set(_pad_lanes(a, offset)).at[1].set(_pad_lanes(b, offset))

    return {
        "norm_ff1": row(norm_ff1[l]), "w_ff1_in": w_ff1_in[l].astype(BF16),
        "w_ff1_out": w_ff1_out[l].astype(BF16), "norm_mix": row(norm_mix[l]),
        "w_mix": w_mix, "w_rest": w_rest,
        "conv_w": jnp.concatenate([gdn_conv_w[l], ssm_conv_w[l]], axis=1).astype(F32),
        "conv_b": jnp.concatenate([jnp.zeros((1, GDN_CONV_CH), F32), row(ssm_conv_b[l])], axis=1),
        "gdn_par": two_rows(gdn_a_log[l], gdn_dt_bias[l], SM_A),
        "gdn_norm": row(gdn_norm[l]),
        "sel_gdn": _pair_selection(0, LANES, GDN_PAIRS),
        "ssm_par": two_rows(ssm_a_log[l], ssm_dt_bias[l], SM_DT),
        "ssm_d_ch": jnp.repeat(ssm_d[l].astype(F32), SSM_HEADDIM).reshape(1, -1),
        "ssm_norm": row(ssm_norm[l]),
        "sel_ssm": _pair_selection(SM_DT, SSM_HEADDIM, SSM_PAIRS // 2),
        "w_branch": w_branch[l].astype(BF16), "w_out": w_out[l].astype(BF16),
        "norm_ff2": row(norm_ff2[l]), "w_ff2_in": w_ff2_in[l].astype(BF16),
        "w_ff2_out": w_ff2_out[l].astype(BF16), "norm_final": row(norm_final),
    }


def kernel(x_prompt, x_sample, mem_prompt, state_gdn, state_gdn_conv, state_ssm, state_ssm_conv,
           cache_mem_k, cache_mem_v, norm_ff1, w_ff1_in, w_ff1_out, norm_mix, w_in,
           gdn_conv_w, gdn_a_log, gdn_dt_bias, gdn_norm, ssm_conv_w, ssm_conv_b, ssm_a_log,
           ssm_dt_bias, ssm_d, ssm_norm, norm_mem, w_mem_kv, w_branch, w_out,
           norm_ff2, w_ff2_in, w_ff2_out, norm_final):
    assert w_in.shape[0] == 1, "the kernels implement the single-layer configuration"
    l = 0
    bp = x_prompt.shape[0]
    w = _prep_weights(l, norm_ff1, w_ff1_in, w_ff1_out, norm_mix, w_in, gdn_conv_w, gdn_a_log,
                      gdn_dt_bias, gdn_norm, ssm_conv_w, ssm_conv_b, ssm_a_log, ssm_dt_bias, ssm_d,
                      ssm_norm, w_branch, w_out, norm_ff2, w_ff2_in, w_ff2_out, norm_final)

    n_mem_tok = mem_prompt.shape[0] * mem_prompt.shape[1]
    kv = _memkv(mem_prompt.reshape(n_mem_tok, D_MODEL), _row(norm_mem[l]), w_mem_kv[l].astype(BF16))
    mk = kv[:, :MEM_WIDTH].reshape(bp, N_MEM, MEM_HEADS, MEM_HEAD_DIM)
    mv = kv[:, MEM_WIDTH:].reshape(bp, N_MEM, MEM_HEADS, MEM_HEAD_DIM)
    dtp = x_prompt.dtype
    yp, sgp, cgp, ssp, csp = _layer(
        x_prompt, mk, mv,
        jnp.zeros((bp, GDN_HEADS, GDN_DK, GDN_DV), dtp), jnp.zeros((bp, CONV_W - 1, GDN_CONV_CH), dtp),
        jnp.zeros((bp, SSM_HEADS, SSM_HEADDIM, SSM_DSTATE), dtp),
        jnp.zeros((bp, CONV_W - 1, SSM_CONV_CH), dtp), w)
    ys_, sgs, cgs, sss, css = _layer(
        x_sample, cache_mem_k[l], cache_mem_v[l], state_gdn[l], state_gdn_conv[l],
        state_ssm[l], state_ssm_conv[l], w)

    lead = lambda a: a[None]
    return (yp, ys_, lead(sgp), lead(cgp), lead(ssp), lead(csp), lead(mk), lead(mv),
            lead(sgs), lead(cgs), lead(sss), lead(css))
```

```python
import collections
import functools

import numpy as np
import jax
import jax.numpy as jnp
from jax import lax
from jax.experimental import pallas as pl
from jax.experimental.pallas import tpu as pltpu

F32 = jnp.float32
BF16 = jnp.bfloat16
EPS = 1e-6

D_MODEL = 1024
FFN_DIM = 2816
CONV_W = 4
CHUNK = 64
LANES = 128
SUBLANES = 8

GDN_HEADS = 8
GDN_PAIRS = GDN_HEADS // 2
GDN_DK = 128
GDN_DV = 128
GDN_QK = GDN_HEADS * GDN_DK
GDN_V = GDN_HEADS * GDN_DV
GDN_CONV_CH = 2 * GDN_QK + GDN_V

SSM_DINNER = 2048
SSM_HEADDIM = 64
SSM_HEADS = 32
SSM_GROUPS = 4
SSM_DSTATE = 128
SSM_CONV_CH = SSM_DINNER + 2 * SSM_GROUPS * SSM_DSTATE
SSM_PAIRS = SSM_HEADS // 2
SSM_GROUP_PAIRS = SSM_PAIRS // SSM_GROUPS
SSM_GROUP_CH = SSM_DINNER // SSM_GROUPS

N_MEM = 256
MEM_HEADS = 4
MEM_HEAD_DIM = 64
MEM_WIDTH = MEM_HEADS * MEM_HEAD_DIM

MIX_QKV = 0
MIX_XBC = MIX_QKV + GDN_CONV_CH
MIX_COLS = MIX_XBC + SSM_CONV_CH
REST_GATE = 0
REST_ZG = REST_GATE + 3 * D_MODEL
REST_ZS = REST_ZG + GDN_V
REST_QM = REST_ZS + SSM_DINNER
REST_SMALL = REST_QM + MEM_WIDTH
REST_COLS = REST_SMALL + LANES
IN_SPLITS = (GDN_CONV_CH, GDN_HEADS, GDN_HEADS, GDN_V, SSM_DINNER, SSM_CONV_CH, SSM_HEADS, MEM_WIDTH,
             3 * D_MODEL)
SM_A = 0
SM_B = GDN_HEADS
SM_DT = 2 * GDN_HEADS

VMEM_LIMIT = 56 * 1024 * 1024


def _params(semantics):
    return pltpu.CompilerParams(dimension_semantics=semantics, vmem_limit_bytes=VMEM_LIMIT)


def _mm(a, b):
    return jnp.dot(a, b, preferred_element_type=F32)


def _mm_nt(a, b):
    return lax.dot_general(a, b, (((1,), (1,)), ((), ())), preferred_element_type=F32)


def _mm_tn(a, b):
    return lax.dot_general(a, b, (((0,), (0,)), ((), ())), preferred_element_type=F32)


def _split3(x):
    hi = x.astype(BF16)
    r1 = x - hi.astype(F32)
    mid = r1.astype(BF16)
    lo = (r1 - mid.astype(F32)).astype(BF16)
    return hi, mid, lo


def _split2_lanes(x):
    hi = x.astype(BF16)
    lo = (x - hi.astype(F32)).astype(BF16)
    return jnp.concatenate([hi, lo], axis=1)


def _rms(x, g):
    return x * lax.rsqrt(jnp.mean(x * x, axis=-1, keepdims=True) + EPS) * g


def _sigmoid(x):
    return 1.0 / (1.0 + jnp.exp2(x * float(-1.0 / np.log(2.0))))


def _silu(x):
    return x * _sigmoid(x)


def _softplus(x):
    return jnp.maximum(x, 0.0) + jnp.log1p(jnp.exp(-jnp.abs(x)))


def _const_spec(shape):
    nd = len(shape)
    return pl.BlockSpec(shape, lambda *_: (0,) * nd, pipeline_mode=pl.Buffered(1))


FFN_TM = 512
MXU_WIDTH = 256
FFN_CHUNKS = ((0, 6 * MXU_WIDTH), (6 * MXU_WIDTH, FFN_DIM))


def _ffn_compute(x, g_ref, wi_ref, wo_ref):
    xn = _rms(x, g_ref[...]).astype(BF16)
    acc = None
    for lo, hi in FFN_CHUNKS:
        gate = _mm(xn, wi_ref[:, lo:hi])
        up = _mm(xn, wi_ref[:, FFN_DIM + lo:FFN_DIM + hi])
        act = (_silu(gate) * up).astype(BF16)
        part = _mm(act, wo_ref[lo:hi, :])
        acc = part if acc is None else acc + part
    return x + 0.5 * acc


def _ffn1_body(x_ref, g_ref, wi_ref, wo_ref, g2_ref, h_ref, u_ref):
    h = _ffn_compute(x_ref[...], g_ref, wi_ref, wo_ref)
    h_ref[...] = h
    u_ref[...] = _rms(h, g2_ref[...]).astype(BF16)


def _ffn1(x, g, wi, wo, g2):
    t = x.shape[0]
    tm = min(FFN_TM, t)
    row = lambda i: (i, 0)
    return pl.pallas_call(
        _ffn1_body,
        grid=(t // tm,),
        in_specs=[pl.BlockSpec((tm, D_MODEL), row), _const_spec(g.shape), _const_spec(wi.shape),
                  _const_spec(wo.shape), _const_spec(g2.shape)],
        out_specs=[pl.BlockSpec((tm, D_MODEL), row), pl.BlockSpec((tm, D_MODEL), row)],
        out_shape=[jax.ShapeDtypeStruct((t, D_MODEL), F32), jax.ShapeDtypeStruct((t, D_MODEL), BF16)],
        compiler_params=_params(("parallel",)),
        name="ffn1",
    )(x, g, wi, wo, g2)


PACK_ROWS = 128


def _pack_body(wt_ref, mix_ref, rest_ref):
    src = np.cumsum((0,) + IN_SPLITS).tolist()
    qkv, ab, _, zg, zs, xbc, dt, qm, gate = [(src[i], IN_SPLITS[i]) for i in range(len(IN_SPLITS))]
    for ref, dst, (lo, n) in ((mix_ref, MIX_QKV, qkv), (mix_ref, MIX_XBC, xbc),
                              (rest_ref, REST_GATE, gate), (rest_ref, REST_ZG, zg),
                              (rest_ref, REST_ZS, zs), (rest_ref, REST_QM, qm)):
        assert lo % SUBLANES == 0 and n % LANES == 0
        for t in range(n // LANES):
            ref[:, dst + t * LANES:dst + (t + 1) * LANES] = (
                wt_ref[lo + t * LANES:lo + (t + 1) * LANES, :].T.astype(BF16))
    assert SM_A == 0 and SM_DT == 2 * GDN_HEADS
    small = jnp.concatenate(
        [wt_ref[ab[0]:ab[0] + 2 * GDN_HEADS, :], wt_ref[dt[0]:dt[0] + SSM_HEADS, :],
         jnp.zeros((LANES - 2 * GDN_HEADS - SSM_HEADS, wt_ref.shape[1]), F32)], axis=0)
    rest_ref[:, REST_SMALL:] = small.T.astype(BF16)


def _pack_w_in(wt):
    rows = lambda i: (i, 0)
    return pl.pallas_call(
        _pack_body,
        grid=(D_MODEL // PACK_ROWS,),
        in_specs=[pl.BlockSpec((wt.shape[0], PACK_ROWS), lambda i: (0, i))],
        out_specs=[pl.BlockSpec((PACK_ROWS, MIX_COLS), rows), pl.BlockSpec((PACK_ROWS, REST_COLS), rows)],
        out_shape=[jax.ShapeDtypeStruct((D_MODEL, MIX_COLS), BF16),
                   jax.ShapeDtypeStruct((D_MODEL, REST_COLS), BF16)],
        compiler_params=_params(("parallel",)),
        name="pack_w_in",
    )(wt)


PROJ_TM = 1024
PROJ_ROWS = 256
MIX_TN = MIX_COLS // 4
REST_TN = REST_COLS // 3


def _proj_body(u_ref, w_ref, o_ref):
    o_ref[...] = _mm(u_ref[...], w_ref[...])


def _in_proj(u, w, tn, name):
    t = u.shape[0]
    tm = min(PROJ_TM, t)
    cols = w.shape[1]
    return pl.pallas_call(
        _proj_body,
        grid=(cols // tn, t // tm),
        in_specs=[pl.BlockSpec((tm, D_MODEL), lambda j, i: (i, 0)),
                  pl.BlockSpec((D_MODEL, tn), lambda j, i: (0, j))],
        out_specs=pl.BlockSpec((tm, tn), lambda j, i: (i, j)),
        out_shape=jax.ShapeDtypeStruct((t, cols), F32),
        compiler_params=_params(("parallel", "parallel")),
        name=name,
    )(u, w)


def _causal_conv_silu(xa, w, bias):
    assert CONV_W == 4 and CONV_PAD == SUBLANES
    shape = (SUBLANES, LANES)
    sub = lax.broadcasted_iota(jnp.int32, shape, 0)
    shift = lambda prev, cur, k: pltpu.roll(jnp.where(sub >= SUBLANES - k, prev, cur), k, axis=0)
    columns = []
    for c in range(xa.shape[1] // LANES):
        cols = slice(c * LANES, (c + 1) * LANES)
        w0, w1, w2, w3 = [jnp.broadcast_to(w[j:j + 1, cols], shape) for j in range(CONV_W)]
        b = jnp.broadcast_to(bias[:, cols], shape)
        x_prev = xa[0:SUBLANES, cols]
        u_prev = x_prev * w1 + shift(x_prev, x_prev, 1) * w0
        outs = []
        for g in range(1, xa.shape[0] // SUBLANES):
            x = xa[g * SUBLANES:(g + 1) * SUBLANES, cols]
            x1 = shift(x_prev, x, 1)
            u = x * w1 + x1 * w0
            y = (x * w3 + b) + x1 * w2 + shift(u_prev, u, 2)
            outs.append(_silu(y))
            x_prev, u_prev = x, u
        columns.append(jnp.concatenate(outs, axis=0))
    return jnp.concatenate(columns, axis=1)


CONVPROJ_TM = 512
CONVPROJ_STEPS = 3
PROJ_STRIPS = 8


def _strips(width, n):
    tiles = width // LANES
    assert width % LANES == 0 and tiles >= n
    bounds = [LANES * ((tiles * k) // n) for k in range(n + 1)]
    return [slice(bounds[k], bounds[k + 1]) for k in range(n)]


def _proj_conv_body(u_ref, wm_ref, wr_ref, cw_ref, cb_ref, cst_ref, mix_ref, rest_ref, tail_ref,
                    carry, *, tiles_per_seq):
    i = pl.program_id(1)

    @pl.when(i % tiles_per_seq == 0)
    def _():
        carry[0:CONV_PAD, :] = cst_ref[0]

    strips_m = _strips(mix_ref.shape[1], PROJ_STRIPS)
    strips_r = _strips(rest_ref.shape[1], PROJ_STRIPS)
    for r in range(u_ref.shape[0] // PROJ_ROWS):
        rows = slice(r * PROJ_ROWS, (r + 1) * PROJ_ROWS)
        u = u_ref[rows, :]
        for cm, cr in zip(strips_m, strips_r):
            carry[CONV_PAD:, cm] = _mm(u, wm_ref[:, cm])
            rest_ref[rows, cr] = _mm(u, wr_ref[:, cr])
            mix_ref[rows, cm] = _causal_conv_silu(carry.at[:, cm], cw_ref[:, cm], cb_ref[:, cm])
            carry[0:CONV_PAD, cm] = carry[PROJ_ROWS:, cm]
    tail_ref[0] = carry[0:CONV_PAD, :]


def _in_proj_conv(u, w_mix, w_rest, cw, cb, cst8, seq):
    t = u.shape[0]
    tm = min(CONVPROJ_TM, seq)
    assert seq % tm == 0 and tm % PROJ_ROWS == 0
    tps = seq // tm
    tn_m = MIX_COLS // CONVPROJ_STEPS
    tn_r = REST_COLS // CONVPROJ_STEPS
    col = lambda j, i: (0, j)
    tile = lambda j, i: (i, j)
    seq_blk = lambda j, i: (i // tps, 0, j)
    return pl.pallas_call(
        functools.partial(_proj_conv_body, tiles_per_seq=tps),
        grid=(CONVPROJ_STEPS, t // tm),
        in_specs=[pl.BlockSpec((tm, D_MODEL), lambda j, i: (i, 0)),
                  pl.BlockSpec((D_MODEL, tn_m), col), pl.BlockSpec((D_MODEL, tn_r), col),
                  pl.BlockSpec((CONV_W, tn_m), col), pl.BlockSpec((1, tn_m), col),
                  pl.BlockSpec((1, CONV_PAD, tn_m), seq_blk)],
        out_specs=[pl.BlockSpec((tm, tn_m), tile), pl.BlockSpec((tm, tn_r), tile),
                   pl.BlockSpec((1, CONV_PAD, tn_m), seq_blk)],
        out_shape=[jax.ShapeDtypeStruct((t, MIX_COLS), F32), jax.ShapeDtypeStruct((t, REST_COLS), F32),
                   jax.ShapeDtypeStruct((t // seq, CONV_PAD, MIX_COLS), F32)],
        scratch_shapes=[pltpu.VMEM((CONV_PAD + PROJ_ROWS, tn_m), F32)],
        compiler_params=_params(("parallel", "arbitrary")),
        name="in_proj_conv",
    )(u, w_mix, w_rest, cw, cb, cst8)


def _memkv_body(x_ref, g_ref, w_ref, o_ref):
    o_ref[...] = _mm(_rms(x_ref[...], g_ref[...]).astype(BF16), w_ref[...])


MEMKV_TM = 512


def _memkv(mem, g, w):
    t = mem.shape[0]
    tm = min(MEMKV_TM, t)
    row = lambda i: (i, 0)
    return pl.pallas_call(
        _memkv_body,
        grid=(t // tm,),
        in_specs=[pl.BlockSpec((tm, D_MODEL), row), _const_spec(g.shape), _const_spec(w.shape)],
        out_specs=pl.BlockSpec((tm, 2 * MEM_WIDTH), row),
        out_shape=jax.ShapeDtypeStruct((t, 2 * MEM_WIDTH), F32),
        compiler_params=_params(("parallel",)),
        name="memkv",
    )(mem, g, w)


MERGE_TM = 256


def _merge_ffn2_body(h_ref, yg_ref, ys_ref, ym_ref, gate_ref, wb_ref, wout_ref,
                     g_ref, wi_ref, wo_ref, gf_ref, y_ref):
    gates = gate_ref[...]
    merged = (_sigmoid(gates[:, 0:D_MODEL]) * _mm(yg_ref[...], wb_ref[0:GDN_V, :])
              + _sigmoid(gates[:, D_MODEL:2 * D_MODEL])
              * _mm(ys_ref[...], wb_ref[GDN_V:GDN_V + SSM_DINNER, :])
              + _sigmoid(gates[:, 2 * D_MODEL:3 * D_MODEL])
              * _mm(ym_ref[...], wb_ref[GDN_V + SSM_DINNER:, :]))
    h = h_ref[...] + _mm(merged.astype(BF16), wout_ref[...])
    h = _ffn_compute(h, g_ref, wi_ref, wo_ref)
    y_ref[...] = _rms(h, gf_ref[...])


def _merge_ffn2(h, yg, ys, ym, proj, wb, wout, g, wi, wo, gf):
    t = h.shape[0]
    tm = min(MERGE_TM, t)
    row = lambda i: (i, 0)
    gate_blk = REST_GATE // (3 * D_MODEL)
    return pl.pallas_call(
        _merge_ffn2_body,
        grid=(t // tm,),
        in_specs=[pl.BlockSpec((tm, D_MODEL), row), pl.BlockSpec((tm, GDN_V), row),
                  pl.BlockSpec((tm, SSM_DINNER), row), pl.BlockSpec((tm, MEM_WIDTH), row),
                  pl.BlockSpec((tm, 3 * D_MODEL), lambda i: (i, gate_blk)),
                  _const_spec(wb.shape), _const_spec(wout.shape), _const_spec(g.shape),
                  _const_spec(wi.shape), _const_spec(wo.shape), _const_spec(gf.shape)],
        out_specs=pl.BlockSpec((tm, D_MODEL), row),
        out_shape=jax.ShapeDtypeStruct((t, D_MODEL), F32),
        compiler_params=_params(("parallel",)),
        name="merge_ffn2",
    )(h, yg, ys, ym, proj, wb, wout, g, wi, wo, gf)


MEMATTN_TL = 512
MEMATTN_NSEQ = 16


def _memattn_body(q_ref, k_ref, v_ref, y_ref, *, nseq, rows):
    lane = lax.broadcasted_iota(jnp.int32, (rows, MEM_WIDTH), 1)
    masks = [(lane >= hh * MEM_HEAD_DIM) & (lane < (hh + 1) * MEM_HEAD_DIM) for hh in range(MEM_HEADS)]
    seqs = range(nseq)
    q = [q_ref[s * rows:(s + 1) * rows, :] for s in seqs]
    q4 = [jnp.concatenate([jnp.where(m, x, 0.0) for m in masks], axis=0).astype(BF16) for x in q]
    sc = [_mm(q4[s], k_ref[s].astype(BF16)) * (MEM_HEAD_DIM ** -0.5) for s in seqs]
    p = [jnp.exp(x - jnp.max(x, axis=-1, keepdims=True)) for x in sc]
    p = [(x / jnp.sum(x, axis=-1, keepdims=True)).astype(BF16) for x in p]
    o4 = [_mm_nt(p[s], v_ref[s].astype(BF16)) for s in seqs]
    ys = []
    for x in o4:
        y = jnp.where(masks[0], x[0:rows], 0.0)
        for hh in range(1, MEM_HEADS):
            y = y + jnp.where(masks[hh], x[hh * rows:(hh + 1) * rows], 0.0)
        ys.append(y)
    y_ref[...] = jnp.concatenate(ys, axis=0).astype(BF16)


def _memattn(proj, k, v, bsz, seq):
    if seq >= MEMATTN_TL:
        nseq, rows, nt = 1, MEMATTN_TL, seq // MEMATTN_TL
    else:
        nseq, rows, nt = MEMATTN_NSEQ, seq, 1
    assert bsz % nseq == 0 and seq % rows == 0
    tl = nseq * rows
    qm_blk = REST_QM // MEM_WIDTH
    kv_spec = pl.BlockSpec((nseq, MEM_WIDTH, N_MEM), lambda b, i: (b, 0, 0))
    return pl.pallas_call(
        functools.partial(_memattn_body, nseq=nseq, rows=rows),
        grid=(bsz // nseq, nt),
        in_specs=[pl.BlockSpec((tl, MEM_WIDTH), lambda b, i: (b * nt + i, qm_blk)), kv_spec, kv_spec],
        out_specs=pl.BlockSpec((tl, MEM_WIDTH), lambda b, i: (b * nt + i, 0)),
        out_shape=jax.ShapeDtypeStruct((bsz * seq, MEM_WIDTH), BF16),
        compiler_params=_params(("parallel", "arbitrary")),
        name="memattn",
    )(proj, k, v)


SEQ_TL = 256
LONG_NSEQ = 2
SHORT_NSEQ = 8
CONV_PAD = SUBLANES
MIN_MXU_ROWS = 16

Geom = collections.namedtuple("Geom", ["nseq", "nch", "vrows"])


def _geom(bsz, seq):
    if seq >= CHUNK:
        assert seq % SEQ_TL == 0
        nseq = LONG_NSEQ if bsz % LONG_NSEQ == 0 else 1
        return Geom(nseq, SEQ_TL // CHUNK, CHUNK), seq // SEQ_TL
    assert bsz % SHORT_NSEQ == 0 and seq % SUBLANES == 0 and (SHORT_NSEQ * seq) % CHUNK == 0
    return Geom(SHORT_NSEQ, 1, seq), 1


class _TokenView:
    def __init__(self, ref):
        self.ref = ref

    def _locate(self, rows):
        per = self.ref.shape[1]
        s = rows.start // per
        assert rows.stop <= (s + 1) * per
        return s, slice(rows.start - s * per, rows.stop - s * per)

    def __getitem__(self, idx):
        rows, cols = idx
        if len(self.ref.shape) == 2:
            return self.ref[rows, cols]
        if rows == slice(None):
            v = self.ref[:, :, cols]
            return v.reshape(v.shape[0] * v.shape[1], v.shape[2])
        s, r = self._locate(rows)
        return self.ref[s, r, cols]

    def __setitem__(self, idx, value):
        rows, cols = idx
        if len(self.ref.shape) == 2:
            self.ref[rows, cols] = value
        elif rows == slice(None):
            nseq, per = self.ref.shape[0], self.ref.shape[1]
            self.ref[:, :, cols] = value.reshape(nseq, per, value.shape[1])
        else:
            s, r = self._locate(rows)
            self.ref[s, r, cols] = value


def _short_conv_silu(x_ref, cst_ref, cw_ref, cb_ref, act, geom):
    assert geom.nch == 1 and geom.vrows < CHUNK and CONV_PAD == SUBLANES
    vr = geom.vrows
    sub = lax.broadcasted_iota(jnp.int32, (SUBLANES, LANES), 0)
    for cb in range(x_ref.shape[1] // LANES):
        cols = slice(cb * LANES, (cb + 1) * LANES)
        planes = [cst_ref[j, :, cols] for j in range(CONV_W - 1)]
        for s in range(geom.nseq):
            before = jnp.zeros((SUBLANES, LANES), F32)
            for j, plane in enumerate(planes):
                before = jnp.where(sub == SUBLANES - (CONV_W - 1) + j,
                                   jnp.broadcast_to(plane[s:s + 1, :], (SUBLANES, LANES)), before)
            xa = jnp.concatenate([before, x_ref[s * vr:(s + 1) * vr, cols]], axis=0)
            act[s * vr:(s + 1) * vr, cols] = _causal_conv_silu(xa, cw_ref[:, cols], cb_ref[:, cols])


def _pair_masks(vr):
    row = lax.broadcasted_iota(jnp.int32, (CHUNK, LANES), 0)
    lane = lax.broadcasted_iota(jnp.int32, (CHUNK, LANES), 1)
    col = jnp.where(lane < CHUNK, lane, lane - CHUNK)
    same = (row // vr) == (col // vr)
    return row, lane, col, same


def _block_diag(x, left):
    zero = jnp.zeros_like(x)
    return jnp.concatenate([jnp.where(left, x, zero), jnp.where(left, zero, x)], axis=0)


def _chunk_cumsums(gates, nprob, vr):
    r = lax.broadcasted_iota(jnp.int32, (CHUNK, CHUNK), 0)
    c = lax.broadcasted_iota(jnp.int32, (CHUNK, CHUNK), 1)
    same = (r // vr) == (c // vr)
    sums = jnp.concatenate([jnp.where(same & (r >= c), 1.0, 0.0), jnp.where(same, 1.0, 0.0)],
                           axis=0).astype(BF16)
    sums3 = jnp.concatenate([sums, sums, sums], axis=1)
    pieces = [jnp.concatenate(_split3(gates[pb * CHUNK:(pb + 1) * CHUNK, 0:LANES]), axis=0)
              for pb in range(nprob)]
    out = _mm(sums3, jnp.concatenate(pieces, axis=1))
    cols = lambda pb: slice(pb * LANES, (pb + 1) * LANES)
    return ([out[0:CHUNK, cols(pb)] for pb in range(nprob)],
            [out[CHUNK:, cols(pb)] for pb in range(nprob)])


def _rows_to_columns(e):
    pieces = jnp.concatenate(_split3(e), axis=0)
    pad = jnp.zeros((MIN_MXU_ROWS - pieces.shape[0], e.shape[1]), BF16)
    row = lax.broadcasted_iota(jnp.int32, (MIN_MXU_ROWS, LANES), 0)
    ones = jnp.where(row < pieces.shape[0], 1.0, 0.0).astype(BF16)
    return _mm_tn(jnp.concatenate([pieces, pad], axis=0), ones)


def _pad_rows(x):
    if x.shape[0] >= MIN_MXU_ROWS:
        return x
    return jnp.concatenate([x, jnp.zeros((MIN_MXU_ROWS - x.shape[0], x.shape[1]), x.dtype)], axis=0)


def _row_form(gc):
    return jnp.concatenate([gc, pltpu.roll(gc, LANES - 1, axis=1)], axis=0).T


def _gdn_body(*refs, geom):
    if geom.vrows == CHUNK:
        (x_ref, sm_ref, z_ref, s0_ref, par_ref, gn_ref, selp_ref, y_ref, sout_ref,
         gates, s_scr, u_scr, w_scr, qd_scr, kd_scr, qk_scr, egl_scr, o_scr) = refs
        act = _TokenView(x_ref)
    else:
        (x_ref, sm_ref, z_ref, cst_ref, cw_ref, cb_ref, s0_ref, par_ref, gn_ref, selp_ref, y_ref,
         sout_ref, act, gates, s_scr, u_scr, w_scr, qd_scr, kd_scr, qk_scr, egl_scr, o_scr) = refs
        _short_conv_silu(x_ref, cst_ref, cw_ref, cb_ref, act, geom)
    tile = pl.program_id(1)
    first = tile == 0
    vr = geom.vrows
    nsub = CHUNK // vr
    nprob = geom.nseq * geom.nch // nsub

    @pl.when(first)
    def _():
        s_scr[...] = s0_ref[...]

    sm = _TokenView(sm_ref)[:, :]
    lane_g = lax.broadcasted_iota(jnp.int32, sm.shape, 1)
    head_lane = lane_g < GDN_HEADS
    gates[:, 0:LANES] = jnp.where(
        head_lane, -jnp.exp(par_ref[0:1, :]) * _softplus(sm + par_ref[1:2, :]), 0.0)
    gates[:, LANES:] = jnp.where(head_lane, pltpu.roll(_sigmoid(sm), LANES - SM_B, axis=1), 0.0)

    row, lane, col, same = _pair_masks(vr)
    left = lane < CHUNK
    incl = same & (row >= col)
    strict = same & (row > col)
    eye = jnp.where(row == col, 1.0, 0.0).astype(F32)
    zeros_h = jnp.zeros((CHUNK, LANES), F32)
    hcols = lambda h: slice(h * LANES, (h + 1) * LANES)
    prows = lambda pb, n=CHUNK: slice(pb * n, (pb + 1) * n)

    gcs, gls = _chunk_cumsums(gates, nprob, vr)
    lhs, xts = [], []
    for pb in range(nprob):
        gc = gcs[pb]
        quantities = jnp.concatenate(
            [gc, gates[prows(pb), LANES:2 * LANES], jnp.exp(gc), jnp.exp(gls[pb] - gc)], axis=0)
        lhs.append(_split2_lanes(quantities))
        xts.append(_row_form(gc))

    def l2n(t):
        return t * lax.rsqrt(jnp.sum(t * t, axis=-1, keepdims=True) + EPS)

    def stacked(e, qi):
        return jnp.concatenate([e[qi * CHUNK:(qi + 1) * CHUNK, 0:LANES],
                                e[qi * CHUNK:(qi + 1) * CHUNK, LANES:]], axis=0)

    def phase1(pbs):
        items = [(pb, p) for pb in pbs for p in range(GDN_PAIRS)]
        exp_ = [_mm(lhs[pb], selp_ref[p]) for pb, p in items]
        decay_l, kq_l, rhs_l, qd_l, kd_l = [], [], [], [], []
        for (pb, p), e in zip(items, exp_):
            ha, hb = 2 * p, 2 * p + 1
            qa, qb = [l2n(act[prows(pb), hcols(h)]) * (GDN_DK ** -0.5) for h in (ha, hb)]
            ka, kb = [l2n(act[prows(pb), hcols(GDN_HEADS + h)]) for h in (ha, hb)]
            va, vb = [act[prows(pb), hcols(2 * GDN_HEADS + h)] for h in (ha, hb)]
            beta2, eg2, kdec2 = stacked(e, 1), stacked(e, 2), stacked(e, 3)
            k2 = jnp.concatenate([ka, kb], axis=0)
            kbeta2 = k2 * beta2
            gci = jnp.where(left, e[0:CHUNK, 0:LANES], e[0:CHUNK, LANES:])
            gcj = jnp.broadcast_to(xts[pb][ha:ha + 1, :], (CHUNK, LANES))
            decay_l.append(jnp.where(incl, jnp.exp(jnp.where(incl, gci - gcj, 0.0)), 0.0))
            k_bd = jnp.concatenate([jnp.concatenate([ka, zeros_h], axis=1),
                                    jnp.concatenate([zeros_h, kb], axis=1)], axis=0).astype(BF16)
            kbq = jnp.concatenate([jnp.concatenate([kbeta2[0:CHUNK], kbeta2[CHUNK:]], axis=1),
                                   jnp.concatenate([qa, qb], axis=1)], axis=0).astype(BF16)
            kq_l.append(_mm_nt(kbq, k_bd))
            rhs_l.append(jnp.concatenate(
                [jnp.concatenate([va, vb], axis=0) * beta2, kbeta2 * eg2], axis=1).astype(BF16))
            qd_l.append(jnp.concatenate([qa, qb], axis=0) * eg2)
            kd_l.append(k2 * kdec2)
            for sub in range(nsub):
                last = 2 * CHUNK + (sub + 1) * vr - 1
                slot = prows(pb * nsub + sub, SUBLANES)
                egl_scr[slot, hcols(ha)] = jnp.broadcast_to(e[last:last + 1, 0:LANES], (SUBLANES, LANES))
                egl_scr[slot, hcols(hb)] = jnp.broadcast_to(e[last:last + 1, LANES:], (SUBLANES, LANES))

        n_pow = [-jnp.where(strict, kq[0:CHUNK] * d, 0.0) for kq, d in zip(kq_l, decay_l)]
        t_inv = [eye + n for n in n_pow]
        for _ in range(int(np.ceil(np.log2(vr))) - 1):
            n_pow = [_mm(n.astype(BF16), _block_diag(n, left).astype(BF16)) for n in n_pow]
            t_inv = [t + _mm(t.astype(BF16), _block_diag(n, left).astype(BF16))
                     for t, n in zip(t_inv, n_pow)]

        uw_l = [_mm(_block_diag(t, left).astype(BF16), rhs) for t, rhs in zip(t_inv, rhs_l)]
        for (pb, p), uw, kq, d, qd, kd in zip(items, uw_l, kq_l, decay_l, qd_l, kd_l):
            qk_scr[prows(pb, 2 * CHUNK), hcols(p)] = _block_diag(kq[CHUNK:] * d, left).astype(BF16)
            for idx, h in enumerate((2 * p, 2 * p + 1)):
                sl = slice(idx * CHUNK, (idx + 1) * CHUNK)
                u_scr[prows(pb), hcols(h)] = uw[sl, 0:LANES]
                w_scr[prows(pb), hcols(h)] = uw[sl, LANES:]
                qd_scr[prows(pb), hcols(h)] = qd[sl]
                kd_scr[prows(pb), hcols(h)] = kd[sl]

    def phase2(c):
        sh = [(s, h) for s in range(geom.nseq) for h in range(GDN_HEADS)]
        trows = lambda s: prows(s * geom.nch + c, vr)
        wqs = [_mm(jnp.concatenate([w_scr[trows(s), hcols(h)], qd_scr[trows(s), hcols(h)]],
                                   axis=0).astype(BF16), s_scr[s, h].astype(BF16)) for s, h in sh]
        vnew = [u_scr[trows(s), hcols(h)] - w[0:vr] for (s, h), w in zip(sh, wqs)]
        for (s, h), vn in zip(sh, vnew):
            slot = (s * geom.nch + c) * SUBLANES
            s_scr[s, h] = (s_scr[s, h] * egl_scr[slot:slot + 1, hcols(h)]
                           + _mm_tn(_pad_rows(kd_scr[trows(s), hcols(h)]).astype(BF16),
                                    _pad_rows(vn).astype(BF16)))
        for pb in range(c * nsub, geom.nseq * geom.nch, geom.nch * nsub):
            subs = [pb // geom.nch + j for j in range(nsub)]
            for p in range(GDN_PAIRS):
                heads = (2 * p, 2 * p + 1)
                vn2 = jnp.concatenate([vnew[s * GDN_HEADS + h] for h in heads for s in subs],
                                      axis=0).astype(BF16)
                intra = _mm(qk_scr[prows((pb // nsub), 2 * CHUNK), hcols(p)], vn2)
                for idx, h in enumerate(heads):
                    for j, s in enumerate(subs):
                        r0 = idx * CHUNK + j * vr
                        o = wqs[s * GDN_HEADS + h][vr:] + intra[r0:r0 + vr]
                        o_scr[trows(s), hcols(h)] = o

    phase1(range(nprob))
    for c in range(geom.nch):
        phase2(c)

    for h in range(GDN_HEADS):
        y = _rms(o_scr[:, hcols(h)], gn_ref[...]) * _silu(_TokenView(z_ref)[:, hcols(h)])
        _TokenView(y_ref)[:, hcols(h)] = y.astype(BF16)

    @pl.when(tile == pl.num_programs(1) - 1)
    def _():
        sout_ref[...] = s_scr[...]


def _conv_operands(conv, geom, width, col_blk):
    if conv is None:
        return [], [], []
    state, cw, cb = conv
    specs = [pl.BlockSpec((CONV_W - 1, geom.nseq, width), lambda b, i: (0, b, 0)),
             pl.BlockSpec((CONV_W, width), lambda b, i: (0, col_blk)),
             pl.BlockSpec((1, width), lambda b, i: (0, col_blk))]
    return [state, cw, cb], specs, [pltpu.VMEM((geom.nseq * geom.vrows, width), F32)]


def _token_layout(geom, bsz, seq, nt):
    tls = geom.nch * geom.vrows
    if geom.vrows == CHUNK:
        view = lambda a: a.reshape(bsz, seq, a.shape[-1])
        spec = lambda width, blk: pl.BlockSpec((geom.nseq, tls, width), lambda b, i: (b, i, blk))
        return view, spec, lambda width: (bsz, seq, width)
    spec = lambda width, blk: pl.BlockSpec((geom.nseq * tls, width), lambda b, i: (b * nt + i, blk))
    return (lambda a: a), spec, lambda width: (bsz * seq, width)


def _gdn(mix, rest, conv, s0, par, gn, selp, bsz, seq):
    geom, nt = _geom(bsz, seq)
    tl = geom.nseq * geom.nch * geom.vrows
    nprob = tl // CHUNK
    view, tok, out_shape = _token_layout(geom, bsz, seq, nt)
    st_spec = pl.BlockSpec((geom.nseq, GDN_HEADS, GDN_DK, GDN_DV), lambda b, i: (b, 0, 0, 0))
    conv_ops, conv_specs, conv_scratch = _conv_operands(conv, geom, GDN_CONV_CH, MIX_QKV // GDN_CONV_CH)
    y, s_new = pl.pallas_call(
        functools.partial(_gdn_body, geom=geom),
        grid=(bsz // geom.nseq, nt),
        in_specs=[tok(GDN_CONV_CH, MIX_QKV // GDN_CONV_CH), tok(LANES, REST_SMALL // LANES),
                  tok(GDN_V, REST_ZG // GDN_V)] + conv_specs + [
                  st_spec, _const_spec(par.shape), _const_spec(gn.shape), _const_spec(selp.shape)],
        out_specs=[tok(GDN_V, 0), st_spec],
        out_shape=[jax.ShapeDtypeStruct(out_shape(GDN_V), BF16),
                   jax.ShapeDtypeStruct((bsz, GDN_HEADS, GDN_DK, GDN_DV), F32)],
        scratch_shapes=conv_scratch + [
            pltpu.VMEM((tl, 2 * LANES), F32),
            pltpu.VMEM((geom.nseq, GDN_HEADS, GDN_DK, GDN_DV), F32),
            pltpu.VMEM((tl, GDN_V), F32),
            pltpu.VMEM((tl, GDN_V), F32),
            pltpu.VMEM((tl, GDN_V), F32),
            pltpu.VMEM((tl, GDN_V), F32),
            pltpu.VMEM((nprob * 2 * CHUNK, GDN_PAIRS * LANES), BF16),
            pltpu.VMEM((geom.nseq * geom.nch * SUBLANES, GDN_V), F32),
            pltpu.VMEM((tl, GDN_V), F32)],
        compiler_params=_params(("parallel", "arbitrary")),
        name="gdn",
    )(view(mix), view(rest), view(rest), *conv_ops, s0, par, gn, selp)
    return y.reshape(bsz * seq, GDN_V), s_new


def _ssd_body(*refs, geom):
    if geom.vrows == CHUNK:
        (x_ref, sm_ref, z_ref, h0_ref, par_ref, dch_ref, nw_ref, seld_ref, y_ref, hout_ref,
         gates, h_scr, o_scr) = refs
        act = _TokenView(x_ref)
        tile = pl.program_id(1)

        @pl.when(tile == 0)
        def _():
            for s in range(geom.nseq):
                for p in range(SSM_PAIRS):
                    h_scr[s, p] = h0_ref[s, p].T
    else:
        (x_ref, sm_ref, z_ref, cst_ref, cw_ref, cb_ref, h0_ref, par_ref, dch_ref, nw_ref, seld_ref,
         y_ref, hout_ref, act, gates, o_scr) = refs
        _short_conv_silu(x_ref, cst_ref, cw_ref, cb_ref, act, geom)
    vr = geom.vrows
    nsub = CHUNK // vr
    nprob = geom.nseq * geom.nch // nsub

    sm = _TokenView(sm_ref)[:, :]
    lane_g = lax.broadcasted_iota(jnp.int32, sm.shape, 1)
    is_dt = (lane_g >= SM_DT) & (lane_g < SM_DT + SSM_HEADS)
    dt = jnp.where(is_dt, _softplus(sm + par_ref[1:2, :]), 0.0)
    gates[:, 0:LANES] = dt * jnp.where(is_dt, -jnp.exp(par_ref[0:1, :]), 0.0)
    gates[:, LANES:] = dt

    row, lane, col, same = _pair_masks(vr)
    left = lane < CHUNK
    incl = same & (row >= col)
    pcols = lambda p: slice(p * LANES, (p + 1) * LANES)
    prows = lambda pb, n=CHUNK: slice(pb * n, (pb + 1) * n)
    quarter = lambda e, qi, j: e[qi * CHUNK:(qi + 1) * CHUNK, j * LANES:(j + 1) * LANES]
    srows = lambda a, j: a[j * vr:(j + 1) * vr]
    mxu_rows = lambda a: _pad_rows(a).astype(BF16)
    b_off = SSM_DINNER
    c_off = SSM_DINNER + SSM_GROUPS * SSM_DSTATE

    gcs, gls = _chunk_cumsums(gates, nprob, vr)
    for pb in range(nprob):
        seqs = [(pb * nsub + j) // geom.nch for j in range(nsub)]
        gc = gcs[pb]
        dt_c = gates[prows(pb), LANES:2 * LANES]
        lhs = _split2_lanes(jnp.concatenate(
            [gc, dt_c, jnp.exp(gc), jnp.exp(gls[pb] - gc) * dt_c], axis=0))
        xt = _row_form(gc)
        exp_ = [_mm(lhs, seld_ref[d]) for d in range(SSM_PAIRS // 2)]

        b_f = [act[prows(pb), b_off + g * SSM_DSTATE:b_off + (g + 1) * SSM_DSTATE]
               for g in range(SSM_GROUPS)]
        c_f = [act[prows(pb), c_off + g * SSM_DSTATE:c_off + (g + 1) * SSM_DSTATE]
               for g in range(SSM_GROUPS)]
        b_bf = [b.astype(BF16) for b in b_f]
        c_bf = [c.astype(BF16) for c in c_f]
        cb2 = [_mm_nt(c, jnp.concatenate([b, b], axis=0)) for b, c in zip(b_bf, c_bf)]

        pairs = range(SSM_PAIRS)
        grp = lambda p: p // SSM_GROUP_PAIRS
        x_l = [act[prows(pb), pcols(p)] for p in pairs]
        gc_l = [quarter(exp_[p // 2], 0, p % 2) for p in pairs]
        eg_l = [quarter(exp_[p // 2], 2, p % 2) for p in pairs]
        xdt_l = [x * quarter(exp_[p // 2], 1, p % 2) for p, x in zip(pairs, x_l)]
        xw_l = [x * quarter(exp_[p // 2], 3, p % 2) for p, x in zip(pairs, x_l)]
        m_l = []
        for p in pairs:
            r = SM_DT + 2 * p
            gcj = jnp.broadcast_to(xt[r:r + 1, :], (CHUNK, LANES))
            decay = jnp.where(incl, jnp.exp(jnp.where(incl, gc_l[p] - gcj, 0.0)), 0.0)
            m_l.append((cb2[grp(p)] * decay).astype(BF16))
        intra = [_mm(m, _block_diag(xdt, left).astype(BF16)) for m, xdt in zip(m_l, xdt_l)]
        last_row = lambda a, j: a[(j + 1) * vr - 1:(j + 1) * vr, :]
        if geom.vrows == CHUNK:
            dstate = [_mm_tn(b_bf[grp(p)], xw_l[p].astype(BF16)) for p in pairs]
            inter = [_mm(c_bf[grp(p)], h_scr[seqs[0], p].astype(BF16)) for p in pairs]
            for p in pairs:
                h_scr[seqs[0], p] = h_scr[seqs[0], p] * last_row(eg_l[p], 0) + dstate[p]
        else:
            gp = SSM_GROUP_PAIRS
            jg = [(j, g) for j in range(nsub) for g in range(SSM_GROUPS)]
            gslice = lambda g: slice(g * gp, (g + 1) * gp)
            lanes = lambda a_l, g, f: jnp.concatenate([f(a) for a in a_l[gslice(g)]], axis=1)
            inter_g = []
            for j, g in jg:
                ds = _mm_tn(mxu_rows(lanes(xw_l, g, lambda a: srows(a, j))), mxu_rows(srows(b_f[g], j)))
                sc = _rows_to_columns(lanes(eg_l, g, lambda a: last_row(a, j)))
                h = h0_ref[seqs[j], gslice(g)].reshape(gp * LANES, SSM_DSTATE)
                inter_g.append(_mm_nt(mxu_rows(srows(c_f[g], j)), h.astype(BF16))[0:vr])
                hout_ref[seqs[j], gslice(g)] = (h * sc + ds).reshape(gp, LANES, SSM_DSTATE)
            inter = [jnp.concatenate(
                [inter_g[j * SSM_GROUPS + grp(p)][:, (p % gp) * LANES:(p % gp + 1) * LANES]
                 for j in range(nsub)], axis=0) for p in pairs]

        for g in range(SSM_GROUPS):
            ys = []
            ssq = None
            for p in range(g * SSM_GROUP_PAIRS, (g + 1) * SSM_GROUP_PAIRS):
                y = intra[p] + inter[p] * eg_l[p] + dch_ref[:, pcols(p)] * x_l[p]
                y = y * _silu(_TokenView(z_ref)[prows(pb), pcols(p)])
                ys.append(y)
                sq = jnp.sum(y * y, axis=-1, keepdims=True)
                ssq = sq if ssq is None else ssq + sq
            inv = lax.rsqrt(ssq * (1.0 / SSM_GROUP_CH) + EPS)
            for pp, y in enumerate(ys):
                o_scr[prows(pb), pcols(g * SSM_GROUP_PAIRS + pp)] = y * inv

    _TokenView(y_ref)[:, slice(None)] = (o_scr[...] * nw_ref[...]).astype(BF16)

    if geom.vrows == CHUNK:
        @pl.when(tile == pl.num_programs(1) - 1)
        def _():
            for s in range(geom.nseq):
                for p in range(SSM_PAIRS):
                    hout_ref[s, p] = h_scr[s, p].T


def _ssd(mix, rest, conv, h0, par, dch, nw, seld, bsz, seq):
    geom, nt = _geom(bsz, seq)
    tl = geom.nseq * geom.nch * geom.vrows
    view, tok, out_shape = _token_layout(geom, bsz, seq, nt)
    st_shape = (geom.nseq, SSM_PAIRS, 2 * SSM_HEADDIM, SSM_DSTATE)
    st_spec = pl.BlockSpec(st_shape, lambda b, i: (b, 0, 0, 0))
    conv_ops, conv_specs, conv_scratch = _conv_operands(conv, geom, SSM_CONV_CH, MIX_XBC // SSM_CONV_CH)
    y, h_new = pl.pallas_call(
        functools.partial(_ssd_body, geom=geom),
        grid=(bsz // geom.nseq, nt),
        in_specs=[tok(SSM_CONV_CH, MIX_XBC // SSM_CONV_CH), tok(LANES, REST_SMALL // LANES),
                  tok(SSM_DINNER, REST_ZS // SSM_DINNER)] + conv_specs + [
                  st_spec, _const_spec(par.shape), _const_spec(dch.shape), _const_spec(nw.shape),
                  _const_spec(seld.shape)],
        out_specs=[tok(SSM_DINNER, 0), st_spec],
        out_shape=[jax.ShapeDtypeStruct(out_shape(SSM_DINNER), BF16),
                   jax.ShapeDtypeStruct((bsz,) + st_shape[1:], F32)],
        scratch_shapes=conv_scratch + [pltpu.VMEM((tl, 2 * LANES), F32)] + (
            [pltpu.VMEM((geom.nseq, SSM_PAIRS, SSM_DSTATE, 2 * SSM_HEADDIM), F32)]
            if conv is None else []) + [
            pltpu.VMEM((tl, SSM_DINNER), F32)],
        compiler_params=_params(("parallel", "arbitrary")),
        name="ssd",
    )(view(mix), view(rest), view(rest), *conv_ops, h0, par, dch, nw, seld)
    return y.reshape(bsz * seq, SSM_DINNER), h_new


def _pad_lanes(v, offset):
    out = jnp.zeros((LANES,), F32)
    return out.at[offset:offset + v.shape[0]].set(v.astype(F32))


def _pair_selection(first_lane, lanes_per_head, n_mats):
    heads_per_mat = 2 * LANES // lanes_per_head
    sel = np.zeros((n_mats, 2 * LANES, 2 * LANES), np.float32)
    for m in range(n_mats):
        for j in range(heads_per_mat):
            src = first_lane + m * heads_per_mat + j
            sel[m, src, j * lanes_per_head:(j + 1) * lanes_per_head] = 1.0
            sel[m, LANES + src, j * lanes_per_head:(j + 1) * lanes_per_head] = 1.0
    return jnp.asarray(sel, BF16)


def _pad_conv_state(st):
    bsz, _, ch = st.shape
    return jnp.concatenate([jnp.zeros((bsz, CONV_PAD - (CONV_W - 1), ch), F32), st.astype(F32)], axis=1)


def _layer(x, mem_k, mem_v, s_gdn, c_gdn, s_ssm, c_ssm, w):
    bsz, seq, _ = x.shape
    t = bsz * seq
    xf = x.reshape(t, D_MODEL)
    h, u = _ffn1(xf, w["norm_ff1"], w["w_ff1_in"], w["w_ff1_out"], w["norm_mix"])
    if seq >= CHUNK:
        cst8 = jnp.concatenate([_pad_conv_state(c_gdn), _pad_conv_state(c_ssm)], axis=2)
        mix, rest, tail = _in_proj_conv(u, w["w_mix"], w["w_rest"], w["conv_w"], w["conv_b"], cst8, seq)
        conv_gdn = conv_ssm = None
    else:
        rest = _in_proj(u, w["w_rest"], REST_TN, "in_proj_rest")
        mix = _in_proj(u, w["w_mix"], MIX_TN, "in_proj_mix")
        tail = mix.reshape(bsz, seq, MIX_COLS)[:, seq - CONV_PAD:, :]
        taps_first = lambda c: jnp.swapaxes(c.astype(F32), 0, 1)
        conv_gdn = (taps_first(c_gdn), w["conv_w"], w["conv_b"])
        conv_ssm = (taps_first(c_ssm), w["conv_w"], w["conv_b"])

    yg, s_gdn_new = _gdn(mix, rest, conv_gdn, s_gdn.astype(F32), w["gdn_par"], w["gdn_norm"],
                         w["sel_gdn"], bsz, seq)
    h0 = s_ssm.astype(F32).reshape(bsz, SSM_PAIRS, 2 * SSM_HEADDIM, SSM_DSTATE)
    ys, s_ssm_new = _ssd(mix, rest, conv_ssm, h0, w["ssm_par"], w["ssm_d_ch"], w["ssm_norm"],
                         w["sel_ssm"], bsz, seq)
    transposed = lambda m: jnp.swapaxes(m.reshape(bsz, N_MEM, MEM_WIDTH), 1, 2)
    ym = _memattn(rest, transposed(mem_k), transposed(mem_v), bsz, seq)
    y = _merge_ffn2(h, yg, ys, ym, rest, w["w_branch"], w["w_out"], w["norm_ff2"],
                    w["w_ff2_in"], w["w_ff2_out"], w["norm_final"])

    c_gdn_new = tail[:, CONV_PAD - (CONV_W - 1):, MIX_QKV:MIX_QKV + GDN_CONV_CH]
    c_ssm_new = tail[:, CONV_PAD - (CONV_W - 1):, MIX_XBC:MIX_XBC + SSM_CONV_CH]
    return (y.reshape(bsz, seq, D_MODEL), s_gdn_new, c_gdn_new,
            s_ssm_new.reshape(bsz, SSM_HEADS, SSM_HEADDIM, SSM_DSTATE), c_ssm_new)


def _row(v):
    return v.astype(F32).reshape(1, -1)


def _prep_weights(l, norm_ff1, w_ff1_in, w_ff1_out, norm_mix, w_in, gdn_conv_w, gdn_a_log,
                  gdn_dt_bias, gdn_norm, ssm_conv_w, ssm_conv_b, ssm_a_log, ssm_dt_bias, ssm_d,
                  ssm_norm, w_branch, w_out, norm_ff2, w_ff2_in, w_ff2_out, norm_final):
    row = _row

    w_mix, w_rest = _pack_w_in(jnp.swapaxes(w_in[l], 0, 1))

    def two_rows(a, b, offset):
        par = jnp.zeros((SUBLANES, LANES), F32)
        return par.at[0].set(_pad_lanes(a, offset)).at[1].set(_pad_lanes(b, offset))

    return {
        "norm_ff1": row(norm_ff1[l]), "w_ff1_in": w_ff1_in[l].astype(BF16),
        "w_ff1_out": w_ff1_out[l].astype(BF16), "norm_mix": row(norm_mix[l]),
        "w_mix": w_mix, "w_rest": w_rest,
        "conv_w": jnp.concatenate([gdn_conv_w[l], ssm_conv_w[l]], axis=1).astype(F32),
        "conv_b": jnp.concatenate([jnp.zeros((1, GDN_CONV_CH), F32), row(ssm_conv_b[l])], axis=1),
        "gdn_par": two_rows(gdn_a_log[l], gdn_dt_bias[l], SM_A),
        "gdn_norm": row(gdn_norm[l]),
        "sel_gdn": _pair_selection(0, LANES, GDN_PAIRS),
        "ssm_par": two_rows(ssm_a_log[l], ssm_dt_bias[l], SM_DT),
        "ssm_d_ch": jnp.repeat(ssm_d[l].astype(F32), SSM_HEADDIM).reshape(1, -1),
        "ssm_norm": row(ssm_norm[l]),
        "sel_ssm": _pair_selection(SM_DT, SSM_HEADDIM, SSM_PAIRS // 2),
        "w_branch": w_branch[l].astype(BF16), "w_out": w_out[l].astype(BF16),
        "norm_ff2": row(norm_ff2[l]), "w_ff2_in": w_ff2_in[l].astype(BF16),
        "w_ff2_out": w_ff2_out[l].astype(BF16), "norm_final": row(norm_final),
    }


def kernel(x_prompt, x_sample, mem_prompt, state_gdn, state_gdn_conv, state_ssm, state_ssm_conv,
           cache_mem_k, cache_mem_v, norm_ff1, w_ff1_in, w_ff1_out, norm_mix, w_in,
           gdn_conv_w, gdn_a_log, gdn_dt_bias, gdn_norm, ssm_conv_w, ssm_conv_b, ssm_a_log,
           ssm_dt_bias, ssm_d, ssm_norm, norm_mem, w_mem_kv, w_branch, w_out,
           norm_ff2, w_ff2_in, w_ff2_out, norm_final):
    assert w_in.shape[0] == 1, "the kernels implement the single-layer configuration"
    l = 0
    bp = x_prompt.shape[0]
    w = _prep_weights(l, norm_ff1, w_ff1_in, w_ff1_out, norm_mix, w_in, gdn_conv_w, gdn_a_log,
                      gdn_dt_bias, gdn_norm, ssm_conv_w, ssm_conv_b, ssm_a_log, ssm_dt_bias, ssm_d,
                      ssm_norm, w_branch, w_out, norm_ff2, w_ff2_in, w_ff2_out, norm_final)

    n_mem_tok = mem_prompt.shape[0] * mem_prompt.shape[1]
    kv = _memkv(mem_prompt.reshape(n_mem_tok, D_MODEL), _row(norm_mem[l]), w_mem_kv[l].astype(BF16))
    mk = kv[:, :MEM_WIDTH].reshape(bp, N_MEM, MEM_HEADS, MEM_HEAD_DIM)
    mv = kv[:, MEM_WIDTH:].reshape(bp, N_MEM, MEM_HEADS, MEM_HEAD_DIM)
    dtp = x_prompt.dtype
    yp, sgp, cgp, ssp, csp = _layer(
        x_prompt, mk, mv,
        jnp.zeros((bp, GDN_HEADS, GDN_DK, GDN_DV), dtp), jnp.zeros((bp, CONV_W - 1, GDN_CONV_CH), dtp),
        jnp.zeros((bp, SSM_HEADS, SSM_HEADDIM, SSM_DSTATE), dtp),
        jnp.zeros((bp, CONV_W - 1, SSM_CONV_CH), dtp), w)
    ys_, sgs, cgs, sss, css = _layer(
        x_sample, cache_mem_k[l], cache_mem_v[l], state_gdn[l], state_gdn_conv[l],
        state_ssm[l], state_ssm_conv[l], w)

    lead = lambda a: a[None]
    return (yp, ys_, lead(sgp), lead(cgp), lead(ssp), lead(csp), lead(mk), lead(mv),
            lead(sgs), lead(cgs), lead(sss), lead(css))
```

```python
import collections
import functools

import numpy as np
import jax
import jax.numpy as jnp
from jax import lax
from jax.experimental import pallas as pl
from jax.experimental.pallas import tpu as pltpu

F32 = jnp.float32
BF16 = jnp.bfloat16
EPS = 1e-6

D_MODEL = 1024
FFN_DIM = 2816
CONV_W = 4
CHUNK = 64
LANES = 128
SUBLANES = 8

GDN_HEADS = 8
GDN_PAIRS = GDN_HEADS // 2
GDN_DK = 128
GDN_DV = 128
GDN_QK = GDN_HEADS * GDN_DK
GDN_V = GDN_HEADS * GDN_DV
GDN_CONV_CH = 2 * GDN_QK + GDN_V

SSM_DINNER = 2048
SSM_HEADDIM = 64
SSM_HEADS = 32
SSM_GROUPS = 4
SSM_DSTATE = 128
SSM_CONV_CH = SSM_DINNER + 2 * SSM_GROUPS * SSM_DSTATE
SSM_PAIRS = SSM_HEADS // 2
SSM_GROUP_PAIRS = SSM_PAIRS // SSM_GROUPS
SSM_GROUP_CH = SSM_DINNER // SSM_GROUPS

N_MEM = 256
MEM_HEADS = 4
MEM_HEAD_DIM = 64
MEM_WIDTH = MEM_HEADS * MEM_HEAD_DIM

MIX_QKV = 0
MIX_XBC = MIX_QKV + GDN_CONV_CH
MIX_COLS = MIX_XBC + SSM_CONV_CH
REST_GATE = 0
REST_ZG = REST_GATE + 3 * D_MODEL
REST_ZS = REST_ZG + GDN_V
REST_QM = REST_ZS + SSM_DINNER
REST_SMALL = REST_QM + MEM_WIDTH
REST_COLS = REST_SMALL + LANES
IN_SPLITS = (GDN_CONV_CH, GDN_HEADS, GDN_HEADS, GDN_V, SSM_DINNER, SSM_CONV_CH, SSM_HEADS, MEM_WIDTH,
             3 * D_MODEL)
SM_A = 0
SM_B = GDN_HEADS
SM_DT = 2 * GDN_HEADS

VMEM_LIMIT = 56 * 1024 * 1024


def _params(semantics):
    return pltpu.CompilerParams(dimension_semantics=semantics, vmem_limit_bytes=VMEM_LIMIT)


def _mm(a, b):
    return jnp.dot(a, b, preferred_element_type=F32)


def _mm_nt(a, b):
    return lax.dot_general(a, b, (((1,), (1,)), ((), ())), preferred_element_type=F32)


def _mm_tn(a, b):
    return lax.dot_general(a, b, (((0,), (0,)), ((), ())), preferred_element_type=F32)


def _split3(x):
    hi = x.astype(BF16)
    r1 = x - hi.astype(F32)
    mid = r1.astype(BF16)
    lo = (r1 - mid.astype(F32)).astype(BF16)
    return hi, mid, lo


def _split2_lanes(x):
    hi = x.astype(BF16)
    lo = (x - hi.astype(F32)).astype(BF16)
    return jnp.concatenate([hi, lo], axis=1)


def _rms(x, g):
    return x * lax.rsqrt(jnp.mean(x * x, axis=-1, keepdims=True) + EPS) * g


def _sigmoid(x):
    return 1.0 / (1.0 + jnp.exp2(x * float(-1.0 / np.log(2.0))))


def _silu(x):
    return x * _sigmoid(x)


def _softplus(x):
    return jnp.maximum(x, 0.0) + jnp.log1p(jnp.exp(-jnp.abs(x)))


def _const_spec(shape):
    nd = len(shape)
    return pl.BlockSpec(shape, lambda *_: (0,) * nd, pipeline_mode=pl.Buffered(1))


FFN_TM = 512
MXU_WIDTH = 256
FFN_CHUNKS = ((0, 6 * MXU_WIDTH), (6 * MXU_WIDTH, FFN_DIM))


def _ffn_compute(x, g_ref, wi_ref, wo_ref):
    xn = _rms(x, g_ref[...]).astype(BF16)
    acc = None
    for lo, hi in FFN_CHUNKS:
        gate = _mm(xn, wi_ref[:, lo:hi])
        up = _mm(xn, wi_ref[:, FFN_DIM + lo:FFN_DIM + hi])
        act = (_silu(gate) * up).astype(BF16)
        part = _mm(act, wo_ref[lo:hi, :])
        acc = part if acc is None else acc + part
    return x + 0.5 * acc


def _ffn1_body(x_ref, g_ref, wi_ref, wo_ref, g2_ref, h_ref, u_ref):
    h = _ffn_compute(x_ref[...], g_ref, wi_ref, wo_ref)
    h_ref[...] = h
    u_ref[...] = _rms(h, g2_ref[...]).astype(BF16)


def _ffn1(x, g, wi, wo, g2):
    t = x.shape[0]
    tm = min(FFN_TM, t)
    row = lambda i: (i, 0)
    return pl.pallas_call(
        _ffn1_body,
        grid=(t // tm,),
        in_specs=[pl.BlockSpec((tm, D_MODEL), row), _const_spec(g.shape), _const_spec(wi.shape),
                  _const_spec(wo.shape), _const_spec(g2.shape)],
        out_specs=[pl.BlockSpec((tm, D_MODEL), row), pl.BlockSpec((tm, D_MODEL), row)],
        out_shape=[jax.ShapeDtypeStruct((t, D_MODEL), F32), jax.ShapeDtypeStruct((t, D_MODEL), BF16)],
        compiler_params=_params(("parallel",)),
        name="ffn1",
    )(x, g, wi, wo, g2)


PACK_ROWS = 128


def _pack_body(wt_ref, mix_ref, rest_ref):
    src = np.cumsum((0,) + IN_SPLITS).tolist()
    qkv, ab, _, zg, zs, xbc, dt, qm, gate = [(src[i], IN_SPLITS[i]) for i in range(len(IN_SPLITS))]
    for ref, dst, (lo, n) in ((mix_ref, MIX_QKV, qkv), (mix_ref, MIX_XBC, xbc),
                              (rest_ref, REST_GATE, gate), (rest_ref, REST_ZG, zg),
                              (rest_ref, REST_ZS, zs), (rest_ref, REST_QM, qm)):
        assert lo % SUBLANES == 0 and n % LANES == 0
        for t in range(n // LANES):
            ref[:, dst + t * LANES:dst + (t + 1) * LANES] = (
                wt_ref[lo + t * LANES:lo + (t + 1) * LANES, :].T.astype(BF16))
    assert SM_A == 0 and SM_DT == 2 * GDN_HEADS
    small = jnp.concatenate(
        [wt_ref[ab[0]:ab[0] + 2 * GDN_HEADS, :], wt_ref[dt[0]:dt[0] + SSM_HEADS, :],
         jnp.zeros((LANES - 2 * GDN_HEADS - SSM_HEADS, wt_ref.shape[1]), F32)], axis=0)
    rest_ref[:, REST_SMALL:] = small.T.astype(BF16)


def _pack_w_in(wt):
    rows = lambda i: (i, 0)
    return pl.pallas_call(
        _pack_body,
        grid=(D_MODEL // PACK_ROWS,),
        in_specs=[pl.BlockSpec((wt.shape[0], PACK_ROWS), lambda i: (0, i))],
        out_specs=[pl.BlockSpec((PACK_ROWS, MIX_COLS), rows), pl.BlockSpec((PACK_ROWS, REST_COLS), rows)],
        out_shape=[jax.ShapeDtypeStruct((D_MODEL, MIX_COLS), BF16),
                   jax.ShapeDtypeStruct((D_MODEL, REST_COLS), BF16)],
        compiler_params=_params(("parallel",)),
        name="pack_w_in",
    )(wt)


PROJ_TM = 1024
PROJ_ROWS = 256
MIX_TN = MIX_COLS // 4
REST_TN = REST_COLS // 3


def _proj_body(u_ref, w_ref, o_ref):
    o_ref[...] = _mm(u_ref[...], w_ref[...])


def _in_proj(u, w, tn, name):
    t = u.shape[0]
    tm = min(PROJ_TM, t)
    cols = w.shape[1]
    return pl.pallas_call(
        _proj_body,
        grid=(cols // tn, t // tm),
        in_specs=[pl.BlockSpec((tm, D_MODEL), lambda j, i: (i, 0)),
                  pl.BlockSpec((D_MODEL, tn), lambda j, i: (0, j))],
        out_specs=pl.BlockSpec((tm, tn), lambda j, i: (i, j)),
        out_shape=jax.ShapeDtypeStruct((t, cols), F32),
        compiler_params=_params(("parallel", "parallel")),
        name=name,
    )(u, w)


def _causal_conv_silu(xa, w, bias):
    assert CONV_W == 4 and CONV_PAD == SUBLANES
    shape = (SUBLANES, LANES)
    sub = lax.broadcasted_iota(jnp.int32, shape, 0)
    shift = lambda prev, cur, k: pltpu.roll(jnp.where(sub >= SUBLANES - k, prev, cur), k, axis=0)
    columns = []
    for c in range(xa.shape[1] // LANES):
        cols = slice(c * LANES, (c + 1) * LANES)
        w0, w1, w2, w3 = [jnp.broadcast_to(0.5 * w[j:j + 1, cols], shape) for j in range(CONV_W)]
        b = jnp.broadcast_to(0.5 * bias[:, cols], shape)
        x_prev = xa[0:SUBLANES, cols]
        u_prev = x_prev * w1 + shift(x_prev, x_prev, 1) * w0
        outs = []
        for g in range(1, xa.shape[0] // SUBLANES):
            x = xa[g * SUBLANES:(g + 1) * SUBLANES, cols]
            x1 = shift(x_prev, x, 1)
            u = x * w1 + x1 * w0
            half = (x * w3 + b) + x1 * w2 + shift(u_prev, u, 2)
            outs.append(half + half * jnp.tanh(half))
            x_prev, u_prev = x, u
        columns.append(jnp.concatenate(outs, axis=0))
    return jnp.concatenate(columns, axis=1)


CONVPROJ_TM = 512
CONVPROJ_STEPS = 3
PROJ_STRIPS = 8


def _strips(width, n):
    tiles = width // LANES
    assert width % LANES == 0 and tiles >= n
    bounds = [LANES * ((tiles * k) // n) for k in range(n + 1)]
    return [slice(bounds[k], bounds[k + 1]) for k in range(n)]


def _proj_conv_body(u_ref, wm_ref, wr_ref, cw_ref, cb_ref, cst_ref, mix_ref, rest_ref, tail_ref,
                    carry, *, tiles_per_seq):
    i = pl.program_id(1)

    @pl.when(i % tiles_per_seq == 0)
    def _():
        carry[0:CONV_PAD, :] = cst_ref[0]

    strips_m = _strips(mix_ref.shape[1], PROJ_STRIPS)
    strips_r = _strips(rest_ref.shape[1], PROJ_STRIPS)
    for r in range(u_ref.shape[0] // PROJ_ROWS):
        rows = slice(r * PROJ_ROWS, (r + 1) * PROJ_ROWS)
        u = u_ref[rows, :]
        for cm, cr in zip(strips_m, strips_r):
            carry[CONV_PAD:, cm] = _mm(u, wm_ref[:, cm])
            rest_ref[rows, cr] = _mm(u, wr_ref[:, cr])
            mix_ref[rows, cm] = _causal_conv_silu(carry.at[:, cm], cw_ref[:, cm], cb_ref[:, cm])
            carry[0:CONV_PAD, cm] = carry[PROJ_ROWS:, cm]
    tail_ref[0] = carry[0:CONV_PAD, :]


def _in_proj_conv(u, w_mix, w_rest, cw, cb, cst8, seq):
    t = u.shape[0]
    tm = min(CONVPROJ_TM, seq)
    assert seq % tm == 0 and tm % PROJ_ROWS == 0
    tps = seq // tm
    tn_m = MIX_COLS // CONVPROJ_STEPS
    tn_r = REST_COLS // CONVPROJ_STEPS
    col = lambda j, i: (0, j)
    tile = lambda j, i: (i, j)
    seq_blk = lambda j, i: (i // tps, 0, j)
    return pl.pallas_call(
        functools.partial(_proj_conv_body, tiles_per_seq=tps),
        grid=(CONVPROJ_STEPS, t // tm),
        in_specs=[pl.BlockSpec((tm, D_MODEL), lambda j, i: (i, 0)),
                  pl.BlockSpec((D_MODEL, tn_m), col), pl.BlockSpec((D_MODEL, tn_r), col),
                  pl.BlockSpec((CONV_W, tn_m), col), pl.BlockSpec((1, tn_m), col),
                  pl.BlockSpec((1, CONV_PAD, tn_m), seq_blk)],
        out_specs=[pl.BlockSpec((tm, tn_m), tile), pl.BlockSpec((tm, tn_r), tile),
                   pl.BlockSpec((1, CONV_PAD, tn_m), seq_blk)],
        out_shape=[jax.ShapeDtypeStruct((t, MIX_COLS), F32), jax.ShapeDtypeStruct((t, REST_COLS), F32),
                   jax.ShapeDtypeStruct((t // seq, CONV_PAD, MIX_COLS), F32)],
        scratch_shapes=[pltpu.VMEM((CONV_PAD + PROJ_ROWS, tn_m), F32)],
        compiler_params=_params(("parallel", "arbitrary")),
        name="in_proj_conv",
    )(u, w_mix, w_rest, cw, cb, cst8)


def _memkv_body(x_ref, g_ref, w_ref, o_ref):
    o_ref[...] = _mm(_rms(x_ref[...], g_ref[...]).astype(BF16), w_ref[...])


MEMKV_TM = 512


def _memkv(mem, g, w):
    t = mem.shape[0]
    tm = min(MEMKV_TM, t)
    row = lambda i: (i, 0)
    return pl.pallas_call(
        _memkv_body,
        grid=(t // tm,),
        in_specs=[pl.BlockSpec((tm, D_MODEL), row), _const_spec(g.shape), _const_spec(w.shape)],
        out_specs=pl.BlockSpec((tm, 2 * MEM_WIDTH), row),
        out_shape=jax.ShapeDtypeStruct((t, 2 * MEM_WIDTH), F32),
        compiler_params=_params(("parallel",)),
        name="memkv",
    )(mem, g, w)


MERGE_TM = 256


def _merge_ffn2_body(h_ref, yg_ref, ys_ref, ym_ref, gate_ref, wb_ref, wout_ref,
                     g_ref, wi_ref, wo_ref, gf_ref, y_ref):
    gates = gate_ref[...]
    merged = (_sigmoid(gates[:, 0:D_MODEL]) * _mm(yg_ref[...], wb_ref[0:GDN_V, :])
              + _sigmoid(gates[:, D_MODEL:2 * D_MODEL])
              * _mm(ys_ref[...], wb_ref[GDN_V:GDN_V + SSM_DINNER, :])
              + _sigmoid(gates[:, 2 * D_MODEL:3 * D_MODEL])
              * _mm(ym_ref[...], wb_ref[GDN_V + SSM_DINNER:, :]))
    h = h_ref[...] + _mm(merged.astype(BF16), wout_ref[...])
    h = _ffn_compute(h, g_ref, wi_ref, wo_ref)
    y_ref[...] = _rms(h, gf_ref[...])


def _merge_ffn2(h, yg, ys, ym, proj, wb, wout, g, wi, wo, gf):
    t = h.shape[0]
    tm = min(MERGE_TM, t)
    row = lambda i: (i, 0)
    gate_blk = REST_GATE // (3 * D_MODEL)
    return pl.pallas_call(
        _merge_ffn2_body,
        grid=(t // tm,),
        in_specs=[pl.BlockSpec((tm, D_MODEL), row), pl.BlockSpec((tm, GDN_V), row),
                  pl.BlockSpec((tm, SSM_DINNER), row), pl.BlockSpec((tm, MEM_WIDTH), row),
                  pl.BlockSpec((tm, 3 * D_MODEL), lambda i: (i, gate_blk)),
                  _const_spec(wb.shape), _const_spec(wout.shape), _const_spec(g.shape),
                  _const_spec(wi.shape), _const_spec(wo.shape), _const_spec(gf.shape)],
        out_specs=pl.BlockSpec((tm, D_MODEL), row),
        out_shape=jax.ShapeDtypeStruct((t, D_MODEL), F32),
        compiler_params=_params(("parallel",)),
        name="merge_ffn2",
    )(h, yg, ys, ym, proj, wb, wout, g, wi, wo, gf)


MEMATTN_TL = 512
MEMATTN_NSEQ = 16


def _memattn_body(q_ref, k_ref, v_ref, y_ref, *, nseq, rows):
    lane = lax.broadcasted_iota(jnp.int32, (rows, MEM_WIDTH), 1)
    masks = [(lane >= hh * MEM_HEAD_DIM) & (lane < (hh + 1) * MEM_HEAD_DIM) for hh in range(MEM_HEADS)]
    seqs = range(nseq)
    q = [q_ref[s * rows:(s + 1) * rows, :] for s in seqs]
    q4 = [jnp.concatenate([jnp.where(m, x, 0.0) for m in masks], axis=0).astype(BF16) for x in q]
    sc = [_mm(q4[s], k_ref[s].astype(BF16)) * (MEM_HEAD_DIM ** -0.5) for s in seqs]
    p = [jnp.exp(x - jnp.max(x, axis=-1, keepdims=True)) for x in sc]
    p = [(x / jnp.sum(x, axis=-1, keepdims=True)).astype(BF16) for x in p]
    o4 = [_mm_nt(p[s], v_ref[s].astype(BF16)) for s in seqs]
    ys = []
    for x in o4:
        y = jnp.where(masks[0], x[0:rows], 0.0)
        for hh in range(1, MEM_HEADS):
            y = y + jnp.where(masks[hh], x[hh * rows:(hh + 1) * rows], 0.0)
        ys.append(y)
    y_ref[...] = jnp.concatenate(ys, axis=0).astype(BF16)


def _memattn(proj, k, v, bsz, seq):
    if seq >= MEMATTN_TL:
        nseq, rows, nt = 1, MEMATTN_TL, seq // MEMATTN_TL
    else:
        nseq, rows, nt = MEMATTN_NSEQ, seq, 1
    assert bsz % nseq == 0 and seq % rows == 0
    tl = nseq * rows
    qm_blk = REST_QM // MEM_WIDTH
    kv_spec = pl.BlockSpec((nseq, MEM_WIDTH, N_MEM), lambda b, i: (b, 0, 0))
    return pl.pallas_call(
        functools.partial(_memattn_body, nseq=nseq, rows=rows),
        grid=(bsz // nseq, nt),
        in_specs=[pl.BlockSpec((tl, MEM_WIDTH), lambda b, i: (b * nt + i, qm_blk)), kv_spec, kv_spec],
        out_specs=pl.BlockSpec((tl, MEM_WIDTH), lambda b, i: (b * nt + i, 0)),
        out_shape=jax.ShapeDtypeStruct((bsz * seq, MEM_WIDTH), BF16),
        compiler_params=_params(("parallel", "arbitrary")),
        name="memattn",
    )(proj, k, v)


SEQ_TL = 256
LONG_NSEQ = 2
SHORT_NSEQ = 8
CONV_PAD = SUBLANES
MIN_MXU_ROWS = 16

Geom = collections.namedtuple("Geom", ["nseq", "nch", "vrows"])


def _geom(bsz, seq):
    if seq >= CHUNK:
        assert seq % SEQ_TL == 0
        nseq = LONG_NSEQ if bsz % LONG_NSEQ == 0 else 1
        return Geom(nseq, SEQ_TL // CHUNK, CHUNK), seq // SEQ_TL
    assert bsz % SHORT_NSEQ == 0 and seq % SUBLANES == 0 and (SHORT_NSEQ * seq) % CHUNK == 0
    return Geom(SHORT_NSEQ, 1, seq), 1


class _TokenView:
    def __init__(self, ref):
        self.ref = ref

    def _locate(self, rows):
        per = self.ref.shape[1]
        s = rows.start // per
        assert rows.stop <= (s + 1) * per
        return s, slice(rows.start - s * per, rows.stop - s * per)

    def __getitem__(self, idx):
        rows, cols = idx
        if len(self.ref.shape) == 2:
            return self.ref[rows, cols]
        if rows == slice(None):
            v = self.ref[:, :, cols]
            return v.reshape(v.shape[0] * v.shape[1], v.shape[2])
        s, r = self._locate(rows)
        return self.ref[s, r, cols]

    def __setitem__(self, idx, value):
        rows, cols = idx
        if len(self.ref.shape) == 2:
            self.ref[rows, cols] = value
        elif rows == slice(None):
            nseq, per = self.ref.shape[0], self.ref.shape[1]
            self.ref[:, :, cols] = value.reshape(nseq, per, value.shape[1])
        else:
            s, r = self._locate(rows)
            self.ref[s, r, cols] = value


def _short_conv_silu(x_ref, cst_ref, cw_ref, cb_ref, act, geom):
    assert geom.nch == 1 and geom.vrows < CHUNK and CONV_PAD == SUBLANES
    vr = geom.vrows
    sub = lax.broadcasted_iota(jnp.int32, (SUBLANES, LANES), 0)
    for cb in range(x_ref.shape[1] // LANES):
        cols = slice(cb * LANES, (cb + 1) * LANES)
        planes = [cst_ref[j, :, cols] for j in range(CONV_W - 1)]
        for s in range(geom.nseq):
            before = jnp.zeros((SUBLANES, LANES), F32)
            for j, plane in enumerate(planes):
                before = jnp.where(sub == SUBLANES - (CONV_W - 1) + j,
                                   jnp.broadcast_to(plane[s:s + 1, :], (SUBLANES, LANES)), before)
            xa = jnp.concatenate([before, x_ref[s * vr:(s + 1) * vr, cols]], axis=0)
            act[s * vr:(s + 1) * vr, cols] = _causal_conv_silu(xa, cw_ref[:, cols], cb_ref[:, cols])


def _pair_masks(vr):
    row = lax.broadcasted_iota(jnp.int32, (CHUNK, LANES), 0)
    lane = lax.broadcasted_iota(jnp.int32, (CHUNK, LANES), 1)
    col = jnp.where(lane < CHUNK, lane, lane - CHUNK)
    same = (row // vr) == (col // vr)
    return row, lane, col, same


def _block_diag(x, left):
    zero = jnp.zeros_like(x)
    return jnp.concatenate([jnp.where(left, x, zero), jnp.where(left, zero, x)], axis=0)


def _chunk_cumsums(gates, nprob, vr):
    r = lax.broadcasted_iota(jnp.int32, (CHUNK, CHUNK), 0)
    c = lax.broadcasted_iota(jnp.int32, (CHUNK, CHUNK), 1)
    same = (r // vr) == (c // vr)
    sums = jnp.concatenate([jnp.where(same & (r >= c), 1.0, 0.0), jnp.where(same, 1.0, 0.0)],
                           axis=0).astype(BF16)
    sums3 = jnp.concatenate([sums, sums, sums], axis=1)
    pieces = [jnp.concatenate(_split3(gates[pb * CHUNK:(pb + 1) * CHUNK, 0:LANES]), axis=0)
              for pb in range(nprob)]
    out = _mm(sums3, jnp.concatenate(pieces, axis=1))
    cols = lambda pb: slice(pb * LANES, (pb + 1) * LANES)
    return ([out[0:CHUNK, cols(pb)] for pb in range(nprob)],
            [out[CHUNK:, cols(pb)] for pb in range(nprob)])


def _rows_to_columns(e):
    pieces = jnp.concatenate(_split3(e), axis=0)
    pad = jnp.zeros((MIN_MXU_ROWS - pieces.shape[0], e.shape[1]), BF16)
    row = lax.broadcasted_iota(jnp.int32, (MIN_MXU_ROWS, LANES), 0)
    ones = jnp.where(row < pieces.shape[0], 1.0, 0.0).astype(BF16)
    return _mm_tn(jnp.concatenate([pieces, pad], axis=0), ones)


def _pad_rows(x):
    if x.shape[0] >= MIN_MXU_ROWS:
        return x
    return jnp.concatenate([x, jnp.zeros((MIN_MXU_ROWS - x.shape[0], x.shape[1]), x.dtype)], axis=0)


def _row_form(gc):
    return jnp.concatenate([gc, pltpu.roll(gc, LANES - 1, axis=1)], axis=0).T


def _gdn_body(*refs, geom):
    if geom.vrows == CHUNK:
        (x_ref, sm_ref, z_ref, s0_ref, par_ref, gn_ref, selp_ref, y_ref, sout_ref,
         gates, s_scr, u_scr, w_scr, qd_scr, kd_scr, qk_scr, egl_scr, o_scr) = refs
        act = _TokenView(x_ref)
    else:
        (x_ref, sm_ref, z_ref, cst_ref, cw_ref, cb_ref, s0_ref, par_ref, gn_ref, selp_ref, y_ref,
         sout_ref, act, gates, s_scr, u_scr, w_scr, qd_scr, kd_scr, qk_scr, egl_scr, o_scr) = refs
        _short_conv_silu(x_ref, cst_ref, cw_ref, cb_ref, act, geom)
    tile = pl.program_id(1)
    first = tile == 0
    vr = geom.vrows
    nsub = CHUNK // vr
    nprob = geom.nseq * geom.nch // nsub

    @pl.when(first)
    def _():
        s_scr[...] = s0_ref[...]

    sm = _TokenView(sm_ref)[:, :]
    lane_g = lax.broadcasted_iota(jnp.int32, sm.shape, 1)
    head_lane = lane_g < GDN_HEADS
    gates[:, 0:LANES] = jnp.where(
        head_lane, -jnp.exp(par_ref[0:1, :]) * _softplus(sm + par_ref[1:2, :]), 0.0)
    gates[:, LANES:] = jnp.where(head_lane, pltpu.roll(_sigmoid(sm), LANES - SM_B, axis=1), 0.0)

    row, lane, col, same = _pair_masks(vr)
    left = lane < CHUNK
    incl = same & (row >= col)
    strict = same & (row > col)
    eye = jnp.where(row == col, 1.0, 0.0).astype(F32)
    zeros_h = jnp.zeros((CHUNK, LANES), F32)
    hcols = lambda h: slice(h * LANES, (h + 1) * LANES)
    prows = lambda pb, n=CHUNK: slice(pb * n, (pb + 1) * n)

    gcs, gls = _chunk_cumsums(gates, nprob, vr)
    lhs, xts = [], []
    for pb in range(nprob):
        gc = gcs[pb]
        quantities = jnp.concatenate(
            [gc, gates[prows(pb), LANES:2 * LANES], jnp.exp(gc), jnp.exp(gls[pb] - gc)], axis=0)
        lhs.append(_split2_lanes(quantities))
        xts.append(_row_form(gc))

    def l2n(t):
        return t * lax.rsqrt(jnp.sum(t * t, axis=-1, keepdims=True) + EPS)

    def stacked(e, qi):
        return jnp.concatenate([e[qi * CHUNK:(qi + 1) * CHUNK, 0:LANES],
                                e[qi * CHUNK:(qi + 1) * CHUNK, LANES:]], axis=0)

    def phase1(pbs):
        items = [(pb, p) for pb in pbs for p in range(GDN_PAIRS)]
        exp_ = [_mm(lhs[pb], selp_ref[p]) for pb, p in items]
        decay_l, kq_l, rhs_l, qd_l, kd_l = [], [], [], [], []
        for (pb, p), e in zip(items, exp_):
            ha, hb = 2 * p, 2 * p + 1
            qa, qb = [l2n(act[prows(pb), hcols(h)]) * (GDN_DK ** -0.5) for h in (ha, hb)]
            ka, kb = [l2n(act[prows(pb), hcols(GDN_HEADS + h)]) for h in (ha, hb)]
            va, vb = [act[prows(pb), hcols(2 * GDN_HEADS + h)] for h in (ha, hb)]
            beta2, eg2, kdec2 = stacked(e, 1), stacked(e, 2), stacked(e, 3)
            k2 = jnp.concatenate([ka, kb], axis=0)
            kbeta2 = k2 * beta2
            gci = jnp.where(left, e[0:CHUNK, 0:LANES], e[0:CHUNK, LANES:])
            gcj = jnp.broadcast_to(xts[pb][ha:ha + 1, :], (CHUNK, LANES))
            decay_l.append(jnp.where(incl, jnp.exp(jnp.where(incl, gci - gcj, 0.0)), 0.0))
            k_bd = jnp.concatenate([jnp.concatenate([ka, zeros_h], axis=1),
                                    jnp.concatenate([zeros_h, kb], axis=1)], axis=0).astype(BF16)
            kbq = jnp.concatenate([jnp.concatenate([kbeta2[0:CHUNK], kbeta2[CHUNK:]], axis=1),
                                   jnp.concatenate([qa, qb], axis=1)], axis=0).astype(BF16)
            kq_l.append(_mm_nt(kbq, k_bd))
            rhs_l.append(jnp.concatenate(
                [jnp.concatenate([va, vb], axis=0) * beta2, kbeta2 * eg2], axis=1).astype(BF16))
            qd_l.append(jnp.concatenate([qa, qb], axis=0) * eg2)
            kd_l.append(k2 * kdec2)
            for sub in range(nsub):
                last = 2 * CHUNK + (sub + 1) * vr - 1
                slot = prows(pb * nsub + sub, SUBLANES)
                egl_scr[slot, hcols(ha)] = jnp.broadcast_to(e[last:last + 1, 0:LANES], (SUBLANES, LANES))
                egl_scr[slot, hcols(hb)] = jnp.broadcast_to(e[last:last + 1, LANES:], (SUBLANES, LANES))

        n_pow = [-jnp.where(strict, kq[0:CHUNK] * d, 0.0) for kq, d in zip(kq_l, decay_l)]
        t_inv = [eye + n for n in n_pow]
        for _ in range(int(np.ceil(np.log2(vr))) - 1):
            n_pow = [_mm(n.astype(BF16), _block_diag(n, left).astype(BF16)) for n in n_pow]
            t_inv = [t + _mm(t.astype(BF16), _block_diag(n, left).astype(BF16))
                     for t, n in zip(t_inv, n_pow)]

        uw_l = [_mm(_block_diag(t, left).astype(BF16), rhs) for t, rhs in zip(t_inv, rhs_l)]
        for (pb, p), uw, kq, d, qd, kd in zip(items, uw_l, kq_l, decay_l, qd_l, kd_l):
            qk_scr[prows(pb, 2 * CHUNK), hcols(p)] = _block_diag(kq[CHUNK:] * d, left).astype(BF16)
            for idx, h in enumerate((2 * p, 2 * p + 1)):
                sl = slice(idx * CHUNK, (idx + 1) * CHUNK)
                u_scr[prows(pb), hcols(h)] = uw[sl, 0:LANES]
                w_scr[prows(pb), hcols(h)] = uw[sl, LANES:]
                qd_scr[prows(pb), hcols(h)] = qd[sl]
                kd_scr[prows(pb), hcols(h)] = kd[sl]

    def phase2(c):
        sh = [(s, h) for s in range(geom.nseq) for h in range(GDN_HEADS)]
        trows = lambda s: prows(s * geom.nch + c, vr)
        wqs = [_mm(jnp.concatenate([w_scr[trows(s), hcols(h)], qd_scr[trows(s), hcols(h)]],
                                   axis=0).astype(BF16), s_scr[s, h].astype(BF16)) for s, h in sh]
        vnew = [u_scr[trows(s), hcols(h)] - w[0:vr] for (s, h), w in zip(sh, wqs)]
        for (s, h), vn in zip(sh, vnew):
            slot = (s * geom.nch + c) * SUBLANES
            s_scr[s, h] = (s_scr[s, h] * egl_scr[slot:slot + 1, hcols(h)]
                           + _mm_tn(_pad_rows(kd_scr[trows(s), hcols(h)]).astype(BF16),
                                    _pad_rows(vn).astype(BF16)))
        for pb in range(c * nsub, geom.nseq * geom.nch, geom.nch * nsub):
            subs = [pb // geom.nch + j for j in range(nsub)]
            for p in range(GDN_PAIRS):
                heads = (2 * p, 2 * p + 1)
                vn2 = jnp.concatenate([vnew[s * GDN_HEADS + h] for h in heads for s in subs],
                                      axis=0).astype(BF16)
                intra = _mm(qk_scr[prows((pb // nsub), 2 * CHUNK), hcols(p)], vn2)
                for idx, h in enumerate(heads):
                    for j, s in enumerate(subs):
                        r0 = idx * CHUNK + j * vr
                        o = wqs[s * GDN_HEADS + h][vr:] + intra[r0:r0 + vr]
                        o_scr[trows(s), hcols(h)] = o

    phase1(range(nprob))
    for c in range(geom.nch):
        phase2(c)

    for h in range(GDN_HEADS):
        y = _rms(o_scr[:, hcols(h)], gn_ref[...]) * _silu(_TokenView(z_ref)[:, hcols(h)])
        _TokenView(y_ref)[:, hcols(h)] = y.astype(BF16)

    @pl.when(tile == pl.num_programs(1) - 1)
    def _():
        sout_ref[...] = s_scr[...]


def _conv_operands(conv, geom, width, col_blk):
    if conv is None:
        return [], [], []
    state, cw, cb = conv
    specs = [pl.BlockSpec((CONV_W - 1, geom.nseq, width), lambda b, i: (0, b, 0)),
             pl.BlockSpec((CONV_W, width), lambda b, i: (0, col_blk)),
             pl.BlockSpec((1, width), lambda b, i: (0, col_blk))]
    return [state, cw, cb], specs, [pltpu.VMEM((geom.nseq * geom.vrows, width), F32)]


def _token_layout(geom, bsz, seq, nt):
    tls = geom.nch * geom.vrows
    if geom.vrows == CHUNK:
        view = lambda a: a.reshape(bsz, seq, a.shape[-1])
        spec = lambda width, blk: pl.BlockSpec((geom.nseq, tls, width), lambda b, i: (b, i, blk))
        return view, spec, lambda width: (bsz, seq, width)
    spec = lambda width, blk: pl.BlockSpec((geom.nseq * tls, width), lambda b, i: (b * nt + i, blk))
    return (lambda a: a), spec, lambda width: (bsz * seq, width)


def _gdn(mix, rest, conv, s0, par, gn, selp, bsz, seq):
    geom, nt = _geom(bsz, seq)
    tl = geom.nseq * geom.nch * geom.vrows
    nprob = tl // CHUNK
    view, tok, out_shape = _token_layout(geom, bsz, seq, nt)
    st_spec = pl.BlockSpec((geom.nseq, GDN_HEADS, GDN_DK, GDN_DV), lambda b, i: (b, 0, 0, 0))
    conv_ops, conv_specs, conv_scratch = _conv_operands(conv, geom, GDN_CONV_CH, MIX_QKV // GDN_CONV_CH)
    y, s_new = pl.pallas_call(
        functools.partial(_gdn_body, geom=geom),
        grid=(bsz // geom.nseq, nt),
        in_specs=[tok(GDN_CONV_CH, MIX_QKV // GDN_CONV_CH), tok(LANES, REST_SMALL // LANES),
                  tok(GDN_V, REST_ZG // GDN_V)] + conv_specs + [
                  st_spec, _const_spec(par.shape), _const_spec(gn.shape), _const_spec(selp.shape)],
        out_specs=[tok(GDN_V, 0), st_spec],
        out_shape=[jax.ShapeDtypeStruct(out_shape(GDN_V), BF16),
                   jax.ShapeDtypeStruct((bsz, GDN_HEADS, GDN_DK, GDN_DV), F32)],
        scratch_shapes=conv_scratch + [
            pltpu.VMEM((tl, 2 * LANES), F32),
            pltpu.VMEM((geom.nseq, GDN_HEADS, GDN_DK, GDN_DV), F32),
            pltpu.VMEM((tl, GDN_V), F32),
            pltpu.VMEM((tl, GDN_V), F32),
            pltpu.VMEM((tl, GDN_V), F32),
            pltpu.VMEM((tl, GDN_V), F32),
            pltpu.VMEM((nprob * 2 * CHUNK, GDN_PAIRS * LANES), BF16),
            pltpu.VMEM((geom.nseq * geom.nch * SUBLANES, GDN_V), F32),
            pltpu.VMEM((tl, GDN_V), F32)],
        compiler_params=_params(("parallel", "arbitrary")),
        name="gdn",
    )(view(mix), view(rest), view(rest), *conv_ops, s0, par, gn, selp)
    return y.reshape(bsz * seq, GDN_V), s_new


def _ssd_body(*refs, geom):
    if geom.vrows == CHUNK:
        (x_ref, sm_ref, z_ref, h0_ref, par_ref, dch_ref, nw_ref, seld_ref, y_ref, hout_ref,
         gates, h_scr, o_scr) = refs
        act = _TokenView(x_ref)
        tile = pl.program_id(1)

        @pl.when(tile == 0)
        def _():
            for s in range(geom.nseq):
                for p in range(SSM_PAIRS):
                    h_scr[s, p] = h0_ref[s, p].T
    else:
        (x_ref, sm_ref, z_ref, cst_ref, cw_ref, cb_ref, h0_ref, par_ref, dch_ref, nw_ref, seld_ref,
         y_ref, hout_ref, act, gates, o_scr) = refs
        _short_conv_silu(x_ref, cst_ref, cw_ref, cb_ref, act, geom)
    vr = geom.vrows
    nsub = CHUNK // vr
    nprob = geom.nseq * geom.nch // nsub

    sm = _TokenView(sm_ref)[:, :]
    lane_g = lax.broadcasted_iota(jnp.int32, sm.shape, 1)
    is_dt = (lane_g >= SM_DT) & (lane_g < SM_DT + SSM_HEADS)
    dt = jnp.where(is_dt, _softplus(sm + par_ref[1:2, :]), 0.0)
    gates[:, 0:LANES] = dt * jnp.where(is_dt, -jnp.exp(par_ref[0:1, :]), 0.0)
    gates[:, LANES:] = dt

    row, lane, col, same = _pair_masks(vr)
    left = lane < CHUNK
    incl = same & (row >= col)
    pcols = lambda p: slice(p * LANES, (p + 1) * LANES)
    prows = lambda pb, n=CHUNK: slice(pb * n, (pb + 1) * n)
    quarter = lambda e, qi, j: e[qi * CHUNK:(qi + 1) * CHUNK, j * LANES:(j + 1) * LANES]
    srows = lambda a, j: a[j * vr:(j + 1) * vr]
    mxu_rows = lambda a: _pad_rows(a).astype(BF16)
    b_off = SSM_DINNER
    c_off = SSM_DINNER + SSM_GROUPS * SSM_DSTATE

    gcs, gls = _chunk_cumsums(gates, nprob, vr)
    for pb in range(nprob):
        seqs = [(pb * nsub + j) // geom.nch for j in range(nsub)]
        gc = gcs[pb]
        dt_c = gates[prows(pb), LANES:2 * LANES]
        lhs = _split2_lanes(jnp.concatenate(
            [gc, dt_c, jnp.exp(gc), jnp.exp(gls[pb] - gc) * dt_c], axis=0))
        xt = _row_form(gc)
        exp_ = [_mm(lhs, seld_ref[d]) for d in range(SSM_PAIRS // 2)]

        b_f = [act[prows(pb), b_off + g * SSM_DSTATE:b_off + (g + 1) * SSM_DSTATE]
               for g in range(SSM_GROUPS)]
        c_f = [act[prows(pb), c_off + g * SSM_DSTATE:c_off + (g + 1) * SSM_DSTATE]
               for g in range(SSM_GROUPS)]
        b_bf = [b.astype(BF16) for b in b_f]
        c_bf = [c.astype(BF16) for c in c_f]
        cb2 = [_mm_nt(c, jnp.concatenate([b, b], axis=0)) for b, c in zip(b_bf, c_bf)]

        pairs = range(SSM_PAIRS)
        grp = lambda p: p // SSM_GROUP_PAIRS
        x_l = [act[prows(pb), pcols(p)] for p in pairs]
        gc_l = [quarter(exp_[p // 2], 0, p % 2) for p in pairs]
        eg_l = [quarter(exp_[p // 2], 2, p % 2) for p in pairs]
        xdt_l = [x * quarter(exp_[p // 2], 1, p % 2) for p, x in zip(pairs, x_l)]
        xw_l = [x * quarter(exp_[p // 2], 3, p % 2) for p, x in zip(pairs, x_l)]
        m_l = []
        for p in pairs:
            r = SM_DT + 2 * p
            gcj = jnp.broadcast_to(xt[r:r + 1, :], (CHUNK, LANES))
            decay = jnp.where(incl, jnp.exp(jnp.where(incl, gc_l[p] - gcj, 0.0)), 0.0)
            m_l.append((cb2[grp(p)] * decay).astype(BF16))
        intra = [_mm(m, _block_diag(xdt, left).astype(BF16)) for m, xdt in zip(m_l, xdt_l)]
        last_row = lambda a, j: a[(j + 1) * vr - 1:(j + 1) * vr, :]
        if geom.vrows == CHUNK:
            dstate = [_mm_tn(b_bf[grp(p)], xw_l[p].astype(BF16)) for p in pairs]
            inter = [_mm(c_bf[grp(p)], h_scr[seqs[0], p].astype(BF16)) for p in pairs]
            for p in pairs:
                h_scr[seqs[0], p] = h_scr[seqs[0], p] * last_row(eg_l[p], 0) + dstate[p]
        else:
            gp = SSM_GROUP_PAIRS
            jg = [(j, g) for j in range(nsub) for g in range(SSM_GROUPS)]
            gslice = lambda g: slice(g * gp, (g + 1) * gp)
            lanes = lambda a_l, g, f: jnp.concatenate([f(a) for a in a_l[gslice(g)]], axis=1)
            inter_g = []
            for j, g in jg:
                ds = _mm_tn(mxu_rows(lanes(xw_l, g, lambda a: srows(a, j))), mxu_rows(srows(b_f[g], j)))
                sc = _rows_to_columns(lanes(eg_l, g, lambda a: last_row(a, j)))
                h = h0_ref[seqs[j], gslice(g)].reshape(gp * LANES, SSM_DSTATE)
                inter_g.append(_mm_nt(mxu_rows(srows(c_f[g], j)), h.astype(BF16))[0:vr])
                hout_ref[seqs[j], gslice(g)] = (h * sc + ds).reshape(gp, LANES, SSM_DSTATE)
            inter = [jnp.concatenate(
                [inter_g[j * SSM_GROUPS + grp(p)][:, (p % gp) * LANES:(p % gp + 1) * LANES]
                 for j in range(nsub)], axis=0) for p in pairs]

        for g in range(SSM_GROUPS):
            ys = []
            ssq = None
            for p in range(g * SSM_GROUP_PAIRS, (g + 1) * SSM_GROUP_PAIRS):
                y = intra[p] + inter[p] * eg_l[p] + dch_ref[:, pcols(p)] * x_l[p]
                y = y * _silu(_TokenView(z_ref)[prows(pb), pcols(p)])
                ys.append(y)
                sq = jnp.sum(y * y, axis=-1, keepdims=True)
                ssq = sq if ssq is None else ssq + sq
            inv = lax.rsqrt(ssq * (1.0 / SSM_GROUP_CH) + EPS)
            for pp, y in enumerate(ys):
                o_scr[prows(pb), pcols(g * SSM_GROUP_PAIRS + pp)] = y * inv

    _TokenView(y_ref)[:, slice(None)] = (o_scr[...] * nw_ref[...]).astype(BF16)

    if geom.vrows == CHUNK:
        @pl.when(tile == pl.num_programs(1) - 1)
        def _():
            for s in range(geom.nseq):
                for p in range(SSM_PAIRS):
                    hout_ref[s, p] = h_scr[s, p].T


def _ssd(mix, rest, conv, h0, par, dch, nw, seld, bsz, seq):
    geom, nt = _geom(bsz, seq)
    tl = geom.nseq * geom.nch * geom.vrows
    view, tok, out_shape = _token_layout(geom, bsz, seq, nt)
    st_shape = (geom.nseq, SSM_PAIRS, 2 * SSM_HEADDIM, SSM_DSTATE)
    st_spec = pl.BlockSpec(st_shape, lambda b, i: (b, 0, 0, 0))
    conv_ops, conv_specs, conv_scratch = _conv_operands(conv, geom, SSM_CONV_CH, MIX_XBC // SSM_CONV_CH)
    y, h_new = pl.pallas_call(
        functools.partial(_ssd_body, geom=geom),
        grid=(bsz // geom.nseq, nt),
        in_specs=[tok(SSM_CONV_CH, MIX_XBC // SSM_CONV_CH), tok(LANES, REST_SMALL // LANES),
                  tok(SSM_DINNER, REST_ZS // SSM_DINNER)] + conv_specs + [
                  st_spec, _const_spec(par.shape), _const_spec(dch.shape), _const_spec(nw.shape),
                  _const_spec(seld.shape)],
        out_specs=[tok(SSM_DINNER, 0), st_spec],
        out_shape=[jax.ShapeDtypeStruct(out_shape(SSM_DINNER), BF16),
                   jax.ShapeDtypeStruct((bsz,) + st_shape[1:], F32)],
        scratch_shapes=conv_scratch + [pltpu.VMEM((tl, 2 * LANES), F32)] + (
            [pltpu.VMEM((geom.nseq, SSM_PAIRS, SSM_DSTATE, 2 * SSM_HEADDIM), F32)]
            if conv is None else []) + [
            pltpu.VMEM((tl, SSM_DINNER), F32)],
        compiler_params=_params(("parallel", "arbitrary")),
        name="ssd",
    )(view(mix), view(rest), view(rest), *conv_ops, h0, par, dch, nw, seld)
    return y.reshape(bsz * seq, SSM_DINNER), h_new


def _pad_lanes(v, offset):
    out = jnp.zeros((LANES,), F32)
    return out.at[offset:offset + v.shape[0]].set(v.astype(F32))


def _pair_selection(first_lane, lanes_per_head, n_mats):
    heads_per_mat = 2 * LANES // lanes_per_head
    sel = np.zeros((n_mats, 2 * LANES, 2 * LANES), np.float32)
    for m in range(n_mats):
        for j in range(heads_per_mat):
            src = first_lane + m * heads_per_mat + j
            sel[m, src, j * lanes_per_head:(j + 1) * lanes_per_head] = 1.0
            sel[m, LANES + src, j * lanes_per_head:(j + 1) * lanes_per_head] = 1.0
    return jnp.asarray(sel, BF16)


def _pad_conv_state(st):
    bsz, _, ch = st.shape
    return jnp.concatenate([jnp.zeros((bsz, CONV_PAD - (CONV_W - 1), ch), F32), st.astype(F32)], axis=1)


def _layer(x, mem_k, mem_v, s_gdn, c_gdn, s_ssm, c_ssm, w):
    bsz, seq, _ = x.shape
    t = bsz * seq
    xf = x.reshape(t, D_MODEL)
    h, u = _ffn1(xf, w["norm_ff1"], w["w_ff1_in"], w["w_ff1_out"], w["norm_mix"])
    if seq >= CHUNK:
        cst8 = jnp.concatenate([_pad_conv_state(c_gdn), _pad_conv_state(c_ssm)], axis=2)
        mix, rest, tail = _in_proj_conv(u, w["w_mix"], w["w_rest"], w["conv_w"], w["conv_b"], cst8, seq)
        conv_gdn = conv_ssm = None
    else:
        rest = _in_proj(u, w["w_rest"], REST_TN, "in_proj_rest")
        mix = _in_proj(u, w["w_mix"], MIX_TN, "in_proj_mix")
        tail = mix.reshape(bsz, seq, MIX_COLS)[:, seq - CONV_PAD:, :]
        taps_first = lambda c: jnp.swapaxes(c.astype(F32), 0, 1)
        conv_gdn = (taps_first(c_gdn), w["conv_w"], w["conv_b"])
        conv_ssm = (taps_first(c_ssm), w["conv_w"], w["conv_b"])

    yg, s_gdn_new = _gdn(mix, rest, conv_gdn, s_gdn.astype(F32), w["gdn_par"], w["gdn_norm"],
                         w["sel_gdn"], bsz, seq)
    h0 = s_ssm.astype(F32).reshape(bsz, SSM_PAIRS, 2 * SSM_HEADDIM, SSM_DSTATE)
    ys, s_ssm_new = _ssd(mix, rest, conv_ssm, h0, w["ssm_par"], w["ssm_d_ch"], w["ssm_norm"],
                         w["sel_ssm"], bsz, seq)
    transposed = lambda m: jnp.swapaxes(m.reshape(bsz, N_MEM, MEM_WIDTH), 1, 2)
    ym = _memattn(rest, transposed(mem_k), transposed(mem_v), bsz, seq)
    y = _merge_ffn2(h, yg, ys, ym, rest, w["w_branch"], w["w_out"], w["norm_ff2"],
                    w["w_ff2_in"], w["w_ff2_out"], w["norm_final"])

    c_gdn_new = tail[:, CONV_PAD - (CONV_W - 1):, MIX_QKV:MIX_QKV + GDN_CONV_CH]
    c_ssm_new = tail[:, CONV_PAD - (CONV_W - 1):, MIX_XBC:MIX_XBC + SSM_CONV_CH]
    return (y.reshape(bsz, seq, D_MODEL), s_gdn_new, c_gdn_new,
            s_ssm_new.reshape(bsz, SSM_HEADS, SSM_HEADDIM, SSM_DSTATE), c_ssm_new)


def _row(v):
    return v.astype(F32).reshape(1, -1)


def _prep_weights(l, norm_ff1, w_ff1_in, w_ff1_out, norm_mix, w_in, gdn_conv_w, gdn_a_log,
                  gdn_dt_bias, gdn_norm, ssm_conv_w, ssm_conv_b, ssm_a_log, ssm_dt_bias, ssm_d,
                  ssm_norm, w_branch, w_out, norm_ff2, w_ff2_in, w_ff2_out, norm_final):
    row = _row

    w_mix, w_rest = _pack_w_in(jnp.swapaxes(w_in[l], 0, 1))

    def two_rows(a, b, offset):
        par = jnp.zeros((SUBLANES, LANES), F32)
        return par.at[0].set(_pad_lanes(a, offset)).at[1].set(_pad_lanes(b, offset))

    return {
        "norm_ff1": row(norm_ff1[l]), "w_ff1_in": w_ff1_in[l].astype(BF16),
        "w_ff1_out": w_ff1_out[l].astype(BF16), "norm_mix": row(norm_mix[l]),
        "w_mix": w_mix, "w_rest": w_rest,
        "conv_w": jnp.concatenate([gdn_conv_w[l], ssm_conv_w[l]], axis=1).astype(F32),
        "conv_b": jnp.concatenate([jnp.zeros((1, GDN_CONV_CH), F32), row(ssm_conv_b[l])], axis=1),
        "gdn_par": two_rows(gdn_a_log[l], gdn_dt_bias[l], SM_A),
        "gdn_norm": row(gdn_norm[l]),
        "sel_gdn": _pair_selection(0, LANES, GDN_PAIRS),
        "ssm_par": two_rows(ssm_a_log[l], ssm_dt_bias[l], SM_DT),
        "ssm_d_ch": jnp.repeat(ssm_d[l].astype(F32), SSM_HEADDIM).reshape(1, -1),
        "ssm_norm": row(ssm_norm[l]),
        "sel_ssm": _pair_selection(SM_DT, SSM_HEADDIM, SSM_PAIRS // 2),
        "w_branch": w_branch[l].astype(BF16), "w_out": w_out[l].astype(BF16),
        "norm_ff2": row(norm_ff2[l]), "w_ff2_in": w_ff2_in[l].astype(BF16),
        "w_ff2_out": w_ff2_out[l].astype(BF16), "norm_final": row(norm_final),
    }


def kernel(x_prompt, x_sample, mem_prompt, state_gdn, state_gdn_conv, state_ssm, state_ssm_conv,
           cache_mem_k, cache_mem_v, norm_ff1, w_ff1_in, w_ff1_out, norm_mix, w_in,
           gdn_conv_w, gdn_a_log, gdn_dt_bias, gdn_norm, ssm_conv_w, ssm_conv_b, ssm_a_log,
           ssm_dt_bias, ssm_d, ssm_norm, norm_mem, w_mem_kv, w_branch, w_out,
           norm_ff2, w_ff2_in, w_ff2_out, norm_final):
    assert w_in.shape[0] == 1, "the kernels implement the single-layer configuration"
    l = 0
    bp = x_prompt.shape[0]
    w = _prep_weights(l, norm_ff1, w_ff1_in, w_ff1_out, norm_mix, w_in, gdn_conv_w, gdn_a_log,
                      gdn_dt_bias, gdn_norm, ssm_conv_w, ssm_conv_b, ssm_a_log, ssm_dt_bias, ssm_d,
                      ssm_norm, w_branch, w_out, norm_ff2, w_ff2_in, w_ff2_out, norm_final)

    n_mem_tok = mem_prompt.shape[0] * mem_prompt.shape[1]
    kv = _memkv(mem_prompt.reshape(n_mem_tok, D_MODEL), _row(norm_mem[l]), w_mem_kv[l].astype(BF16))
    mk = kv[:, :MEM_WIDTH].reshape(bp, N_MEM, MEM_HEADS, MEM_HEAD_DIM)
    mv = kv[:, MEM_WIDTH:].reshape(bp, N_MEM, MEM_HEADS, MEM_HEAD_DIM)
    dtp = x_prompt.dtype
    yp, sgp, cgp, ssp, csp = _layer(
        x_prompt, mk, mv,
        jnp.zeros((bp, GDN_HEADS, GDN_DK, GDN_DV), dtp), jnp.zeros((bp, CONV_W - 1, GDN_CONV_CH), dtp),
        jnp.zeros((bp, SSM_HEADS, SSM_HEADDIM, SSM_DSTATE), dtp),
        jnp.zeros((bp, CONV_W - 1, SSM_CONV_CH), dtp), w)
    ys_, sgs, cgs, sss, css = _layer(
        x_sample, cache_mem_k[l], cache_mem_v[l], state_gdn[l], state_gdn_conv[l],
        state_ssm[l], state_ssm_conv[l], w)

    lead = lambda a: a[None]
    return (yp, ys_, lead(sgp), lead(cgp), lead(ssp), lead(csp), lead(mk), lead(mv),
            lead(sgs), lead(cgs), lead(sss), lead(css))
```

```python
import collections
import functools

import numpy as np
import jax
import jax.numpy as jnp
from jax import lax
from jax.experimental import pallas as pl
from jax.experimental.pallas import tpu as pltpu

F32 = jnp.float32
BF16 = jnp.bfloat16
EPS = 1e-6

D_MODEL = 1024
FFN_DIM = 2816
CONV_W = 4
CHUNK = 64
LANES = 128
SUBLANES = 8

GDN_HEADS = 8
GDN_PAIRS = GDN_HEADS // 2
GDN_DK = 128
GDN_DV = 128
GDN_QK = GDN_HEADS * GDN_DK
GDN_V = GDN_HEADS * GDN_DV
GDN_CONV_CH = 2 * GDN_QK + GDN_V

SSM_DINNER = 2048
SSM_HEADDIM = 64
SSM_HEADS = 32
SSM_GROUPS = 4
SSM_DSTATE = 128
SSM_CONV_CH = SSM_DINNER + 2 * SSM_GROUPS * SSM_DSTATE
SSM_PAIRS = SSM_HEADS // 2
SSM_GROUP_PAIRS = SSM_PAIRS // SSM_GROUPS
SSM_GROUP_CH = SSM_DINNER // SSM_GROUPS

N_MEM = 256
MEM_HEADS = 4
MEM_HEAD_DIM = 64
MEM_WIDTH = MEM_HEADS * MEM_HEAD_DIM

MIX_QKV = 0
MIX_XBC = MIX_QKV + GDN_CONV_CH
MIX_COLS = MIX_XBC + SSM_CONV_CH
REST_GATE = 0
REST_ZG = REST_GATE + 3 * D_MODEL
REST_ZS = REST_ZG + GDN_V
REST_QM = REST_ZS + SSM_DINNER
REST_SMALL = REST_QM + MEM_WIDTH
REST_COLS = REST_SMALL + LANES
IN_SPLITS = (GDN_CONV_CH, GDN_HEADS, GDN_HEADS, GDN_V, SSM_DINNER, SSM_CONV_CH, SSM_HEADS, MEM_WIDTH,
             3 * D_MODEL)
SM_A = 0
SM_B = GDN_HEADS
SM_DT = 2 * GDN_HEADS

VMEM_LIMIT = 56 * 1024 * 1024


def _params(semantics):
    return pltpu.CompilerParams(dimension_semantics=semantics, vmem_limit_bytes=VMEM_LIMIT)


def _mm(a, b):
    return jnp.dot(a, b, preferred_element_type=F32)


def _mm_nt(a, b):
    return lax.dot_general(a, b, (((1,), (1,)), ((), ())), preferred_element_type=F32)


def _mm_tn(a, b):
    return lax.dot_general(a, b, (((0,), (0,)), ((), ())), preferred_element_type=F32)


def _split3(x):
    hi = x.astype(BF16)
    r1 = x - hi.astype(F32)
    mid = r1.astype(BF16)
    lo = (r1 - mid.astype(F32)).astype(BF16)
    return hi, mid, lo


def _split2_lanes(x):
    hi = x.astype(BF16)
    lo = (x - hi.astype(F32)).astype(BF16)
    return jnp.concatenate([hi, lo], axis=1)


def _rms(x, g):
    return x * lax.rsqrt(jnp.mean(x * x, axis=-1, keepdims=True) + EPS) * g


def _sigmoid(x):
    return 1.0 / (1.0 + jnp.exp2(x * float(-1.0 / np.log(2.0))))


def _silu(x):
    return x * _sigmoid(x)


def _softplus(x):
    return jnp.maximum(x, 0.0) + jnp.log1p(jnp.exp(-jnp.abs(x)))


def _const_spec(shape):
    nd = len(shape)
    return pl.BlockSpec(shape, lambda *_: (0,) * nd, pipeline_mode=pl.Buffered(1))


FFN_TM = 512
MXU_WIDTH = 256
FFN_CHUNKS = ((0, 6 * MXU_WIDTH), (6 * MXU_WIDTH, FFN_DIM))


def _ffn_compute(x, g_ref, wi_ref, wo_ref):
    xn = _rms(x, g_ref[...]).astype(BF16)
    acc = None
    for lo, hi in FFN_CHUNKS:
        gate = _mm(xn, wi_ref[:, lo:hi])
        up = _mm(xn, wi_ref[:, FFN_DIM + lo:FFN_DIM + hi])
        act = (_silu(gate) * up).astype(BF16)
        part = _mm(act, wo_ref[lo:hi, :])
        acc = part if acc is None else acc + part
    return x + 0.5 * acc


def _ffn1_body(x_ref, g_ref, wi_ref, wo_ref, g2_ref, h_ref, u_ref):
    h = _ffn_compute(x_ref[...], g_ref, wi_ref, wo_ref)
    h_ref[...] = h
    u_ref[...] = _rms(h, g2_ref[...]).astype(BF16)


def _ffn1(x, g, wi, wo, g2):
    t = x.shape[0]
    tm = min(FFN_TM, t)
    row = lambda i: (i, 0)
    return pl.pallas_call(
        _ffn1_body,
        grid=(t // tm,),
        in_specs=[pl.BlockSpec((tm, D_MODEL), row), _const_spec(g.shape), _const_spec(wi.shape),
                  _const_spec(wo.shape), _const_spec(g2.shape)],
        out_specs=[pl.BlockSpec((tm, D_MODEL), row), pl.BlockSpec((tm, D_MODEL), row)],
        out_shape=[jax.ShapeDtypeStruct((t, D_MODEL), F32), jax.ShapeDtypeStruct((t, D_MODEL), BF16)],
        compiler_params=_params(("parallel",)),
        name="ffn1",
    )(x, g, wi, wo, g2)


PACK_ROWS = 128


def _pack_body(wt_ref, mix_ref, rest_ref):
    src = np.cumsum((0,) + IN_SPLITS).tolist()
    qkv, ab, _, zg, zs, xbc, dt, qm, gate = [(src[i], IN_SPLITS[i]) for i in range(len(IN_SPLITS))]
    for ref, dst, (lo, n) in ((mix_ref, MIX_QKV, qkv), (mix_ref, MIX_XBC, xbc),
                              (rest_ref, REST_GATE, gate), (rest_ref, REST_ZG, zg),
                              (rest_ref, REST_ZS, zs), (rest_ref, REST_QM, qm)):
        assert lo % SUBLANES == 0 and n % LANES == 0
        for t in range(n // LANES):
            ref[:, dst + t * LANES:dst + (t + 1) * LANES] = (
                wt_ref[lo + t * LANES:lo + (t + 1) * LANES, :].T.astype(BF16))
    assert SM_A == 0 and SM_DT == 2 * GDN_HEADS
    small = jnp.concatenate(
        [wt_ref[ab[0]:ab[0] + 2 * GDN_HEADS, :], wt_ref[dt[0]:dt[0] + SSM_HEADS, :],
         jnp.zeros((LANES - 2 * GDN_HEADS - SSM_HEADS, wt_ref.shape[1]), F32)], axis=0)
    rest_ref[:, REST_SMALL:] = small.T.astype(BF16)


def _pack_w_in(wt):
    rows = lambda i: (i, 0)
    return pl.pallas_call(
        _pack_body,
        grid=(D_MODEL // PACK_ROWS,),
        in_specs=[pl.BlockSpec((wt.shape[0], PACK_ROWS), lambda i: (0, i))],
        out_specs=[pl.BlockSpec((PACK_ROWS, MIX_COLS), rows), pl.BlockSpec((PACK_ROWS, REST_COLS), rows)],
        out_shape=[jax.ShapeDtypeStruct((D_MODEL, MIX_COLS), BF16),
                   jax.ShapeDtypeStruct((D_MODEL, REST_COLS), BF16)],
        compiler_params=_params(("parallel",)),
        name="pack_w_in",
    )(wt)


PROJ_TM = 1024
PROJ_ROWS = 256
MIX_TN = MIX_COLS // 4
REST_TN = REST_COLS // 3


def _proj_body(u_ref, w_ref, o_ref):
    o_ref[...] = _mm(u_ref[...], w_ref[...])


def _in_proj(u, w, tn, name):
    t = u.shape[0]
    tm = min(PROJ_TM, t)
    cols = w.shape[1]
    return pl.pallas_call(
        _proj_body,
        grid=(cols // tn, t // tm),
        in_specs=[pl.BlockSpec((tm, D_MODEL), lambda j, i: (i, 0)),
                  pl.BlockSpec((D_MODEL, tn), lambda j, i: (0, j))],
        out_specs=pl.BlockSpec((tm, tn), lambda j, i: (i, j)),
        out_shape=jax.ShapeDtypeStruct((t, cols), F32),
        compiler_params=_params(("parallel", "parallel")),
        name=name,
    )(u, w)


def _causal_conv_silu(xa, w, bias):
    assert CONV_W == 4 and CONV_PAD == SUBLANES
    shape = (SUBLANES, LANES)
    sub = lax.broadcasted_iota(jnp.int32, shape, 0)
    shift = lambda prev, cur, k: pltpu.roll(jnp.where(sub >= SUBLANES - k, prev, cur), k, axis=0)
    columns = []
    for c in range(xa.shape[1] // LANES):
        cols = slice(c * LANES, (c + 1) * LANES)
        w0, w1, w2, w3 = [jnp.broadcast_to(0.5 * w[j:j + 1, cols], shape) for j in range(CONV_W)]
        b = jnp.broadcast_to(0.5 * bias[:, cols], shape)
        x_prev = xa[0:SUBLANES, cols]
        u_prev = x_prev * w1 + shift(x_prev, x_prev, 1) * w0
        outs = []
        for g in range(1, xa.shape[0] // SUBLANES):
            x = xa[g * SUBLANES:(g + 1) * SUBLANES, cols]
            x1 = shift(x_prev, x, 1)
            u = x * w1 + x1 * w0
            half = (x * w3 + b) + x1 * w2 + shift(u_prev, u, 2)
            outs.append(half + half * jnp.tanh(half))
            x_prev, u_prev = x, u
        columns.append(jnp.concatenate(outs, axis=0))
    return jnp.concatenate(columns, axis=1)


CONVPROJ_TM = 512
CONVPROJ_STEPS = 3
PROJ_STRIPS = 8


def _strips(width, n):
    tiles = width // LANES
    assert width % LANES == 0 and tiles >= n
    bounds = [LANES * ((tiles * k) // n) for k in range(n + 1)]
    return [slice(bounds[k], bounds[k + 1]) for k in range(n)]


def _proj_conv_body(u_ref, wm_ref, wr_ref, cw_ref, cb_ref, cst_ref, mix_ref, rest_ref, tail_ref,
                    carry, *, tiles_per_seq):
    i = pl.program_id(1)

    @pl.when(i % tiles_per_seq == 0)
    def _():
        carry[0:CONV_PAD, :] = cst_ref[0]

    strips_m = _strips(mix_ref.shape[1], PROJ_STRIPS)
    strips_r = _strips(rest_ref.shape[1], PROJ_STRIPS)
    for r in range(u_ref.shape[0] // PROJ_ROWS):
        rows = slice(r * PROJ_ROWS, (r + 1) * PROJ_ROWS)
        u = u_ref[rows, :]
        for cm, cr in zip(strips_m, strips_r):
            carry[CONV_PAD:, cm] = _mm(u, wm_ref[:, cm])
            rest_ref[rows, cr] = _mm(u, wr_ref[:, cr])
            mix_ref[rows, cm] = _causal_conv_silu(carry.at[:, cm], cw_ref[:, cm], cb_ref[:, cm])
            carry[0:CONV_PAD, cm] = carry[PROJ_ROWS:, cm]
    tail_ref[0] = carry[0:CONV_PAD, :]


def _in_proj_conv(u, w_mix, w_rest, cw, cb, cst8, seq):
    t = u.shape[0]
    tm = min(CONVPROJ_TM, seq)
    assert seq % tm == 0 and tm % PROJ_ROWS == 0
    tps = seq // tm
    tn_m = MIX_COLS // CONVPROJ_STEPS
    tn_r = REST_COLS // CONVPROJ_STEPS
    col = lambda j, i: (0, j)
    tile = lambda j, i: (i, j)
    seq_blk = lambda j, i: (i // tps, 0, j)
    return pl.pallas_call(
        functools.partial(_proj_conv_body, tiles_per_seq=tps),
        grid=(CONVPROJ_STEPS, t // tm),
        in_specs=[pl.BlockSpec((tm, D_MODEL), lambda j, i: (i, 0)),
                  pl.BlockSpec((D_MODEL, tn_m), col), pl.BlockSpec((D_MODEL, tn_r), col),
                  pl.BlockSpec((CONV_W, tn_m), col), pl.BlockSpec((1, tn_m), col),
                  pl.BlockSpec((1, CONV_PAD, tn_m), seq_blk)],
        out_specs=[pl.BlockSpec((tm, tn_m), tile), pl.BlockSpec((tm, tn_r), tile),
                   pl.BlockSpec((1, CONV_PAD, tn_m), seq_blk)],
        out_shape=[jax.ShapeDtypeStruct((t, MIX_COLS), F32), jax.ShapeDtypeStruct((t, REST_COLS), F32),
                   jax.ShapeDtypeStruct((t // seq, CONV_PAD, MIX_COLS), F32)],
        scratch_shapes=[pltpu.VMEM((CONV_PAD + PROJ_ROWS, tn_m), F32)],
        compiler_params=_params(("parallel", "arbitrary")),
        name="in_proj_conv",
    )(u, w_mix, w_rest, cw, cb, cst8)


def _memkv_body(x_ref, g_ref, w_ref, o_ref):
    o_ref[...] = _mm(_rms(x_ref[...], g_ref[...]).astype(BF16), w_ref[...])


MEMKV_TM = 512


def _memkv(mem, g, w):
    t = mem.shape[0]
    tm = min(MEMKV_TM, t)
    row = lambda i: (i, 0)
    return pl.pallas_call(
        _memkv_body,
        grid=(t // tm,),
        in_specs=[pl.BlockSpec((tm, D_MODEL), row), _const_spec(g.shape), _const_spec(w.shape)],
        out_specs=pl.BlockSpec((tm, 2 * MEM_WIDTH), row),
        out_shape=jax.ShapeDtypeStruct((t, 2 * MEM_WIDTH), F32),
        compiler_params=_params(("parallel",)),
        name="memkv",
    )(mem, g, w)


MERGE_TM = 256


def _merge_ffn2_body(h_ref, yg_ref, ys_ref, ym_ref, gate_ref, wb_ref, wout_ref,
                     g_ref, wi_ref, wo_ref, gf_ref, y_ref):
    gates = gate_ref[...]
    merged = (_sigmoid(gates[:, 0:D_MODEL]) * _mm(yg_ref[...], wb_ref[0:GDN_V, :])
              + _sigmoid(gates[:, D_MODEL:2 * D_MODEL])
              * _mm(ys_ref[...], wb_ref[GDN_V:GDN_V + SSM_DINNER, :])
              + _sigmoid(gates[:, 2 * D_MODEL:3 * D_MODEL])
              * _mm(ym_ref[...], wb_ref[GDN_V + SSM_DINNER:, :]))
    h = h_ref[...] + _mm(merged.astype(BF16), wout_ref[...])
    h = _ffn_compute(h, g_ref, wi_ref, wo_ref)
    y_ref[...] = _rms(h, gf_ref[...])


def _merge_ffn2(h, yg, ys, ym, proj, wb, wout, g, wi, wo, gf):
    t = h.shape[0]
    tm = min(MERGE_TM, t)
    row = lambda i: (i, 0)
    gate_blk = REST_GATE // (3 * D_MODEL)
    return pl.pallas_call(
        _merge_ffn2_body,
        grid=(t // tm,),
        in_specs=[pl.BlockSpec((tm, D_MODEL), row), pl.BlockSpec((tm, GDN_V), row),
                  pl.BlockSpec((tm, SSM_DINNER), row), pl.BlockSpec((tm, MEM_WIDTH), row),
                  pl.BlockSpec((tm, 3 * D_MODEL), lambda i: (i, gate_blk)),
                  _const_spec(wb.shape), _const_spec(wout.shape), _const_spec(g.shape),
                  _const_spec(wi.shape), _const_spec(wo.shape), _const_spec(gf.shape)],
        out_specs=pl.BlockSpec((tm, D_MODEL), row),
        out_shape=jax.ShapeDtypeStruct((t, D_MODEL), F32),
        compiler_params=_params(("parallel",)),
        name="merge_ffn2",
    )(h, yg, ys, ym, proj, wb, wout, g, wi, wo, gf)


MEMATTN_TL = 512
MEMATTN_NSEQ = 16


def _memattn_body(q_ref, k_ref, v_ref, y_ref, *, nseq, rows):
    lane = lax.broadcasted_iota(jnp.int32, (rows, MEM_WIDTH), 1)
    masks = [(lane >= hh * MEM_HEAD_DIM) & (lane < (hh + 1) * MEM_HEAD_DIM) for hh in range(MEM_HEADS)]
    seqs = range(nseq)
    q = [q_ref[s * rows:(s + 1) * rows, :] for s in seqs]
    q4 = [jnp.concatenate([jnp.where(m, x, 0.0) for m in masks], axis=0).astype(BF16) for x in q]
    sc = [_mm(q4[s], k_ref[s].astype(BF16)) * (MEM_HEAD_DIM ** -0.5) for s in seqs]
    p = [jnp.exp(x - jnp.max(x, axis=-1, keepdims=True)) for x in sc]
    p = [(x / jnp.sum(x, axis=-1, keepdims=True)).astype(BF16) for x in p]
    o4 = [_mm_nt(p[s], v_ref[s].astype(BF16)) for s in seqs]
    ys = []
    for x in o4:
        y = jnp.where(masks[0], x[0:rows], 0.0)
        for hh in range(1, MEM_HEADS):
            y = y + jnp.where(masks[hh], x[hh * rows:(hh + 1) * rows], 0.0)
        ys.append(y)
    y_ref[...] = jnp.concatenate(ys, axis=0).astype(BF16)


def _memattn(proj, k, v, bsz, seq):
    if seq >= MEMATTN_TL:
        nseq, rows, nt = 1, MEMATTN_TL, seq // MEMATTN_TL
    else:
        nseq, rows, nt = MEMATTN_NSEQ, seq, 1
    assert bsz % nseq == 0 and seq % rows == 0
    tl = nseq * rows
    qm_blk = REST_QM // MEM_WIDTH
    kv_spec = pl.BlockSpec((nseq, MEM_WIDTH, N_MEM), lambda b, i: (b, 0, 0))
    return pl.pallas_call(
        functools.partial(_memattn_body, nseq=nseq, rows=rows),
        grid=(bsz // nseq, nt),
        in_specs=[pl.BlockSpec((tl, MEM_WIDTH), lambda b, i: (b * nt + i, qm_blk)), kv_spec, kv_spec],
        out_specs=pl.BlockSpec((tl, MEM_WIDTH), lambda b, i: (b * nt + i, 0)),
        out_shape=jax.ShapeDtypeStruct((bsz * seq, MEM_WIDTH), BF16),
        compiler_params=_params(("parallel", "arbitrary")),
        name="memattn",
    )(proj, k, v)


SEQ_TL = 256
LONG_NSEQ = 2
SHORT_NSEQ = 8
CONV_PAD = SUBLANES
MIN_MXU_ROWS = 16

Geom = collections.namedtuple("Geom", ["nseq", "nch", "vrows"])


def _geom(bsz, seq, short_nseq=SHORT_NSEQ):
    if seq >= CHUNK:
        assert seq % SEQ_TL == 0
        nseq = LONG_NSEQ if bsz % LONG_NSEQ == 0 else 1
        return Geom(nseq, SEQ_TL // CHUNK, CHUNK), seq // SEQ_TL
    if bsz % short_nseq:
        short_nseq = SHORT_NSEQ
    assert bsz % short_nseq == 0 and seq % SUBLANES == 0 and (short_nseq * seq) % CHUNK == 0
    return Geom(short_nseq, 1, seq), 1


class _TokenView:
    def __init__(self, ref):
        self.ref = ref

    def _locate(self, rows):
        per = self.ref.shape[1]
        s = rows.start // per
        assert rows.stop <= (s + 1) * per
        return s, slice(rows.start - s * per, rows.stop - s * per)

    def __getitem__(self, idx):
        rows, cols = idx
        if len(self.ref.shape) == 2:
            return self.ref[rows, cols]
        if rows == slice(None):
            v = self.ref[:, :, cols]
            return v.reshape(v.shape[0] * v.shape[1], v.shape[2])
        s, r = self._locate(rows)
        return self.ref[s, r, cols]

    def __setitem__(self, idx, value):
        rows, cols = idx
        if len(self.ref.shape) == 2:
            self.ref[rows, cols] = value
        elif rows == slice(None):
            nseq, per = self.ref.shape[0], self.ref.shape[1]
            self.ref[:, :, cols] = value.reshape(nseq, per, value.shape[1])
        else:
            s, r = self._locate(rows)
            self.ref[s, r, cols] = value


def _short_conv_silu(x_ref, cst_ref, cw_ref, cb_ref, act, geom):
    assert geom.nch == 1 and geom.vrows < CHUNK and CONV_PAD == SUBLANES
    vr = geom.vrows
    sub = lax.broadcasted_iota(jnp.int32, (SUBLANES, LANES), 0)
    for cb in range(x_ref.shape[1] // LANES):
        cols = slice(cb * LANES, (cb + 1) * LANES)
        planes = [cst_ref[j, :, cols] for j in range(CONV_W - 1)]
        for s in range(geom.nseq):
            before = jnp.zeros((SUBLANES, LANES), F32)
            for j, plane in enumerate(planes):
                before = jnp.where(sub == SUBLANES - (CONV_W - 1) + j,
                                   jnp.broadcast_to(plane[s:s + 1, :], (SUBLANES, LANES)), before)
            xa = jnp.concatenate([before, x_ref[s * vr:(s + 1) * vr, cols]], axis=0)
            act[s * vr:(s + 1) * vr, cols] = _causal_conv_silu(xa, cw_ref[:, cols], cb_ref[:, cols])


def _pair_masks(vr):
    row = lax.broadcasted_iota(jnp.int32, (CHUNK, LANES), 0)
    lane = lax.broadcasted_iota(jnp.int32, (CHUNK, LANES), 1)
    col = jnp.where(lane < CHUNK, lane, lane - CHUNK)
    same = (row // vr) == (col // vr)
    return row, lane, col, same


def _block_diag(x, left):
    zero = jnp.zeros_like(x)
    return jnp.concatenate([jnp.where(left, x, zero), jnp.where(left, zero, x)], axis=0)


def _chunk_cumsums(gates, nprob, vr):
    r = lax.broadcasted_iota(jnp.int32, (CHUNK, CHUNK), 0)
    c = lax.broadcasted_iota(jnp.int32, (CHUNK, CHUNK), 1)
    same = (r // vr) == (c // vr)
    sums = jnp.concatenate([jnp.where(same & (r >= c), 1.0, 0.0), jnp.where(same, 1.0, 0.0)],
                           axis=0).astype(BF16)
    sums3 = jnp.concatenate([sums, sums, sums], axis=1)
    pieces = [jnp.concatenate(_split3(gates[pb * CHUNK:(pb + 1) * CHUNK, 0:LANES]), axis=0)
              for pb in range(nprob)]
    out = _mm(sums3, jnp.concatenate(pieces, axis=1))
    cols = lambda pb: slice(pb * LANES, (pb + 1) * LANES)
    return ([out[0:CHUNK, cols(pb)] for pb in range(nprob)],
            [out[CHUNK:, cols(pb)] for pb in range(nprob)])


def _rows_to_columns(e):
    pieces = jnp.concatenate(_split3(e), axis=0)
    pad = jnp.zeros((MIN_MXU_ROWS - pieces.shape[0], e.shape[1]), BF16)
    row = lax.broadcasted_iota(jnp.int32, (MIN_MXU_ROWS, LANES), 0)
    ones = jnp.where(row < pieces.shape[0], 1.0, 0.0).astype(BF16)
    return _mm_tn(jnp.concatenate([pieces, pad], axis=0), ones)


def _pad_rows(x):
    if x.shape[0] >= MIN_MXU_ROWS:
        return x
    return jnp.concatenate([x, jnp.zeros((MIN_MXU_ROWS - x.shape[0], x.shape[1]), x.dtype)], axis=0)


def _row_form(gc):
    return jnp.concatenate([gc, pltpu.roll(gc, LANES - 1, axis=1)], axis=0).T


def _gdn_body(*refs, geom):
    if geom.vrows == CHUNK:
        (x_ref, sm_ref, z_ref, s0_ref, par_ref, gn_ref, selp_ref, y_ref, sout_ref,
         gates, s_scr, u_scr, w_scr, qd_scr, kd_scr, qk_scr, egl_scr, o_scr) = refs
        act = _TokenView(x_ref)
    else:
        (x_ref, sm_ref, z_ref, cst_ref, cw_ref, cb_ref, s0_ref, par_ref, gn_ref, selp_ref, y_ref,
         sout_ref, act, gates, s_scr, u_scr, w_scr, qd_scr, kd_scr, qk_scr, egl_scr, o_scr) = refs
        _short_conv_silu(x_ref, cst_ref, cw_ref, cb_ref, act, geom)
    tile = pl.program_id(1)
    first = tile == 0
    vr = geom.vrows
    nsub = CHUNK // vr
    nprob = geom.nseq * geom.nch // nsub

    @pl.when(first)
    def _():
        s_scr[...] = s0_ref[...]

    sm = _TokenView(sm_ref)[:, :]
    lane_g = lax.broadcasted_iota(jnp.int32, sm.shape, 1)
    head_lane = lane_g < GDN_HEADS
    gates[:, 0:LANES] = jnp.where(
        head_lane, -jnp.exp(par_ref[0:1, :]) * _softplus(sm + par_ref[1:2, :]), 0.0)
    gates[:, LANES:] = jnp.where(head_lane, pltpu.roll(_sigmoid(sm), LANES - SM_B, axis=1), 0.0)

    row, lane, col, same = _pair_masks(vr)
    left = lane < CHUNK
    incl = same & (row >= col)
    strict = same & (row > col)
    eye = jnp.where(row == col, 1.0, 0.0).astype(F32)
    zeros_h = jnp.zeros((CHUNK, LANES), F32)
    hcols = lambda h: slice(h * LANES, (h + 1) * LANES)
    prows = lambda pb, n=CHUNK: slice(pb * n, (pb + 1) * n)

    gcs, gls = _chunk_cumsums(gates, nprob, vr)
    lhs, xts = [], []
    for pb in range(nprob):
        gc = gcs[pb]
        quantities = jnp.concatenate(
            [gc, gates[prows(pb), LANES:2 * LANES], jnp.exp(gc), jnp.exp(gls[pb] - gc)], axis=0)
        lhs.append(_split2_lanes(quantities))
        xts.append(_row_form(gc))

    def l2n(t):
        return t * lax.rsqrt(jnp.sum(t * t, axis=-1, keepdims=True) + EPS)

    def stacked(e, qi):
        return jnp.concatenate([e[qi * CHUNK:(qi + 1) * CHUNK, 0:LANES],
                                e[qi * CHUNK:(qi + 1) * CHUNK, LANES:]], axis=0)

    def phase1(pbs):
        items = [(pb, p) for pb in pbs for p in range(GDN_PAIRS)]
        exp_ = [_mm(lhs[pb], selp_ref[p]) for pb, p in items]
        decay_l, kq_l, rhs_l, qd_l, kd_l = [], [], [], [], []
        for (pb, p), e in zip(items, exp_):
            ha, hb = 2 * p, 2 * p + 1
            qa, qb = [l2n(act[prows(pb), hcols(h)]) * (GDN_DK ** -0.5) for h in (ha, hb)]
            ka, kb = [l2n(act[prows(pb), hcols(GDN_HEADS + h)]) for h in (ha, hb)]
            va, vb = [act[prows(pb), hcols(2 * GDN_HEADS + h)] for h in (ha, hb)]
            beta2, eg2, kdec2 = stacked(e, 1), stacked(e, 2), stacked(e, 3)
            k2 = jnp.concatenate([ka, kb], axis=0)
            kbeta2 = k2 * beta2
            gci = jnp.where(left, e[0:CHUNK, 0:LANES], e[0:CHUNK, LANES:])
            gcj = jnp.broadcast_to(xts[pb][ha:ha + 1, :], (CHUNK, LANES))
            decay_l.append(jnp.where(incl, jnp.exp(jnp.where(incl, gci - gcj, 0.0)), 0.0))
            k_bd = jnp.concatenate([jnp.concatenate([ka, zeros_h], axis=1),
                                    jnp.concatenate([zeros_h, kb], axis=1)], axis=0).astype(BF16)
            kbq = jnp.concatenate([jnp.concatenate([kbeta2[0:CHUNK], kbeta2[CHUNK:]], axis=1),
                                   jnp.concatenate([qa, qb], axis=1)], axis=0).astype(BF16)
            kq_l.append(_mm_nt(kbq, k_bd))
            rhs_l.append(jnp.concatenate(
                [jnp.concatenate([va, vb], axis=0) * beta2, kbeta2 * eg2], axis=1).astype(BF16))
            qd_l.append(jnp.concatenate([qa, qb], axis=0) * eg2)
            kd_l.append(k2 * kdec2)
            for sub in range(nsub):
                last = 2 * CHUNK + (sub + 1) * vr - 1
                slot = prows(pb * nsub + sub, SUBLANES)
                egl_scr[slot, hcols(ha)] = jnp.broadcast_to(e[last:last + 1, 0:LANES], (SUBLANES, LANES))
                egl_scr[slot, hcols(hb)] = jnp.broadcast_to(e[last:last + 1, LANES:], (SUBLANES, LANES))

        n_pow = [-jnp.where(strict, kq[0:CHUNK] * d, 0.0) for kq, d in zip(kq_l, decay_l)]
        t_inv = [eye + n for n in n_pow]
        for _ in range(int(np.ceil(np.log2(vr))) - 1):
            n_pow = [_mm(n.astype(BF16), _block_diag(n, left).astype(BF16)) for n in n_pow]
            t_inv = [t + _mm(t.astype(BF16), _block_diag(n, left).astype(BF16))
                     for t, n in zip(t_inv, n_pow)]

        uw_l = [_mm(_block_diag(t, left).astype(BF16), rhs) for t, rhs in zip(t_inv, rhs_l)]
        for (pb, p), uw, kq, d, qd, kd in zip(items, uw_l, kq_l, decay_l, qd_l, kd_l):
            qk_scr[prows(pb, 2 * CHUNK), hcols(p)] = _block_diag(kq[CHUNK:] * d, left).astype(BF16)
            for idx, h in enumerate((2 * p, 2 * p + 1)):
                sl = slice(idx * CHUNK, (idx + 1) * CHUNK)
                u_scr[prows(pb), hcols(h)] = uw[sl, 0:LANES]
                w_scr[prows(pb), hcols(h)] = uw[sl, LANES:]
                qd_scr[prows(pb), hcols(h)] = qd[sl]
                kd_scr[prows(pb), hcols(h)] = kd[sl]

    def phase2(c):
        sh = [(s, h) for s in range(geom.nseq) for h in range(GDN_HEADS)]
        trows = lambda s: prows(s * geom.nch + c, vr)
        wqs = [_mm(jnp.concatenate([w_scr[trows(s), hcols(h)], qd_scr[trows(s), hcols(h)]],
                                   axis=0).astype(BF16), s_scr[s, h].astype(BF16)) for s, h in sh]
        vnew = [u_scr[trows(s), hcols(h)] - w[0:vr] for (s, h), w in zip(sh, wqs)]
        for (s, h), vn in zip(sh, vnew):
            slot = (s * geom.nch + c) * SUBLANES
            s_scr[s, h] = (s_scr[s, h] * egl_scr[slot:slot + 1, hcols(h)]
                           + _mm_tn(_pad_rows(kd_scr[trows(s), hcols(h)]).astype(BF16),
                                    _pad_rows(vn).astype(BF16)))
        for pb in range(c * nsub, geom.nseq * geom.nch, geom.nch * nsub):
            subs = [pb // geom.nch + j for j in range(nsub)]
            for p in range(GDN_PAIRS):
                heads = (2 * p, 2 * p + 1)
                vn2 = jnp.concatenate([vnew[s * GDN_HEADS + h] for h in heads for s in subs],
                                      axis=0).astype(BF16)
                intra = _mm(qk_scr[prows((pb // nsub), 2 * CHUNK), hcols(p)], vn2)
                for idx, h in enumerate(heads):
                    for j, s in enumerate(subs):
                        r0 = idx * CHUNK + j * vr
                        o = wqs[s * GDN_HEADS + h][vr:] + intra[r0:r0 + vr]
                        o_scr[trows(s), hcols(h)] = o

    phase1(range(nprob))
    for c in range(geom.nch):
        phase2(c)

    for h in range(GDN_HEADS):
        y = _rms(o_scr[:, hcols(h)], gn_ref[...]) * _silu(_TokenView(z_ref)[:, hcols(h)])
        _TokenView(y_ref)[:, hcols(h)] = y.astype(BF16)

    @pl.when(tile == pl.num_programs(1) - 1)
    def _():
        sout_ref[...] = s_scr[...]


def _conv_operands(conv, geom, width, col_blk):
    if conv is None:
        return [], [], []
    state, cw, cb = conv
    specs = [pl.BlockSpec((CONV_W - 1, geom.nseq, width), lambda b, i: (0, b, 0)),
             pl.BlockSpec((CONV_W, width), lambda b, i: (0, col_blk)),
             pl.BlockSpec((1, width), lambda b, i: (0, col_blk))]
    return [state, cw, cb], specs, [pltpu.VMEM((geom.nseq * geom.vrows, width), F32)]


def _token_layout(geom, bsz, seq, nt):
    tls = geom.nch * geom.vrows
    if geom.vrows == CHUNK:
        view = lambda a: a.reshape(bsz, seq, a.shape[-1])
        spec = lambda width, blk: pl.BlockSpec((geom.nseq, tls, width), lambda b, i: (b, i, blk))
        return view, spec, lambda width: (bsz, seq, width)
    spec = lambda width, blk: pl.BlockSpec((geom.nseq * tls, width), lambda b, i: (b * nt + i, blk))
    return (lambda a: a), spec, lambda width: (bsz * seq, width)


GDN_SHORT_NSEQ = 16


def _gdn(mix, rest, conv, s0, par, gn, selp, bsz, seq):
    geom, nt = _geom(bsz, seq, GDN_SHORT_NSEQ)
    tl = geom.nseq * geom.nch * geom.vrows
    nprob = tl // CHUNK
    view, tok, out_shape = _token_layout(geom, bsz, seq, nt)
    st_spec = pl.BlockSpec((geom.nseq, GDN_HEADS, GDN_DK, GDN_DV), lambda b, i: (b, 0, 0, 0))
    conv_ops, conv_specs, conv_scratch = _conv_operands(conv, geom, GDN_CONV_CH, MIX_QKV // GDN_CONV_CH)
    y, s_new = pl.pallas_call(
        functools.partial(_gdn_body, geom=geom),
        grid=(bsz // geom.nseq, nt),
        in_specs=[tok(GDN_CONV_CH, MIX_QKV // GDN_CONV_CH), tok(LANES, REST_SMALL // LANES),
                  tok(GDN_V, REST_ZG // GDN_V)] + conv_specs + [
                  st_spec, _const_spec(par.shape), _const_spec(gn.shape), _const_spec(selp.shape)],
        out_specs=[tok(GDN_V, 0), st_spec],
        out_shape=[jax.ShapeDtypeStruct(out_shape(GDN_V), BF16),
                   jax.ShapeDtypeStruct((bsz, GDN_HEADS, GDN_DK, GDN_DV), F32)],
        scratch_shapes=conv_scratch + [
            pltpu.VMEM((tl, 2 * LANES), F32),
            pltpu.VMEM((geom.nseq, GDN_HEADS, GDN_DK, GDN_DV), F32),
            pltpu.VMEM((tl, GDN_V), F32),
            pltpu.VMEM((tl, GDN_V), F32),
            pltpu.VMEM((tl, GDN_V), F32),
            pltpu.VMEM((tl, GDN_V), F32),
            pltpu.VMEM((nprob * 2 * CHUNK, GDN_PAIRS * LANES), BF16),
            pltpu.VMEM((geom.nseq * geom.nch * SUBLANES, GDN_V), F32),
            pltpu.VMEM((tl, GDN_V), F32)],
        compiler_params=_params(("parallel", "arbitrary")),
        name="gdn",
    )(view(mix), view(rest), view(rest), *conv_ops, s0, par, gn, selp)
    return y.reshape(bsz * seq, GDN_V), s_new


def _ssd_body(*refs, geom):
    if geom.vrows == CHUNK:
        (x_ref, sm_ref, z_ref, h0_ref, par_ref, dch_ref, nw_ref, seld_ref, y_ref, hout_ref,
         gates, h_scr, o_scr) = refs
        act = _TokenView(x_ref)
        tile = pl.program_id(1)

        @pl.when(tile == 0)
        def _():
            for s in range(geom.nseq):
                for p in range(SSM_PAIRS):
                    h_scr[s, p] = h0_ref[s, p].T
    else:
        (x_ref, sm_ref, z_ref, cst_ref, cw_ref, cb_ref, h0_ref, par_ref, dch_ref, nw_ref, seld_ref,
         y_ref, hout_ref, act, gates, o_scr) = refs
        _short_conv_silu(x_ref, cst_ref, cw_ref, cb_ref, act, geom)
    vr = geom.vrows
    nsub = CHUNK // vr
    nprob = geom.nseq * geom.nch // nsub

    sm = _TokenView(sm_ref)[:, :]
    lane_g = lax.broadcasted_iota(jnp.int32, sm.shape, 1)
    is_dt = (lane_g >= SM_DT) & (lane_g < SM_DT + SSM_HEADS)
    dt = jnp.where(is_dt, _softplus(sm + par_ref[1:2, :]), 0.0)
    gates[:, 0:LANES] = dt * jnp.where(is_dt, -jnp.exp(par_ref[0:1, :]), 0.0)
    gates[:, LANES:] = dt

    row, lane, col, same = _pair_masks(vr)
    left = lane < CHUNK
    incl = same & (row >= col)
    pcols = lambda p: slice(p * LANES, (p + 1) * LANES)
    prows = lambda pb, n=CHUNK: slice(pb * n, (pb + 1) * n)
    quarter = lambda e, qi, j: e[qi * CHUNK:(qi + 1) * CHUNK, j * LANES:(j + 1) * LANES]
    srows = lambda a, j: a[j * vr:(j + 1) * vr]
    mxu_rows = lambda a: _pad_rows(a).astype(BF16)
    b_off = SSM_DINNER
    c_off = SSM_DINNER + SSM_GROUPS * SSM_DSTATE

    gcs, gls = _chunk_cumsums(gates, nprob, vr)
    for pb in range(nprob):
        seqs = [(pb * nsub + j) // geom.nch for j in range(nsub)]
        gc = gcs[pb]
        dt_c = gates[prows(pb), LANES:2 * LANES]
        lhs = _split2_lanes(jnp.concatenate(
            [gc, dt_c, jnp.exp(gc), jnp.exp(gls[pb] - gc) * dt_c], axis=0))
        xt = _row_form(gc)
        exp_ = [_mm(lhs, seld_ref[d]) for d in range(SSM_PAIRS // 2)]

        b_f = [act[prows(pb), b_off + g * SSM_DSTATE:b_off + (g + 1) * SSM_DSTATE]
               for g in range(SSM_GROUPS)]
        c_f = [act[prows(pb), c_off + g * SSM_DSTATE:c_off + (g + 1) * SSM_DSTATE]
               for g in range(SSM_GROUPS)]
        b_bf = [b.astype(BF16) for b in b_f]
        c_bf = [c.astype(BF16) for c in c_f]
        cb2 = [_mm_nt(c, jnp.concatenate([b, b], axis=0)) for b, c in zip(b_bf, c_bf)]

        pairs = range(SSM_PAIRS)
        grp = lambda p: p // SSM_GROUP_PAIRS
        x_l = [act[prows(pb), pcols(p)] for p in pairs]
        gc_l = [quarter(exp_[p // 2], 0, p % 2) for p in pairs]
        eg_l = [quarter(exp_[p // 2], 2, p % 2) for p in pairs]
        xdt_l = [x * quarter(exp_[p // 2], 1, p % 2) for p, x in zip(pairs, x_l)]
        xw_l = [x * quarter(exp_[p // 2], 3, p % 2) for p, x in zip(pairs, x_l)]
        m_l = []
        for p in pairs:
            r = SM_DT + 2 * p
            gcj = jnp.broadcast_to(xt[r:r + 1, :], (CHUNK, LANES))
            decay = jnp.where(incl, jnp.exp(jnp.where(incl, gc_l[p] - gcj, 0.0)), 0.0)
            m_l.append((cb2[grp(p)] * decay).astype(BF16))
        intra = [_mm(m, _block_diag(xdt, left).astype(BF16)) for m, xdt in zip(m_l, xdt_l)]
        last_row = lambda a, j: a[(j + 1) * vr - 1:(j + 1) * vr, :]
        if geom.vrows == CHUNK:
            dstate = [_mm_tn(b_bf[grp(p)], xw_l[p].astype(BF16)) for p in pairs]
            inter = [_mm(c_bf[grp(p)], h_scr[seqs[0], p].astype(BF16)) for p in pairs]
            for p in pairs:
                h_scr[seqs[0], p] = h_scr[seqs[0], p] * last_row(eg_l[p], 0) + dstate[p]
        else:
            gp = SSM_GROUP_PAIRS
            jg = [(j, g) for j in range(nsub) for g in range(SSM_GROUPS)]
            gslice = lambda g: slice(g * gp, (g + 1) * gp)
            lanes = lambda a_l, g, f: jnp.concatenate([f(a) for a in a_l[gslice(g)]], axis=1)
            inter_g = []
            for j, g in jg:
                ds = _mm_tn(mxu_rows(lanes(xw_l, g, lambda a: srows(a, j))), mxu_rows(srows(b_f[g], j)))
                sc = _rows_to_columns(lanes(eg_l, g, lambda a: last_row(a, j)))
                h = h0_ref[seqs[j], gslice(g)].reshape(gp * LANES, SSM_DSTATE)
                inter_g.append(_mm_nt(mxu_rows(srows(c_f[g], j)), h.astype(BF16))[0:vr])
                hout_ref[seqs[j], gslice(g)] = (h * sc + ds).reshape(gp, LANES, SSM_DSTATE)
            inter = [jnp.concatenate(
                [inter_g[j * SSM_GROUPS + grp(p)][:, (p % gp) * LANES:(p % gp + 1) * LANES]
                 for j in range(nsub)], axis=0) for p in pairs]

        for g in range(SSM_GROUPS):
            ys = []
            ssq = None
            for p in range(g * SSM_GROUP_PAIRS, (g + 1) * SSM_GROUP_PAIRS):
                y = intra[p] + inter[p] * eg_l[p] + dch_ref[:, pcols(p)] * x_l[p]
                y = y * _silu(_TokenView(z_ref)[prows(pb), pcols(p)])
                ys.append(y)
                sq = jnp.sum(y * y, axis=-1, keepdims=True)
                ssq = sq if ssq is None else ssq + sq
            inv = lax.rsqrt(ssq * (1.0 / SSM_GROUP_CH) + EPS)
            for pp, y in enumerate(ys):
                o_scr[prows(pb), pcols(g * SSM_GROUP_PAIRS + pp)] = y * inv

    _TokenView(y_ref)[:, slice(None)] = (o_scr[...] * nw_ref[...]).astype(BF16)

    if geom.vrows == CHUNK:
        @pl.when(tile == pl.num_programs(1) - 1)
        def _():
            for s in range(geom.nseq):
                for p in range(SSM_PAIRS):
                    hout_ref[s, p] = h_scr[s, p].T


def _ssd(mix, rest, conv, h0, par, dch, nw, seld, bsz, seq):
    geom, nt = _geom(bsz, seq)
    tl = geom.nseq * geom.nch * geom.vrows
    view, tok, out_shape = _token_layout(geom, bsz, seq, nt)
    st_shape = (geom.nseq, SSM_PAIRS, 2 * SSM_HEADDIM, SSM_DSTATE)
    st_spec = pl.BlockSpec(st_shape, lambda b, i: (b, 0, 0, 0))
    conv_ops, conv_specs, conv_scratch = _conv_operands(conv, geom, SSM_CONV_CH, MIX_XBC // SSM_CONV_CH)
    y, h_new = pl.pallas_call(
        functools.partial(_ssd_body, geom=geom),
        grid=(bsz // geom.nseq, nt),
        in_specs=[tok(SSM_CONV_CH, MIX_XBC // SSM_CONV_CH), tok(LANES, REST_SMALL // LANES),
                  tok(SSM_DINNER, REST_ZS // SSM_DINNER)] + conv_specs + [
                  st_spec, _const_spec(par.shape), _const_spec(dch.shape), _const_spec(nw.shape),
                  _const_spec(seld.shape)],
        out_specs=[tok(SSM_DINNER, 0), st_spec],
        out_shape=[jax.ShapeDtypeStruct(out_shape(SSM_DINNER), BF16),
                   jax.ShapeDtypeStruct((bsz,) + st_shape[1:], F32)],
        scratch_shapes=conv_scratch + [pltpu.VMEM((tl, 2 * LANES), F32)] + (
            [pltpu.VMEM((geom.nseq, SSM_PAIRS, SSM_DSTATE, 2 * SSM_HEADDIM), F32)]
            if conv is None else []) + [
            pltpu.VMEM((tl, SSM_DINNER), F32)],
        compiler_params=_params(("parallel", "arbitrary")),
        name="ssd",
    )(view(mix), view(rest), view(rest), *conv_ops, h0, par, dch, nw, seld)
    return y.reshape(bsz * seq, SSM_DINNER), h_new


def _pad_lanes(v, offset):
    out = jnp.zeros((LANES,), F32)
    return out.at[offset:offset + v.shape[0]].set(v.astype(F32))


def _pair_selection(first_lane, lanes_per_head, n_mats):
    heads_per_mat = 2 * LANES // lanes_per_head
    sel = np.zeros((n_mats, 2 * LANES, 2 * LANES), np.float32)
    for m in range(n_mats):
        for j in range(heads_per_mat):
            src = first_lane + m * heads_per_mat + j
            sel[m, src, j * lanes_per_head:(j + 1) * lanes_per_head] = 1.0
            sel[m, LANES + src, j * lanes_per_head:(j + 1) * lanes_per_head] = 1.0
    return jnp.asarray(sel, BF16)


def _pad_conv_state(st):
    bsz, _, ch = st.shape
    return jnp.concatenate([jnp.zeros((bsz, CONV_PAD - (CONV_W - 1), ch), F32), st.astype(F32)], axis=1)


def _layer(x, mem_k, mem_v, s_gdn, c_gdn, s_ssm, c_ssm, w):
    bsz, seq, _ = x.shape
    t = bsz * seq
    xf = x.reshape(t, D_MODEL)
    h, u = _ffn1(xf, w["norm_ff1"], w["w_ff1_in"], w["w_ff1_out"], w["norm_mix"])
    if seq >= CHUNK:
        cst8 = jnp.concatenate([_pad_conv_state(c_gdn), _pad_conv_state(c_ssm)], axis=2)
        mix, rest, tail = _in_proj_conv(u, w["w_mix"], w["w_rest"], w["conv_w"], w["conv_b"], cst8, seq)
        conv_gdn = conv_ssm = None
    else:
        rest = _in_proj(u, w["w_rest"], REST_TN, "in_proj_rest")
        mix = _in_proj(u, w["w_mix"], MIX_TN, "in_proj_mix")
        tail = mix.reshape(bsz, seq, MIX_COLS)[:, seq - CONV_PAD:, :]
        taps_first = lambda c: jnp.swapaxes(c.astype(F32), 0, 1)
        conv_gdn = (taps_first(c_gdn), w["conv_w"], w["conv_b"])
        conv_ssm = (taps_first(c_ssm), w["conv_w"], w["conv_b"])

    yg, s_gdn_new = _gdn(mix, rest, conv_gdn, s_gdn.astype(F32), w["gdn_par"], w["gdn_norm"],
                         w["sel_gdn"], bsz, seq)
    h0 = s_ssm.astype(F32).reshape(bsz, SSM_PAIRS, 2 * SSM_HEADDIM, SSM_DSTATE)
    ys, s_ssm_new = _ssd(mix, rest, conv_ssm, h0, w["ssm_par"], w["ssm_d_ch"], w["ssm_norm"],
                         w["sel_ssm"], bsz, seq)
    transposed = lambda m: jnp.swapaxes(m.reshape(bsz, N_MEM, MEM_WIDTH), 1, 2)
    ym = _memattn(rest, transposed(mem_k), transposed(mem_v), bsz, seq)
    y = _merge_ffn2(h, yg, ys, ym, rest, w["w_branch"], w["w_out"], w["norm_ff2"],
                    w["w_ff2_in"], w["w_ff2_out"], w["norm_final"])

    c_gdn_new = tail[:, CONV_PAD - (CONV_W - 1):, MIX_QKV:MIX_QKV + GDN_CONV_CH]
    c_ssm_new = tail[:, CONV_PAD - (CONV_W - 1):, MIX_XBC:MIX_XBC + SSM_CONV_CH]
    return (y.reshape(bsz, seq, D_MODEL), s_gdn_new, c_gdn_new,
            s_ssm_new.reshape(bsz, SSM_HEADS, SSM_HEADDIM, SSM_DSTATE), c_ssm_new)


def _row(v):
    return v.astype(F32).reshape(1, -1)


def _prep_weights(l, norm_ff1, w_ff1_in, w_ff1_out, norm_mix, w_in, gdn_conv_w, gdn_a_log,
                  gdn_dt_bias, gdn_norm, ssm_conv_w, ssm_conv_b, ssm_a_log, ssm_dt_bias, ssm_d,
                  ssm_norm, w_branch, w_out, norm_ff2, w_ff2_in, w_ff2_out, norm_final):
    row = _row

    w_mix, w_rest = _pack_w_in(jnp.swapaxes(w_in[l], 0, 1))

    def two_rows(a, b, offset):
        par = jnp.zeros((SUBLANES, LANES), F32)
        return par.at[0].set(_pad_lanes(a, offset)).at[1].set(_pad_lanes(b, offset))

    return {
        "norm_ff1": row(norm_ff1[l]), "w_ff1_in": w_ff1_in[l].astype(BF16),
        "w_ff1_out": w_ff1_out[l].astype(BF16), "norm_mix": row(norm_mix[l]),
        "w_mix": w_mix, "w_rest": w_rest,
        "conv_w": jnp.concatenate([gdn_conv_w[l], ssm_conv_w[l]], axis=1).astype(F32),
        "conv_b": jnp.concatenate([jnp.zeros((1, GDN_CONV_CH), F32), row(ssm_conv_b[l])], axis=1),
        "gdn_par": two_rows(gdn_a_log[l], gdn_dt_bias[l], SM_A),
        "gdn_norm": row(gdn_norm[l]),
        "sel_gdn": _pair_selection(0, LANES, GDN_PAIRS),
        "ssm_par": two_rows(ssm_a_log[l], ssm_dt_bias[l], SM_DT),
        "ssm_d_ch": jnp.repeat(ssm_d[l].astype(F32), SSM_HEADDIM).reshape(1, -1),
        "ssm_norm": row(ssm_norm[l]),
        "sel_ssm": _pair_selection(SM_DT, SSM_HEADDIM, SSM_PAIRS // 2),
        "w_branch": w_branch[l].astype(BF16), "w_out": w_out[l].astype(BF16),
        "norm_ff2": row(norm_ff2[l]), "w_ff2_in": w_ff2_in[l].astype(BF16),
        "w_ff2_out": w_ff2_out[l].astype(BF16), "norm_final": row(norm_final),
    }


def kernel(x_prompt, x_sample, mem_prompt, state_gdn, state_gdn_conv, state_ssm, state_ssm_conv,
           cache_mem_k, cache_mem_v, norm_ff1, w_ff1_in, w_ff1_out, norm_mix, w_in,
           gdn_conv_w, gdn_a_log, gdn_dt_bias, gdn_norm, ssm_conv_w, ssm_conv_b, ssm_a_log,
           ssm_dt_bias, ssm_d, ssm_norm, norm_mem, w_mem_kv, w_branch, w_out,
           norm_ff2, w_ff2_in, w_ff2_out, norm_final):
    assert w_in.shape[0] == 1, "the kernels implement the single-layer configuration"
    l = 0
    bp = x_prompt.shape[0]
    w = _prep_weights(l, norm_ff1, w_ff1_in, w_ff1_out, norm_mix, w_in, gdn_conv_w, gdn_a_log,
                      gdn_dt_bias, gdn_norm, ssm_conv_w, ssm_conv_b, ssm_a_log, ssm_dt_bias, ssm_d,
                      ssm_norm, w_branch, w_out, norm_ff2, w_ff2_in, w_ff2_out, norm_final)

    n_mem_tok = mem_prompt.shape[0] * mem_prompt.shape[1]
    kv = _memkv(mem_prompt.reshape(n_mem_tok, D_MODEL), _row(norm_mem[l]), w_mem_kv[l].astype(BF16))
    mk = kv[:, :MEM_WIDTH].reshape(bp, N_MEM, MEM_HEADS, MEM_HEAD_DIM)
    mv = kv[:, MEM_WIDTH:].reshape(bp, N_MEM, MEM_HEADS, MEM_HEAD_DIM)
    dtp = x_prompt.dtype
    yp, sgp, cgp, ssp, csp = _layer(
        x_prompt, mk, mv,
        jnp.zeros((bp, GDN_HEADS, GDN_DK, GDN_DV), dtp), jnp.zeros((bp, CONV_W - 1, GDN_CONV_CH), dtp),
        jnp.zeros((bp, SSM_HEADS, SSM_HEADDIM, SSM_DSTATE), dtp),
        jnp.zeros((bp, CONV_W - 1, SSM_CONV_CH), dtp), w)
    ys_, sgs, cgs, sss, css = _layer(
        x_sample, cache_mem_k[l], cache_mem_v[l], state_gdn[l], state_gdn_conv[l],
        state_ssm[l], state_ssm_conv[l], w)

    lead = lambda a: a[None]
    return (yp, ys_, lead(sgp), lead(cgp), lead(ssp), lead(csp), lead(mk), lead(mv),
            lead(sgs), lead(cgs), lead(sss), lead(css))
```
